```python
import math
import jax
import jax.numpy as jnp
from jax import lax
import numpy as np

D_MODEL = 2048
BATCH = 1
SEQ = 16384
DEPTH = 1
DEC_BATCH = 8
DEC_SEQ = 32
PAST_LEN = 1024

CHUNK = 64
EPS = 1e-6
S5_WIDTH = D_MODEL // 2
S5_GROUP = 16
S5_GROUPS = S5_WIDTH // S5_GROUP
S5_STATE = 64
S5_MAX_RE = -1e-4
N_HEADS = 8
N_KV_HEADS = 2
Q_PER_KV = N_HEADS // N_KV_HEADS
HEAD_DIM = 128
ATT_WIDTH = N_HEADS * HEAD_DIM
KV_WIDTH = N_KV_HEADS * HEAD_DIM
N_IDX_HEADS = 8
IDX_DIM = 64
TOP_K_MAX = 256
Q_BLOCK = 128
N_GROUPS = 4
EXPERTS_PER_GROUP = 8
N_EXPERTS = N_GROUPS * EXPERTS_PER_GROUP
D_EXPERT = D_MODEL // 4
TOP_K_EXPERTS = 2
MOE_BLOCK = 128
IN_SIZES = (S5_WIDTH, ATT_WIDTH, KV_WIDTH, KV_WIDTH, N_IDX_HEADS * IDX_DIM, IDX_DIM, N_IDX_HEADS, D_MODEL, D_MODEL)
IN_COLS = S5_WIDTH + ATT_WIDTH + 2 * KV_WIDTH + N_IDX_HEADS * IDX_DIM + IDX_DIM + N_IDX_HEADS + 2 * D_MODEL

kernel_name = 'hybrid_s5_dsa_hiermoe_stream_step'


def rmsnorm(x, g):
    xf = x.astype(jnp.float32)
    y = xf * lax.rsqrt(jnp.mean(xf * xf, axis=-1, keepdims=True) + EPS)
    return (y * g.astype(jnp.float32)).astype(x.dtype)


def s5_discretise(a_re, a_im, log_dt, b_re, b_im):
    f32 = jnp.float32
    lr = jnp.minimum(a_re.astype(f32), S5_MAX_RE)
    li = a_im.astype(f32)
    dt = jnp.exp(log_dt.astype(f32))[:, None]
    mag = jnp.exp(lr * dt)
    lbr = mag * jnp.cos(li * dt)
    lbi = mag * jnp.sin(li * dt)
    den = lr * lr + li * li
    fr = ((lbr - 1.0) * lr + lbi * li) / den
    fi = (lbi * lr - (lbr - 1.0) * li) / den
    br = b_re.astype(f32)
    bi = b_im.astype(f32)
    bbr = fr[..., None] * br - fi[..., None] * bi
    bbi = fr[..., None] * bi + fi[..., None] * br
    return lbr, lbi, bbr, bbi


def s5_mixer(u, h0_re, h0_im, a_re, a_im, log_dt, b_re, b_im, c_re, c_im, d, w_glu):
    f32 = jnp.float32
    bsz, length, _ = u.shape
    lbr, lbi, bbr, bbi = s5_discretise(a_re, a_im, log_dt, b_re, b_im)
    ug = u.astype(f32).reshape(bsz, length, S5_GROUPS, S5_GROUP)
    bu_re = jnp.einsum('blgc,gpc->blgp', ug, bbr)
    bu_im = jnp.einsum('blgc,gpc->blgp', ug, bbi)
    ar = jnp.broadcast_to(lbr, bu_re.shape)
    ai = jnp.broadcast_to(lbi, bu_re.shape)

    def combine(e1, e2):
        a1r, a1i, b1r, b1i = e1
        a2r, a2i, b2r, b2i = e2
        return (a2r * a1r - a2i * a1i,
                a2r * a1i + a2i * a1r,
                a2r * b1r - a2i * b1i + b2r,
                a2r * b1i + a2i * b1r + b2i)

    acr, aci, hr, hi = lax.associative_scan(combine, (ar, ai, bu_re, bu_im), axis=1)
    h0r = h0_re.astype(f32)[:, None]
    h0i = h0_im.astype(f32)[:, None]
    hr = hr + acr * h0r - aci * h0i
    hi = hi + acr * h0i + aci * h0r
    y = (jnp.einsum('blgp,gcp->blgc', hr, c_re.astype(f32))
         - jnp.einsum('blgp,gcp->blgc', hi, c_im.astype(f32))
         + d.astype(f32) * ug)
    y = jax.nn.gelu(y.reshape(bsz, length, S5_WIDTH))
    za, zb = jnp.split(y @ w_glu.astype(f32), 2, axis=-1)
    out = za * jax.nn.sigmoid(zb)
    return out.astype(u.dtype), hr[:, -1], hi[:, -1]


def dsa_attend(q, qi, wi, q_pos, k_all, v_all, ki_all, n_top):
    f32 = jnp.float32
    bsz, tq = q.shape[:2]
    s_idx = jax.nn.relu(jnp.einsum('bqhd,bkd->bqhk', qi.astype(f32), ki_all.astype(f32)))
    score = jnp.einsum('bqhk,bqh->bqk', s_idx, wi.astype(f32))
    k_pos = jnp.arange(k_all.shape[1])
    q_chunk = q_pos // CHUNK
    admissible = (k_pos[None, :] // CHUNK) <= q_chunk[:, None]
    score = jnp.where(admissible[None], score, -jnp.inf)
    _, top = lax.top_k(score, n_top)
    valid = (top // CHUNK) <= q_chunk[None, :, None]
    kg = jax.vmap(lambda kk, ii: kk[ii])(k_all, top)
    vg = jax.vmap(lambda vv, ii: vv[ii])(v_all, top)
    qg = q.reshape(bsz, tq, N_KV_HEADS, Q_PER_KV, HEAD_DIM).astype(f32)
    logits = jnp.einsum('bqgrd,bqkgd->bqgrk', qg, kg.astype(f32)) * (HEAD_DIM ** -0.5)
    logits = jnp.where(valid[:, :, None, None, :], logits, -jnp.inf)
    probs = jax.nn.softmax(logits, axis=-1)
    out = jnp.einsum('bqgrk,bqkgd->bqgrd', probs, vg.astype(f32))
    return out.reshape(bsz, tq, ATT_WIDTH).astype(q.dtype)


def hier_moe(h, w_rg, b_rg, w_re, b_re, w_gate, w_up, w_down):
    f32 = jnp.float32
    n_tok, dm = h.shape
    hf = h.astype(f32)
    pg = jax.nn.softmax(hf @ w_rg.astype(f32) + b_rg.astype(f32), axis=-1)
    g_sel = jnp.argmax(pg, axis=-1)
    g_w = jnp.take_along_axis(pg, g_sel[:, None], axis=1)
    le = (hf @ w_re.astype(f32) + b_re.astype(f32)).reshape(n_tok, N_GROUPS, EXPERTS_PER_GROUP)
    le_sel = jnp.take_along_axis(le, g_sel[:, None, None], axis=1)[:, 0]
    pe = jax.nn.softmax(le_sel, axis=-1)
    top_p, top_i = lax.top_k(pe, TOP_K_EXPERTS)
    top_p = top_p / jnp.sum(top_p, axis=-1, keepdims=True)
    gate = g_w * top_p
    expert = g_sel[:, None] * EXPERTS_PER_GROUP + top_i
    m = n_tok * TOP_K_EXPERTS
    flat_e = expert.reshape(-1).astype(jnp.int32)
    flat_w = gate.reshape(-1)
    flat_t = jnp.repeat(jnp.arange(n_tok, dtype=jnp.int32), TOP_K_EXPERTS)
    order = jnp.argsort(flat_e)
    se = flat_e[order]
    counts = jnp.bincount(flat_e, length=N_EXPERTS)
    padded = (counts + MOE_BLOCK - 1) // MOE_BLOCK * MOE_BLOCK
    pad_end = jnp.cumsum(padded)
    pad_start = pad_end - padded
    start = jnp.cumsum(counts) - counts
    dest = pad_start[se] + jnp.arange(m, dtype=jnp.int32) - start[se]
    n_blocks = -(-(m + N_EXPERTS * (MOE_BLOCK - 1)) // MOE_BLOCK)
    rows = n_blocks * MOE_BLOCK
    row_tok = jnp.zeros((rows,), jnp.int32).at[dest].set(flat_t[order])
    row_w = jnp.zeros((rows,), f32).at[dest].set(flat_w[order])
    block_e = jnp.minimum(jnp.searchsorted(pad_end, jnp.arange(n_blocks) * MOE_BLOCK, side='right'), N_EXPERTS - 1)
    xs = h[row_tok].reshape(n_blocks, MOE_BLOCK, dm)

    def expert_block(args):
        xb, e = args
        return (jax.nn.silu(xb @ w_gate[e]) * (xb @ w_up[e])) @ w_down[e]

    ys = lax.map(expert_block, (xs, block_e)).reshape(rows, dm).astype(f32) * row_w[:, None]
    out = jax.ops.segment_sum(ys, row_tok, num_segments=n_tok)
    return out.astype(h.dtype)


def trunk_layer(x, h0_re, h0_im, k_past, v_past, ki_past, p):
    f32 = jnp.float32
    bsz, t, _ = x.shape
    xn = rmsnorm(x, p['norm_mix'])
    proj = xn @ p['w_in']
    pts = []
    acc = 0
    for s in IN_SIZES[:-1]:
        acc += s
        pts.append(acc)
    u, q, k, v, qi, ki, wi, g_a, g_b = jnp.split(proj, pts, axis=-1)
    y_a, s_re, s_im = s5_mixer(u, h0_re, h0_im, p['s5_a_re'], p['s5_a_im'], p['s5_log_dt'], p['s5_b_re'],
                               p['s5_b_im'], p['s5_c_re'], p['s5_c_im'], p['s5_d'], p['w_glu'])
    q = rmsnorm(q.reshape(bsz, t, N_HEADS, HEAD_DIM), p['q_norm'])
    k = rmsnorm(k.reshape(bsz, t, N_KV_HEADS, HEAD_DIM), p['k_norm'])
    v = v.reshape(bsz, t, N_KV_HEADS, HEAD_DIM)
    qi = qi.reshape(bsz, t, N_IDX_HEADS, IDX_DIM)
    ki = rmsnorm(ki, p['idx_k_norm'])
    wi = wi.astype(f32) * (N_IDX_HEADS ** -0.5) * (IDX_DIM ** -0.5)
    if k_past is None:
        pos0 = 0
        k_all, v_all, ki_all = k, v, ki
    else:
        pos0 = k_past.shape[1]
        k_all = jnp.concatenate([k_past.astype(k.dtype), k], axis=1)
        v_all = jnp.concatenate([v_past.astype(v.dtype), v], axis=1)
        ki_all = jnp.concatenate([ki_past.astype(ki.dtype), ki], axis=1)
    n_keys = k_all.shape[1]
    n_top = min(TOP_K_MAX, n_keys // 4)
    q_pos = pos0 + jnp.arange(t)
    if t % Q_BLOCK == 0:
        nb = t // Q_BLOCK
        qs = q.reshape(bsz, nb, Q_BLOCK, N_HEADS, HEAD_DIM).transpose(1, 0, 2, 3, 4)
        qis = qi.reshape(bsz, nb, Q_BLOCK, N_IDX_HEADS, IDX_DIM).transpose(1, 0, 2, 3, 4)
        wis = wi.reshape(bsz, nb, Q_BLOCK, N_IDX_HEADS).transpose(1, 0, 2, 3)
        ps = q_pos.reshape(nb, Q_BLOCK)
        ob = lax.map(lambda a: dsa_attend(a[0], a[1], a[2], a[3], k_all, v_all, ki_all, n_top), (qs, qis, wis, ps))
        y_b = ob.transpose(1, 0, 2, 3).reshape(bsz, t, ATT_WIDTH)
    else:
        y_b = dsa_attend(q, qi, wi, q_pos, k_all, v_all, ki_all, n_top)
    merged = (jax.nn.sigmoid(g_a.astype(f32)) * (y_a @ p['w_branch_a']).astype(f32)
              + jax.nn.sigmoid(g_b.astype(f32)) * (y_b @ p['w_branch_b']).astype(f32))
    x = x + (merged.astype(x.dtype) @ p['w_out']).astype(x.dtype)
    hn = rmsnorm(x, p['norm_ffn']).reshape(bsz * t, D_MODEL)
    ff = hier_moe(hn, p['w_router_group'], p['b_router_group'], p['w_router_expert'], p['b_router_expert'],
                  p['w_gate'], p['w_up'], p['w_down'])
    x = x + ff.reshape(bsz, t, D_MODEL).astype(x.dtype)
    return x, s_re, s_im, k, v, ki


def setup_inputs(seed: int = 0) -> dict:
    key = jax.random.key(seed)
    ks = jax.random.split(key, 32)
    f32 = jnp.float32

    def nrm(k, shape, scale):
        return jax.random.normal(k, shape, f32) * scale

    n_idx = jnp.arange(S5_STATE, dtype=f32)
    return {
        'x_prompt': nrm(ks[0], (BATCH, SEQ, D_MODEL), 1.0),
        'x_sample': nrm(ks[1], (DEC_BATCH, DEC_SEQ, D_MODEL), 1.0),
        'state_s5_re': nrm(ks[2], (DEC_BATCH, S5_GROUPS, S5_STATE), 0.5),
        'state_s5_im': nrm(ks[3], (DEC_BATCH, S5_GROUPS, S5_STATE), 0.5),
        'cache_k': nrm(ks[4], (DEC_BATCH, PAST_LEN, N_KV_HEADS, HEAD_DIM), 1.0),
        'cache_v': nrm(ks[5], (DEC_BATCH, PAST_LEN, N_KV_HEADS, HEAD_DIM), 1.0),
        'cache_idx_k': nrm(ks[6], (DEC_BATCH, PAST_LEN, IDX_DIM), 1.0),
        'norm_mix': 1.0 + nrm(ks[7], (D_MODEL,), 0.02),
        'w_in': nrm(ks[8], (D_MODEL, IN_COLS), D_MODEL ** -0.5),
        's5_a_re': -0.5 + nrm(ks[9], (S5_GROUPS, S5_STATE), 0.01),
        's5_a_im': math.pi * n_idx[None, :] + nrm(ks[10], (S5_GROUPS, S5_STATE), 0.01),
        's5_log_dt': jax.random.uniform(ks[11], (S5_GROUPS,), f32, math.log(1e-3), math.log(1e-1)),
        's5_b_re': nrm(ks[12], (S5_GROUPS, S5_STATE, S5_GROUP), (2 * S5_GROUP) ** -0.5),
        's5_b_im': nrm(ks[13], (S5_GROUPS, S5_STATE, S5_GROUP), (2 * S5_GROUP) ** -0.5),
        's5_c_re': nrm(ks[14], (S5_GROUPS, S5_GROUP, S5_STATE), (2 * S5_STATE) ** -0.5),
        's5_c_im': nrm(ks[15], (S5_GROUPS, S5_GROUP, S5_STATE), (2 * S5_STATE) ** -0.5),
        's5_d': nrm(ks[16], (S5_GROUPS, S5_GROUP), 1.0),
        'w_glu': nrm(ks[17], (S5_WIDTH, 2 * S5_WIDTH), S5_WIDTH ** -0.5),
        'q_norm': 1.0 + nrm(ks[18], (HEAD_DIM,), 0.02),
        'k_norm': 1.0 + nrm(ks[19], (HEAD_DIM,), 0.02),
        'idx_k_norm': 1.0 + nrm(ks[20], (IDX_DIM,), 0.02),
        'w_branch_a': nrm(ks[21], (S5_WIDTH, D_MODEL), S5_WIDTH ** -0.5),
        'w_branch_b': nrm(ks[22], (ATT_WIDTH, D_MODEL), ATT_WIDTH ** -0.5),
        'w_out': nrm(ks[23], (D_MODEL, D_MODEL), D_MODEL ** -0.5),
        'norm_ffn': 1.0 + nrm(ks[24], (D_MODEL,), 0.02),
        'w_router_group': nrm(ks[25], (D_MODEL, N_GROUPS), D_MODEL ** -0.5),
        'b_router_group': nrm(ks[26], (N_GROUPS,), 0.01),
        'w_router_expert': nrm(ks[27], (D_MODEL, N_EXPERTS), D_MODEL ** -0.5),
        'b_router_expert': nrm(ks[28], (N_EXPERTS,), 0.01),
        'w_gate': nrm(ks[29], (N_EXPERTS, D_MODEL, D_EXPERT), D_MODEL ** -0.5),
        'w_up': nrm(ks[30], (N_EXPERTS, D_MODEL, D_EXPERT), D_MODEL ** -0.5),
        'w_down': nrm(ks[31], (N_EXPERTS, D_EXPERT, D_MODEL), D_EXPERT ** -0.5),
    }


def reference(x_prompt, x_sample, state_s5_re, state_s5_im, cache_k, cache_v, cache_idx_k,
              norm_mix, w_in, s5_a_re, s5_a_im, s5_log_dt, s5_b_re, s5_b_im, s5_c_re, s5_c_im, s5_d,
              w_glu, q_norm, k_norm, idx_k_norm, w_branch_a, w_branch_b, w_out, norm_ffn,
              w_router_group, b_router_group, w_router_expert, b_router_expert, w_gate, w_up, w_down):
    p = dict(norm_mix=norm_mix, w_in=w_in, s5_a_re=s5_a_re, s5_a_im=s5_a_im, s5_log_dt=s5_log_dt,
             s5_b_re=s5_b_re, s5_b_im=s5_b_im, s5_c_re=s5_c_re, s5_c_im=s5_c_im, s5_d=s5_d, w_glu=w_glu,
             q_norm=q_norm, k_norm=k_norm, idx_k_norm=idx_k_norm, w_branch_a=w_branch_a,
             w_branch_b=w_branch_b, w_out=w_out, norm_ffn=norm_ffn, w_router_group=w_router_group,
             b_router_group=b_router_group, w_router_expert=w_router_expert,
             b_router_expert=b_router_expert, w_gate=w_gate, w_up=w_up, w_down=w_down)
    h0 = jnp.zeros((x_prompt.shape[0], S5_GROUPS, S5_STATE), jnp.float32)
    y_prompt, s5_re_prompt, s5_im_prompt, k_prompt, v_prompt, idx_k_prompt = trunk_layer(
        x_prompt, h0, h0, None, None, None, p)
    y_sample, s5_re_sample, s5_im_sample, k_sample, v_sample, idx_k_sample = trunk_layer(
        x_sample, state_s5_re, state_s5_im, cache_k, cache_v, cache_idx_k, p)
    return (y_prompt, y_sample, s5_re_prompt, s5_im_prompt, k_prompt, v_prompt, idx_k_prompt,
            s5_re_sample, s5_im_sample, k_sample, v_sample, idx_k_sample)
```

```python
import functools
import math

import jax
import jax.numpy as jnp
from jax import lax
from jax.experimental import pallas as pl
from jax.experimental.pallas import tpu as pltpu

F32 = jnp.float32
BF16 = jnp.bfloat16
I32 = jnp.int32

D_MODEL = 2048
CHUNK = 64
EPS = 1e-6
S5_WIDTH = 1024
S5_GROUP = 16
S5_GROUPS = 64
S5_STATE = 64
S5_MAX_RE = -1e-4
N_HEADS = 8
N_KV_HEADS = 2
Q_PER_KV = 4
HEAD_DIM = 128
ATT_WIDTH = 1024
KV_WIDTH = 256
N_IDX_HEADS = 8
IDX_DIM = 64
TOP_K_MAX = 256
N_GROUPS = 4
EXPERTS_PER_GROUP = 8
N_EXPERTS = 32
D_EXPERT = 512
IN_SIZES = (S5_WIDTH, ATT_WIDTH, KV_WIDTH, KV_WIDTH, N_IDX_HEADS * IDX_DIM, IDX_DIM, N_IDX_HEADS, D_MODEL, D_MODEL)

LANES = 128
PROJ_TN = 512
COL_U, COL_Q, COL_GA, COL_GB, COL_QI, COL_K, COL_V, COL_KI, COL_WI = 0, 1024, 2048, 4096, 6144, 6656, 6912, 7168, 7232
PROJ_COLS = 7680
S5_LANE_BLOCKS = S5_WIDTH // LANES
S5_BLOCK_STATE = (LANES // S5_GROUP) * S5_STATE
VMEM_LIMIT = 56 * 1024 * 1024
INT_MIN = -2 ** 31
NEG_BIG = -1e30


def _dot(a, b):
    return jnp.dot(a, b, preferred_element_type=F32)


def _split_bf16(x):
    hi = x.astype(BF16)
    lo = (x - hi.astype(F32)).astype(BF16)
    return hi, lo


def _sigmoid(x):
    return 1.0 / (1.0 + jnp.exp(-x))


def _params(sem, **kw):
    return pltpu.CompilerParams(dimension_semantics=sem, vmem_limit_bytes=VMEM_LIMIT, **kw)


def _group_norm(a, gain):
    ms = jnp.mean(a * a, axis=-1, keepdims=True)
    return a * lax.rsqrt(ms + EPS) * gain


def _inproj_kernel(x_ref, g_ref, w_ref, gain_ref, o_ref, xn_ref):
    j = pl.program_id(1)

    @pl.when(j == 0)
    def _():
        xf = x_ref[...]
        ms = jnp.mean(xf * xf, axis=-1, keepdims=True)
        xn_ref[...] = (xf * lax.rsqrt(ms + EPS) * g_ref[...]).astype(BF16)

    acc = _dot(xn_ref[...], w_ref[...])
    gain = gain_ref[...]
    groups = [slice(c * LANES, (c + 1) * LANES) for c in range(PROJ_TN // LANES)]

    @pl.when((j < 2) | (j == 12))
    def _():
        o_ref[...] = acc

    @pl.when((j == 2) | (j == 3))
    def _():
        for s in groups:
            o_ref[:, s] = _group_norm(acc[:, s], gain[:, s])

    @pl.when((j >= 4) & (j < 12))
    def _():
        o_ref[...] = _sigmoid(acc)

    @pl.when(j == 13)
    def _():
        for s in groups[:2]:
            o_ref[:, s] = _group_norm(acc[:, s], gain[:, s])
        o_ref[:, 2 * LANES:] = acc[:, 2 * LANES:]

    @pl.when(j == 14)
    def _():
        a = acc[:, :LANES]
        lane = lax.broadcasted_iota(I32, a.shape, 1)
        is_ki = lane < IDX_DIM
        ms = jnp.sum(jnp.where(is_ki, a * a, 0.0), axis=-1, keepdims=True) * (1.0 / IDX_DIM)
        ki = a * lax.rsqrt(ms + EPS) * gain[:, :LANES]
        wi = a * (N_IDX_HEADS ** -0.5) * (IDX_DIM ** -0.5)
        o_ref[:, :LANES] = jnp.where(is_ki, ki, wi)
        o_ref[:, LANES:] = acc[:, LANES:]


def _inproj(x, g, w, gain, tm):
    n = x.shape[0]
    return pl.pallas_call(
        _inproj_kernel,
        grid=(n // tm, PROJ_COLS // PROJ_TN),
        in_specs=[pl.BlockSpec((tm, D_MODEL), lambda i, j: (i, 0)),
                  pl.BlockSpec((1, D_MODEL), lambda i, j: (0, 0)),
                  pl.BlockSpec((D_MODEL, PROJ_TN), lambda i, j: (0, j)),
                  pl.BlockSpec((1, PROJ_TN), lambda i, j: (0, j))],
        out_specs=pl.BlockSpec((tm, PROJ_TN), lambda i, j: (i, j)),
        out_shape=jax.ShapeDtypeStruct((n, PROJ_COLS), F32),
        scratch_shapes=[pltpu.VMEM((tm, D_MODEL), BF16)],
        compiler_params=_params(("arbitrary", "arbitrary")),
        name="inproj",
    )(x, g, w, gain)


def _gelu_tanh(y):
    return 0.5 * y * (1.0 + jnp.tanh(math.sqrt(2.0 / math.pi) * (y + 0.044715 * (y * y * y))))


def _s5_kernel(u_ref, h0r_ref, h0i_ref, bhi_ref, blo_ref, air_ref, aii_ref, apr_ref, api_ref,
               a1r_ref, a1i_ref, tri_ref, cd_ref, d_ref, yg_ref, sr_ref, si_ref,
               bu_scr, h_scr, hr_scr, hi_scr, *, lc, tt):
    t = pl.program_id(2)
    ns = S5_BLOCK_STATE

    @pl.when(t == 0)
    def _():
        hr_scr[...] = h0r_ref[...]
        hi_scr[...] = h0i_ref[...]

    u = u_ref[...]
    u_hi, u_lo = _split_bf16(u)
    bhi = bhi_ref[...]
    bu_scr[...] = _dot(u_hi, bhi) + _dot(u_lo, bhi) + _dot(u_hi, blo_ref[...])
    tri = tri_ref[...]
    air, aii, apr, api = air_ref[...], aii_ref[...], apr_ref[...], api_ref[...]
    a1r, a1i = a1r_ref[...], a1i_ref[...]

    def chunk(s, carry):
        h_re, h_im = carry
        r0 = pl.multiple_of(s * lc, lc)
        br = bu_scr[pl.ds(r0, lc), 0:ns]
        bi = bu_scr[pl.ds(r0, lc), ns:2 * ns]
        z = jnp.concatenate([air * br - aii * bi, air * bi + aii * br], axis=1)
        z_hi, z_lo = _split_bf16(z)
        c = _dot(tri, z_hi) + _dot(tri, z_lo)
        cr = c[:, 0:ns] + (a1r * h_re - a1i * h_im)
        ci = c[:, ns:2 * ns] + (a1r * h_im + a1i * h_re)
        hr = apr * cr - api * ci
        hi = apr * ci + api * cr
        h_scr[pl.ds(r0, lc), 0:ns] = hr.astype(BF16)
        h_scr[pl.ds(r0, lc), ns:2 * ns] = hi.astype(BF16)
        return hr[lc - 1:lc, :], hi[lc - 1:lc, :]

    h_re, h_im = lax.fori_loop(0, tt // lc, chunk, (hr_scr[...], hi_scr[...]))
    hr_scr[...] = h_re
    hi_scr[...] = h_im
    sr_ref[...] = h_re
    si_ref[...] = h_im
    y = _dot(h_scr[...], cd_ref[...]) + d_ref[...] * u
    yg_ref[...] = _gelu_tanh(y).astype(BF16)


def _s5_tables(a_re, a_im, log_dt, b_re, b_im, c_re, c_im, lc):
    lr = jnp.minimum(a_re.astype(F32), S5_MAX_RE)
    li = a_im.astype(F32)
    dt = jnp.exp(log_dt.astype(F32))[:, None]
    mag = jnp.exp(lr * dt)
    lbr = mag * jnp.cos(li * dt)
    lbi = mag * jnp.sin(li * dt)
    den = lr * lr + li * li
    fr = ((lbr - 1.0) * lr + lbi * li) / den
    fi = (lbi * lr - (lbr - 1.0) * li) / den
    br = b_re.astype(F32)
    bi = b_im.astype(F32)
    bbr = fr[..., None] * br - fi[..., None] * bi
    bbi = fr[..., None] * bi + fi[..., None] * br
    j = jnp.arange(lc, dtype=F32)[:, None, None]
    ang = j * (li * dt)[None]
    lmag = j * (lr * dt)[None]
    n_state = S5_GROUPS * S5_STATE
    apr = (jnp.exp(lmag) * jnp.cos(ang)).reshape(lc, n_state)
    api = (jnp.exp(lmag) * jnp.sin(ang)).reshape(lc, n_state)
    air = (jnp.exp(-lmag) * jnp.cos(ang)).reshape(lc, n_state)
    aii = (-jnp.exp(-lmag) * jnp.sin(ang)).reshape(lc, n_state)
    a1r = lbr.reshape(1, n_state)
    a1i = lbi.reshape(1, n_state)
    gpb = LANES // S5_GROUP
    eye = jnp.eye(gpb, dtype=F32)

    def bdiag(b):
        return jnp.einsum('kgpc,gh->kgchp', b.reshape(S5_LANE_BLOCKS, gpb, S5_STATE, S5_GROUP), eye).reshape(
            S5_LANE_BLOCKS, LANES, S5_BLOCK_STATE)

    def cdiag(c):
        return jnp.einsum('kgcp,gh->kgphc', c.reshape(S5_LANE_BLOCKS, gpb, S5_GROUP, S5_STATE), eye).reshape(
            S5_LANE_BLOCKS, S5_BLOCK_STATE, LANES)

    bd = jnp.concatenate([bdiag(bbr), bdiag(bbi)], axis=-1)
    bd_hi = bd.astype(BF16)
    bd_lo = (bd - bd_hi.astype(F32)).astype(BF16)
    cd = jnp.concatenate([cdiag(c_re.astype(F32)), -cdiag(c_im.astype(F32))], axis=1).astype(BF16)
    tri = jnp.tril(jnp.ones((lc, lc), F32)).astype(BF16)
    return dict(bd_hi=bd_hi, bd_lo=bd_lo, air=air, aii=aii, apr=apr, api=api, a1r=a1r, a1i=a1i, tri=tri, cd=cd)


def _s5(proj3, h0_re, h0_im, tb, dvec, lc, tt):
    bsz, t, _ = proj3.shape
    ns = S5_BLOCK_STATE
    n_state = S5_GROUPS * S5_STATE
    tab = lambda: pl.BlockSpec((lc, ns), lambda b, k, i: (0, k))
    row = lambda: pl.BlockSpec((1, ns), lambda b, k, i: (0, k))
    st = lambda: pl.BlockSpec((None, 1, ns), lambda b, k, i: (b, 0, k))
    return pl.pallas_call(
        functools.partial(_s5_kernel, lc=lc, tt=tt),
        grid=(bsz, S5_LANE_BLOCKS, t // tt),
        in_specs=[pl.BlockSpec((None, tt, LANES), lambda b, k, i: (b, i, k)),
                  st(), st(),
                  pl.BlockSpec((None, LANES, 2 * ns), lambda b, k, i: (k, 0, 0)),
                  pl.BlockSpec((None, LANES, 2 * ns), lambda b, k, i: (k, 0, 0)),
                  tab(), tab(), tab(), tab(), row(), row(),
                  pl.BlockSpec((lc, lc), lambda b, k, i: (0, 0)),
                  pl.BlockSpec((None, 2 * ns, LANES), lambda b, k, i: (k, 0, 0)),
                  pl.BlockSpec((1, LANES), lambda b, k, i: (0, k))],
        out_specs=[pl.BlockSpec((None, tt, LANES), lambda b, k, i: (b, i, k)), st(), st()],
        out_shape=[jax.ShapeDtypeStruct((bsz, t, S5_WIDTH), BF16),
                   jax.ShapeDtypeStruct((bsz, 1, n_state), F32),
                   jax.ShapeDtypeStruct((bsz, 1, n_state), F32)],
        scratch_shapes=[pltpu.VMEM((tt, 2 * ns), F32), pltpu.VMEM((tt, 2 * ns), BF16),
                        pltpu.VMEM((1, ns), F32), pltpu.VMEM((1, ns), F32)],
        compiler_params=_params(("arbitrary", "arbitrary", "arbitrary")),
        name="s5",
    )(proj3, h0_re, h0_im, tb['bd_hi'], tb['bd_lo'], tb['air'], tb['aii'], tb['apr'], tb['api'],
      tb['a1r'], tb['a1i'], tb['tri'], tb['cd'], dvec)


def _glu_kernel(y_ref, w_ref, o_ref):
    acc = _dot(y_ref[...], w_ref[...])
    half = acc.shape[1] // 2
    o_ref[...] = (acc[:, :half] * _sigmoid(acc[:, half:])).astype(BF16)


def _glu(yg, w, tm):
    n = yg.shape[0]
    tn = 512
    return pl.pallas_call(
        _glu_kernel,
        grid=(n // tm, 2 * S5_WIDTH // tn),
        in_specs=[pl.BlockSpec((tm, S5_WIDTH), lambda i, j: (i, 0)),
                  pl.BlockSpec((S5_WIDTH, tn), lambda i, j: (0, j))],
        out_specs=pl.BlockSpec((tm, tn // 2), lambda i, j: (i, j)),
        out_shape=jax.ShapeDtypeStruct((n, S5_WIDTH), BF16),
        compiler_params=_params(("arbitrary", "arbitrary")),
        name="glu",
    )(yg, w)


def _dsa_kernel(q_ref, qi_ref, kiw_ref, kit_ref, kt_ref, v_ref, o_ref,
                key_scr, qs_scr, qis_scr, wis_scr, m_scr, l_scr, acc_scr,
                *, tq, kb, pos0, n_keys, n_top):
    i = pl.program_id(1)
    q_start = pos0 + i * tq
    k_end = jnp.minimum(n_keys, (q_start + tq - 1) // CHUNK * CHUNK + CHUNK)
    nkb = (k_end + kb - 1) // kb

    q = q_ref[...] * (HEAD_DIM ** -0.5)
    for h in range(N_HEADS):
        g, r = divmod(h, Q_PER_KV)
        qs_scr[g, r * tq:(r + 1) * tq, :] = q[:, h * HEAD_DIM:(h + 1) * HEAD_DIM].astype(BF16)
    qi = qi_ref[...]
    kiw = kiw_ref[...]
    for h in range(N_IDX_HEADS):
        qis_scr[h * tq:(h + 1) * tq, :] = qi[:, h * IDX_DIM:(h + 1) * IDX_DIM].astype(BF16)
        wis_scr[h * tq:(h + 1) * tq, :] = kiw[:, IDX_DIM + h:IDX_DIM + h + 1]

    q_chunk = (q_start + lax.broadcasted_iota(I32, (tq, 1), 0)) // CHUNK

    def score_block(b, carry):
        c0 = pl.multiple_of(b * kb, kb)
        s = _dot(qis_scr[...], kit_ref[:, pl.ds(c0, kb)])
        s = jnp.maximum(s, 0.0) * wis_scr[...]
        sc = s[0:tq]
        for h in range(1, N_IDX_HEADS):
            sc = sc + s[h * tq:(h + 1) * tq]
        bits = pltpu.bitcast(sc, I32)
        keys = jnp.where(bits < 0, bits ^ 0x7FFFFFFF, bits)
        k_pos = c0 + lax.broadcasted_iota(I32, (1, kb), 1)
        adm = (k_pos // CHUNK <= q_chunk) & (k_pos < n_keys)
        key_scr[:, pl.ds(c0, kb)] = jnp.where(adm, keys, INT_MIN)
        return carry

    lax.fori_loop(0, nkb, score_block, 0)

    def count_ge(cand):
        def body(b, acc):
            c0 = pl.multiple_of(b * kb, kb)
            hit = jnp.where(key_scr[:, pl.ds(c0, kb)] >= cand, 1.0, 0.0)
            for c in range(kb // LANES):
                acc = acc + hit[:, c * LANES:(c + 1) * LANES]
            return acc
        acc = lax.fori_loop(0, nkb, body, jnp.zeros((tq, LANES), F32))
        return jnp.sum(acc, axis=-1, keepdims=True)

    def bit_pass(it, t_off):
        cand_off = t_off | lax.shift_left(jnp.int32(1), 31 - it)
        cnt = count_ge(cand_off ^ INT_MIN)
        return jnp.where(cnt >= n_top, cand_off, t_off)

    t_off = lax.fori_loop(0, 32, bit_pass, jnp.zeros((tq, 1), I32))
    thr = jnp.maximum(t_off ^ INT_MIN, INT_MIN + 1)

    m_scr[...] = jnp.full(m_scr.shape, NEG_BIG, F32)
    l_scr[...] = jnp.zeros(l_scr.shape, F32)
    acc_scr[...] = jnp.zeros(acc_scr.shape, F32)

    def attend_block(b, carry):
        c0 = pl.multiple_of(b * kb, kb)
        sel = key_scr[:, pl.ds(c0, kb)] >= thr
        sel4 = jnp.concatenate([sel] * Q_PER_KV, axis=0)
        for g in range(N_KV_HEADS):
            lg = _dot(qs_scr[g], kt_ref[g * HEAD_DIM:(g + 1) * HEAD_DIM, pl.ds(c0, kb)])
            lg = jnp.where(sel4, lg, NEG_BIG)
            m_old = m_scr[g]
            m_new = jnp.maximum(m_old, jnp.max(lg, axis=-1, keepdims=True))
            p = jnp.exp(lg - m_new)
            alpha = jnp.exp(m_old - m_new)
            l_scr[g] = alpha * l_scr[g] + jnp.sum(p, axis=-1, keepdims=True)
            pv = _dot(p.astype(BF16), v_ref[pl.ds(c0, kb), g * HEAD_DIM:(g + 1) * HEAD_DIM])
            acc_scr[g] = alpha * acc_scr[g] + pv
            m_scr[g] = m_new
        return carry

    lax.fori_loop(0, nkb, attend_block, 0)

    for h in range(N_HEADS):
        g, r = divmod(h, Q_PER_KV)
        rows = slice(r * tq, (r + 1) * tq)
        o_ref[:, h * HEAD_DIM:(h + 1) * HEAD_DIM] = (acc_scr[g, rows, :] / l_scr[g, rows, :]).astype(BF16)


def _dsa(proj3, kit, kt, v, *, tq, kb, pos0, n_keys, n_top):
    bsz, t, _ = proj3.shape
    nkp = kit.shape[-1]
    kern = functools.partial(_dsa_kernel, tq=tq, kb=kb, pos0=pos0, n_keys=n_keys, n_top=n_top)
    return pl.pallas_call(
        kern,
        grid=(bsz, t // tq),
        in_specs=[pl.BlockSpec((None, tq, ATT_WIDTH), lambda b, i: (b, i, COL_Q // ATT_WIDTH)),
                  pl.BlockSpec((None, tq, 512), lambda b, i: (b, i, COL_QI // 512)),
                  pl.BlockSpec((None, tq, 512), lambda b, i: (b, i, COL_KI // 512)),
                  pl.BlockSpec((None, IDX_DIM, nkp), lambda b, i: (b, 0, 0)),
                  pl.BlockSpec((None, KV_WIDTH, nkp), lambda b, i: (b, 0, 0)),
                  pl.BlockSpec((None, nkp, KV_WIDTH), lambda b, i: (b, 0, 0))],
        out_specs=pl.BlockSpec((None, tq, ATT_WIDTH), lambda b, i: (b, i, 0)),
        out_shape=jax.ShapeDtypeStruct((bsz, t, ATT_WIDTH), BF16),
        scratch_shapes=[pltpu.VMEM((tq, nkp), I32),
                        pltpu.VMEM((N_KV_HEADS, Q_PER_KV * tq, HEAD_DIM), BF16),
                        pltpu.VMEM((N_IDX_HEADS * tq, IDX_DIM), BF16),
                        pltpu.VMEM((N_IDX_HEADS * tq, 1), F32),
                        pltpu.VMEM((N_KV_HEADS, Q_PER_KV * tq, 1), F32),
                        pltpu.VMEM((N_KV_HEADS, Q_PER_KV * tq, 1), F32),
                        pltpu.VMEM((N_KV_HEADS, Q_PER_KV * tq, HEAD_DIM), F32)],
        compiler_params=_params(("arbitrary", "arbitrary")),
        name="dsa",
    )(proj3, proj3, proj3, kit, kt, v)


def _merge_kernel(ya_ref, yb_ref, wa_ref, wb_ref, ga_ref, gb_ref, o_ref):
    o_ref[...] = (ga_ref[...] * _dot(ya_ref[...], wa_ref[...])
                  + gb_ref[...] * _dot(yb_ref[...], wb_ref[...])).astype(BF16)


def _merge(ya, yb, wa, wb, proj, tm):
    n = ya.shape[0]
    tn = 512
    return pl.pallas_call(
        _merge_kernel,
        grid=(n // tm, D_MODEL // tn),
        in_specs=[pl.BlockSpec((tm, S5_WIDTH), lambda i, j: (i, 0)),
                  pl.BlockSpec((tm, ATT_WIDTH), lambda i, j: (i, 0)),
                  pl.BlockSpec((S5_WIDTH, tn), lambda i, j: (0, j)),
                  pl.BlockSpec((ATT_WIDTH, tn), lambda i, j: (0, j)),
                  pl.BlockSpec((tm, tn), lambda i, j: (i, COL_GA // tn + j)),
                  pl.BlockSpec((tm, tn), lambda i, j: (i, COL_GB // tn + j))],
        out_specs=pl.BlockSpec((tm, tn), lambda i, j: (i, j)),
        out_shape=jax.ShapeDtypeStruct((n, D_MODEL), BF16),
        compiler_params=_params(("arbitrary", "arbitrary")),
        name="merge",
    )(ya, yb, wa, wb, proj, proj)


def _outproj_kernel(m_ref, w_ref, x_ref, o_ref):
    o_ref[...] = x_ref[...] + _dot(m_ref[...], w_ref[...])


def _outproj(merged, w, x, tm):
    n = x.shape[0]
    tn = 512
    return pl.pallas_call(
        _outproj_kernel,
        grid=(n // tm, D_MODEL // tn),
        in_specs=[pl.BlockSpec((tm, D_MODEL), lambda i, j: (i, 0)),
                  pl.BlockSpec((D_MODEL, tn), lambda i, j: (0, j)),
                  pl.BlockSpec((tm, tn), lambda i, j: (i, j))],
        out_specs=pl.BlockSpec((tm, tn), lambda i, j: (i, j)),
        out_shape=jax.ShapeDtypeStruct((n, D_MODEL), F32),
        compiler_params=_params(("arbitrary", "arbitrary")),
        name="outproj",
    )(merged, w, x)


ROUTER_GROUP_LANE = N_EXPERTS


def _router_kernel(x_ref, g_ref, whi_ref, wlo_ref, b_ref, hn_ref, ids_ref, gates_ref):
    xf = x_ref[...]
    ms = jnp.mean(xf * xf, axis=-1, keepdims=True)
    hn = xf * lax.rsqrt(ms + EPS) * g_ref[...]
    hn_ref[...] = hn
    h_hi, h_lo = _split_bf16(hn)
    whi = whi_ref[...]
    lg = _dot(h_hi, whi) + _dot(h_lo, whi) + _dot(h_hi, wlo_ref[...]) + b_ref[...]
    lane = lax.broadcasted_iota(I32, lg.shape, 1)
    lane_f = lane.astype(F32)
    big = float(LANES)
    is_g = (lane >= ROUTER_GROUP_LANE) & (lane < ROUTER_GROUP_LANE + N_GROUPS)
    g_max = jnp.max(jnp.where(is_g, lg, -jnp.inf), axis=-1, keepdims=True)
    g_den = jnp.sum(jnp.where(is_g, jnp.exp(lg - g_max), 0.0), axis=-1, keepdims=True)
    g_w = 1.0 / g_den
    g_lane = jnp.min(jnp.where(is_g & (lg == g_max), lane_f, big), axis=-1, keepdims=True)
    g_sel = g_lane.astype(I32) - ROUTER_GROUP_LANE
    is_e = (lane < N_EXPERTS) & (lane // EXPERTS_PER_GROUP == g_sel)
    e_max = jnp.max(jnp.where(is_e, lg, -jnp.inf), axis=-1, keepdims=True)
    pe = jnp.where(is_e, jnp.exp(lg - e_max), 0.0)
    pe = pe / jnp.sum(pe, axis=-1, keepdims=True)
    pe = jnp.where(is_e, pe, -1.0)
    p1 = jnp.max(pe, axis=-1, keepdims=True)
    i1 = jnp.min(jnp.where(pe == p1, lane_f, big), axis=-1, keepdims=True)
    pe2 = jnp.where(lane_f == i1, -1.0, pe)
    p2 = jnp.max(pe2, axis=-1, keepdims=True)
    i2 = jnp.min(jnp.where(pe2 == p2, lane_f, big), axis=-1, keepdims=True)
    tot = p1 + p2
    ids_ref[...] = jnp.where(lane == 0, i1, jnp.where(lane == 1, i2, 0.0)).astype(I32)
    gates_ref[...] = jnp.where(lane == 0, g_w * (p1 / tot), jnp.where(lane == 1, g_w * (p2 / tot), 0.0))


def _router(x1, g, whi, wlo, bias, tm):
    n = x1.shape[0]
    return pl.pallas_call(
        _router_kernel,
        grid=(n // tm,),
        in_specs=[pl.BlockSpec((tm, D_MODEL), lambda i: (i, 0)),
                  pl.BlockSpec((1, D_MODEL), lambda i: (0, 0)),
                  pl.BlockSpec((D_MODEL, LANES), lambda i: (0, 0)),
                  pl.BlockSpec((D_MODEL, LANES), lambda i: (0, 0)),
                  pl.BlockSpec((1, LANES), lambda i: (0, 0))],
        out_specs=[pl.BlockSpec((tm, D_MODEL), lambda i: (i, 0)),
                   pl.BlockSpec((tm, LANES), lambda i: (i, 0)),
                   pl.BlockSpec((tm, LANES), lambda i: (i, 0))],
        out_shape=[jax.ShapeDtypeStruct((n, D_MODEL), F32),
                   jax.ShapeDtypeStruct((n, LANES), I32),
                   jax.ShapeDtypeStruct((n, LANES), F32)],
        compiler_params=_params(("arbitrary",)),
        name="router",
    )(x1, g, whi, wlo, bias)


def _row_copy(src, dst, s_row, d_row, sem):
    return pltpu.make_async_copy(src.at[pl.ds(s_row, 1)], dst.at[pl.ds(d_row, 1)], sem)


def _dispatch_kernel(dest_ref, hn_ref, xs_in_ref, xs_ref, sem, *, ch):
    del xs_in_ref
    i = pl.program_id(0)

    def issue(a, carry):
        tok = (i * ch + a) // 2
        _row_copy(hn_ref, xs_ref, tok, dest_ref[0, 0, a], sem).start()
        return carry

    lax.fori_loop(0, ch, issue, 0)

    def drain(a, carry):
        _row_copy(hn_ref, xs_ref, 0, 0, sem).wait()
        return carry

    lax.fori_loop(0, ch, drain, 0)


def _dispatch(dest, hn, xs0, ch):
    m = dest.shape[0]
    return pl.pallas_call(
        functools.partial(_dispatch_kernel, ch=ch),
        grid=(m // ch,),
        in_specs=[pl.BlockSpec((1, 1, ch), lambda i: (i, 0, 0), memory_space=pltpu.SMEM),
                  pl.BlockSpec(memory_space=pl.ANY),
                  pl.BlockSpec(memory_space=pl.ANY)],
        out_specs=pl.BlockSpec(memory_space=pl.ANY),
        out_shape=jax.ShapeDtypeStruct(xs0.shape, xs0.dtype),
        scratch_shapes=[pltpu.SemaphoreType.DMA(())],
        input_output_aliases={2: 0},
        compiler_params=_params(("arbitrary",), disable_bounds_checks=True, has_side_effects=True),
        name="dispatch",
    )(dest.reshape(m // ch, 1, ch), hn, xs0)


def _expert_kernel(be_ref, nu_ref, x_ref, wg_ref, wu_ref, wd_ref, o_ref, wg_s, wu_s, wd_s):
    b = pl.program_id(0)
    prev = be_ref[jnp.maximum(b - 1, 0)]

    @pl.when(b < nu_ref[0])
    def _():
        @pl.when((b == 0) | (be_ref[b] != prev))
        def _():
            wg_s[...] = wg_ref[...].astype(BF16)
            wu_s[...] = wu_ref[...].astype(BF16)
            wd_s[...] = wd_ref[...].astype(BF16)

        x = x_ref[...].astype(BF16)
        gate = _dot(x, wg_s[...])
        h = gate * _sigmoid(gate) * _dot(x, wu_s[...])
        o_ref[...] = _dot(h.astype(BF16), wd_s[...])

    @pl.when(b >= nu_ref[0])
    def _():
        o_ref[...] = jnp.zeros(o_ref.shape, F32)


def _experts(block_e, n_used, xs, wg, wu, wd, bm):
    rows = xs.shape[0]
    nb = rows // bm
    grid_spec = pltpu.PrefetchScalarGridSpec(
        num_scalar_prefetch=2,
        grid=(nb,),
        in_specs=[pl.BlockSpec((bm, D_MODEL), lambda b, be, nu: (b, 0)),
                  pl.BlockSpec((None, D_MODEL, D_EXPERT), lambda b, be, nu: (be[b], 0, 0)),
                  pl.BlockSpec((None, D_MODEL, D_EXPERT), lambda b, be, nu: (be[b], 0, 0)),
                  pl.BlockSpec((None, D_EXPERT, D_MODEL), lambda b, be, nu: (be[b], 0, 0))],
        out_specs=pl.BlockSpec((bm, D_MODEL), lambda b, be, nu: (b, 0)),
        scratch_shapes=[pltpu.VMEM((D_MODEL, D_EXPERT), BF16), pltpu.VMEM((D_MODEL, D_EXPERT), BF16),
                        pltpu.VMEM((D_EXPERT, D_MODEL), BF16)])
    return pl.pallas_call(
        _expert_kernel,
        grid_spec=grid_spec,
        out_shape=jax.ShapeDtypeStruct((rows, D_MODEL), F32),
        compiler_params=_params(("arbitrary",)),
        name="experts",
    )(block_e, n_used, xs, wg, wu, wd)


def _combine_kernel(dest_ref, gates_ref, x_ref, ys_ref, o_ref, buf, sem, *, tc):
    def issue(a, carry):
        _row_copy(ys_ref, buf.at[a % 2], dest_ref[0, 0, a], a // 2, sem).start()
        return carry

    lax.fori_loop(0, 2 * tc, issue, 0)

    def drain(a, carry):
        _row_copy(ys_ref, buf.at[0], 0, 0, sem).wait()
        return carry

    lax.fori_loop(0, 2 * tc, drain, 0)
    gates = gates_ref[...]
    o_ref[...] = x_ref[...] + gates[:, 0:1] * buf[0] + gates[:, 1:2] * buf[1]


def _combine(dest, gates, x1, ys, tc):
    n = x1.shape[0]
    return pl.pallas_call(
        functools.partial(_combine_kernel, tc=tc),
        grid=(n // tc,),
        in_specs=[pl.BlockSpec((1, 1, 2 * tc), lambda i: (i, 0, 0), memory_space=pltpu.SMEM),
                  pl.BlockSpec((tc, LANES), lambda i: (i, 0)),
                  pl.BlockSpec((tc, D_MODEL), lambda i: (i, 0)),
                  pl.BlockSpec(memory_space=pl.ANY)],
        out_specs=pl.BlockSpec((tc, D_MODEL), lambda i: (i, 0)),
        out_shape=jax.ShapeDtypeStruct((n, D_MODEL), F32),
        scratch_shapes=[pltpu.VMEM((2, tc, D_MODEL), F32), pltpu.SemaphoreType.DMA(())],
        compiler_params=_params(("arbitrary",), disable_bounds_checks=True),
        name="combine",
    )(dest.reshape(n // tc, 1, 2 * tc), gates, x1, ys)


def _moe(x1, p, tm, bm):
    n = x1.shape[0]
    hn, ids, gates = _router(x1, p['norm_ffn'], p['wr_hi'], p['wr_lo'], p['b_router'], tm)
    flat_e = ids[:, :2].reshape(-1)
    m = flat_e.shape[0]
    onehot = (flat_e[:, None] == jnp.arange(N_EXPERTS, dtype=I32)[None, :]).astype(I32)
    csum = jnp.cumsum(onehot, axis=0)
    counts = csum[-1]
    padded = (counts + bm - 1) // bm * bm
    pad_end = jnp.cumsum(padded)
    pad_start = pad_end - padded
    dest = jnp.sum(onehot * (csum - 1 + pad_start[None, :]), axis=1).astype(I32)
    nb = -(-(m + N_EXPERTS * (bm - 1)) // bm)
    block_e = jnp.minimum(jnp.searchsorted(pad_end, jnp.arange(nb, dtype=I32) * bm, side='right'),
                          N_EXPERTS - 1).astype(I32)
    n_used = (pad_end[-1] // bm).astype(I32).reshape(1)
    xs0 = jnp.zeros((nb * bm, D_MODEL), F32)
    xs = _dispatch(dest, hn, xs0, min(m, 1024))
    ys = _experts(block_e, n_used, xs, p['w_gate'], p['w_up'], p['w_down'], bm)
    return _combine(dest, gates, x1, ys, min(n, 128))


def _prep(norm_mix, w_in, s5_a_re, s5_a_im, s5_log_dt, s5_b_re, s5_b_im, s5_c_re, s5_c_im, s5_d,
          w_glu, q_norm, k_norm, idx_k_norm, w_branch_a, w_branch_b, w_out, norm_ffn,
          w_router_group, b_router_group, w_router_expert, b_router_expert, w_gate, w_up, w_down):
    pts = []
    acc = 0
    for s in IN_SIZES[:-1]:
        acc += s
        pts.append(acc)
    w_u, w_q, w_k, w_v, w_qi, w_ki, w_wi, w_ga, w_gb = jnp.split(w_in, pts, axis=1)
    pad = PROJ_COLS - sum(IN_SIZES)
    w_proj = jnp.concatenate([w_u, w_q, w_ga, w_gb, w_qi, w_k, w_v, w_ki, w_wi,
                              jnp.zeros((D_MODEL, pad), w_in.dtype)], axis=1).astype(BF16)
    one = lambda k: jnp.ones((k,), F32)
    gain = jnp.concatenate([one(S5_WIDTH), jnp.tile(q_norm.astype(F32), N_HEADS), one(2 * D_MODEL),
                            one(N_IDX_HEADS * IDX_DIM), jnp.tile(k_norm.astype(F32), N_KV_HEADS), one(KV_WIDTH),
                            idx_k_norm.astype(F32), one(N_IDX_HEADS + pad)]).reshape(1, PROJ_COLS)
    half = 256
    wv, wg = w_glu[:, :S5_WIDTH], w_glu[:, S5_WIDTH:]
    w_glu_p = jnp.concatenate(
        [jnp.concatenate([wv[:, c * half:(c + 1) * half], wg[:, c * half:(c + 1) * half]], axis=1)
         for c in range(S5_WIDTH // half)], axis=1).astype(BF16)
    w_r = jnp.concatenate([w_router_expert.astype(F32), w_router_group.astype(F32),
                           jnp.zeros((D_MODEL, LANES - N_EXPERTS - N_GROUPS), F32)], axis=1)
    wr_hi = w_r.astype(BF16)
    wr_lo = (w_r - wr_hi.astype(F32)).astype(BF16)
    b_r = jnp.concatenate([b_router_expert.astype(F32), b_router_group.astype(F32),
                           jnp.zeros((LANES - N_EXPERTS - N_GROUPS,), F32)]).reshape(1, LANES)
    return dict(norm_mix=norm_mix.astype(F32).reshape(1, D_MODEL), w_proj=w_proj, gain=gain,
                s5=(s5_a_re, s5_a_im, s5_log_dt, s5_b_re, s5_b_im, s5_c_re, s5_c_im),
                s5_d=s5_d.astype(F32).reshape(1, S5_WIDTH), w_glu=w_glu_p,
                w_a=w_branch_a.astype(BF16), w_b=w_branch_b.astype(BF16), w_out=w_out.astype(BF16),
                norm_ffn=norm_ffn.astype(F32).reshape(1, D_MODEL), wr_hi=wr_hi, wr_lo=wr_lo, b_router=b_r,
                w_gate=w_gate, w_up=w_up, w_down=w_down)


def _layer(x, h0_re, h0_im, k_past, v_past, ki_past, p, *, tm, lc, tt, tq, kb, bm):
    bsz, t, _ = x.shape
    n = bsz * t
    x2 = x.reshape(n, D_MODEL)
    proj = _inproj(x2, p['norm_mix'], p['w_proj'], p['gain'], tm)
    proj3 = proj.reshape(bsz, t, PROJ_COLS)
    k = proj3[:, :, COL_K:COL_K + KV_WIDTH]
    v = proj3[:, :, COL_V:COL_V + KV_WIDTH]
    ki = proj3[:, :, COL_KI:COL_KI + IDX_DIM]
    n_state = S5_GROUPS * S5_STATE
    tables = _s5_tables(*p['s5'], lc)
    yg, s_re, s_im = _s5(proj3, h0_re.reshape(bsz, 1, n_state), h0_im.reshape(bsz, 1, n_state),
                         tables, p['s5_d'], lc, tt)
    y_a = _glu(yg.reshape(n, S5_WIDTH), p['w_glu'], tm)
    if k_past is None:
        pos0 = 0
        k_all, v_all, ki_all = k, v, ki
    else:
        pos0 = k_past.shape[1]
        k_all = jnp.concatenate([k_past.reshape(bsz, pos0, KV_WIDTH), k], axis=1)
        v_all = jnp.concatenate([v_past.reshape(bsz, pos0, KV_WIDTH), v], axis=1)
        ki_all = jnp.concatenate([ki_past, ki], axis=1)
    n_keys = k_all.shape[1]
    n_top = min(TOP_K_MAX, n_keys // 4)
    nkp = -(-n_keys // kb) * kb
    padk = lambda a: jnp.pad(a.astype(BF16), ((0, 0), (0, nkp - n_keys), (0, 0)))
    kit = jnp.swapaxes(padk(ki_all), 1, 2)
    kt = jnp.swapaxes(padk(k_all), 1, 2)
    y_b = _dsa(proj3, kit, kt, padk(v_all), tq=tq, kb=kb, pos0=pos0, n_keys=n_keys, n_top=n_top)
    merged = _merge(y_a, y_b.reshape(n, ATT_WIDTH), p['w_a'], p['w_b'], proj, tm)
    x1 = _outproj(merged, p['w_out'], x2, tm)
    y = _moe(x1, p, tm, bm)
    return (y.reshape(bsz, t, D_MODEL), s_re.reshape(bsz, S5_GROUPS, S5_STATE),
            s_im.reshape(bsz, S5_GROUPS, S5_STATE), k.reshape(bsz, t, N_KV_HEADS, HEAD_DIM),
            v.reshape(bsz, t, N_KV_HEADS, HEAD_DIM), ki)


def kernel(x_prompt, x_sample, state_s5_re, state_s5_im, cache_k, cache_v, cache_idx_k, norm_mix, w_in, s5_a_re, s5_a_im, s5_log_dt, s5_b_re, s5_b_im, s5_c_re, s5_c_im, s5_d, w_glu, q_norm, k_norm, idx_k_norm, w_branch_a, w_branch_b, w_out, norm_ffn, w_router_group, b_router_group, w_router_expert, b_router_expert, w_gate, w_up, w_down):
    p = _prep(norm_mix, w_in, s5_a_re, s5_a_im, s5_log_dt, s5_b_re, s5_b_im, s5_c_re, s5_c_im, s5_d,
              w_glu, q_norm, k_norm, idx_k_norm, w_branch_a, w_branch_b, w_out, norm_ffn,
              w_router_group, b_router_group, w_router_expert, b_router_expert, w_gate, w_up, w_down)
    h0 = jnp.zeros((x_prompt.shape[0], S5_GROUPS, S5_STATE), F32)
    yp, srp, sip, kp, vp, kip = _layer(x_prompt, h0, h0, None, None, None, p,
                                       tm=1024, lc=64, tt=1024, tq=128, kb=512, bm=256)
    ys, srs, sis, ks, vs, kis = _layer(x_sample, state_s5_re, state_s5_im, cache_k, cache_v, cache_idx_k, p,
                                       tm=256, lc=32, tt=32, tq=32, kb=384, bm=64)
    return (yp, ys, srp, sip, kp, vp, kip, srs, sis, ks, vs, kis)
```

```python
import functools
import math

import jax
import jax.numpy as jnp
from jax import lax
from jax.experimental import pallas as pl
from jax.experimental.pallas import tpu as pltpu

F32 = jnp.float32
BF16 = jnp.bfloat16
I32 = jnp.int32

D_MODEL = 2048
CHUNK = 64
EPS = 1e-6
S5_WIDTH = 1024
S5_GROUP = 16
S5_GROUPS = 64
S5_STATE = 64
S5_MAX_RE = -1e-4
N_HEADS = 8
N_KV_HEADS = 2
Q_PER_KV = 4
HEAD_DIM = 128
ATT_WIDTH = 1024
KV_WIDTH = 256
N_IDX_HEADS = 8
IDX_DIM = 64
TOP_K_MAX = 256
N_GROUPS = 4
EXPERTS_PER_GROUP = 8
N_EXPERTS = 32
D_EXPERT = 512
IN_SIZES = (S5_WIDTH, ATT_WIDTH, KV_WIDTH, KV_WIDTH, N_IDX_HEADS * IDX_DIM, IDX_DIM, N_IDX_HEADS, D_MODEL, D_MODEL)

LANES = 128
PROJ_TN = 512
COL_U, COL_Q, COL_GA, COL_GB, COL_QI, COL_K, COL_V, COL_KI, COL_WI = 0, 1024, 2048, 4096, 6144, 6656, 6912, 7168, 7232
PROJ_COLS = 7680
S5_LANE_BLOCKS = S5_WIDTH // LANES
S5_BLOCK_STATE = (LANES // S5_GROUP) * S5_STATE
VMEM_LIMIT = 56 * 1024 * 1024
INT_MIN = -2 ** 31
NEG_BIG = -1e30


def _dot(a, b):
    return jnp.dot(a, b, preferred_element_type=F32)


def _split_bf16(x):
    hi = x.astype(BF16)
    lo = (x - hi.astype(F32)).astype(BF16)
    return hi, lo


def _sigmoid(x):
    return 1.0 / (1.0 + jnp.exp(-x))


def _params(sem, **kw):
    return pltpu.CompilerParams(dimension_semantics=sem, vmem_limit_bytes=VMEM_LIMIT, **kw)


def _group_norm(a, gain):
    ms = jnp.mean(a * a, axis=-1, keepdims=True)
    return a * lax.rsqrt(ms + EPS) * gain


def _inproj_kernel(x_ref, g_ref, w_ref, gain_ref, o_ref, xn_ref):
    j = pl.program_id(1)

    @pl.when(j == 0)
    def _():
        xf = x_ref[...]
        ms = jnp.mean(xf * xf, axis=-1, keepdims=True)
        xn_ref[...] = (xf * lax.rsqrt(ms + EPS) * g_ref[...]).astype(BF16)

    acc = _dot(xn_ref[...], w_ref[...])
    gain = gain_ref[...]
    groups = [slice(c * LANES, (c + 1) * LANES) for c in range(PROJ_TN // LANES)]

    @pl.when((j < 2) | (j == 12))
    def _():
        o_ref[...] = acc

    @pl.when((j == 2) | (j == 3))
    def _():
        for s in groups:
            o_ref[:, s] = _group_norm(acc[:, s], gain[:, s])

    @pl.when((j >= 4) & (j < 12))
    def _():
        o_ref[...] = _sigmoid(acc)

    @pl.when(j == 13)
    def _():
        for s in groups[:2]:
            o_ref[:, s] = _group_norm(acc[:, s], gain[:, s])
        o_ref[:, 2 * LANES:] = acc[:, 2 * LANES:]

    @pl.when(j == 14)
    def _():
        a = acc[:, :LANES]
        lane = lax.broadcasted_iota(I32, a.shape, 1)
        is_ki = lane < IDX_DIM
        ms = jnp.sum(jnp.where(is_ki, a * a, 0.0), axis=-1, keepdims=True) * (1.0 / IDX_DIM)
        ki = a * lax.rsqrt(ms + EPS) * gain[:, :LANES]
        wi = a * (N_IDX_HEADS ** -0.5) * (IDX_DIM ** -0.5)
        o_ref[:, :LANES] = jnp.where(is_ki, ki, wi)
        o_ref[:, LANES:] = acc[:, LANES:]


def _inproj(x, g, w, gain, tm):
    n = x.shape[0]
    return pl.pallas_call(
        _inproj_kernel,
        grid=(n // tm, PROJ_COLS // PROJ_TN),
        in_specs=[pl.BlockSpec((tm, D_MODEL), lambda i, j: (i, 0)),
                  pl.BlockSpec((1, D_MODEL), lambda i, j: (0, 0)),
                  pl.BlockSpec((D_MODEL, PROJ_TN), lambda i, j: (0, j)),
                  pl.BlockSpec((1, PROJ_TN), lambda i, j: (0, j))],
        out_specs=pl.BlockSpec((tm, PROJ_TN), lambda i, j: (i, j)),
        out_shape=jax.ShapeDtypeStruct((n, PROJ_COLS), F32),
        scratch_shapes=[pltpu.VMEM((tm, D_MODEL), BF16)],
        compiler_params=_params(("arbitrary", "arbitrary")),
        name="inproj",
    )(x, g, w, gain)


def _gelu_tanh(y):
    return 0.5 * y * (1.0 + jnp.tanh(math.sqrt(2.0 / math.pi) * (y + 0.044715 * (y * y * y))))


def _s5_kernel(u_ref, h0r_ref, h0i_ref, bhi_ref, blo_ref, air_ref, aii_ref, apr_ref, api_ref,
               a1r_ref, a1i_ref, tri_ref, cd_ref, d_ref, yg_ref, sr_ref, si_ref,
               bu_scr, h_scr, hr_scr, hi_scr, *, lc, tt):
    t = pl.program_id(2)
    ns = S5_BLOCK_STATE

    @pl.when(t == 0)
    def _():
        hr_scr[...] = h0r_ref[...]
        hi_scr[...] = h0i_ref[...]

    u = u_ref[...]
    u_hi, u_lo = _split_bf16(u)
    bhi = bhi_ref[...]
    bu_scr[...] = _dot(u_hi, bhi) + _dot(u_lo, bhi) + _dot(u_hi, blo_ref[...])
    tri = tri_ref[...]
    air, aii, apr, api = air_ref[...], aii_ref[...], apr_ref[...], api_ref[...]
    a1r, a1i = a1r_ref[...], a1i_ref[...]

    def chunk(s, carry):
        h_re, h_im = carry
        r0 = pl.multiple_of(s * lc, lc)
        br = bu_scr[pl.ds(r0, lc), 0:ns]
        bi = bu_scr[pl.ds(r0, lc), ns:2 * ns]
        z = jnp.concatenate([air * br - aii * bi, air * bi + aii * br], axis=1)
        z_hi, z_lo = _split_bf16(z)
        c = _dot(tri, z_hi) + _dot(tri, z_lo)
        cr = c[:, 0:ns] + (a1r * h_re - a1i * h_im)
        ci = c[:, ns:2 * ns] + (a1r * h_im + a1i * h_re)
        hr = apr * cr - api * ci
        hi = apr * ci + api * cr
        h_scr[pl.ds(r0, lc), 0:ns] = hr.astype(BF16)
        h_scr[pl.ds(r0, lc), ns:2 * ns] = hi.astype(BF16)
        return hr[lc - 1:lc, :], hi[lc - 1:lc, :]

    h_re, h_im = lax.fori_loop(0, tt // lc, chunk, (hr_scr[...], hi_scr[...]))
    hr_scr[...] = h_re
    hi_scr[...] = h_im
    sr_ref[...] = h_re
    si_ref[...] = h_im
    y = _dot(h_scr[...], cd_ref[...]) + d_ref[...] * u
    yg_ref[...] = _gelu_tanh(y).astype(BF16)


def _s5_tables(a_re, a_im, log_dt, b_re, b_im, c_re, c_im, lc):
    lr = jnp.minimum(a_re.astype(F32), S5_MAX_RE)
    li = a_im.astype(F32)
    dt = jnp.exp(log_dt.astype(F32))[:, None]
    mag = jnp.exp(lr * dt)
    lbr = mag * jnp.cos(li * dt)
    lbi = mag * jnp.sin(li * dt)
    den = lr * lr + li * li
    fr = ((lbr - 1.0) * lr + lbi * li) / den
    fi = (lbi * lr - (lbr - 1.0) * li) / den
    br = b_re.astype(F32)
    bi = b_im.astype(F32)
    bbr = fr[..., None] * br - fi[..., None] * bi
    bbi = fr[..., None] * bi + fi[..., None] * br
    j = jnp.arange(lc, dtype=F32)[:, None, None]
    ang = j * (li * dt)[None]
    lmag = j * (lr * dt)[None]
    n_state = S5_GROUPS * S5_STATE
    apr = (jnp.exp(lmag) * jnp.cos(ang)).reshape(lc, n_state)
    api = (jnp.exp(lmag) * jnp.sin(ang)).reshape(lc, n_state)
    air = (jnp.exp(-lmag) * jnp.cos(ang)).reshape(lc, n_state)
    aii = (-jnp.exp(-lmag) * jnp.sin(ang)).reshape(lc, n_state)
    a1r = lbr.reshape(1, n_state)
    a1i = lbi.reshape(1, n_state)
    gpb = LANES // S5_GROUP
    eye = jnp.eye(gpb, dtype=F32)

    def bdiag(b):
        return jnp.einsum('kgpc,gh->kgchp', b.reshape(S5_LANE_BLOCKS, gpb, S5_STATE, S5_GROUP), eye).reshape(
            S5_LANE_BLOCKS, LANES, S5_BLOCK_STATE)

    def cdiag(c):
        return jnp.einsum('kgcp,gh->kgphc', c.reshape(S5_LANE_BLOCKS, gpb, S5_GROUP, S5_STATE), eye).reshape(
            S5_LANE_BLOCKS, S5_BLOCK_STATE, LANES)

    bd = jnp.concatenate([bdiag(bbr), bdiag(bbi)], axis=-1)
    bd_hi = bd.astype(BF16)
    bd_lo = (bd - bd_hi.astype(F32)).astype(BF16)
    cd = jnp.concatenate([cdiag(c_re.astype(F32)), -cdiag(c_im.astype(F32))], axis=1).astype(BF16)
    tri = jnp.tril(jnp.ones((lc, lc), F32)).astype(BF16)
    return dict(bd_hi=bd_hi, bd_lo=bd_lo, air=air, aii=aii, apr=apr, api=api, a1r=a1r, a1i=a1i, tri=tri, cd=cd)


def _s5(proj3, h0_re, h0_im, tb, dvec, lc, tt):
    bsz, t, _ = proj3.shape
    ns = S5_BLOCK_STATE
    n_state = S5_GROUPS * S5_STATE
    tab = lambda: pl.BlockSpec((lc, ns), lambda b, k, i: (0, k))
    row = lambda: pl.BlockSpec((1, ns), lambda b, k, i: (0, k))
    st = lambda: pl.BlockSpec((None, 1, ns), lambda b, k, i: (b, 0, k))
    return pl.pallas_call(
        functools.partial(_s5_kernel, lc=lc, tt=tt),
        grid=(bsz, S5_LANE_BLOCKS, t // tt),
        in_specs=[pl.BlockSpec((None, tt, LANES), lambda b, k, i: (b, i, k)),
                  st(), st(),
                  pl.BlockSpec((None, LANES, 2 * ns), lambda b, k, i: (k, 0, 0)),
                  pl.BlockSpec((None, LANES, 2 * ns), lambda b, k, i: (k, 0, 0)),
                  tab(), tab(), tab(), tab(), row(), row(),
                  pl.BlockSpec((lc, lc), lambda b, k, i: (0, 0)),
                  pl.BlockSpec((None, 2 * ns, LANES), lambda b, k, i: (k, 0, 0)),
                  pl.BlockSpec((1, LANES), lambda b, k, i: (0, k))],
        out_specs=[pl.BlockSpec((None, tt, LANES), lambda b, k, i: (b, i, k)), st(), st()],
        out_shape=[jax.ShapeDtypeStruct((bsz, t, S5_WIDTH), BF16),
                   jax.ShapeDtypeStruct((bsz, 1, n_state), F32),
                   jax.ShapeDtypeStruct((bsz, 1, n_state), F32)],
        scratch_shapes=[pltpu.VMEM((tt, 2 * ns), F32), pltpu.VMEM((tt, 2 * ns), BF16),
                        pltpu.VMEM((1, ns), F32), pltpu.VMEM((1, ns), F32)],
        compiler_params=_params(("arbitrary", "arbitrary", "arbitrary")),
        name="s5",
    )(proj3, h0_re, h0_im, tb['bd_hi'], tb['bd_lo'], tb['air'], tb['aii'], tb['apr'], tb['api'],
      tb['a1r'], tb['a1i'], tb['tri'], tb['cd'], dvec)


def _glu_kernel(y_ref, w_ref, o_ref):
    acc = _dot(y_ref[...], w_ref[...])
    half = acc.shape[1] // 2
    o_ref[...] = (acc[:, :half] * _sigmoid(acc[:, half:])).astype(BF16)


def _glu(yg, w, tm):
    n = yg.shape[0]
    tn = 512
    return pl.pallas_call(
        _glu_kernel,
        grid=(n // tm, 2 * S5_WIDTH // tn),
        in_specs=[pl.BlockSpec((tm, S5_WIDTH), lambda i, j: (i, 0)),
                  pl.BlockSpec((S5_WIDTH, tn), lambda i, j: (0, j))],
        out_specs=pl.BlockSpec((tm, tn // 2), lambda i, j: (i, j)),
        out_shape=jax.ShapeDtypeStruct((n, S5_WIDTH), BF16),
        compiler_params=_params(("arbitrary", "arbitrary")),
        name="glu",
    )(yg, w)


def _dsa_kernel(q_ref, qi_ref, kiw_ref, kit_ref, kt_ref, v_ref, o_ref,
                key_scr, qs_scr, qis_scr, wis_scr, m_scr, l_scr, acc_scr,
                *, tq, kb, pos0, n_keys, n_top):
    i = pl.program_id(1)
    q_start = pos0 + i * tq
    k_end = jnp.minimum(n_keys, (q_start + tq - 1) // CHUNK * CHUNK + CHUNK)
    nkb = (k_end + kb - 1) // kb

    q = q_ref[...] * (HEAD_DIM ** -0.5)
    for h in range(N_HEADS):
        g, r = divmod(h, Q_PER_KV)
        qs_scr[g, r * tq:(r + 1) * tq, :] = q[:, h * HEAD_DIM:(h + 1) * HEAD_DIM].astype(BF16)
    qi = qi_ref[...]
    kiw = kiw_ref[...]
    for h in range(N_IDX_HEADS):
        qis_scr[h * tq:(h + 1) * tq, :] = qi[:, h * IDX_DIM:(h + 1) * IDX_DIM].astype(BF16)
        wis_scr[h * tq:(h + 1) * tq, :] = kiw[:, IDX_DIM + h:IDX_DIM + h + 1]

    q_chunk = (q_start + lax.broadcasted_iota(I32, (tq, 1), 0)) // CHUNK

    def score_block(b, carry):
        c0 = pl.multiple_of(b * kb, kb)
        s = _dot(qis_scr[...], kit_ref[:, pl.ds(c0, kb)])
        s = jnp.maximum(s, 0.0) * wis_scr[...]
        sc = s[0:tq]
        for h in range(1, N_IDX_HEADS):
            sc = sc + s[h * tq:(h + 1) * tq]
        bits = pltpu.bitcast(sc, I32)
        keys = jnp.where(bits < 0, bits ^ 0x7FFFFFFF, bits)
        k_pos = c0 + lax.broadcasted_iota(I32, (1, kb), 1)
        adm = (k_pos // CHUNK <= q_chunk) & (k_pos < n_keys)
        key_scr[:, pl.ds(c0, kb)] = jnp.where(adm, keys, INT_MIN)
        return carry

    lax.fori_loop(0, nkb, score_block, 0)

    def count_ge(cand):
        def body(b, acc):
            c0 = pl.multiple_of(b * kb, kb)
            hit = jnp.where(key_scr[:, pl.ds(c0, kb)] >= cand, 1.0, 0.0)
            for c in range(kb // LANES):
                acc = acc + hit[:, c * LANES:(c + 1) * LANES]
            return acc
        acc = lax.fori_loop(0, nkb, body, jnp.zeros((tq, LANES), F32))
        return jnp.sum(acc, axis=-1, keepdims=True)

    def bit_pass(it, t_off):
        cand_off = t_off | lax.shift_left(jnp.int32(1), 31 - it)
        cnt = count_ge(cand_off ^ INT_MIN)
        return jnp.where(cnt >= n_top, cand_off, t_off)

    t_off = lax.fori_loop(0, 32, bit_pass, jnp.zeros((tq, 1), I32))
    thr = jnp.maximum(t_off ^ INT_MIN, INT_MIN + 1)

    m_scr[...] = jnp.full(m_scr.shape, NEG_BIG, F32)
    l_scr[...] = jnp.zeros(l_scr.shape, F32)
    acc_scr[...] = jnp.zeros(acc_scr.shape, F32)

    def attend_block(b, carry):
        c0 = pl.multiple_of(b * kb, kb)
        sel = key_scr[:, pl.ds(c0, kb)] >= thr
        sel4 = jnp.concatenate([sel] * Q_PER_KV, axis=0)
        for g in range(N_KV_HEADS):
            lg = _dot(qs_scr[g], kt_ref[g * HEAD_DIM:(g + 1) * HEAD_DIM, pl.ds(c0, kb)])
            lg = jnp.where(sel4, lg, NEG_BIG)
            m_old = m_scr[g]
            m_new = jnp.maximum(m_old, jnp.max(lg, axis=-1, keepdims=True))
            p = jnp.exp(lg - m_new)
            alpha = jnp.exp(m_old - m_new)
            l_scr[g] = alpha * l_scr[g] + jnp.sum(p, axis=-1, keepdims=True)
            pv = _dot(p.astype(BF16), v_ref[pl.ds(c0, kb), g * HEAD_DIM:(g + 1) * HEAD_DIM])
            acc_scr[g] = alpha * acc_scr[g] + pv
            m_scr[g] = m_new
        return carry

    lax.fori_loop(0, nkb, attend_block, 0)

    for h in range(N_HEADS):
        g, r = divmod(h, Q_PER_KV)
        rows = slice(r * tq, (r + 1) * tq)
        o_ref[:, h * HEAD_DIM:(h + 1) * HEAD_DIM] = (acc_scr[g, rows, :] / l_scr[g, rows, :]).astype(BF16)


def _dsa(proj3, kit, kt, v, *, tq, kb, pos0, n_keys, n_top):
    bsz, t, _ = proj3.shape
    nkp = kit.shape[-1]
    kern = functools.partial(_dsa_kernel, tq=tq, kb=kb, pos0=pos0, n_keys=n_keys, n_top=n_top)
    return pl.pallas_call(
        kern,
        grid=(bsz, t // tq),
        in_specs=[pl.BlockSpec((None, tq, ATT_WIDTH), lambda b, i: (b, i, COL_Q // ATT_WIDTH)),
                  pl.BlockSpec((None, tq, 512), lambda b, i: (b, i, COL_QI // 512)),
                  pl.BlockSpec((None, tq, 512), lambda b, i: (b, i, COL_KI // 512)),
                  pl.BlockSpec((None, IDX_DIM, nkp), lambda b, i: (b, 0, 0)),
                  pl.BlockSpec((None, KV_WIDTH, nkp), lambda b, i: (b, 0, 0)),
                  pl.BlockSpec((None, nkp, KV_WIDTH), lambda b, i: (b, 0, 0))],
        out_specs=pl.BlockSpec((None, tq, ATT_WIDTH), lambda b, i: (b, i, 0)),
        out_shape=jax.ShapeDtypeStruct((bsz, t, ATT_WIDTH), BF16),
        scratch_shapes=[pltpu.VMEM((tq, nkp), I32),
                        pltpu.VMEM((N_KV_HEADS, Q_PER_KV * tq, HEAD_DIM), BF16),
                        pltpu.VMEM((N_IDX_HEADS * tq, IDX_DIM), BF16),
                        pltpu.VMEM((N_IDX_HEADS * tq, 1), F32),
                        pltpu.VMEM((N_KV_HEADS, Q_PER_KV * tq, 1), F32),
                        pltpu.VMEM((N_KV_HEADS, Q_PER_KV * tq, 1), F32),
                        pltpu.VMEM((N_KV_HEADS, Q_PER_KV * tq, HEAD_DIM), F32)],
        compiler_params=_params(("arbitrary", "arbitrary")),
        name="dsa",
    )(proj3, proj3, proj3, kit, kt, v)


def _merge_kernel(ya_ref, yb_ref, wa_ref, wb_ref, ga_ref, gb_ref, o_ref):
    o_ref[...] = (ga_ref[...] * _dot(ya_ref[...], wa_ref[...])
                  + gb_ref[...] * _dot(yb_ref[...], wb_ref[...])).astype(BF16)


def _merge(ya, yb, wa, wb, proj, tm):
    n = ya.shape[0]
    tn = 512
    return pl.pallas_call(
        _merge_kernel,
        grid=(n // tm, D_MODEL // tn),
        in_specs=[pl.BlockSpec((tm, S5_WIDTH), lambda i, j: (i, 0)),
                  pl.BlockSpec((tm, ATT_WIDTH), lambda i, j: (i, 0)),
                  pl.BlockSpec((S5_WIDTH, tn), lambda i, j: (0, j)),
                  pl.BlockSpec((ATT_WIDTH, tn), lambda i, j: (0, j)),
                  pl.BlockSpec((tm, tn), lambda i, j: (i, COL_GA // tn + j)),
                  pl.BlockSpec((tm, tn), lambda i, j: (i, COL_GB // tn + j))],
        out_specs=pl.BlockSpec((tm, tn), lambda i, j: (i, j)),
        out_shape=jax.ShapeDtypeStruct((n, D_MODEL), BF16),
        compiler_params=_params(("arbitrary", "arbitrary")),
        name="merge",
    )(ya, yb, wa, wb, proj, proj)


def _outproj_kernel(m_ref, w_ref, x_ref, o_ref):
    o_ref[...] = x_ref[...] + _dot(m_ref[...], w_ref[...])


def _outproj(merged, w, x, tm):
    n = x.shape[0]
    tn = 512
    return pl.pallas_call(
        _outproj_kernel,
        grid=(n // tm, D_MODEL // tn),
        in_specs=[pl.BlockSpec((tm, D_MODEL), lambda i, j: (i, 0)),
                  pl.BlockSpec((D_MODEL, tn), lambda i, j: (0, j)),
                  pl.BlockSpec((tm, tn), lambda i, j: (i, j))],
        out_specs=pl.BlockSpec((tm, tn), lambda i, j: (i, j)),
        out_shape=jax.ShapeDtypeStruct((n, D_MODEL), F32),
        compiler_params=_params(("arbitrary", "arbitrary")),
        name="outproj",
    )(merged, w, x)


ROUTER_GROUP_LANE = N_EXPERTS


def _router_kernel(x_ref, g_ref, whi_ref, wlo_ref, b_ref, hn_ref, ids_ref, gates_ref):
    xf = x_ref[...]
    ms = jnp.mean(xf * xf, axis=-1, keepdims=True)
    hn = xf * lax.rsqrt(ms + EPS) * g_ref[...]
    hn_ref[...] = hn
    h_hi, h_lo = _split_bf16(hn)
    whi = whi_ref[...]
    lg = _dot(h_hi, whi) + _dot(h_lo, whi) + _dot(h_hi, wlo_ref[...]) + b_ref[...]
    lane = lax.broadcasted_iota(I32, lg.shape, 1)
    lane_f = lane.astype(F32)
    big = float(LANES)
    is_g = (lane >= ROUTER_GROUP_LANE) & (lane < ROUTER_GROUP_LANE + N_GROUPS)
    g_max = jnp.max(jnp.where(is_g, lg, -jnp.inf), axis=-1, keepdims=True)
    g_den = jnp.sum(jnp.where(is_g, jnp.exp(lg - g_max), 0.0), axis=-1, keepdims=True)
    g_w = 1.0 / g_den
    g_lane = jnp.min(jnp.where(is_g & (lg == g_max), lane_f, big), axis=-1, keepdims=True)
    g_sel = g_lane.astype(I32) - ROUTER_GROUP_LANE
    is_e = (lane < N_EXPERTS) & (lane // EXPERTS_PER_GROUP == g_sel)
    e_max = jnp.max(jnp.where(is_e, lg, -jnp.inf), axis=-1, keepdims=True)
    pe = jnp.where(is_e, jnp.exp(lg - e_max), 0.0)
    pe = pe / jnp.sum(pe, axis=-1, keepdims=True)
    pe = jnp.where(is_e, pe, -1.0)
    p1 = jnp.max(pe, axis=-1, keepdims=True)
    i1 = jnp.min(jnp.where(pe == p1, lane_f, big), axis=-1, keepdims=True)
    pe2 = jnp.where(lane_f == i1, -1.0, pe)
    p2 = jnp.max(pe2, axis=-1, keepdims=True)
    i2 = jnp.min(jnp.where(pe2 == p2, lane_f, big), axis=-1, keepdims=True)
    tot = p1 + p2
    ids_ref[...] = jnp.where(lane == 0, i1, jnp.where(lane == 1, i2, 0.0)).astype(I32)
    gates_ref[...] = jnp.where(lane == 0, g_w * (p1 / tot), jnp.where(lane == 1, g_w * (p2 / tot), 0.0))


def _router(x1, g, whi, wlo, bias, tm):
    n = x1.shape[0]
    return pl.pallas_call(
        _router_kernel,
        grid=(n // tm,),
        in_specs=[pl.BlockSpec((tm, D_MODEL), lambda i: (i, 0)),
                  pl.BlockSpec((1, D_MODEL), lambda i: (0, 0)),
                  pl.BlockSpec((D_MODEL, LANES), lambda i: (0, 0)),
                  pl.BlockSpec((D_MODEL, LANES), lambda i: (0, 0)),
                  pl.BlockSpec((1, LANES), lambda i: (0, 0))],
        out_specs=[pl.BlockSpec((tm, D_MODEL), lambda i: (i, 0)),
                   pl.BlockSpec((tm, LANES), lambda i: (i, 0)),
                   pl.BlockSpec((tm, LANES), lambda i: (i, 0))],
        out_shape=[jax.ShapeDtypeStruct((n, D_MODEL), F32),
                   jax.ShapeDtypeStruct((n, LANES), I32),
                   jax.ShapeDtypeStruct((n, LANES), F32)],
        compiler_params=_params(("arbitrary",)),
        name="router",
    )(x1, g, whi, wlo, bias)


def _row_copy(src, dst, s_row, d_row, sem):
    return pltpu.make_async_copy(src.at[pl.ds(s_row, 1)], dst.at[pl.ds(d_row, 1)], sem)


def _dispatch_kernel(dest_ref, hn_ref, xs_in_ref, xs_ref, sem, *, ch):
    del xs_in_ref

    def issue(a, carry):
        _row_copy(hn_ref, xs_ref, a // 2, dest_ref[0, 0, a], sem).start()
        return carry

    lax.fori_loop(0, ch, issue, 0)

    def drain(a, carry):
        _row_copy(hn_ref, xs_ref, 0, 0, sem).wait()
        return carry

    lax.fori_loop(0, ch, drain, 0)


def _dispatch(dest, hn, xs0, ch):
    m = dest.shape[0]
    return pl.pallas_call(
        functools.partial(_dispatch_kernel, ch=ch),
        grid=(m // ch,),
        in_specs=[pl.BlockSpec((1, 1, ch), lambda i: (i, 0, 0), memory_space=pltpu.SMEM),
                  pl.BlockSpec((ch // 2, D_MODEL), lambda i: (i, 0)),
                  pl.BlockSpec(memory_space=pl.ANY)],
        out_specs=pl.BlockSpec(memory_space=pl.ANY),
        out_shape=jax.ShapeDtypeStruct(xs0.shape, xs0.dtype),
        scratch_shapes=[pltpu.SemaphoreType.DMA(())],
        input_output_aliases={2: 0},
        compiler_params=_params(("arbitrary",), disable_bounds_checks=True, has_side_effects=True),
        name="dispatch",
    )(dest.reshape(m // ch, 1, ch), hn, xs0)


def _expert_kernel(be_ref, nu_ref, x_ref, wg_ref, wu_ref, wd_ref, o_ref, wg_s, wu_s, wd_s):
    b = pl.program_id(0)
    prev = be_ref[jnp.maximum(b - 1, 0)]

    @pl.when(b < nu_ref[0])
    def _():
        @pl.when((b == 0) | (be_ref[b] != prev))
        def _():
            wg_s[...] = wg_ref[...].astype(BF16)
            wu_s[...] = wu_ref[...].astype(BF16)
            wd_s[...] = wd_ref[...].astype(BF16)

        x = x_ref[...].astype(BF16)
        gate = _dot(x, wg_s[...])
        h = gate * _sigmoid(gate) * _dot(x, wu_s[...])
        o_ref[...] = _dot(h.astype(BF16), wd_s[...])

    @pl.when(b >= nu_ref[0])
    def _():
        o_ref[...] = jnp.zeros(o_ref.shape, F32)


def _experts(block_e, n_used, xs, wg, wu, wd, bm):
    rows = xs.shape[0]
    nb = rows // bm
    grid_spec = pltpu.PrefetchScalarGridSpec(
        num_scalar_prefetch=2,
        grid=(nb,),
        in_specs=[pl.BlockSpec((bm, D_MODEL), lambda b, be, nu: (b, 0)),
                  pl.BlockSpec((None, D_MODEL, D_EXPERT), lambda b, be, nu: (be[b], 0, 0)),
                  pl.BlockSpec((None, D_MODEL, D_EXPERT), lambda b, be, nu: (be[b], 0, 0)),
                  pl.BlockSpec((None, D_EXPERT, D_MODEL), lambda b, be, nu: (be[b], 0, 0))],
        out_specs=pl.BlockSpec((bm, D_MODEL), lambda b, be, nu: (b, 0)),
        scratch_shapes=[pltpu.VMEM((D_MODEL, D_EXPERT), BF16), pltpu.VMEM((D_MODEL, D_EXPERT), BF16),
                        pltpu.VMEM((D_EXPERT, D_MODEL), BF16)])
    return pl.pallas_call(
        _expert_kernel,
        grid_spec=grid_spec,
        out_shape=jax.ShapeDtypeStruct((rows, D_MODEL), F32),
        compiler_params=_params(("arbitrary",)),
        name="experts",
    )(block_e, n_used, xs, wg, wu, wd)


def _combine_kernel(dest_ref, gates_ref, x_ref, ys_ref, o_ref, buf, sem, *, tc):
    def issue(a, carry):
        _row_copy(ys_ref, buf.at[a % 2], dest_ref[0, 0, a], a // 2, sem).start()
        return carry

    lax.fori_loop(0, 2 * tc, issue, 0)

    def drain(a, carry):
        _row_copy(ys_ref, buf.at[0], 0, 0, sem).wait()
        return carry

    lax.fori_loop(0, 2 * tc, drain, 0)
    gates = gates_ref[...]
    o_ref[...] = x_ref[...] + gates[:, 0:1] * buf[0] + gates[:, 1:2] * buf[1]


def _combine(dest, gates, x1, ys, tc):
    n = x1.shape[0]
    return pl.pallas_call(
        functools.partial(_combine_kernel, tc=tc),
        grid=(n // tc,),
        in_specs=[pl.BlockSpec((1, 1, 2 * tc), lambda i: (i, 0, 0), memory_space=pltpu.SMEM),
                  pl.BlockSpec((tc, LANES), lambda i: (i, 0)),
                  pl.BlockSpec((tc, D_MODEL), lambda i: (i, 0)),
                  pl.BlockSpec(memory_space=pl.ANY)],
        out_specs=pl.BlockSpec((tc, D_MODEL), lambda i: (i, 0)),
        out_shape=jax.ShapeDtypeStruct((n, D_MODEL), F32),
        scratch_shapes=[pltpu.VMEM((2, tc, D_MODEL), F32), pltpu.SemaphoreType.DMA(())],
        compiler_params=_params(("arbitrary",), disable_bounds_checks=True),
        name="combine",
    )(dest.reshape(n // tc, 1, 2 * tc), gates, x1, ys)


def _moe(x1, p, tm, bm):
    n = x1.shape[0]
    hn, ids, gates = _router(x1, p['norm_ffn'], p['wr_hi'], p['wr_lo'], p['b_router'], tm)
    flat_e = ids[:, :2].reshape(-1)
    m = flat_e.shape[0]
    onehot = (flat_e[:, None] == jnp.arange(N_EXPERTS, dtype=I32)[None, :]).astype(I32)
    csum = jnp.cumsum(onehot, axis=0)
    counts = csum[-1]
    padded = (counts + bm - 1) // bm * bm
    pad_end = jnp.cumsum(padded)
    pad_start = pad_end - padded
    dest = jnp.sum(onehot * (csum - 1 + pad_start[None, :]), axis=1).astype(I32)
    nb = -(-(m + N_EXPERTS * (bm - 1)) // bm)
    block_e = jnp.minimum(jnp.searchsorted(pad_end, jnp.arange(nb, dtype=I32) * bm, side='right'),
                          N_EXPERTS - 1).astype(I32)
    n_used = (pad_end[-1] // bm).astype(I32).reshape(1)
    xs0 = jnp.zeros((nb * bm, D_MODEL), F32)
    xs = _dispatch(dest, hn, xs0, min(m, 1024))
    ys = _experts(block_e, n_used, xs, p['w_gate'], p['w_up'], p['w_down'], bm)
    return _combine(dest, gates, x1, ys, min(n, 128))


def _prep(norm_mix, w_in, s5_a_re, s5_a_im, s5_log_dt, s5_b_re, s5_b_im, s5_c_re, s5_c_im, s5_d,
          w_glu, q_norm, k_norm, idx_k_norm, w_branch_a, w_branch_b, w_out, norm_ffn,
          w_router_group, b_router_group, w_router_expert, b_router_expert, w_gate, w_up, w_down):
    pts = []
    acc = 0
    for s in IN_SIZES[:-1]:
        acc += s
        pts.append(acc)
    w_u, w_q, w_k, w_v, w_qi, w_ki, w_wi, w_ga, w_gb = jnp.split(w_in, pts, axis=1)
    pad = PROJ_COLS - sum(IN_SIZES)
    w_proj = jnp.concatenate([w_u, w_q, w_ga, w_gb, w_qi, w_k, w_v, w_ki, w_wi,
                              jnp.zeros((D_MODEL, pad), w_in.dtype)], axis=1).astype(BF16)
    one = lambda k: jnp.ones((k,), F32)
    gain = jnp.concatenate([one(S5_WIDTH), jnp.tile(q_norm.astype(F32), N_HEADS), one(2 * D_MODEL),
                            one(N_IDX_HEADS * IDX_DIM), jnp.tile(k_norm.astype(F32), N_KV_HEADS), one(KV_WIDTH),
                            idx_k_norm.astype(F32), one(N_IDX_HEADS + pad)]).reshape(1, PROJ_COLS)
    half = 256
    wv, wg = w_glu[:, :S5_WIDTH], w_glu[:, S5_WIDTH:]
    w_glu_p = jnp.concatenate(
        [jnp.concatenate([wv[:, c * half:(c + 1) * half], wg[:, c * half:(c + 1) * half]], axis=1)
         for c in range(S5_WIDTH // half)], axis=1).astype(BF16)
    w_r = jnp.concatenate([w_router_expert.astype(F32), w_router_group.astype(F32),
                           jnp.zeros((D_MODEL, LANES - N_EXPERTS - N_GROUPS), F32)], axis=1)
    wr_hi = w_r.astype(BF16)
    wr_lo = (w_r - wr_hi.astype(F32)).astype(BF16)
    b_r = jnp.concatenate([b_router_expert.astype(F32), b_router_group.astype(F32),
                           jnp.zeros((LANES - N_EXPERTS - N_GROUPS,), F32)]).reshape(1, LANES)
    return dict(norm_mix=norm_mix.astype(F32).reshape(1, D_MODEL), w_proj=w_proj, gain=gain,
                s5=(s5_a_re, s5_a_im, s5_log_dt, s5_b_re, s5_b_im, s5_c_re, s5_c_im),
                s5_d=s5_d.astype(F32).reshape(1, S5_WIDTH), w_glu=w_glu_p,
                w_a=w_branch_a.astype(BF16), w_b=w_branch_b.astype(BF16), w_out=w_out.astype(BF16),
                norm_ffn=norm_ffn.astype(F32).reshape(1, D_MODEL), wr_hi=wr_hi, wr_lo=wr_lo, b_router=b_r,
                w_gate=w_gate, w_up=w_up, w_down=w_down)


def _layer(x, h0_re, h0_im, k_past, v_past, ki_past, p, *, tm, lc, tt, tq, kb, bm):
    bsz, t, _ = x.shape
    n = bsz * t
    x2 = x.reshape(n, D_MODEL)
    proj = _inproj(x2, p['norm_mix'], p['w_proj'], p['gain'], tm)
    proj3 = proj.reshape(bsz, t, PROJ_COLS)
    k = proj3[:, :, COL_K:COL_K + KV_WIDTH]
    v = proj3[:, :, COL_V:COL_V + KV_WIDTH]
    ki = proj3[:, :, COL_KI:COL_KI + IDX_DIM]
    n_state = S5_GROUPS * S5_STATE
    tables = _s5_tables(*p['s5'], lc)
    yg, s_re, s_im = _s5(proj3, h0_re.reshape(bsz, 1, n_state), h0_im.reshape(bsz, 1, n_state),
                         tables, p['s5_d'], lc, tt)
    y_a = _glu(yg.reshape(n, S5_WIDTH), p['w_glu'], tm)
    if k_past is None:
        pos0 = 0
        k_all, v_all, ki_all = k, v, ki
    else:
        pos0 = k_past.shape[1]
        k_all = jnp.concatenate([k_past.reshape(bsz, pos0, KV_WIDTH), k], axis=1)
        v_all = jnp.concatenate([v_past.reshape(bsz, pos0, KV_WIDTH), v], axis=1)
        ki_all = jnp.concatenate([ki_past, ki], axis=1)
    n_keys = k_all.shape[1]
    n_top = min(TOP_K_MAX, n_keys // 4)
    nkp = -(-n_keys // kb) * kb
    padk = lambda a: jnp.pad(a.astype(BF16), ((0, 0), (0, nkp - n_keys), (0, 0)))
    kit = jnp.swapaxes(padk(ki_all), 1, 2)
    kt = jnp.swapaxes(padk(k_all), 1, 2)
    y_b = _dsa(proj3, kit, kt, padk(v_all), tq=tq, kb=kb, pos0=pos0, n_keys=n_keys, n_top=n_top)
    merged = _merge(y_a, y_b.reshape(n, ATT_WIDTH), p['w_a'], p['w_b'], proj, tm)
    x1 = _outproj(merged, p['w_out'], x2, tm)
    y = _moe(x1, p, tm, bm)
    return (y.reshape(bsz, t, D_MODEL), s_re.reshape(bsz, S5_GROUPS, S5_STATE),
            s_im.reshape(bsz, S5_GROUPS, S5_STATE), k.reshape(bsz, t, N_KV_HEADS, HEAD_DIM),
            v.reshape(bsz, t, N_KV_HEADS, HEAD_DIM), ki)


def kernel(x_prompt, x_sample, state_s5_re, state_s5_im, cache_k, cache_v, cache_idx_k, norm_mix, w_in, s5_a_re, s5_a_im, s5_log_dt, s5_b_re, s5_b_im, s5_c_re, s5_c_im, s5_d, w_glu, q_norm, k_norm, idx_k_norm, w_branch_a, w_branch_b, w_out, norm_ffn, w_router_group, b_router_group, w_router_expert, b_router_expert, w_gate, w_up, w_down):
    p = _prep(norm_mix, w_in, s5_a_re, s5_a_im, s5_log_dt, s5_b_re, s5_b_im, s5_c_re, s5_c_im, s5_d,
              w_glu, q_norm, k_norm, idx_k_norm, w_branch_a, w_branch_b, w_out, norm_ffn,
              w_router_group, b_router_group, w_router_expert, b_router_expert, w_gate, w_up, w_down)
    h0 = jnp.zeros((x_prompt.shape[0], S5_GROUPS, S5_STATE), F32)
    yp, srp, sip, kp, vp, kip = _layer(x_prompt, h0, h0, None, None, None, p,
                                       tm=1024, lc=64, tt=1024, tq=128, kb=512, bm=256)
    ys, srs, sis, ks, vs, kis = _layer(x_sample, state_s5_re, state_s5_im, cache_k, cache_v, cache_idx_k, p,
                                       tm=256, lc=32, tt=32, tq=32, kb=384, bm=64)
    return (yp, ys, srp, sip, kp, vp, kip, srs, sis, ks, vs, kis)
```

```python
import functools
import math

import jax
import jax.numpy as jnp
from jax import lax
from jax.experimental import pallas as pl
from jax.experimental.pallas import tpu as pltpu

F32 = jnp.float32
BF16 = jnp.bfloat16
I32 = jnp.int32

D_MODEL = 2048
CHUNK = 64
EPS = 1e-6
S5_WIDTH = 1024
S5_GROUP = 16
S5_GROUPS = 64
S5_STATE = 64
S5_MAX_RE = -1e-4
N_HEADS = 8
N_KV_HEADS = 2
Q_PER_KV = 4
HEAD_DIM = 128
ATT_WIDTH = 1024
KV_WIDTH = 256
N_IDX_HEADS = 8
IDX_DIM = 64
TOP_K_MAX = 256
N_GROUPS = 4
EXPERTS_PER_GROUP = 8
N_EXPERTS = 32
D_EXPERT = 512
IN_SIZES = (S5_WIDTH, ATT_WIDTH, KV_WIDTH, KV_WIDTH, N_IDX_HEADS * IDX_DIM, IDX_DIM, N_IDX_HEADS, D_MODEL, D_MODEL)

LANES = 128
PROJ_TN = 512
COL_U, COL_Q, COL_GA, COL_GB, COL_QI, COL_K, COL_V, COL_KI, COL_WI = 0, 1024, 2048, 4096, 6144, 6656, 6912, 7168, 7232
PROJ_COLS = 7680
S5_LANE_BLOCKS = S5_WIDTH // LANES
S5_BLOCK_STATE = (LANES // S5_GROUP) * S5_STATE
VMEM_LIMIT = 56 * 1024 * 1024
INT_MIN = -2 ** 31
KEY_LOWEST_FINITE = -2 ** 31 + 0x00800000
NEG_BIG = -1e30
LOG2E = 1.4426950408889634
FAST_SOFTMAX_BOUND = 40.0
TOPK_PER_LANE = 12


def _dot(a, b):
    return jnp.dot(a, b, preferred_element_type=F32)


def _split_bf16(x):
    hi = x.astype(BF16)
    lo = (x - hi.astype(F32)).astype(BF16)
    return hi, lo


def _sigmoid(x):
    return 1.0 / (1.0 + jnp.exp(-x))


def _params(sem, **kw):
    return pltpu.CompilerParams(dimension_semantics=sem, vmem_limit_bytes=VMEM_LIMIT, **kw)


def _group_norm(a, gain):
    ms = jnp.mean(a * a, axis=-1, keepdims=True)
    return a * lax.rsqrt(ms + EPS) * gain


def _inproj_kernel(x_ref, g_ref, w_ref, gain_ref, o_ref, xn_ref):
    j = pl.program_id(1)

    @pl.when(j == 0)
    def _():
        xf = x_ref[...]
        ms = jnp.mean(xf * xf, axis=-1, keepdims=True)
        xn_ref[...] = (xf * lax.rsqrt(ms + EPS) * g_ref[...]).astype(BF16)

    acc = _dot(xn_ref[...], w_ref[...])
    gain = gain_ref[...]
    groups = [slice(c * LANES, (c + 1) * LANES) for c in range(PROJ_TN // LANES)]

    @pl.when((j < 2) | (j == 12))
    def _():
        o_ref[...] = acc

    @pl.when((j == 2) | (j == 3))
    def _():
        for s in groups:
            o_ref[:, s] = _group_norm(acc[:, s], gain[:, s])

    @pl.when((j >= 4) & (j < 12))
    def _():
        o_ref[...] = _sigmoid(acc)

    @pl.when(j == 13)
    def _():
        for s in groups[:2]:
            o_ref[:, s] = _group_norm(acc[:, s], gain[:, s])
        o_ref[:, 2 * LANES:] = acc[:, 2 * LANES:]

    @pl.when(j == 14)
    def _():
        a = acc[:, :LANES]
        lane = lax.broadcasted_iota(I32, a.shape, 1)
        is_ki = lane < IDX_DIM
        ms = jnp.sum(jnp.where(is_ki, a * a, 0.0), axis=-1, keepdims=True) * (1.0 / IDX_DIM)
        ki = a * lax.rsqrt(ms + EPS) * gain[:, :LANES]
        wi = a * (N_IDX_HEADS ** -0.5) * (IDX_DIM ** -0.5)
        o_ref[:, :LANES] = jnp.where(is_ki, ki, wi)
        o_ref[:, LANES:] = acc[:, LANES:]


def _inproj(x, g, w, gain, tm):
    n = x.shape[0]
    return pl.pallas_call(
        _inproj_kernel,
        grid=(n // tm, PROJ_COLS // PROJ_TN),
        in_specs=[pl.BlockSpec((tm, D_MODEL), lambda i, j: (i, 0)),
                  pl.BlockSpec((1, D_MODEL), lambda i, j: (0, 0)),
                  pl.BlockSpec((D_MODEL, PROJ_TN), lambda i, j: (0, j)),
                  pl.BlockSpec((1, PROJ_TN), lambda i, j: (0, j))],
        out_specs=pl.BlockSpec((tm, PROJ_TN), lambda i, j: (i, j)),
        out_shape=jax.ShapeDtypeStruct((n, PROJ_COLS), F32),
        scratch_shapes=[pltpu.VMEM((tm, D_MODEL), BF16)],
        compiler_params=_params(("arbitrary", "arbitrary")),
        name="inproj",
    )(x, g, w, gain)


def _gelu_tanh(y):
    return 0.5 * y * (1.0 + jnp.tanh(math.sqrt(2.0 / math.pi) * (y + 0.044715 * (y * y * y))))


def _s5_kernel(u_ref, h0r_ref, h0i_ref, bhi_ref, blo_ref, air_ref, aii_ref, apr_ref, api_ref,
               a1r_ref, a1i_ref, tri_ref, cd_ref, d_ref, yg_ref, sr_ref, si_ref,
               bu_scr, h_scr, hr_scr, hi_scr, *, lc, tt):
    t = pl.program_id(2)
    ns = S5_BLOCK_STATE

    @pl.when(t == 0)
    def _():
        hr_scr[...] = h0r_ref[...]
        hi_scr[...] = h0i_ref[...]

    u = u_ref[...]
    u_hi, u_lo = _split_bf16(u)
    bhi = bhi_ref[...]
    bu_scr[...] = _dot(u_hi, bhi) + _dot(u_lo, bhi) + _dot(u_hi, blo_ref[...])
    tri = tri_ref[...]
    air, aii, apr, api = air_ref[...], aii_ref[...], apr_ref[...], api_ref[...]
    a1r, a1i = a1r_ref[...], a1i_ref[...]

    def chunk(s, carry):
        h_re, h_im = carry
        r0 = pl.multiple_of(s * lc, lc)
        br = bu_scr[pl.ds(r0, lc), 0:ns]
        bi = bu_scr[pl.ds(r0, lc), ns:2 * ns]
        z = jnp.concatenate([air * br - aii * bi, air * bi + aii * br], axis=1)
        z_hi, z_lo = _split_bf16(z)
        c = _dot(tri, z_hi) + _dot(tri, z_lo)
        cr = c[:, 0:ns] + (a1r * h_re - a1i * h_im)
        ci = c[:, ns:2 * ns] + (a1r * h_im + a1i * h_re)
        hr = apr * cr - api * ci
        hi = apr * ci + api * cr
        h_scr[pl.ds(r0, lc), 0:ns] = hr.astype(BF16)
        h_scr[pl.ds(r0, lc), ns:2 * ns] = hi.astype(BF16)
        return hr[lc - 1:lc, :], hi[lc - 1:lc, :]

    h_re, h_im = lax.fori_loop(0, tt // lc, chunk, (hr_scr[...], hi_scr[...]))
    hr_scr[...] = h_re
    hi_scr[...] = h_im
    sr_ref[...] = h_re
    si_ref[...] = h_im
    y = _dot(h_scr[...], cd_ref[...]) + d_ref[...] * u
    yg_ref[...] = _gelu_tanh(y).astype(BF16)


def _s5_tables(a_re, a_im, log_dt, b_re, b_im, c_re, c_im, lc):
    lr = jnp.minimum(a_re.astype(F32), S5_MAX_RE)
    li = a_im.astype(F32)
    dt = jnp.exp(log_dt.astype(F32))[:, None]
    mag = jnp.exp(lr * dt)
    lbr = mag * jnp.cos(li * dt)
    lbi = mag * jnp.sin(li * dt)
    den = lr * lr + li * li
    fr = ((lbr - 1.0) * lr + lbi * li) / den
    fi = (lbi * lr - (lbr - 1.0) * li) / den
    br = b_re.astype(F32)
    bi = b_im.astype(F32)
    bbr = fr[..., None] * br - fi[..., None] * bi
    bbi = fr[..., None] * bi + fi[..., None] * br
    j = jnp.arange(lc, dtype=F32)[:, None, None]
    ang = j * (li * dt)[None]
    lmag = j * (lr * dt)[None]
    n_state = S5_GROUPS * S5_STATE
    apr = (jnp.exp(lmag) * jnp.cos(ang)).reshape(lc, n_state)
    api = (jnp.exp(lmag) * jnp.sin(ang)).reshape(lc, n_state)
    air = (jnp.exp(-lmag) * jnp.cos(ang)).reshape(lc, n_state)
    aii = (-jnp.exp(-lmag) * jnp.sin(ang)).reshape(lc, n_state)
    a1r = lbr.reshape(1, n_state)
    a1i = lbi.reshape(1, n_state)
    gpb = LANES // S5_GROUP
    eye = jnp.eye(gpb, dtype=F32)

    def bdiag(b):
        return jnp.einsum('kgpc,gh->kgchp', b.reshape(S5_LANE_BLOCKS, gpb, S5_STATE, S5_GROUP), eye).reshape(
            S5_LANE_BLOCKS, LANES, S5_BLOCK_STATE)

    def cdiag(c):
        return jnp.einsum('kgcp,gh->kgphc', c.reshape(S5_LANE_BLOCKS, gpb, S5_GROUP, S5_STATE), eye).reshape(
            S5_LANE_BLOCKS, S5_BLOCK_STATE, LANES)

    bd = jnp.concatenate([bdiag(bbr), bdiag(bbi)], axis=-1)
    bd_hi = bd.astype(BF16)
    bd_lo = (bd - bd_hi.astype(F32)).astype(BF16)
    cd = jnp.concatenate([cdiag(c_re.astype(F32)), -cdiag(c_im.astype(F32))], axis=1).astype(BF16)
    tri = jnp.tril(jnp.ones((lc, lc), F32)).astype(BF16)
    return dict(bd_hi=bd_hi, bd_lo=bd_lo, air=air, aii=aii, apr=apr, api=api, a1r=a1r, a1i=a1i, tri=tri, cd=cd)


def _s5(proj3, h0_re, h0_im, tb, dvec, lc, tt):
    bsz, t, _ = proj3.shape
    ns = S5_BLOCK_STATE
    n_state = S5_GROUPS * S5_STATE
    tab = lambda: pl.BlockSpec((lc, ns), lambda b, k, i: (0, k))
    row = lambda: pl.BlockSpec((1, ns), lambda b, k, i: (0, k))
    st = lambda: pl.BlockSpec((None, 1, ns), lambda b, k, i: (b, 0, k))
    return pl.pallas_call(
        functools.partial(_s5_kernel, lc=lc, tt=tt),
        grid=(bsz, S5_LANE_BLOCKS, t // tt),
        in_specs=[pl.BlockSpec((None, tt, LANES), lambda b, k, i: (b, i, k)),
                  st(), st(),
                  pl.BlockSpec((None, LANES, 2 * ns), lambda b, k, i: (k, 0, 0)),
                  pl.BlockSpec((None, LANES, 2 * ns), lambda b, k, i: (k, 0, 0)),
                  tab(), tab(), tab(), tab(), row(), row(),
                  pl.BlockSpec((lc, lc), lambda b, k, i: (0, 0)),
                  pl.BlockSpec((None, 2 * ns, LANES), lambda b, k, i: (k, 0, 0)),
                  pl.BlockSpec((1, LANES), lambda b, k, i: (0, k))],
        out_specs=[pl.BlockSpec((None, tt, LANES), lambda b, k, i: (b, i, k)), st(), st()],
        out_shape=[jax.ShapeDtypeStruct((bsz, t, S5_WIDTH), BF16),
                   jax.ShapeDtypeStruct((bsz, 1, n_state), F32),
                   jax.ShapeDtypeStruct((bsz, 1, n_state), F32)],
        scratch_shapes=[pltpu.VMEM((tt, 2 * ns), F32), pltpu.VMEM((tt, 2 * ns), BF16),
                        pltpu.VMEM((1, ns), F32), pltpu.VMEM((1, ns), F32)],
        compiler_params=_params(("arbitrary", "arbitrary", "arbitrary")),
        name="s5",
    )(proj3, h0_re, h0_im, tb['bd_hi'], tb['bd_lo'], tb['air'], tb['aii'], tb['apr'], tb['api'],
      tb['a1r'], tb['a1i'], tb['tri'], tb['cd'], dvec)


def _glu_kernel(y_ref, w_ref, o_ref):
    acc = _dot(y_ref[...], w_ref[...])
    half = acc.shape[1] // 2
    o_ref[...] = (acc[:, :half] * _sigmoid(acc[:, half:])).astype(BF16)


def _glu(yg, w, tm):
    n = yg.shape[0]
    tn = 512
    return pl.pallas_call(
        _glu_kernel,
        grid=(n // tm, 2 * S5_WIDTH // tn),
        in_specs=[pl.BlockSpec((tm, S5_WIDTH), lambda i, j: (i, 0)),
                  pl.BlockSpec((S5_WIDTH, tn), lambda i, j: (0, j))],
        out_specs=pl.BlockSpec((tm, tn // 2), lambda i, j: (i, j)),
        out_shape=jax.ShapeDtypeStruct((n, S5_WIDTH), BF16),
        compiler_params=_params(("arbitrary", "arbitrary")),
        name="glu",
    )(yg, w)


def _dsa_kernel(q_ref, qi_ref, kiw_ref, kit_ref, kt_ref, v_ref, o_ref,
                key_scr, qs_scr, qis_scr, wis_scr, m_scr, l_scr, acc_scr,
                kmax_scr, mb_scr, lsum_scr, cand_scr, ckey_scr, thr_scr,
                *, tq, kb, pos0, n_keys, n_top):
    i = pl.program_id(1)
    q_start = pos0 + i * tq
    k_end = jnp.minimum(n_keys, (q_start + tq - 1) // CHUNK * CHUNK + CHUNK)
    nkb = (k_end + kb - 1) // kb
    nkp = key_scr.shape[1]

    @pl.when(i == 0)
    def _():
        def norm_block(b, carry):
            c0 = pl.multiple_of(b * kb, kb)
            out = []
            for g in range(N_KV_HEADS):
                kk = kt_ref[g * HEAD_DIM:(g + 1) * HEAD_DIM, pl.ds(c0, kb)].astype(F32)
                out.append(jnp.maximum(carry[g], jnp.sum(kk * kk, axis=0, keepdims=True)))
            return tuple(out)

        res = lax.fori_loop(0, nkp // kb, norm_block, (jnp.zeros((1, kb), F32),) * N_KV_HEADS)
        for g in range(N_KV_HEADS):
            kmax_scr[g] = jnp.broadcast_to(jnp.max(res[g], axis=-1, keepdims=True), (1, LANES))

    q = q_ref[...] * (HEAD_DIM ** -0.5 * LOG2E)
    for h in range(N_HEADS):
        g, r = divmod(h, Q_PER_KV)
        qs_scr[g, r * tq:(r + 1) * tq, :] = q[:, h * HEAD_DIM:(h + 1) * HEAD_DIM].astype(BF16)
    qi = qi_ref[...]
    kiw = kiw_ref[...]
    for h in range(N_IDX_HEADS):
        qis_scr[h * tq:(h + 1) * tq, :] = qi[:, h * IDX_DIM:(h + 1) * IDX_DIM].astype(BF16)
        wis_scr[h * tq:(h + 1) * tq, :] = kiw[:, IDX_DIM + h:IDX_DIM + h + 1]

    q_chunk = (q_start + lax.broadcasted_iota(I32, (tq, 1), 0)) // CHUNK

    n_cand = cand_scr.shape[0]
    cand_scr[...] = jnp.full(cand_scr.shape, -jnp.inf, F32)

    def to_key(x):
        bits = pltpu.bitcast(x, I32)
        return jnp.where(bits < 0, bits ^ 0x7FFFFFFF, bits)

    def score_block(b, carry):
        c0 = pl.multiple_of(b * kb, kb)
        s = _dot(qis_scr[...], kit_ref[:, pl.ds(c0, kb)])
        s = jnp.maximum(s, 0.0) * wis_scr[...]
        sc = s[0:tq]
        for h in range(1, N_IDX_HEADS):
            sc = sc + s[h * tq:(h + 1) * tq]
        k_pos = c0 + lax.broadcasted_iota(I32, (1, kb), 1)
        adm = (k_pos // CHUNK <= q_chunk) & (k_pos < n_keys)
        key_scr[:, pl.ds(c0, kb)] = jnp.where(adm, sc + 0.0, -jnp.inf)

        def insert(rg, carry2):
            r0 = pl.multiple_of(rg * 8, 8)
            top = [cand_scr[j, pl.ds(r0, 8), :] for j in range(n_cand)]
            for c in range(kb // LANES):
                x = key_scr[pl.ds(r0, 8), pl.ds(c0 + c * LANES, LANES)]
                for j in range(n_cand):
                    hi = jnp.maximum(top[j], x)
                    x = jnp.minimum(top[j], x)
                    top[j] = hi
            for j in range(n_cand):
                cand_scr[j, pl.ds(r0, 8), :] = top[j]
            return carry2

        lax.fori_loop(0, tq // 8, insert, 0)
        return carry

    lax.fori_loop(0, nkb, score_block, 0)

    def count_all(hit_fn):
        def body(b, acc):
            c0 = pl.multiple_of(b * kb, kb)
            hit = hit_fn(key_scr[:, pl.ds(c0, kb)])
            for c in range(kb // LANES):
                acc = acc + hit[:, c * LANES:(c + 1) * LANES]
            return acc
        acc = lax.fori_loop(0, nkb, body, jnp.zeros((tq, LANES), F32))
        return jnp.sum(acc, axis=-1, keepdims=True)

    def count_cand(cand):
        acc = jnp.zeros((tq, LANES), F32)
        for j in range(n_cand):
            acc = acc + jnp.where(ckey_scr[j] >= cand, 1.0, 0.0)
        return jnp.sum(acc, axis=-1, keepdims=True)

    def bisect(count_fn):
        def bit_pass(it, t_off):
            cand_off = t_off | lax.shift_left(jnp.int32(1), 31 - it)
            cnt = count_fn(cand_off ^ INT_MIN)
            return jnp.where(cnt >= n_top, cand_off, t_off)
        t_off = lax.fori_loop(0, 32, bit_pass, jnp.zeros((tq, 1), I32))
        key = jnp.maximum(t_off ^ INT_MIN, KEY_LOWEST_FINITE)
        return pltpu.bitcast(jnp.where(key < 0, key ^ 0x7FFFFFFF, key), F32)

    for j in range(n_cand):
        ckey_scr[j] = to_key(cand_scr[j])
    thr_cand = bisect(count_cand)
    thr_scr[...] = thr_cand
    over = jnp.max(count_all(lambda x: jnp.where(x >= thr_cand, 1.0, 0.0))) > n_top

    @pl.when(over)
    def _():
        thr_scr[...] = bisect(lambda cand: count_all(lambda x: jnp.where(to_key(x) >= cand, 1.0, 0.0)))

    thr = thr_scr[...]

    acc_scr[...] = jnp.zeros(acc_scr.shape, F32)
    bound_max = jnp.float32(0.0)
    for g in range(N_KV_HEADS):
        qg = qs_scr[g].astype(F32)
        qn2 = jnp.sum(qg * qg, axis=-1, keepdims=True)
        bound = jnp.sqrt(qn2 * kmax_scr[g][:, 0:1]) * 1.002
        mb_scr[g] = jnp.broadcast_to(bound, (Q_PER_KV * tq, LANES))
        bound_max = jnp.maximum(bound_max, jnp.max(bound))
    fast = bound_max <= FAST_SOFTMAX_BOUND

    @pl.when(fast)
    def _():
        lsum_scr[...] = jnp.zeros(lsum_scr.shape, F32)

        def attend_block(b, carry):
            c0 = pl.multiple_of(b * kb, kb)
            bias = jnp.where(key_scr[:, pl.ds(c0, kb)] >= thr, 0.0, NEG_BIG)
            for g in range(N_KV_HEADS):
                lg = _dot(qs_scr[g], kt_ref[g * HEAD_DIM:(g + 1) * HEAD_DIM, pl.ds(c0, kb)])
                parts = []
                for r in range(Q_PER_KV):
                    rows = slice(r * tq, (r + 1) * tq)
                    mb = mb_scr[g, rows, :]
                    e = [jnp.exp2((lg[rows, c * LANES:(c + 1) * LANES] - mb) + bias[:, c * LANES:(c + 1) * LANES])
                         for c in range(kb // LANES)]
                    ls = e[0]
                    for c in range(1, kb // LANES):
                        ls = ls + e[c]
                    lsum_scr[g, rows, :] += ls
                    parts.append(jnp.concatenate(e, axis=1).astype(BF16))
                p = jnp.concatenate(parts, axis=0)
                acc_scr[g] += _dot(p, v_ref[pl.ds(c0, kb), g * HEAD_DIM:(g + 1) * HEAD_DIM])
            return carry

        lax.fori_loop(0, nkb, attend_block, 0)
        for h in range(N_HEADS):
            g, r = divmod(h, Q_PER_KV)
            rows = slice(r * tq, (r + 1) * tq)
            l = jnp.sum(lsum_scr[g, rows, :], axis=-1, keepdims=True)
            o_ref[:, h * HEAD_DIM:(h + 1) * HEAD_DIM] = (acc_scr[g, rows, :] / l).astype(BF16)

    @pl.when(jnp.logical_not(fast))
    def _():
        m_scr[...] = jnp.full(m_scr.shape, NEG_BIG, F32)
        l_scr[...] = jnp.zeros(l_scr.shape, F32)

        def attend_block(b, carry):
            c0 = pl.multiple_of(b * kb, kb)
            sel = key_scr[:, pl.ds(c0, kb)] >= thr
            sel4 = jnp.concatenate([sel] * Q_PER_KV, axis=0)
            for g in range(N_KV_HEADS):
                lg = _dot(qs_scr[g], kt_ref[g * HEAD_DIM:(g + 1) * HEAD_DIM, pl.ds(c0, kb)])
                lg = jnp.where(sel4, lg, NEG_BIG)
                m_old = m_scr[g]
                m_new = jnp.maximum(m_old, jnp.max(lg, axis=-1, keepdims=True))
                p = jnp.exp2(lg - m_new)
                alpha = jnp.exp2(m_old - m_new)
                l_scr[g] = alpha * l_scr[g] + jnp.sum(p, axis=-1, keepdims=True)
                pv = _dot(p.astype(BF16), v_ref[pl.ds(c0, kb), g * HEAD_DIM:(g + 1) * HEAD_DIM])
                acc_scr[g] = alpha * acc_scr[g] + pv
                m_scr[g] = m_new
            return carry

        lax.fori_loop(0, nkb, attend_block, 0)
        for h in range(N_HEADS):
            g, r = divmod(h, Q_PER_KV)
            rows = slice(r * tq, (r + 1) * tq)
            o_ref[:, h * HEAD_DIM:(h + 1) * HEAD_DIM] = (acc_scr[g, rows, :] / l_scr[g, rows, :]).astype(BF16)


def _dsa(proj3, kit, kt, v, *, tq, kb, pos0, n_keys, n_top):
    bsz, t, _ = proj3.shape
    nkp = kit.shape[-1]
    kern = functools.partial(_dsa_kernel, tq=tq, kb=kb, pos0=pos0, n_keys=n_keys, n_top=n_top)
    return pl.pallas_call(
        kern,
        grid=(bsz, t // tq),
        in_specs=[pl.BlockSpec((None, tq, ATT_WIDTH), lambda b, i: (b, i, COL_Q // ATT_WIDTH)),
                  pl.BlockSpec((None, tq, 512), lambda b, i: (b, i, COL_QI // 512)),
                  pl.BlockSpec((None, tq, 512), lambda b, i: (b, i, COL_KI // 512)),
                  pl.BlockSpec((None, IDX_DIM, nkp), lambda b, i: (b, 0, 0)),
                  pl.BlockSpec((None, KV_WIDTH, nkp), lambda b, i: (b, 0, 0)),
                  pl.BlockSpec((None, nkp, KV_WIDTH), lambda b, i: (b, 0, 0))],
        out_specs=pl.BlockSpec((None, tq, ATT_WIDTH), lambda b, i: (b, i, 0)),
        out_shape=jax.ShapeDtypeStruct((bsz, t, ATT_WIDTH), BF16),
        scratch_shapes=[pltpu.VMEM((tq, nkp), F32),
                        pltpu.VMEM((N_KV_HEADS, Q_PER_KV * tq, HEAD_DIM), BF16),
                        pltpu.VMEM((N_IDX_HEADS * tq, IDX_DIM), BF16),
                        pltpu.VMEM((N_IDX_HEADS * tq, 1), F32),
                        pltpu.VMEM((N_KV_HEADS, Q_PER_KV * tq, 1), F32),
                        pltpu.VMEM((N_KV_HEADS, Q_PER_KV * tq, 1), F32),
                        pltpu.VMEM((N_KV_HEADS, Q_PER_KV * tq, HEAD_DIM), F32),
                        pltpu.VMEM((N_KV_HEADS, 1, LANES), F32),
                        pltpu.VMEM((N_KV_HEADS, Q_PER_KV * tq, LANES), F32),
                        pltpu.VMEM((N_KV_HEADS, Q_PER_KV * tq, LANES), F32),
                        pltpu.VMEM((TOPK_PER_LANE, tq, LANES), F32),
                        pltpu.VMEM((TOPK_PER_LANE, tq, LANES), I32),
                        pltpu.VMEM((tq, 1), F32)],
        compiler_params=_params(("arbitrary", "arbitrary")),
        name="dsa",
    )(proj3, proj3, proj3, kit, kt, v)


def _merge_kernel(ya_ref, yb_ref, wa_ref, wb_ref, ga_ref, gb_ref, o_ref):
    o_ref[...] = (ga_ref[...] * _dot(ya_ref[...], wa_ref[...])
                  + gb_ref[...] * _dot(yb_ref[...], wb_ref[...])).astype(BF16)


def _merge(ya, yb, wa, wb, proj, tm):
    n = ya.shape[0]
    tn = 512
    return pl.pallas_call(
        _merge_kernel,
        grid=(n // tm, D_MODEL // tn),
        in_specs=[pl.BlockSpec((tm, S5_WIDTH), lambda i, j: (i, 0)),
                  pl.BlockSpec((tm, ATT_WIDTH), lambda i, j: (i, 0)),
                  pl.BlockSpec((S5_WIDTH, tn), lambda i, j: (0, j)),
                  pl.BlockSpec((ATT_WIDTH, tn), lambda i, j: (0, j)),
                  pl.BlockSpec((tm, tn), lambda i, j: (i, COL_GA // tn + j)),
                  pl.BlockSpec((tm, tn), lambda i, j: (i, COL_GB // tn + j))],
        out_specs=pl.BlockSpec((tm, tn), lambda i, j: (i, j)),
        out_shape=jax.ShapeDtypeStruct((n, D_MODEL), BF16),
        compiler_params=_params(("arbitrary", "arbitrary")),
        name="merge",
    )(ya, yb, wa, wb, proj, proj)


def _outproj_kernel(m_ref, w_ref, x_ref, o_ref):
    o_ref[...] = x_ref[...] + _dot(m_ref[...], w_ref[...])


def _outproj(merged, w, x, tm):
    n = x.shape[0]
    tn = 512
    return pl.pallas_call(
        _outproj_kernel,
        grid=(n // tm, D_MODEL // tn),
        in_specs=[pl.BlockSpec((tm, D_MODEL), lambda i, j: (i, 0)),
                  pl.BlockSpec((D_MODEL, tn), lambda i, j: (0, j)),
                  pl.BlockSpec((tm, tn), lambda i, j: (i, j))],
        out_specs=pl.BlockSpec((tm, tn), lambda i, j: (i, j)),
        out_shape=jax.ShapeDtypeStruct((n, D_MODEL), F32),
        compiler_params=_params(("arbitrary", "arbitrary")),
        name="outproj",
    )(merged, w, x)


ROUTER_GROUP_LANE = N_EXPERTS


def _router_kernel(x_ref, g_ref, whi_ref, wlo_ref, b_ref, hn_ref, ids_ref, gates_ref):
    xf = x_ref[...]
    ms = jnp.mean(xf * xf, axis=-1, keepdims=True)
    hn = xf * lax.rsqrt(ms + EPS) * g_ref[...]
    hn_ref[...] = hn
    h_hi, h_lo = _split_bf16(hn)
    whi = whi_ref[...]
    lg = _dot(h_hi, whi) + _dot(h_lo, whi) + _dot(h_hi, wlo_ref[...]) + b_ref[...]
    lane = lax.broadcasted_iota(I32, lg.shape, 1)
    lane_f = lane.astype(F32)
    big = float(LANES)
    is_g = (lane >= ROUTER_GROUP_LANE) & (lane < ROUTER_GROUP_LANE + N_GROUPS)
    g_max = jnp.max(jnp.where(is_g, lg, -jnp.inf), axis=-1, keepdims=True)
    g_den = jnp.sum(jnp.where(is_g, jnp.exp(lg - g_max), 0.0), axis=-1, keepdims=True)
    g_w = 1.0 / g_den
    g_lane = jnp.min(jnp.where(is_g & (lg == g_max), lane_f, big), axis=-1, keepdims=True)
    g_sel = g_lane.astype(I32) - ROUTER_GROUP_LANE
    is_e = (lane < N_EXPERTS) & (lane // EXPERTS_PER_GROUP == g_sel)
    e_max = jnp.max(jnp.where(is_e, lg, -jnp.inf), axis=-1, keepdims=True)
    pe = jnp.where(is_e, jnp.exp(lg - e_max), 0.0)
    pe = pe / jnp.sum(pe, axis=-1, keepdims=True)
    pe = jnp.where(is_e, pe, -1.0)
    p1 = jnp.max(pe, axis=-1, keepdims=True)
    i1 = jnp.min(jnp.where(pe == p1, lane_f, big), axis=-1, keepdims=True)
    pe2 = jnp.where(lane_f == i1, -1.0, pe)
    p2 = jnp.max(pe2, axis=-1, keepdims=True)
    i2 = jnp.min(jnp.where(pe2 == p2, lane_f, big), axis=-1, keepdims=True)
    tot = p1 + p2
    ids_ref[...] = jnp.where(lane == 0, i1, jnp.where(lane == 1, i2, 0.0)).astype(I32)
    gates_ref[...] = jnp.where(lane == 0, g_w * (p1 / tot), jnp.where(lane == 1, g_w * (p2 / tot), 0.0))


def _router(x1, g, whi, wlo, bias, tm):
    n = x1.shape[0]
    return pl.pallas_call(
        _router_kernel,
        grid=(n // tm,),
        in_specs=[pl.BlockSpec((tm, D_MODEL), lambda i: (i, 0)),
                  pl.BlockSpec((1, D_MODEL), lambda i: (0, 0)),
                  pl.BlockSpec((D_MODEL, LANES), lambda i: (0, 0)),
                  pl.BlockSpec((D_MODEL, LANES), lambda i: (0, 0)),
                  pl.BlockSpec((1, LANES), lambda i: (0, 0))],
        out_specs=[pl.BlockSpec((tm, D_MODEL), lambda i: (i, 0)),
                   pl.BlockSpec((tm, LANES), lambda i: (i, 0)),
                   pl.BlockSpec((tm, LANES), lambda i: (i, 0))],
        out_shape=[jax.ShapeDtypeStruct((n, D_MODEL), F32),
                   jax.ShapeDtypeStruct((n, LANES), I32),
                   jax.ShapeDtypeStruct((n, LANES), F32)],
        compiler_params=_params(("arbitrary",)),
        name="router",
    )(x1, g, whi, wlo, bias)


def _row_copy(src, dst, s_row, d_row, sem):
    return pltpu.make_async_copy(src.at[pl.ds(s_row, 1)], dst.at[pl.ds(d_row, 1)], sem)


def _dispatch_kernel(dest_ref, hn_ref, xs_in_ref, xs_ref, sem, *, ch):
    del xs_in_ref

    def issue(a, carry):
        _row_copy(hn_ref, xs_ref, a // 2, dest_ref[0, 0, a], sem).start()
        return carry

    lax.fori_loop(0, ch, issue, 0)

    def drain(a, carry):
        _row_copy(hn_ref, xs_ref, 0, 0, sem).wait()
        return carry

    lax.fori_loop(0, ch, drain, 0)


def _dispatch(dest, hn, xs0, ch):
    m = dest.shape[0]
    return pl.pallas_call(
        functools.partial(_dispatch_kernel, ch=ch),
        grid=(m // ch,),
        in_specs=[pl.BlockSpec((1, 1, ch), lambda i: (i, 0, 0), memory_space=pltpu.SMEM),
                  pl.BlockSpec((ch // 2, D_MODEL), lambda i: (i, 0)),
                  pl.BlockSpec(memory_space=pl.ANY)],
        out_specs=pl.BlockSpec(memory_space=pl.ANY),
        out_shape=jax.ShapeDtypeStruct(xs0.shape, xs0.dtype),
        scratch_shapes=[pltpu.SemaphoreType.DMA(())],
        input_output_aliases={2: 0},
        compiler_params=_params(("arbitrary",), disable_bounds_checks=True, has_side_effects=True),
        name="dispatch",
    )(dest.reshape(m // ch, 1, ch), hn, xs0)


def _expert_kernel(be_ref, nu_ref, x_ref, wg_ref, wu_ref, wd_ref, o_ref, wg_s, wu_s, wd_s):
    b = pl.program_id(0)
    prev = be_ref[jnp.maximum(b - 1, 0)]

    @pl.when(b < nu_ref[0])
    def _():
        @pl.when((b == 0) | (be_ref[b] != prev))
        def _():
            wg_s[...] = wg_ref[...].astype(BF16)
            wu_s[...] = wu_ref[...].astype(BF16)
            wd_s[...] = wd_ref[...].astype(BF16)

        x = x_ref[...].astype(BF16)
        gate = _dot(x, wg_s[...])
        h = gate * _sigmoid(gate) * _dot(x, wu_s[...])
        o_ref[...] = _dot(h.astype(BF16), wd_s[...])

    @pl.when(b >= nu_ref[0])
    def _():
        o_ref[...] = jnp.zeros(o_ref.shape, F32)


def _experts(block_e, n_used, xs, wg, wu, wd, bm):
    rows = xs.shape[0]
    nb = rows // bm
    grid_spec = pltpu.PrefetchScalarGridSpec(
        num_scalar_prefetch=2,
        grid=(nb,),
        in_specs=[pl.BlockSpec((bm, D_MODEL), lambda b, be, nu: (b, 0)),
                  pl.BlockSpec((None, D_MODEL, D_EXPERT), lambda b, be, nu: (be[b], 0, 0)),
                  pl.BlockSpec((None, D_MODEL, D_EXPERT), lambda b, be, nu: (be[b], 0, 0)),
                  pl.BlockSpec((None, D_EXPERT, D_MODEL), lambda b, be, nu: (be[b], 0, 0))],
        out_specs=pl.BlockSpec((bm, D_MODEL), lambda b, be, nu: (b, 0)),
        scratch_shapes=[pltpu.VMEM((D_MODEL, D_EXPERT), BF16), pltpu.VMEM((D_MODEL, D_EXPERT), BF16),
                        pltpu.VMEM((D_EXPERT, D_MODEL), BF16)])
    return pl.pallas_call(
        _expert_kernel,
        grid_spec=grid_spec,
        out_shape=jax.ShapeDtypeStruct((rows, D_MODEL), F32),
        compiler_params=_params(("arbitrary",)),
        name="experts",
    )(block_e, n_used, xs, wg, wu, wd)


def _combine_kernel(dest_ref, gates_ref, x_ref, ys_ref, o_ref, buf, sem, *, tc):
    def issue(a, carry):
        _row_copy(ys_ref, buf.at[a % 2], dest_ref[0, 0, a], a // 2, sem).start()
        return carry

    lax.fori_loop(0, 2 * tc, issue, 0)

    def drain(a, carry):
        _row_copy(ys_ref, buf.at[0], 0, 0, sem).wait()
        return carry

    lax.fori_loop(0, 2 * tc, drain, 0)
    gates = gates_ref[...]
    o_ref[...] = x_ref[...] + gates[:, 0:1] * buf[0] + gates[:, 1:2] * buf[1]


def _combine(dest, gates, x1, ys, tc):
    n = x1.shape[0]
    return pl.pallas_call(
        functools.partial(_combine_kernel, tc=tc),
        grid=(n // tc,),
        in_specs=[pl.BlockSpec((1, 1, 2 * tc), lambda i: (i, 0, 0), memory_space=pltpu.SMEM),
                  pl.BlockSpec((tc, LANES), lambda i: (i, 0)),
                  pl.BlockSpec((tc, D_MODEL), lambda i: (i, 0)),
                  pl.BlockSpec(memory_space=pl.ANY)],
        out_specs=pl.BlockSpec((tc, D_MODEL), lambda i: (i, 0)),
        out_shape=jax.ShapeDtypeStruct((n, D_MODEL), F32),
        scratch_shapes=[pltpu.VMEM((2, tc, D_MODEL), F32), pltpu.SemaphoreType.DMA(())],
        compiler_params=_params(("arbitrary",), disable_bounds_checks=True),
        name="combine",
    )(dest.reshape(n // tc, 1, 2 * tc), gates, x1, ys)


def _moe(x1, p, tm, bm):
    n = x1.shape[0]
    hn, ids, gates = _router(x1, p['norm_ffn'], p['wr_hi'], p['wr_lo'], p['b_router'], tm)
    flat_e = ids[:, :2].reshape(-1)
    m = flat_e.shape[0]
    onehot = (flat_e[:, None] == jnp.arange(N_EXPERTS, dtype=I32)[None, :]).astype(I32)
    csum = jnp.cumsum(onehot, axis=0)
    counts = csum[-1]
    padded = (counts + bm - 1) // bm * bm
    pad_end = jnp.cumsum(padded)
    pad_start = pad_end - padded
    dest = jnp.sum(onehot * (csum - 1 + pad_start[None, :]), axis=1).astype(I32)
    nb = -(-(m + N_EXPERTS * (bm - 1)) // bm)
    block_start = jnp.arange(nb, dtype=I32) * bm
    block_e = jnp.minimum(jnp.sum((pad_end[None, :] <= block_start[:, None]).astype(I32), axis=1),
                          N_EXPERTS - 1).astype(I32)
    n_used = (pad_end[-1] // bm).astype(I32).reshape(1)
    xs0 = jnp.zeros((nb * bm, D_MODEL), F32)
    xs = _dispatch(dest, hn, xs0, min(m, 1024))
    ys = _experts(block_e, n_used, xs, p['w_gate'], p['w_up'], p['w_down'], bm)
    return _combine(dest, gates, x1, ys, min(n, 128))


def _prep(norm_mix, w_in, s5_a_re, s5_a_im, s5_log_dt, s5_b_re, s5_b_im, s5_c_re, s5_c_im, s5_d,
          w_glu, q_norm, k_norm, idx_k_norm, w_branch_a, w_branch_b, w_out, norm_ffn,
          w_router_group, b_router_group, w_router_expert, b_router_expert, w_gate, w_up, w_down):
    pts = []
    acc = 0
    for s in IN_SIZES[:-1]:
        acc += s
        pts.append(acc)
    w_u, w_q, w_k, w_v, w_qi, w_ki, w_wi, w_ga, w_gb = jnp.split(w_in, pts, axis=1)
    pad = PROJ_COLS - sum(IN_SIZES)
    w_proj = jnp.concatenate([w_u, w_q, w_ga, w_gb, w_qi, w_k, w_v, w_ki, w_wi,
                              jnp.zeros((D_MODEL, pad), w_in.dtype)], axis=1).astype(BF16)
    one = lambda k: jnp.ones((k,), F32)
    gain = jnp.concatenate([one(S5_WIDTH), jnp.tile(q_norm.astype(F32), N_HEADS), one(2 * D_MODEL),
                            one(N_IDX_HEADS * IDX_DIM), jnp.tile(k_norm.astype(F32), N_KV_HEADS), one(KV_WIDTH),
                            idx_k_norm.astype(F32), one(N_IDX_HEADS + pad)]).reshape(1, PROJ_COLS)
    half = 256
    wv, wg = w_glu[:, :S5_WIDTH], w_glu[:, S5_WIDTH:]
    w_glu_p = jnp.concatenate(
        [jnp.concatenate([wv[:, c * half:(c + 1) * half], wg[:, c * half:(c + 1) * half]], axis=1)
         for c in range(S5_WIDTH // half)], axis=1).astype(BF16)
    w_r = jnp.concatenate([w_router_expert.astype(F32), w_router_group.astype(F32),
                           jnp.zeros((D_MODEL, LANES - N_EXPERTS - N_GROUPS), F32)], axis=1)
    wr_hi = w_r.astype(BF16)
    wr_lo = (w_r - wr_hi.astype(F32)).astype(BF16)
    b_r = jnp.concatenate([b_router_expert.astype(F32), b_router_group.astype(F32),
                           jnp.zeros((LANES - N_EXPERTS - N_GROUPS,), F32)]).reshape(1, LANES)
    return dict(norm_mix=norm_mix.astype(F32).reshape(1, D_MODEL), w_proj=w_proj, gain=gain,
                s5=(s5_a_re, s5_a_im, s5_log_dt, s5_b_re, s5_b_im, s5_c_re, s5_c_im),
                s5_d=s5_d.astype(F32).reshape(1, S5_WIDTH), w_glu=w_glu_p,
                w_a=w_branch_a.astype(BF16), w_b=w_branch_b.astype(BF16), w_out=w_out.astype(BF16),
                norm_ffn=norm_ffn.astype(F32).reshape(1, D_MODEL), wr_hi=wr_hi, wr_lo=wr_lo, b_router=b_r,
                w_gate=w_gate, w_up=w_up, w_down=w_down)


def _layer(x, h0_re, h0_im, k_past, v_past, ki_past, p, *, tm, lc, tt, tq, kb, bm):
    bsz, t, _ = x.shape
    n = bsz * t
    x2 = x.reshape(n, D_MODEL)
    proj = _inproj(x2, p['norm_mix'], p['w_proj'], p['gain'], tm)
    proj3 = proj.reshape(bsz, t, PROJ_COLS)
    k = proj3[:, :, COL_K:COL_K + KV_WIDTH]
    v = proj3[:, :, COL_V:COL_V + KV_WIDTH]
    ki = proj3[:, :, COL_KI:COL_KI + IDX_DIM]
    n_state = S5_GROUPS * S5_STATE
    tables = _s5_tables(*p['s5'], lc)
    yg, s_re, s_im = _s5(proj3, h0_re.reshape(bsz, 1, n_state), h0_im.reshape(bsz, 1, n_state),
                         tables, p['s5_d'], lc, tt)
    y_a = _glu(yg.reshape(n, S5_WIDTH), p['w_glu'], tm)
    if k_past is None:
        pos0 = 0
        k_all, v_all, ki_all = k, v, ki
    else:
        pos0 = k_past.shape[1]
        k_all = jnp.concatenate([k_past.reshape(bsz, pos0, KV_WIDTH), k], axis=1)
        v_all = jnp.concatenate([v_past.reshape(bsz, pos0, KV_WIDTH), v], axis=1)
        ki_all = jnp.concatenate([ki_past, ki], axis=1)
    n_keys = k_all.shape[1]
    n_top = min(TOP_K_MAX, n_keys // 4)
    nkp = -(-n_keys // kb) * kb
    padk = lambda a: jnp.pad(a.astype(BF16), ((0, 0), (0, nkp - n_keys), (0, 0)))
    kit = jnp.swapaxes(padk(ki_all), 1, 2)
    kt = jnp.swapaxes(padk(k_all), 1, 2)
    y_b = _dsa(proj3, kit, kt, padk(v_all), tq=tq, kb=kb, pos0=pos0, n_keys=n_keys, n_top=n_top)
    merged = _merge(y_a, y_b.reshape(n, ATT_WIDTH), p['w_a'], p['w_b'], proj, tm)
    x1 = _outproj(merged, p['w_out'], x2, tm)
    y = _moe(x1, p, tm, bm)
    return (y.reshape(bsz, t, D_MODEL), s_re.reshape(bsz, S5_GROUPS, S5_STATE),
            s_im.reshape(bsz, S5_GROUPS, S5_STATE), k.reshape(bsz, t, N_KV_HEADS, HEAD_DIM),
            v.reshape(bsz, t, N_KV_HEADS, HEAD_DIM), ki)


def kernel(x_prompt, x_sample, state_s5_re, state_s5_im, cache_k, cache_v, cache_idx_k, norm_mix, w_in, s5_a_re, s5_a_im, s5_log_dt, s5_b_re, s5_b_im, s5_c_re, s5_c_im, s5_d, w_glu, q_norm, k_norm, idx_k_norm, w_branch_a, w_branch_b, w_out, norm_ffn, w_router_group, b_router_group, w_router_expert, b_router_expert, w_gate, w_up, w_down):
    p = _prep(norm_mix, w_in, s5_a_re, s5_a_im, s5_log_dt, s5_b_re, s5_b_im, s5_c_re, s5_c_im, s5_d,
              w_glu, q_norm, k_norm, idx_k_norm, w_branch_a, w_branch_b, w_out, norm_ffn,
              w_router_group, b_router_group, w_router_expert, b_router_expert, w_gate, w_up, w_down)
    h0 = jnp.zeros((x_prompt.shape[0], S5_GROUPS, S5_STATE), F32)
    yp, srp, sip, kp, vp, kip = _layer(x_prompt, h0, h0, None, None, None, p,
                                       tm=1024, lc=64, tt=1024, tq=128, kb=512, bm=256)
    ys, srs, sis, ks, vs, kis = _layer(x_sample, state_s5_re, state_s5_im, cache_k, cache_v, cache_idx_k, p,
                                       tm=256, lc=32, tt=32, tq=32, kb=384, bm=64)
    return (yp, ys, srp, sip, kp, vp, kip, srs, sis, ks, vs, kis)
```

```python
import functools
import math

import jax
import jax.numpy as jnp
from jax import lax
from jax.experimental import pallas as pl
from jax.experimental.pallas import tpu as pltpu

F32 = jnp.float32
BF16 = jnp.bfloat16
I32 = jnp.int32

D_MODEL = 2048
CHUNK = 64
EPS = 1e-6
S5_WIDTH = 1024
S5_GROUP = 16
S5_GROUPS = 64
S5_STATE = 64
S5_MAX_RE = -1e-4
N_HEADS = 8
N_KV_HEADS = 2
Q_PER_KV = 4
HEAD_DIM = 128
ATT_WIDTH = 1024
KV_WIDTH = 256
N_IDX_HEADS = 8
IDX_DIM = 64
TOP_K_MAX = 256
N_GROUPS = 4
EXPERTS_PER_GROUP = 8
N_EXPERTS = 32
D_EXPERT = 512
IN_SIZES = (S5_WIDTH, ATT_WIDTH, KV_WIDTH, KV_WIDTH, N_IDX_HEADS * IDX_DIM, IDX_DIM, N_IDX_HEADS, D_MODEL, D_MODEL)

LANES = 128
PROJ_TN = 512
COL_U, COL_Q, COL_GA, COL_GB, COL_QI, COL_K, COL_V, COL_KI, COL_WI = 0, 1024, 2048, 4096, 6144, 6656, 6912, 7168, 7232
PROJ_COLS = 7680
S5_LANE_BLOCKS = S5_WIDTH // LANES
S5_BLOCK_STATE = (LANES // S5_GROUP) * S5_STATE
VMEM_LIMIT = 56 * 1024 * 1024
INT_MIN = -2 ** 31
KEY_LOWEST_FINITE = -2 ** 31 + 0x00800000
NEG_BIG = -1e30
LOG2E = 1.4426950408889634
FAST_SOFTMAX_BOUND = 40.0
TOPK_PER_LANE = 12


def _dot(a, b):
    return jnp.dot(a, b, preferred_element_type=F32)


def _split_bf16(x):
    hi = x.astype(BF16)
    lo = (x - hi.astype(F32)).astype(BF16)
    return hi, lo


def _sigmoid(x):
    return 1.0 / (1.0 + jnp.exp(-x))


def _params(sem, **kw):
    return pltpu.CompilerParams(dimension_semantics=sem, vmem_limit_bytes=VMEM_LIMIT, **kw)


def _group_norm(a, gain):
    ms = jnp.mean(a * a, axis=-1, keepdims=True)
    return a * lax.rsqrt(ms + EPS) * gain


def _inproj_kernel(x_ref, g_ref, w_ref, gain_ref, o_ref, xn_ref):
    j = pl.program_id(1)

    @pl.when(j == 0)
    def _():
        xf = x_ref[...]
        ms = jnp.mean(xf * xf, axis=-1, keepdims=True)
        xn_ref[...] = (xf * lax.rsqrt(ms + EPS) * g_ref[...]).astype(BF16)

    acc = _dot(xn_ref[...], w_ref[...])
    gain = gain_ref[...]
    groups = [slice(c * LANES, (c + 1) * LANES) for c in range(PROJ_TN // LANES)]

    @pl.when((j < 2) | (j == 12))
    def _():
        o_ref[...] = acc

    @pl.when((j == 2) | (j == 3))
    def _():
        for s in groups:
            o_ref[:, s] = _group_norm(acc[:, s], gain[:, s])

    @pl.when((j >= 4) & (j < 12))
    def _():
        o_ref[...] = _sigmoid(acc)

    @pl.when(j == 13)
    def _():
        for s in groups[:2]:
            o_ref[:, s] = _group_norm(acc[:, s], gain[:, s])
        o_ref[:, 2 * LANES:] = acc[:, 2 * LANES:]

    @pl.when(j == 14)
    def _():
        a = acc[:, :LANES]
        lane = lax.broadcasted_iota(I32, a.shape, 1)
        is_ki = lane < IDX_DIM
        ms = jnp.sum(jnp.where(is_ki, a * a, 0.0), axis=-1, keepdims=True) * (1.0 / IDX_DIM)
        ki = a * lax.rsqrt(ms + EPS) * gain[:, :LANES]
        wi = a * (N_IDX_HEADS ** -0.5) * (IDX_DIM ** -0.5)
        o_ref[:, :LANES] = jnp.where(is_ki, ki, wi)
        o_ref[:, LANES:] = acc[:, LANES:]


def _inproj(x, g, w, gain, tm):
    n = x.shape[0]
    return pl.pallas_call(
        _inproj_kernel,
        grid=(n // tm, PROJ_COLS // PROJ_TN),
        in_specs=[pl.BlockSpec((tm, D_MODEL), lambda i, j: (i, 0)),
                  pl.BlockSpec((1, D_MODEL), lambda i, j: (0, 0)),
                  pl.BlockSpec((D_MODEL, PROJ_TN), lambda i, j: (0, j)),
                  pl.BlockSpec((1, PROJ_TN), lambda i, j: (0, j))],
        out_specs=pl.BlockSpec((tm, PROJ_TN), lambda i, j: (i, j)),
        out_shape=jax.ShapeDtypeStruct((n, PROJ_COLS), F32),
        scratch_shapes=[pltpu.VMEM((tm, D_MODEL), BF16)],
        compiler_params=_params(("arbitrary", "arbitrary")),
        name="inproj",
    )(x, g, w, gain)


def _gelu_tanh(y):
    return 0.5 * y * (1.0 + jnp.tanh(math.sqrt(2.0 / math.pi) * (y + 0.044715 * (y * y * y))))


def _s5_kernel(u_ref, h0r_ref, h0i_ref, bhi_ref, blo_ref, air_ref, aii_ref, apr_ref, api_ref,
               a1r_ref, a1i_ref, tri_ref, cd_ref, d_ref, yg_ref, sr_ref, si_ref,
               bu_scr, h_scr, hr_scr, hi_scr, *, lc, tt):
    t = pl.program_id(2)
    ns = S5_BLOCK_STATE

    @pl.when(t == 0)
    def _():
        hr_scr[...] = h0r_ref[...]
        hi_scr[...] = h0i_ref[...]

    u = u_ref[...]
    u_hi, u_lo = _split_bf16(u)
    bhi = bhi_ref[...]
    bu_scr[...] = _dot(u_hi, bhi) + _dot(u_lo, bhi) + _dot(u_hi, blo_ref[...])
    tri = tri_ref[...]
    air, aii, apr, api = air_ref[...], aii_ref[...], apr_ref[...], api_ref[...]
    a1r, a1i = a1r_ref[...], a1i_ref[...]

    def chunk(s, carry):
        h_re, h_im = carry
        r0 = pl.multiple_of(s * lc, lc)
        br = bu_scr[pl.ds(r0, lc), 0:ns]
        bi = bu_scr[pl.ds(r0, lc), ns:2 * ns]
        z = jnp.concatenate([air * br - aii * bi, air * bi + aii * br], axis=1)
        z_hi, z_lo = _split_bf16(z)
        c = _dot(tri, z_hi) + _dot(tri, z_lo)
        cr = c[:, 0:ns] + (a1r * h_re - a1i * h_im)
        ci = c[:, ns:2 * ns] + (a1r * h_im + a1i * h_re)
        hr = apr * cr - api * ci
        hi = apr * ci + api * cr
        h_scr[pl.ds(r0, lc), 0:ns] = hr.astype(BF16)
        h_scr[pl.ds(r0, lc), ns:2 * ns] = hi.astype(BF16)
        return hr[lc - 1:lc, :], hi[lc - 1:lc, :]

    h_re, h_im = lax.fori_loop(0, tt // lc, chunk, (hr_scr[...], hi_scr[...]), unroll=min(4, tt // lc))
    hr_scr[...] = h_re
    hi_scr[...] = h_im
    sr_ref[...] = h_re
    si_ref[...] = h_im
    y = _dot(h_scr[...], cd_ref[...]) + d_ref[...] * u
    yg_ref[...] = _gelu_tanh(y).astype(BF16)


def _s5_tables(a_re, a_im, log_dt, b_re, b_im, c_re, c_im, lc):
    lr = jnp.minimum(a_re.astype(F32), S5_MAX_RE)
    li = a_im.astype(F32)
    dt = jnp.exp(log_dt.astype(F32))[:, None]
    mag = jnp.exp(lr * dt)
    lbr = mag * jnp.cos(li * dt)
    lbi = mag * jnp.sin(li * dt)
    den = lr * lr + li * li
    fr = ((lbr - 1.0) * lr + lbi * li) / den
    fi = (lbi * lr - (lbr - 1.0) * li) / den
    br = b_re.astype(F32)
    bi = b_im.astype(F32)
    bbr = fr[..., None] * br - fi[..., None] * bi
    bbi = fr[..., None] * bi + fi[..., None] * br
    j = jnp.arange(lc, dtype=F32)[:, None, None]
    ang = j * (li * dt)[None]
    lmag = j * (lr * dt)[None]
    n_state = S5_GROUPS * S5_STATE
    apr = (jnp.exp(lmag) * jnp.cos(ang)).reshape(lc, n_state)
    api = (jnp.exp(lmag) * jnp.sin(ang)).reshape(lc, n_state)
    air = (jnp.exp(-lmag) * jnp.cos(ang)).reshape(lc, n_state)
    aii = (-jnp.exp(-lmag) * jnp.sin(ang)).reshape(lc, n_state)
    a1r = lbr.reshape(1, n_state)
    a1i = lbi.reshape(1, n_state)
    gpb = LANES // S5_GROUP
    eye = jnp.eye(gpb, dtype=F32)

    def bdiag(b):
        return jnp.einsum('kgpc,gh->kgchp', b.reshape(S5_LANE_BLOCKS, gpb, S5_STATE, S5_GROUP), eye).reshape(
            S5_LANE_BLOCKS, LANES, S5_BLOCK_STATE)

    def cdiag(c):
        return jnp.einsum('kgcp,gh->kgphc', c.reshape(S5_LANE_BLOCKS, gpb, S5_GROUP, S5_STATE), eye).reshape(
            S5_LANE_BLOCKS, S5_BLOCK_STATE, LANES)

    bd = jnp.concatenate([bdiag(bbr), bdiag(bbi)], axis=-1)
    bd_hi = bd.astype(BF16)
    bd_lo = (bd - bd_hi.astype(F32)).astype(BF16)
    cd = jnp.concatenate([cdiag(c_re.astype(F32)), -cdiag(c_im.astype(F32))], axis=1).astype(BF16)
    tri = jnp.tril(jnp.ones((lc, lc), F32)).astype(BF16)
    return dict(bd_hi=bd_hi, bd_lo=bd_lo, air=air, aii=aii, apr=apr, api=api, a1r=a1r, a1i=a1i, tri=tri, cd=cd)


def _s5(proj3, h0_re, h0_im, tb, dvec, lc, tt):
    bsz, t, _ = proj3.shape
    ns = S5_BLOCK_STATE
    n_state = S5_GROUPS * S5_STATE
    tab = lambda: pl.BlockSpec((lc, ns), lambda b, k, i: (0, k))
    row = lambda: pl.BlockSpec((1, ns), lambda b, k, i: (0, k))
    st = lambda: pl.BlockSpec((None, 1, ns), lambda b, k, i: (b, 0, k))
    return pl.pallas_call(
        functools.partial(_s5_kernel, lc=lc, tt=tt),
        grid=(bsz, S5_LANE_BLOCKS, t // tt),
        in_specs=[pl.BlockSpec((None, tt, LANES), lambda b, k, i: (b, i, k)),
                  st(), st(),
                  pl.BlockSpec((None, LANES, 2 * ns), lambda b, k, i: (k, 0, 0)),
                  pl.BlockSpec((None, LANES, 2 * ns), lambda b, k, i: (k, 0, 0)),
                  tab(), tab(), tab(), tab(), row(), row(),
                  pl.BlockSpec((lc, lc), lambda b, k, i: (0, 0)),
                  pl.BlockSpec((None, 2 * ns, LANES), lambda b, k, i: (k, 0, 0)),
                  pl.BlockSpec((1, LANES), lambda b, k, i: (0, k))],
        out_specs=[pl.BlockSpec((None, tt, LANES), lambda b, k, i: (b, i, k)), st(), st()],
        out_shape=[jax.ShapeDtypeStruct((bsz, t, S5_WIDTH), BF16),
                   jax.ShapeDtypeStruct((bsz, 1, n_state), F32),
                   jax.ShapeDtypeStruct((bsz, 1, n_state), F32)],
        scratch_shapes=[pltpu.VMEM((tt, 2 * ns), F32), pltpu.VMEM((tt, 2 * ns), BF16),
                        pltpu.VMEM((1, ns), F32), pltpu.VMEM((1, ns), F32)],
        compiler_params=_params(("arbitrary", "arbitrary", "arbitrary")),
        name="s5",
    )(proj3, h0_re, h0_im, tb['bd_hi'], tb['bd_lo'], tb['air'], tb['aii'], tb['apr'], tb['api'],
      tb['a1r'], tb['a1i'], tb['tri'], tb['cd'], dvec)


def _glu_kernel(y_ref, w_ref, o_ref):
    acc = _dot(y_ref[...], w_ref[...])
    half = acc.shape[1] // 2
    o_ref[...] = (acc[:, :half] * _sigmoid(acc[:, half:])).astype(BF16)


def _glu(yg, w, tm):
    n = yg.shape[0]
    tn = 512
    return pl.pallas_call(
        _glu_kernel,
        grid=(n // tm, 2 * S5_WIDTH // tn),
        in_specs=[pl.BlockSpec((tm, S5_WIDTH), lambda i, j: (i, 0)),
                  pl.BlockSpec((S5_WIDTH, tn), lambda i, j: (0, j))],
        out_specs=pl.BlockSpec((tm, tn // 2), lambda i, j: (i, j)),
        out_shape=jax.ShapeDtypeStruct((n, S5_WIDTH), BF16),
        compiler_params=_params(("arbitrary", "arbitrary")),
        name="glu",
    )(yg, w)


def _dsa_kernel(q_ref, qi_ref, kiw_ref, kit_ref, kt_ref, v_ref, o_ref,
                key_scr, qs_scr, qis_scr, wis_scr, m_scr, l_scr, acc_scr,
                kmax_scr, mb_scr, lsum_scr, cand_scr, ckey_scr, thr_scr,
                *, tq, kb, pos0, n_keys, n_top):
    i = pl.program_id(1)
    q_start = pos0 + i * tq
    k_end = jnp.minimum(n_keys, (q_start + tq - 1) // CHUNK * CHUNK + CHUNK)
    nkb = (k_end + kb - 1) // kb
    nkp = key_scr.shape[1]

    @pl.when(i == 0)
    def _():
        def norm_block(b, carry):
            c0 = pl.multiple_of(b * kb, kb)
            out = []
            for g in range(N_KV_HEADS):
                kk = kt_ref[g * HEAD_DIM:(g + 1) * HEAD_DIM, pl.ds(c0, kb)].astype(F32)
                out.append(jnp.maximum(carry[g], jnp.sum(kk * kk, axis=0, keepdims=True)))
            return tuple(out)

        res = lax.fori_loop(0, nkp // kb, norm_block, (jnp.zeros((1, kb), F32),) * N_KV_HEADS)
        for g in range(N_KV_HEADS):
            kmax_scr[g] = jnp.broadcast_to(jnp.max(res[g], axis=-1, keepdims=True), (1, LANES))

    q = q_ref[...] * (HEAD_DIM ** -0.5 * LOG2E)
    for h in range(N_HEADS):
        g, r = divmod(h, Q_PER_KV)
        qs_scr[g, r * tq:(r + 1) * tq, :] = q[:, h * HEAD_DIM:(h + 1) * HEAD_DIM].astype(BF16)
    qi = qi_ref[...]
    kiw = kiw_ref[...]
    for h in range(N_IDX_HEADS):
        qis_scr[h * tq:(h + 1) * tq, :] = qi[:, h * IDX_DIM:(h + 1) * IDX_DIM].astype(BF16)
        wis_scr[h * tq:(h + 1) * tq, :] = kiw[:, IDX_DIM + h:IDX_DIM + h + 1]

    q_chunk = (q_start + lax.broadcasted_iota(I32, (tq, 1), 0)) // CHUNK

    n_cand = cand_scr.shape[0]
    cand_scr[...] = jnp.full(cand_scr.shape, -jnp.inf, F32)

    def to_key(x):
        bits = pltpu.bitcast(x, I32)
        return jnp.where(bits < 0, bits ^ 0x7FFFFFFF, bits)

    def score_block(b, carry):
        c0 = pl.multiple_of(b * kb, kb)
        s = _dot(qis_scr[...], kit_ref[:, pl.ds(c0, kb)])
        s = jnp.maximum(s, 0.0) * wis_scr[...]
        sc = s[0:tq]
        for h in range(1, N_IDX_HEADS):
            sc = sc + s[h * tq:(h + 1) * tq]
        k_pos = c0 + lax.broadcasted_iota(I32, (1, kb), 1)
        adm = (k_pos // CHUNK <= q_chunk) & (k_pos < n_keys)
        key_scr[:, pl.ds(c0, kb)] = jnp.where(adm, sc + 0.0, -jnp.inf)

        def insert(rg, carry2):
            r0 = pl.multiple_of(rg * 8, 8)
            top = [cand_scr[j, pl.ds(r0, 8), :] for j in range(n_cand)]
            for c in range(kb // LANES):
                x = key_scr[pl.ds(r0, 8), pl.ds(c0 + c * LANES, LANES)]
                for j in range(n_cand):
                    hi = jnp.maximum(top[j], x)
                    x = jnp.minimum(top[j], x)
                    top[j] = hi
            for j in range(n_cand):
                cand_scr[j, pl.ds(r0, 8), :] = top[j]
            return carry2

        lax.fori_loop(0, tq // 8, insert, 0, unroll=2)
        return carry

    lax.fori_loop(0, nkb, score_block, 0)

    def count_all(hit_fn):
        def body(b, acc):
            c0 = pl.multiple_of(b * kb, kb)
            hit = hit_fn(key_scr[:, pl.ds(c0, kb)])
            for c in range(kb // LANES):
                acc = acc + hit[:, c * LANES:(c + 1) * LANES]
            return acc
        acc = lax.fori_loop(0, nkb, body, jnp.zeros((tq, LANES), F32))
        return jnp.sum(acc, axis=-1, keepdims=True)

    def count_cand(cand):
        acc = jnp.zeros((tq, LANES), F32)
        for j in range(n_cand):
            acc = acc + jnp.where(ckey_scr[j] >= cand, 1.0, 0.0)
        return jnp.sum(acc, axis=-1, keepdims=True)

    def bisect(count_fn):
        def bit_pass(it, t_off):
            cand_off = t_off | lax.shift_left(jnp.int32(1), 31 - it)
            cnt = count_fn(cand_off ^ INT_MIN)
            return jnp.where(cnt >= n_top, cand_off, t_off)
        t_off = lax.fori_loop(0, 32, bit_pass, jnp.zeros((tq, 1), I32))
        key = jnp.maximum(t_off ^ INT_MIN, KEY_LOWEST_FINITE)
        return pltpu.bitcast(jnp.where(key < 0, key ^ 0x7FFFFFFF, key), F32)

    for j in range(n_cand):
        ckey_scr[j] = to_key(cand_scr[j])
    thr_cand = bisect(count_cand)
    thr_scr[...] = thr_cand
    over = jnp.max(count_all(lambda x: jnp.where(x >= thr_cand, 1.0, 0.0))) > n_top

    @pl.when(over)
    def _():
        thr_scr[...] = bisect(lambda cand: count_all(lambda x: jnp.where(to_key(x) >= cand, 1.0, 0.0)))

    thr = thr_scr[...]

    acc_scr[...] = jnp.zeros(acc_scr.shape, F32)
    bound_max = jnp.float32(0.0)
    for g in range(N_KV_HEADS):
        qg = qs_scr[g].astype(F32)
        qn2 = jnp.sum(qg * qg, axis=-1, keepdims=True)
        bound = jnp.sqrt(qn2 * kmax_scr[g][:, 0:1]) * 1.002
        mb_scr[g] = jnp.broadcast_to(bound, (Q_PER_KV * tq, LANES))
        bound_max = jnp.maximum(bound_max, jnp.max(bound))
    fast = bound_max <= FAST_SOFTMAX_BOUND

    @pl.when(fast)
    def _():
        lsum_scr[...] = jnp.zeros(lsum_scr.shape, F32)

        def attend_block(b, carry):
            c0 = pl.multiple_of(b * kb, kb)
            bias = jnp.where(key_scr[:, pl.ds(c0, kb)] >= thr, 0.0, NEG_BIG)
            for g in range(N_KV_HEADS):
                lg = _dot(qs_scr[g], kt_ref[g * HEAD_DIM:(g + 1) * HEAD_DIM, pl.ds(c0, kb)])
                parts = []
                for r in range(Q_PER_KV):
                    rows = slice(r * tq, (r + 1) * tq)
                    mb = mb_scr[g, rows, :]
                    e = [jnp.exp2((lg[rows, c * LANES:(c + 1) * LANES] - mb) + bias[:, c * LANES:(c + 1) * LANES])
                         for c in range(kb // LANES)]
                    ls = e[0]
                    for c in range(1, kb // LANES):
                        ls = ls + e[c]
                    lsum_scr[g, rows, :] += ls
                    parts.append(jnp.concatenate(e, axis=1).astype(BF16))
                p = jnp.concatenate(parts, axis=0)
                acc_scr[g] += _dot(p, v_ref[pl.ds(c0, kb), g * HEAD_DIM:(g + 1) * HEAD_DIM])
            return carry

        lax.fori_loop(0, nkb, attend_block, 0)
        for h in range(N_HEADS):
            g, r = divmod(h, Q_PER_KV)
            rows = slice(r * tq, (r + 1) * tq)
            l = jnp.sum(lsum_scr[g, rows, :], axis=-1, keepdims=True)
            o_ref[:, h * HEAD_DIM:(h + 1) * HEAD_DIM] = (acc_scr[g, rows, :] / l).astype(BF16)

    @pl.when(jnp.logical_not(fast))
    def _():
        m_scr[...] = jnp.full(m_scr.shape, NEG_BIG, F32)
        l_scr[...] = jnp.zeros(l_scr.shape, F32)

        def attend_block(b, carry):
            c0 = pl.multiple_of(b * kb, kb)
            sel = key_scr[:, pl.ds(c0, kb)] >= thr
            sel4 = jnp.concatenate([sel] * Q_PER_KV, axis=0)
            for g in range(N_KV_HEADS):
                lg = _dot(qs_scr[g], kt_ref[g * HEAD_DIM:(g + 1) * HEAD_DIM, pl.ds(c0, kb)])
                lg = jnp.where(sel4, lg, NEG_BIG)
                m_old = m_scr[g]
                m_new = jnp.maximum(m_old, jnp.max(lg, axis=-1, keepdims=True))
                p = jnp.exp2(lg - m_new)
                alpha = jnp.exp2(m_old - m_new)
                l_scr[g] = alpha * l_scr[g] + jnp.sum(p, axis=-1, keepdims=True)
                pv = _dot(p.astype(BF16), v_ref[pl.ds(c0, kb), g * HEAD_DIM:(g + 1) * HEAD_DIM])
                acc_scr[g] = alpha * acc_scr[g] + pv
                m_scr[g] = m_new
            return carry

        lax.fori_loop(0, nkb, attend_block, 0)
        for h in range(N_HEADS):
            g, r = divmod(h, Q_PER_KV)
            rows = slice(r * tq, (r + 1) * tq)
            o_ref[:, h * HEAD_DIM:(h + 1) * HEAD_DIM] = (acc_scr[g, rows, :] / l_scr[g, rows, :]).astype(BF16)


def _dsa(proj3, kit, kt, v, *, tq, kb, pos0, n_keys, n_top):
    bsz, t, _ = proj3.shape
    nkp = kit.shape[-1]
    kern = functools.partial(_dsa_kernel, tq=tq, kb=kb, pos0=pos0, n_keys=n_keys, n_top=n_top)
    return pl.pallas_call(
        kern,
        grid=(bsz, t // tq),
        in_specs=[pl.BlockSpec((None, tq, ATT_WIDTH), lambda b, i: (b, i, COL_Q // ATT_WIDTH)),
                  pl.BlockSpec((None, tq, 512), lambda b, i: (b, i, COL_QI // 512)),
                  pl.BlockSpec((None, tq, 512), lambda b, i: (b, i, COL_KI // 512)),
                  pl.BlockSpec((None, IDX_DIM, nkp), lambda b, i: (b, 0, 0)),
                  pl.BlockSpec((None, KV_WIDTH, nkp), lambda b, i: (b, 0, 0)),
                  pl.BlockSpec((None, nkp, KV_WIDTH), lambda b, i: (b, 0, 0))],
        out_specs=pl.BlockSpec((None, tq, ATT_WIDTH), lambda b, i: (b, i, 0)),
        out_shape=jax.ShapeDtypeStruct((bsz, t, ATT_WIDTH), BF16),
        scratch_shapes=[pltpu.VMEM((tq, nkp), F32),
                        pltpu.VMEM((N_KV_HEADS, Q_PER_KV * tq, HEAD_DIM), BF16),
                        pltpu.VMEM((N_IDX_HEADS * tq, IDX_DIM), BF16),
                        pltpu.VMEM((N_IDX_HEADS * tq, 1), F32),
                        pltpu.VMEM((N_KV_HEADS, Q_PER_KV * tq, 1), F32),
                        pltpu.VMEM((N_KV_HEADS, Q_PER_KV * tq, 1), F32),
                        pltpu.VMEM((N_KV_HEADS, Q_PER_KV * tq, HEAD_DIM), F32),
                        pltpu.VMEM((N_KV_HEADS, 1, LANES), F32),
                        pltpu.VMEM((N_KV_HEADS, Q_PER_KV * tq, LANES), F32),
                        pltpu.VMEM((N_KV_HEADS, Q_PER_KV * tq, LANES), F32),
                        pltpu.VMEM((TOPK_PER_LANE, tq, LANES), F32),
                        pltpu.VMEM((TOPK_PER_LANE, tq, LANES), I32),
                        pltpu.VMEM((tq, 1), F32)],
        compiler_params=_params(("arbitrary", "arbitrary")),
        name="dsa",
    )(proj3, proj3, proj3, kit, kt, v)


def _merge_kernel(ya_ref, yb_ref, wa_ref, wb_ref, ga_ref, gb_ref, o_ref):
    o_ref[...] = (ga_ref[...] * _dot(ya_ref[...], wa_ref[...])
                  + gb_ref[...] * _dot(yb_ref[...], wb_ref[...])).astype(BF16)


def _merge(ya, yb, wa, wb, proj, tm):
    n = ya.shape[0]
    tn = 512
    return pl.pallas_call(
        _merge_kernel,
        grid=(n // tm, D_MODEL // tn),
        in_specs=[pl.BlockSpec((tm, S5_WIDTH), lambda i, j: (i, 0)),
                  pl.BlockSpec((tm, ATT_WIDTH), lambda i, j: (i, 0)),
                  pl.BlockSpec((S5_WIDTH, tn), lambda i, j: (0, j)),
                  pl.BlockSpec((ATT_WIDTH, tn), lambda i, j: (0, j)),
                  pl.BlockSpec((tm, tn), lambda i, j: (i, COL_GA // tn + j)),
                  pl.BlockSpec((tm, tn), lambda i, j: (i, COL_GB // tn + j))],
        out_specs=pl.BlockSpec((tm, tn), lambda i, j: (i, j)),
        out_shape=jax.ShapeDtypeStruct((n, D_MODEL), BF16),
        compiler_params=_params(("arbitrary", "arbitrary")),
        name="merge",
    )(ya, yb, wa, wb, proj, proj)


def _outproj_kernel(m_ref, w_ref, x_ref, o_ref):
    o_ref[...] = x_ref[...] + _dot(m_ref[...], w_ref[...])


def _outproj(merged, w, x, tm):
    n = x.shape[0]
    tn = 512
    return pl.pallas_call(
        _outproj_kernel,
        grid=(n // tm, D_MODEL // tn),
        in_specs=[pl.BlockSpec((tm, D_MODEL), lambda i, j: (i, 0)),
                  pl.BlockSpec((D_MODEL, tn), lambda i, j: (0, j)),
                  pl.BlockSpec((tm, tn), lambda i, j: (i, j))],
        out_specs=pl.BlockSpec((tm, tn), lambda i, j: (i, j)),
        out_shape=jax.ShapeDtypeStruct((n, D_MODEL), F32),
        compiler_params=_params(("arbitrary", "arbitrary")),
        name="outproj",
    )(merged, w, x)


ROUTER_GROUP_LANE = N_EXPERTS


def _router_kernel(x_ref, g_ref, whi_ref, wlo_ref, b_ref, hn_ref, ids_ref, gates_ref):
    xf = x_ref[...]
    ms = jnp.mean(xf * xf, axis=-1, keepdims=True)
    hn = xf * lax.rsqrt(ms + EPS) * g_ref[...]
    hn_ref[...] = hn
    h_hi, h_lo = _split_bf16(hn)
    whi = whi_ref[...]
    lg = _dot(h_hi, whi) + _dot(h_lo, whi) + _dot(h_hi, wlo_ref[...]) + b_ref[...]
    lane = lax.broadcasted_iota(I32, lg.shape, 1)
    lane_f = lane.astype(F32)
    big = float(LANES)
    is_g = (lane >= ROUTER_GROUP_LANE) & (lane < ROUTER_GROUP_LANE + N_GROUPS)
    g_max = jnp.max(jnp.where(is_g, lg, -jnp.inf), axis=-1, keepdims=True)
    g_den = jnp.sum(jnp.where(is_g, jnp.exp(lg - g_max), 0.0), axis=-1, keepdims=True)
    g_w = 1.0 / g_den
    g_lane = jnp.min(jnp.where(is_g & (lg == g_max), lane_f, big), axis=-1, keepdims=True)
    g_sel = g_lane.astype(I32) - ROUTER_GROUP_LANE
    is_e = (lane < N_EXPERTS) & (lane // EXPERTS_PER_GROUP == g_sel)
    e_max = jnp.max(jnp.where(is_e, lg, -jnp.inf), axis=-1, keepdims=True)
    pe = jnp.where(is_e, jnp.exp(lg - e_max), 0.0)
    pe = pe / jnp.sum(pe, axis=-1, keepdims=True)
    pe = jnp.where(is_e, pe, -1.0)
    p1 = jnp.max(pe, axis=-1, keepdims=True)
    i1 = jnp.min(jnp.where(pe == p1, lane_f, big), axis=-1, keepdims=True)
    pe2 = jnp.where(lane_f == i1, -1.0, pe)
    p2 = jnp.max(pe2, axis=-1, keepdims=True)
    i2 = jnp.min(jnp.where(pe2 == p2, lane_f, big), axis=-1, keepdims=True)
    tot = p1 + p2
    ids_ref[...] = jnp.where(lane == 0, i1, jnp.where(lane == 1, i2, 0.0)).astype(I32)
    gates_ref[...] = jnp.where(lane == 0, g_w * (p1 / tot), jnp.where(lane == 1, g_w * (p2 / tot), 0.0))


def _router(x1, g, whi, wlo, bias, tm):
    n = x1.shape[0]
    return pl.pallas_call(
        _router_kernel,
        grid=(n // tm,),
        in_specs=[pl.BlockSpec((tm, D_MODEL), lambda i: (i, 0)),
                  pl.BlockSpec((1, D_MODEL), lambda i: (0, 0)),
                  pl.BlockSpec((D_MODEL, LANES), lambda i: (0, 0)),
                  pl.BlockSpec((D_MODEL, LANES), lambda i: (0, 0)),
                  pl.BlockSpec((1, LANES), lambda i: (0, 0))],
        out_specs=[pl.BlockSpec((tm, D_MODEL), lambda i: (i, 0)),
                   pl.BlockSpec((tm, LANES), lambda i: (i, 0)),
                   pl.BlockSpec((tm, LANES), lambda i: (i, 0))],
        out_shape=[jax.ShapeDtypeStruct((n, D_MODEL), F32),
                   jax.ShapeDtypeStruct((n, LANES), I32),
                   jax.ShapeDtypeStruct((n, LANES), F32)],
        compiler_params=_params(("arbitrary",)),
        name="router",
    )(x1, g, whi, wlo, bias)


def _row_copy(src, dst, s_row, d_row, sem):
    return pltpu.make_async_copy(src.at[pl.ds(s_row, 1)], dst.at[pl.ds(d_row, 1)], sem)


def _dispatch_kernel(dest_ref, hn_ref, xs_in_ref, xs_ref, sem, *, ch):
    del xs_in_ref

    def issue(t, carry):
        for s in range(2):
            _row_copy(hn_ref, xs_ref, t, dest_ref[0, 0, 2 * t + s], sem).start()
        return carry

    lax.fori_loop(0, ch // 2, issue, 0, unroll=8)
    for s in range(2):
        pltpu.make_async_copy(hn_ref, xs_ref.at[pl.ds(0, ch // 2)], sem).wait()


def _dispatch(dest, hn, xs0, ch):
    m = dest.shape[0]
    return pl.pallas_call(
        functools.partial(_dispatch_kernel, ch=ch),
        grid=(m // ch,),
        in_specs=[pl.BlockSpec((1, 1, ch), lambda i: (i, 0, 0), memory_space=pltpu.SMEM),
                  pl.BlockSpec((ch // 2, D_MODEL), lambda i: (i, 0)),
                  pl.BlockSpec(memory_space=pl.ANY)],
        out_specs=pl.BlockSpec(memory_space=pl.ANY),
        out_shape=jax.ShapeDtypeStruct(xs0.shape, xs0.dtype),
        scratch_shapes=[pltpu.SemaphoreType.DMA(())],
        input_output_aliases={2: 0},
        compiler_params=_params(("arbitrary",), disable_bounds_checks=True, has_side_effects=True),
        name="dispatch",
    )(dest.reshape(m // ch, 1, ch), hn, xs0)


def _expert_kernel(be_ref, nu_ref, x_ref, wg_ref, wu_ref, wd_ref, o_ref, wg_s, wu_s, wd_s):
    b = pl.program_id(0)
    prev = be_ref[jnp.maximum(b - 1, 0)]

    @pl.when(b < nu_ref[0])
    def _():
        @pl.when((b == 0) | (be_ref[b] != prev))
        def _():
            wg_s[...] = wg_ref[...].astype(BF16)
            wu_s[...] = wu_ref[...].astype(BF16)
            wd_s[...] = wd_ref[...].astype(BF16)

        x = x_ref[...].astype(BF16)
        gate = _dot(x, wg_s[...])
        h = gate * _sigmoid(gate) * _dot(x, wu_s[...])
        o_ref[...] = _dot(h.astype(BF16), wd_s[...])

    @pl.when(b >= nu_ref[0])
    def _():
        o_ref[...] = jnp.zeros(o_ref.shape, F32)


def _experts(block_e, n_used, xs, wg, wu, wd, bm):
    rows = xs.shape[0]
    nb = rows // bm
    grid_spec = pltpu.PrefetchScalarGridSpec(
        num_scalar_prefetch=2,
        grid=(nb,),
        in_specs=[pl.BlockSpec((bm, D_MODEL), lambda b, be, nu: (b, 0)),
                  pl.BlockSpec((None, D_MODEL, D_EXPERT), lambda b, be, nu: (be[b], 0, 0)),
                  pl.BlockSpec((None, D_MODEL, D_EXPERT), lambda b, be, nu: (be[b], 0, 0)),
                  pl.BlockSpec((None, D_EXPERT, D_MODEL), lambda b, be, nu: (be[b], 0, 0))],
        out_specs=pl.BlockSpec((bm, D_MODEL), lambda b, be, nu: (b, 0)),
        scratch_shapes=[pltpu.VMEM((D_MODEL, D_EXPERT), BF16), pltpu.VMEM((D_MODEL, D_EXPERT), BF16),
                        pltpu.VMEM((D_EXPERT, D_MODEL), BF16)])
    return pl.pallas_call(
        _expert_kernel,
        grid_spec=grid_spec,
        out_shape=jax.ShapeDtypeStruct((rows, D_MODEL), F32),
        compiler_params=_params(("arbitrary",)),
        name="experts",
    )(block_e, n_used, xs, wg, wu, wd)


def _combine_kernel(dest_ref, gates_ref, x_ref, ys_ref, o_ref, buf, sem, *, tc):
    def issue(t, carry):
        for s in range(2):
            _row_copy(ys_ref, buf.at[s], dest_ref[0, 0, 2 * t + s], t, sem).start()
        return carry

    lax.fori_loop(0, tc, issue, 0, unroll=8)
    for s in range(2):
        pltpu.make_async_copy(ys_ref.at[pl.ds(0, tc)], buf.at[s], sem).wait()
    gates = gates_ref[...]
    o_ref[...] = x_ref[...] + gates[:, 0:1] * buf[0] + gates[:, 1:2] * buf[1]


def _combine(dest, gates, x1, ys, tc):
    n = x1.shape[0]
    return pl.pallas_call(
        functools.partial(_combine_kernel, tc=tc),
        grid=(n // tc,),
        in_specs=[pl.BlockSpec((1, 1, 2 * tc), lambda i: (i, 0, 0), memory_space=pltpu.SMEM),
                  pl.BlockSpec((tc, LANES), lambda i: (i, 0)),
                  pl.BlockSpec((tc, D_MODEL), lambda i: (i, 0)),
                  pl.BlockSpec(memory_space=pl.ANY)],
        out_specs=pl.BlockSpec((tc, D_MODEL), lambda i: (i, 0)),
        out_shape=jax.ShapeDtypeStruct((n, D_MODEL), F32),
        scratch_shapes=[pltpu.VMEM((2, tc, D_MODEL), F32), pltpu.SemaphoreType.DMA(())],
        compiler_params=_params(("arbitrary",), disable_bounds_checks=True),
        name="combine",
    )(dest.reshape(n // tc, 1, 2 * tc), gates, x1, ys)


def _moe(x1, p, tm, bm):
    n = x1.shape[0]
    hn, ids, gates = _router(x1, p['norm_ffn'], p['wr_hi'], p['wr_lo'], p['b_router'], tm)
    flat_e = ids[:, :2].reshape(-1)
    m = flat_e.shape[0]
    onehot = (flat_e[:, None] == jnp.arange(N_EXPERTS, dtype=I32)[None, :]).astype(I32)
    csum = jnp.cumsum(onehot, axis=0)
    counts = csum[-1]
    padded = (counts + bm - 1) // bm * bm
    pad_end = jnp.cumsum(padded)
    pad_start = pad_end - padded
    dest = jnp.sum(onehot * (csum - 1 + pad_start[None, :]), axis=1).astype(I32)
    nb = -(-(m + N_EXPERTS * (bm - 1)) // bm)
    block_start = jnp.arange(nb, dtype=I32) * bm
    block_e = jnp.minimum(jnp.sum((pad_end[None, :] <= block_start[:, None]).astype(I32), axis=1),
                          N_EXPERTS - 1).astype(I32)
    n_used = (pad_end[-1] // bm).astype(I32).reshape(1)
    xs0 = jnp.zeros((nb * bm, D_MODEL), F32)
    xs = _dispatch(dest, hn, xs0, min(m, 1024))
    ys = _experts(block_e, n_used, xs, p['w_gate'], p['w_up'], p['w_down'], bm)
    return _combine(dest, gates, x1, ys, min(n, 256))


def _prep(norm_mix, w_in, s5_a_re, s5_a_im, s5_log_dt, s5_b_re, s5_b_im, s5_c_re, s5_c_im, s5_d,
          w_glu, q_norm, k_norm, idx_k_norm, w_branch_a, w_branch_b, w_out, norm_ffn,
          w_router_group, b_router_group, w_router_expert, b_router_expert, w_gate, w_up, w_down):
    pts = []
    acc = 0
    for s in IN_SIZES[:-1]:
        acc += s
        pts.append(acc)
    w_u, w_q, w_k, w_v, w_qi, w_ki, w_wi, w_ga, w_gb = jnp.split(w_in, pts, axis=1)
    pad = PROJ_COLS - sum(IN_SIZES)
    w_proj = jnp.concatenate([w_u, w_q, w_ga, w_gb, w_qi, w_k, w_v, w_ki, w_wi,
                              jnp.zeros((D_MODEL, pad), w_in.dtype)], axis=1).astype(BF16)
    one = lambda k: jnp.ones((k,), F32)
    gain = jnp.concatenate([one(S5_WIDTH), jnp.tile(q_norm.astype(F32), N_HEADS), one(2 * D_MODEL),
                            one(N_IDX_HEADS * IDX_DIM), jnp.tile(k_norm.astype(F32), N_KV_HEADS), one(KV_WIDTH),
                            idx_k_norm.astype(F32), one(N_IDX_HEADS + pad)]).reshape(1, PROJ_COLS)
    half = 256
    wv, wg = w_glu[:, :S5_WIDTH], w_glu[:, S5_WIDTH:]
    w_glu_p = jnp.concatenate(
        [jnp.concatenate([wv[:, c * half:(c + 1) * half], wg[:, c * half:(c + 1) * half]], axis=1)
         for c in range(S5_WIDTH // half)], axis=1).astype(BF16)
    w_r = jnp.concatenate([w_router_expert.astype(F32), w_router_group.astype(F32),
                           jnp.zeros((D_MODEL, LANES - N_EXPERTS - N_GROUPS), F32)], axis=1)
    wr_hi = w_r.astype(BF16)
    wr_lo = (w_r - wr_hi.astype(F32)).astype(BF16)
    b_r = jnp.concatenate([b_router_expert.astype(F32), b_router_group.astype(F32),
                           jnp.zeros((LANES - N_EXPERTS - N_GROUPS,), F32)]).reshape(1, LANES)
    return dict(norm_mix=norm_mix.astype(F32).reshape(1, D_MODEL), w_proj=w_proj, gain=gain,
                s5=(s5_a_re, s5_a_im, s5_log_dt, s5_b_re, s5_b_im, s5_c_re, s5_c_im),
                s5_d=s5_d.astype(F32).reshape(1, S5_WIDTH), w_glu=w_glu_p,
                w_a=w_branch_a.astype(BF16), w_b=w_branch_b.astype(BF16), w_out=w_out.astype(BF16),
                norm_ffn=norm_ffn.astype(F32).reshape(1, D_MODEL), wr_hi=wr_hi, wr_lo=wr_lo, b_router=b_r,
                w_gate=w_gate, w_up=w_up, w_down=w_down)


def _layer(x, h0_re, h0_im, k_past, v_past, ki_past, p, *, tm, lc, tt, tq, kb, bm):
    bsz, t, _ = x.shape
    n = bsz * t
    x2 = x.reshape(n, D_MODEL)
    proj = _inproj(x2, p['norm_mix'], p['w_proj'], p['gain'], tm)
    proj3 = proj.reshape(bsz, t, PROJ_COLS)
    k = proj3[:, :, COL_K:COL_K + KV_WIDTH]
    v = proj3[:, :, COL_V:COL_V + KV_WIDTH]
    ki = proj3[:, :, COL_KI:COL_KI + IDX_DIM]
    n_state = S5_GROUPS * S5_STATE
    tables = _s5_tables(*p['s5'], lc)
    yg, s_re, s_im = _s5(proj3, h0_re.reshape(bsz, 1, n_state), h0_im.reshape(bsz, 1, n_state),
                         tables, p['s5_d'], lc, tt)
    y_a = _glu(yg.reshape(n, S5_WIDTH), p['w_glu'], tm)
    if k_past is None:
        pos0 = 0
        k_all, v_all, ki_all = k, v, ki
    else:
        pos0 = k_past.shape[1]
        k_all = jnp.concatenate([k_past.reshape(bsz, pos0, KV_WIDTH), k], axis=1)
        v_all = jnp.concatenate([v_past.reshape(bsz, pos0, KV_WIDTH), v], axis=1)
        ki_all = jnp.concatenate([ki_past, ki], axis=1)
    n_keys = k_all.shape[1]
    n_top = min(TOP_K_MAX, n_keys // 4)
    nkp = -(-n_keys // kb) * kb
    padk = lambda a: jnp.pad(a.astype(BF16), ((0, 0), (0, nkp - n_keys), (0, 0)))
    kit = jnp.swapaxes(padk(ki_all), 1, 2)
    kt = jnp.swapaxes(padk(k_all), 1, 2)
    y_b = _dsa(proj3, kit, kt, padk(v_all), tq=tq, kb=kb, pos0=pos0, n_keys=n_keys, n_top=n_top)
    merged = _merge(y_a, y_b.reshape(n, ATT_WIDTH), p['w_a'], p['w_b'], proj, tm)
    x1 = _outproj(merged, p['w_out'], x2, tm)
    y = _moe(x1, p, tm, bm)
    return (y.reshape(bsz, t, D_MODEL), s_re.reshape(bsz, S5_GROUPS, S5_STATE),
            s_im.reshape(bsz, S5_GROUPS, S5_STATE), k.reshape(bsz, t, N_KV_HEADS, HEAD_DIM),
            v.reshape(bsz, t, N_KV_HEADS, HEAD_DIM), ki)


def kernel(x_prompt, x_sample, state_s5_re, state_s5_im, cache_k, cache_v, cache_idx_k, norm_mix, w_in, s5_a_re, s5_a_im, s5_log_dt, s5_b_re, s5_b_im, s5_c_re, s5_c_im, s5_d, w_glu, q_norm, k_norm, idx_k_norm, w_branch_a, w_branch_b, w_out, norm_ffn, w_router_group, b_router_group, w_router_expert, b_router_expert, w_gate, w_up, w_down):
    p = _prep(norm_mix, w_in, s5_a_re, s5_a_im, s5_log_dt, s5_b_re, s5_b_im, s5_c_re, s5_c_im, s5_d,
              w_glu, q_norm, k_norm, idx_k_norm, w_branch_a, w_branch_b, w_out, norm_ffn,
              w_router_group, b_router_group, w_router_expert, b_router_expert, w_gate, w_up, w_down)
    h0 = jnp.zeros((x_prompt.shape[0], S5_GROUPS, S5_STATE), F32)
    yp, srp, sip, kp, vp, kip = _layer(x_prompt, h0, h0, None, None, None, p,
                                       tm=1024, lc=64, tt=1024, tq=128, kb=512, bm=256)
    ys, srs, sis, ks, vs, kis = _layer(x_sample, state_s5_re, state_s5_im, cache_k, cache_v, cache_idx_k, p,
                                       tm=256, lc=32, tt=32, tq=32, kb=384, bm=64)
    return (yp, ys, srp, sip, kp, vp, kip, srs, sis, ks, vs, kis)
```

```python
import functools
import math

import jax
import jax.numpy as jnp
from jax import lax
from jax.experimental import pallas as pl
from jax.experimental.pallas import tpu as pltpu

F32 = jnp.float32
BF16 = jnp.bfloat16
I32 = jnp.int32

D_MODEL = 2048
CHUNK = 64
EPS = 1e-6
S5_WIDTH = 1024
S5_GROUP = 16
S5_GROUPS = 64
S5_STATE = 64
S5_MAX_RE = -1e-4
N_HEADS = 8
N_KV_HEADS = 2
Q_PER_KV = 4
HEAD_DIM = 128
ATT_WIDTH = 1024
KV_WIDTH = 256
N_IDX_HEADS = 8
IDX_DIM = 64
TOP_K_MAX = 256
N_GROUPS = 4
EXPERTS_PER_GROUP = 8
N_EXPERTS = 32
D_EXPERT = 512
IN_SIZES = (S5_WIDTH, ATT_WIDTH, KV_WIDTH, KV_WIDTH, N_IDX_HEADS * IDX_DIM, IDX_DIM, N_IDX_HEADS, D_MODEL, D_MODEL)

LANES = 128
PROJ_TN = 512
COL_U, COL_Q, COL_GA, COL_GB, COL_QI, COL_K, COL_V, COL_KI, COL_WI = 0, 1024, 2048, 4096, 6144, 6656, 6912, 7168, 7232
PROJ_COLS = 7680
S5_LANE_BLOCKS = S5_WIDTH // LANES
S5_BLOCK_STATE = (LANES // S5_GROUP) * S5_STATE
VMEM_LIMIT = 56 * 1024 * 1024
INT_MIN = -2 ** 31
KEY_LOWEST_FINITE = -2 ** 31 + 0x00800000
NEG_BIG = -1e30
LOG2E = 1.4426950408889634
FAST_SOFTMAX_BOUND = 40.0
TOPK_PER_LANE = 12


def _dot(a, b):
    return jnp.dot(a, b, preferred_element_type=F32)


def _split_bf16(x):
    hi = x.astype(BF16)
    lo = (x - hi.astype(F32)).astype(BF16)
    return hi, lo


def _sigmoid(x):
    return 1.0 / (1.0 + jnp.exp(-x))


def _params(sem, **kw):
    return pltpu.CompilerParams(dimension_semantics=sem, vmem_limit_bytes=VMEM_LIMIT, **kw)


def _group_norm(a, gain):
    ms = jnp.mean(a * a, axis=-1, keepdims=True)
    return a * lax.rsqrt(ms + EPS) * gain


def _inproj_kernel(x_ref, g_ref, w_ref, gain_ref, o_ref, xn_ref):
    j = pl.program_id(1)

    @pl.when(j == 0)
    def _():
        xf = x_ref[...]
        ms = jnp.mean(xf * xf, axis=-1, keepdims=True)
        xn_ref[...] = (xf * lax.rsqrt(ms + EPS) * g_ref[...]).astype(BF16)

    acc = _dot(xn_ref[...], w_ref[...])
    gain = gain_ref[...]
    groups = [slice(c * LANES, (c + 1) * LANES) for c in range(PROJ_TN // LANES)]

    @pl.when((j < 2) | (j == 12))
    def _():
        o_ref[...] = acc

    @pl.when((j == 2) | (j == 3))
    def _():
        for s in groups:
            o_ref[:, s] = _group_norm(acc[:, s], gain[:, s])

    @pl.when((j >= 4) & (j < 12))
    def _():
        o_ref[...] = _sigmoid(acc)

    @pl.when(j == 13)
    def _():
        for s in groups[:2]:
            o_ref[:, s] = _group_norm(acc[:, s], gain[:, s])
        o_ref[:, 2 * LANES:] = acc[:, 2 * LANES:]

    @pl.when(j == 14)
    def _():
        a = acc[:, :LANES]
        lane = lax.broadcasted_iota(I32, a.shape, 1)
        is_ki = lane < IDX_DIM
        ms = jnp.sum(jnp.where(is_ki, a * a, 0.0), axis=-1, keepdims=True) * (1.0 / IDX_DIM)
        ki = a * lax.rsqrt(ms + EPS) * gain[:, :LANES]
        wi = a * (N_IDX_HEADS ** -0.5) * (IDX_DIM ** -0.5)
        o_ref[:, :LANES] = jnp.where(is_ki, ki, wi)
        o_ref[:, LANES:] = acc[:, LANES:]


def _inproj(x, g, w, gain, tm):
    n = x.shape[0]
    return pl.pallas_call(
        _inproj_kernel,
        grid=(n // tm, PROJ_COLS // PROJ_TN),
        in_specs=[pl.BlockSpec((tm, D_MODEL), lambda i, j: (i, 0)),
                  pl.BlockSpec((1, D_MODEL), lambda i, j: (0, 0)),
                  pl.BlockSpec((D_MODEL, PROJ_TN), lambda i, j: (0, j)),
                  pl.BlockSpec((1, PROJ_TN), lambda i, j: (0, j))],
        out_specs=pl.BlockSpec((tm, PROJ_TN), lambda i, j: (i, j)),
        out_shape=jax.ShapeDtypeStruct((n, PROJ_COLS), F32),
        scratch_shapes=[pltpu.VMEM((tm, D_MODEL), BF16)],
        compiler_params=_params(("arbitrary", "arbitrary")),
        name="inproj",
    )(x, g, w, gain)


def _gelu_tanh(y):
    return 0.5 * y * (1.0 + jnp.tanh(math.sqrt(2.0 / math.pi) * (y + 0.044715 * (y * y * y))))


def _s5_kernel(u_ref, h0r_ref, h0i_ref, bhi_ref, blo_ref, air_ref, aii_ref, apr_ref, api_ref,
               a1r_ref, a1i_ref, tri_ref, cd_ref, d_ref, yg_ref, sr_ref, si_ref,
               bu_scr, h_scr, hr_scr, hi_scr, *, lc, tt):
    t = pl.program_id(2)
    ns = S5_BLOCK_STATE

    @pl.when(t == 0)
    def _():
        hr_scr[...] = h0r_ref[...]
        hi_scr[...] = h0i_ref[...]

    u = u_ref[...]
    u_hi, u_lo = _split_bf16(u)
    bhi = bhi_ref[...]
    bu_scr[...] = _dot(u_hi, bhi) + _dot(u_lo, bhi) + _dot(u_hi, blo_ref[...])
    tri = tri_ref[...]
    air, aii, apr, api = air_ref[...], aii_ref[...], apr_ref[...], api_ref[...]
    a1r, a1i = a1r_ref[...], a1i_ref[...]

    def chunk(s, carry):
        h_re, h_im = carry
        r0 = pl.multiple_of(s * lc, lc)
        br = bu_scr[pl.ds(r0, lc), 0:ns]
        bi = bu_scr[pl.ds(r0, lc), ns:2 * ns]
        z = jnp.concatenate([air * br - aii * bi, air * bi + aii * br], axis=1)
        z_hi, z_lo = _split_bf16(z)
        c = _dot(tri, z_hi) + _dot(tri, z_lo)
        cr = c[:, 0:ns] + (a1r * h_re - a1i * h_im)
        ci = c[:, ns:2 * ns] + (a1r * h_im + a1i * h_re)
        hr = apr * cr - api * ci
        hi = apr * ci + api * cr
        h_scr[pl.ds(r0, lc), 0:ns] = hr.astype(BF16)
        h_scr[pl.ds(r0, lc), ns:2 * ns] = hi.astype(BF16)
        return hr[lc - 1:lc, :], hi[lc - 1:lc, :]

    h_re, h_im = lax.fori_loop(0, tt // lc, chunk, (hr_scr[...], hi_scr[...]), unroll=min(4, tt // lc))
    hr_scr[...] = h_re
    hi_scr[...] = h_im
    sr_ref[...] = h_re
    si_ref[...] = h_im
    y = _dot(h_scr[...], cd_ref[...]) + d_ref[...] * u
    yg_ref[...] = _gelu_tanh(y).astype(BF16)


def _s5_tables(a_re, a_im, log_dt, b_re, b_im, c_re, c_im, lc):
    lr = jnp.minimum(a_re.astype(F32), S5_MAX_RE)
    li = a_im.astype(F32)
    dt = jnp.exp(log_dt.astype(F32))[:, None]
    mag = jnp.exp(lr * dt)
    lbr = mag * jnp.cos(li * dt)
    lbi = mag * jnp.sin(li * dt)
    den = lr * lr + li * li
    fr = ((lbr - 1.0) * lr + lbi * li) / den
    fi = (lbi * lr - (lbr - 1.0) * li) / den
    br = b_re.astype(F32)
    bi = b_im.astype(F32)
    bbr = fr[..., None] * br - fi[..., None] * bi
    bbi = fr[..., None] * bi + fi[..., None] * br
    j = jnp.arange(lc, dtype=F32)[:, None, None]
    ang = j * (li * dt)[None]
    lmag = j * (lr * dt)[None]
    n_state = S5_GROUPS * S5_STATE
    apr = (jnp.exp(lmag) * jnp.cos(ang)).reshape(lc, n_state)
    api = (jnp.exp(lmag) * jnp.sin(ang)).reshape(lc, n_state)
    air = (jnp.exp(-lmag) * jnp.cos(ang)).reshape(lc, n_state)
    aii = (-jnp.exp(-lmag) * jnp.sin(ang)).reshape(lc, n_state)
    a1r = lbr.reshape(1, n_state)
    a1i = lbi.reshape(1, n_state)
    gpb = LANES // S5_GROUP
    eye = jnp.eye(gpb, dtype=F32)

    def bdiag(b):
        return jnp.einsum('kgpc,gh->kgchp', b.reshape(S5_LANE_BLOCKS, gpb, S5_STATE, S5_GROUP), eye).reshape(
            S5_LANE_BLOCKS, LANES, S5_BLOCK_STATE)

    def cdiag(c):
        return jnp.einsum('kgcp,gh->kgphc', c.reshape(S5_LANE_BLOCKS, gpb, S5_GROUP, S5_STATE), eye).reshape(
            S5_LANE_BLOCKS, S5_BLOCK_STATE, LANES)

    bd = jnp.concatenate([bdiag(bbr), bdiag(bbi)], axis=-1)
    bd_hi = bd.astype(BF16)
    bd_lo = (bd - bd_hi.astype(F32)).astype(BF16)
    cd = jnp.concatenate([cdiag(c_re.astype(F32)), -cdiag(c_im.astype(F32))], axis=1).astype(BF16)
    tri = jnp.tril(jnp.ones((lc, lc), F32)).astype(BF16)
    return dict(bd_hi=bd_hi, bd_lo=bd_lo, air=air, aii=aii, apr=apr, api=api, a1r=a1r, a1i=a1i, tri=tri, cd=cd)


def _s5(proj3, h0_re, h0_im, tb, dvec, lc, tt):
    bsz, t, _ = proj3.shape
    ns = S5_BLOCK_STATE
    n_state = S5_GROUPS * S5_STATE
    tab = lambda: pl.BlockSpec((lc, ns), lambda b, k, i: (0, k))
    row = lambda: pl.BlockSpec((1, ns), lambda b, k, i: (0, k))
    st = lambda: pl.BlockSpec((None, 1, ns), lambda b, k, i: (b, 0, k))
    return pl.pallas_call(
        functools.partial(_s5_kernel, lc=lc, tt=tt),
        grid=(bsz, S5_LANE_BLOCKS, t // tt),
        in_specs=[pl.BlockSpec((None, tt, LANES), lambda b, k, i: (b, i, k)),
                  st(), st(),
                  pl.BlockSpec((None, LANES, 2 * ns), lambda b, k, i: (k, 0, 0)),
                  pl.BlockSpec((None, LANES, 2 * ns), lambda b, k, i: (k, 0, 0)),
                  tab(), tab(), tab(), tab(), row(), row(),
                  pl.BlockSpec((lc, lc), lambda b, k, i: (0, 0)),
                  pl.BlockSpec((None, 2 * ns, LANES), lambda b, k, i: (k, 0, 0)),
                  pl.BlockSpec((1, LANES), lambda b, k, i: (0, k))],
        out_specs=[pl.BlockSpec((None, tt, LANES), lambda b, k, i: (b, i, k)), st(), st()],
        out_shape=[jax.ShapeDtypeStruct((bsz, t, S5_WIDTH), BF16),
                   jax.ShapeDtypeStruct((bsz, 1, n_state), F32),
                   jax.ShapeDtypeStruct((bsz, 1, n_state), F32)],
        scratch_shapes=[pltpu.VMEM((tt, 2 * ns), F32), pltpu.VMEM((tt, 2 * ns), BF16),
                        pltpu.VMEM((1, ns), F32), pltpu.VMEM((1, ns), F32)],
        compiler_params=_params(("arbitrary", "arbitrary", "arbitrary")),
        name="s5",
    )(proj3, h0_re, h0_im, tb['bd_hi'], tb['bd_lo'], tb['air'], tb['aii'], tb['apr'], tb['api'],
      tb['a1r'], tb['a1i'], tb['tri'], tb['cd'], dvec)


def _glu_kernel(y_ref, w_ref, o_ref):
    acc = _dot(y_ref[...], w_ref[...])
    half = acc.shape[1] // 2
    o_ref[...] = (acc[:, :half] * _sigmoid(acc[:, half:])).astype(BF16)


def _glu(yg, w, tm):
    n = yg.shape[0]
    tn = 512
    return pl.pallas_call(
        _glu_kernel,
        grid=(n // tm, 2 * S5_WIDTH // tn),
        in_specs=[pl.BlockSpec((tm, S5_WIDTH), lambda i, j: (i, 0)),
                  pl.BlockSpec((S5_WIDTH, tn), lambda i, j: (0, j))],
        out_specs=pl.BlockSpec((tm, tn // 2), lambda i, j: (i, j)),
        out_shape=jax.ShapeDtypeStruct((n, S5_WIDTH), BF16),
        compiler_params=_params(("arbitrary", "arbitrary")),
        name="glu",
    )(yg, w)


def _dsa_kernel(q_ref, qi_ref, kiw_ref, kit_ref, kt_ref, v_ref, o_ref,
                key_scr, qs_scr, qis_scr, wis_scr, m_scr, l_scr, acc_scr,
                kmax_scr, mb_scr, lsum_scr, cand_scr, ckey_scr, thr_scr,
                *, tq, kb, pos0, n_keys, n_top, nt):
    i = pl.program_id(1)
    nkp = key_scr.shape[2]
    n_cand = cand_scr.shape[1]
    slot_s = i % 2
    slot_a = 1 - slot_s
    tile_s = jnp.minimum(i, nt - 1)
    tile_a = jnp.maximum(i - 1, 0)

    def n_blocks(tile):
        k_end = jnp.minimum(n_keys, (pos0 + tile * tq + tq - 1) // CHUNK * CHUNK + CHUNK)
        return (k_end + kb - 1) // kb

    nkb_s = jnp.where(i < nt, n_blocks(tile_s), 0)
    nkb_a = jnp.where(i >= 1, n_blocks(tile_a), 0)

    @pl.when(i == 0)
    def _():
        def norm_block(b, carry):
            c0 = pl.multiple_of(b * kb, kb)
            out = []
            for g in range(N_KV_HEADS):
                kk = kt_ref[g * HEAD_DIM:(g + 1) * HEAD_DIM, pl.ds(c0, kb)].astype(F32)
                out.append(jnp.maximum(carry[g], jnp.sum(kk * kk, axis=0, keepdims=True)))
            return tuple(out)

        res = lax.fori_loop(0, nkp // kb, norm_block, (jnp.zeros((1, kb), F32),) * N_KV_HEADS)
        for g in range(N_KV_HEADS):
            kmax_scr[g] = jnp.broadcast_to(jnp.max(res[g], axis=-1, keepdims=True), (1, LANES))

    q = q_ref[...] * (HEAD_DIM ** -0.5 * LOG2E)
    for h in range(N_HEADS):
        g, r = divmod(h, Q_PER_KV)
        qs_scr[g, r * tq:(r + 1) * tq, :] = q[:, h * HEAD_DIM:(h + 1) * HEAD_DIM].astype(BF16)
    qi = qi_ref[...]
    kiw = kiw_ref[...]
    for h in range(N_IDX_HEADS):
        qis_scr[h * tq:(h + 1) * tq, :] = qi[:, h * IDX_DIM:(h + 1) * IDX_DIM].astype(BF16)
        wis_scr[h * tq:(h + 1) * tq, :] = kiw[:, IDX_DIM + h:IDX_DIM + h + 1]

    q_chunk = (pos0 + tile_s * tq + lax.broadcasted_iota(I32, (tq, 1), 0)) // CHUNK

    def to_key(x):
        bits = pltpu.bitcast(x, I32)
        return jnp.where(bits < 0, bits ^ 0x7FFFFFFF, bits)

    cand_scr[slot_s] = jnp.full(cand_scr.shape[1:], -jnp.inf, F32)

    def score_block(b):
        c0 = pl.multiple_of(b * kb, kb)
        s = _dot(qis_scr[...], kit_ref[:, pl.ds(c0, kb)])
        s = jnp.maximum(s, 0.0) * wis_scr[...]
        sc = s[0:tq]
        for h in range(1, N_IDX_HEADS):
            sc = sc + s[h * tq:(h + 1) * tq]
        k_pos = c0 + lax.broadcasted_iota(I32, (1, kb), 1)
        adm = (k_pos // CHUNK <= q_chunk) & (k_pos < n_keys)
        blk = jnp.where(adm, sc + 0.0, -jnp.inf)
        key_scr[slot_s, :, pl.ds(c0, kb)] = blk
        for rg in range(tq // 8):
            rows = slice(rg * 8, rg * 8 + 8)
            top = [cand_scr[slot_s, j, rows, :] for j in range(n_cand)]
            for c in range(kb // LANES):
                x = blk[rows, c * LANES:(c + 1) * LANES]
                for j in range(n_cand):
                    hi = jnp.maximum(top[j], x)
                    x = jnp.minimum(top[j], x)
                    top[j] = hi
            for j in range(n_cand):
                cand_scr[slot_s, j, rows, :] = top[j]

    def count_all(hit_fn):
        def body(b, acc):
            c0 = pl.multiple_of(b * kb, kb)
            hit = hit_fn(key_scr[slot_a, :, pl.ds(c0, kb)])
            for c in range(kb // LANES):
                acc = acc + hit[:, c * LANES:(c + 1) * LANES]
            return acc
        acc = lax.fori_loop(0, nkb_a, body, jnp.zeros((tq, LANES), F32))
        return jnp.sum(acc, axis=-1, keepdims=True)

    def count_cand(cand):
        acc = jnp.zeros((tq, LANES), F32)
        for j in range(n_cand):
            acc = acc + jnp.where(ckey_scr[j] >= cand, 1.0, 0.0)
        return jnp.sum(acc, axis=-1, keepdims=True)

    def bisect(count_fn):
        def bit_pass(it, t_off):
            cand_off = t_off | lax.shift_left(jnp.int32(1), 31 - it)
            cnt = count_fn(cand_off ^ INT_MIN)
            return jnp.where(cnt >= n_top, cand_off, t_off)
        t_off = lax.fori_loop(0, 32, bit_pass, jnp.zeros((tq, 1), I32))
        key = jnp.maximum(t_off ^ INT_MIN, KEY_LOWEST_FINITE)
        return pltpu.bitcast(jnp.where(key < 0, key ^ 0x7FFFFFFF, key), F32)

    @pl.when(i >= 1)
    def _():
        for j in range(n_cand):
            ckey_scr[j] = to_key(cand_scr[slot_a, j])
        thr_cand = bisect(count_cand)
        thr_scr[...] = thr_cand
        over = jnp.max(count_all(lambda x: jnp.where(x >= thr_cand, 1.0, 0.0))) > n_top

        @pl.when(over)
        def _():
            thr_scr[...] = bisect(lambda cand: count_all(lambda x: jnp.where(to_key(x) >= cand, 1.0, 0.0)))

    thr = thr_scr[...]

    acc_scr[...] = jnp.zeros(acc_scr.shape, F32)
    lsum_scr[...] = jnp.zeros(lsum_scr.shape, F32)
    m_scr[...] = jnp.full(m_scr.shape, NEG_BIG, F32)
    l_scr[...] = jnp.zeros(l_scr.shape, F32)
    bound_max = jnp.float32(0.0)
    for g in range(N_KV_HEADS):
        qg = qs_scr[g].astype(F32)
        qn2 = jnp.sum(qg * qg, axis=-1, keepdims=True)
        bound = jnp.sqrt(qn2 * kmax_scr[g][:, 0:1]) * 1.002
        mb_scr[g] = jnp.broadcast_to(bound, (Q_PER_KV * tq, LANES))
        bound_max = jnp.maximum(bound_max, jnp.max(bound))
    fast = bound_max <= FAST_SOFTMAX_BOUND

    def attend_fast(b):
        c0 = pl.multiple_of(b * kb, kb)
        bias = jnp.where(key_scr[slot_a, :, pl.ds(c0, kb)] >= thr, 0.0, NEG_BIG)
        for g in range(N_KV_HEADS):
            lg = _dot(qs_scr[g], kt_ref[g * HEAD_DIM:(g + 1) * HEAD_DIM, pl.ds(c0, kb)])
            parts = []
            for r in range(Q_PER_KV):
                rows = slice(r * tq, (r + 1) * tq)
                mb = mb_scr[g, rows, :]
                e = [jnp.exp2((lg[rows, c * LANES:(c + 1) * LANES] - mb) + bias[:, c * LANES:(c + 1) * LANES])
                     for c in range(kb // LANES)]
                ls = e[0]
                for c in range(1, kb // LANES):
                    ls = ls + e[c]
                lsum_scr[g, rows, :] += ls
                parts.append(jnp.concatenate(e, axis=1).astype(BF16))
            p = jnp.concatenate(parts, axis=0)
            acc_scr[g] += _dot(p, v_ref[pl.ds(c0, kb), g * HEAD_DIM:(g + 1) * HEAD_DIM])

    def attend_general(b):
        c0 = pl.multiple_of(b * kb, kb)
        sel = key_scr[slot_a, :, pl.ds(c0, kb)] >= thr
        sel4 = jnp.concatenate([sel] * Q_PER_KV, axis=0)
        for g in range(N_KV_HEADS):
            lg = _dot(qs_scr[g], kt_ref[g * HEAD_DIM:(g + 1) * HEAD_DIM, pl.ds(c0, kb)])
            lg = jnp.where(sel4, lg, NEG_BIG)
            m_old = m_scr[g]
            m_new = jnp.maximum(m_old, jnp.max(lg, axis=-1, keepdims=True))
            p = jnp.exp2(lg - m_new)
            alpha = jnp.exp2(m_old - m_new)
            l_scr[g] = alpha * l_scr[g] + jnp.sum(p, axis=-1, keepdims=True)
            pv = _dot(p.astype(BF16), v_ref[pl.ds(c0, kb), g * HEAD_DIM:(g + 1) * HEAD_DIM])
            acc_scr[g] = alpha * acc_scr[g] + pv
            m_scr[g] = m_new

    def loop(lo, hi, *fns):
        def body(b, carry):
            for fn in fns:
                fn(b)
            return carry
        lax.fori_loop(lo, hi, body, 0)

    n_fused = jnp.where(fast, jnp.minimum(nkb_a, nkb_s), 0)
    loop(0, n_fused, score_block, attend_fast)
    loop(n_fused, nkb_s, score_block)
    loop(n_fused, jnp.where(fast, nkb_a, 0), attend_fast)
    loop(0, jnp.where(fast, 0, nkb_a), attend_general)

    @pl.when((i >= 1) & fast)
    def _():
        for h in range(N_HEADS):
            g, r = divmod(h, Q_PER_KV)
            rows = slice(r * tq, (r + 1) * tq)
            l = jnp.sum(lsum_scr[g, rows, :], axis=-1, keepdims=True)
            o_ref[:, h * HEAD_DIM:(h + 1) * HEAD_DIM] = (acc_scr[g, rows, :] / l).astype(BF16)

    @pl.when((i >= 1) & jnp.logical_not(fast))
    def _():
        for h in range(N_HEADS):
            g, r = divmod(h, Q_PER_KV)
            rows = slice(r * tq, (r + 1) * tq)
            o_ref[:, h * HEAD_DIM:(h + 1) * HEAD_DIM] = (acc_scr[g, rows, :] / l_scr[g, rows, :]).astype(BF16)


def _dsa(proj3, kit, kt, v, *, tq, kb, pos0, n_keys, n_top):
    bsz, t, _ = proj3.shape
    nkp = kit.shape[-1]
    nt = t // tq
    kern = functools.partial(_dsa_kernel, tq=tq, kb=kb, pos0=pos0, n_keys=n_keys, n_top=n_top, nt=nt)
    resident = dict(pipeline_mode=pl.Buffered(1))
    prev = lambda i: jnp.maximum(i - 1, 0)
    cur = lambda i: jnp.minimum(i, nt - 1)
    return pl.pallas_call(
        kern,
        grid=(bsz, nt + 1),
        in_specs=[pl.BlockSpec((None, tq, ATT_WIDTH), lambda b, i: (b, prev(i), COL_Q // ATT_WIDTH)),
                  pl.BlockSpec((None, tq, 512), lambda b, i: (b, cur(i), COL_QI // 512)),
                  pl.BlockSpec((None, tq, 512), lambda b, i: (b, cur(i), COL_KI // 512)),
                  pl.BlockSpec((None, IDX_DIM, nkp), lambda b, i: (b, 0, 0), **resident),
                  pl.BlockSpec((None, KV_WIDTH, nkp), lambda b, i: (b, 0, 0), **resident),
                  pl.BlockSpec((None, nkp, KV_WIDTH), lambda b, i: (b, 0, 0), **resident)],
        out_specs=pl.BlockSpec((None, tq, ATT_WIDTH), lambda b, i: (b, prev(i), 0)),
        out_shape=jax.ShapeDtypeStruct((bsz, t, ATT_WIDTH), BF16),
        scratch_shapes=[pltpu.VMEM((2, tq, nkp), F32),
                        pltpu.VMEM((N_KV_HEADS, Q_PER_KV * tq, HEAD_DIM), BF16),
                        pltpu.VMEM((N_IDX_HEADS * tq, IDX_DIM), BF16),
                        pltpu.VMEM((N_IDX_HEADS * tq, 1), F32),
                        pltpu.VMEM((N_KV_HEADS, Q_PER_KV * tq, 1), F32),
                        pltpu.VMEM((N_KV_HEADS, Q_PER_KV * tq, 1), F32),
                        pltpu.VMEM((N_KV_HEADS, Q_PER_KV * tq, HEAD_DIM), F32),
                        pltpu.VMEM((N_KV_HEADS, 1, LANES), F32),
                        pltpu.VMEM((N_KV_HEADS, Q_PER_KV * tq, LANES), F32),
                        pltpu.VMEM((N_KV_HEADS, Q_PER_KV * tq, LANES), F32),
                        pltpu.VMEM((2, TOPK_PER_LANE, tq, LANES), F32),
                        pltpu.VMEM((TOPK_PER_LANE, tq, LANES), I32),
                        pltpu.VMEM((tq, 1), F32)],
        compiler_params=_params(("arbitrary", "arbitrary")),
        name="dsa",
    )(proj3, proj3, proj3, kit, kt, v)


def _merge_kernel(ya_ref, yb_ref, wa_ref, wb_ref, ga_ref, gb_ref, o_ref):
    o_ref[...] = (ga_ref[...] * _dot(ya_ref[...], wa_ref[...])
                  + gb_ref[...] * _dot(yb_ref[...], wb_ref[...])).astype(BF16)


def _merge(ya, yb, wa, wb, proj, tm):
    n = ya.shape[0]
    tn = 512
    return pl.pallas_call(
        _merge_kernel,
        grid=(n // tm, D_MODEL // tn),
        in_specs=[pl.BlockSpec((tm, S5_WIDTH), lambda i, j: (i, 0)),
                  pl.BlockSpec((tm, ATT_WIDTH), lambda i, j: (i, 0)),
                  pl.BlockSpec((S5_WIDTH, tn), lambda i, j: (0, j)),
                  pl.BlockSpec((ATT_WIDTH, tn), lambda i, j: (0, j)),
                  pl.BlockSpec((tm, tn), lambda i, j: (i, COL_GA // tn + j)),
                  pl.BlockSpec((tm, tn), lambda i, j: (i, COL_GB // tn + j))],
        out_specs=pl.BlockSpec((tm, tn), lambda i, j: (i, j)),
        out_shape=jax.ShapeDtypeStruct((n, D_MODEL), BF16),
        compiler_params=_params(("arbitrary", "arbitrary")),
        name="merge",
    )(ya, yb, wa, wb, proj, proj)


def _outproj_kernel(m_ref, w_ref, x_ref, o_ref):
    o_ref[...] = x_ref[...] + _dot(m_ref[...], w_ref[...])


def _outproj(merged, w, x, tm):
    n = x.shape[0]
    tn = 512
    return pl.pallas_call(
        _outproj_kernel,
        grid=(n // tm, D_MODEL // tn),
        in_specs=[pl.BlockSpec((tm, D_MODEL), lambda i, j: (i, 0)),
                  pl.BlockSpec((D_MODEL, tn), lambda i, j: (0, j)),
                  pl.BlockSpec((tm, tn), lambda i, j: (i, j))],
        out_specs=pl.BlockSpec((tm, tn), lambda i, j: (i, j)),
        out_shape=jax.ShapeDtypeStruct((n, D_MODEL), F32),
        compiler_params=_params(("arbitrary", "arbitrary")),
        name="outproj",
    )(merged, w, x)


ROUTER_GROUP_LANE = N_EXPERTS


def _router_kernel(x_ref, g_ref, whi_ref, wlo_ref, b_ref, hn_ref, ids_ref, gates_ref):
    xf = x_ref[...]
    ms = jnp.mean(xf * xf, axis=-1, keepdims=True)
    hn = xf * lax.rsqrt(ms + EPS) * g_ref[...]
    hn_ref[...] = hn
    h_hi, h_lo = _split_bf16(hn)
    whi = whi_ref[...]
    lg = _dot(h_hi, whi) + _dot(h_lo, whi) + _dot(h_hi, wlo_ref[...]) + b_ref[...]
    lane = lax.broadcasted_iota(I32, lg.shape, 1)
    lane_f = lane.astype(F32)
    big = float(LANES)
    is_g = (lane >= ROUTER_GROUP_LANE) & (lane < ROUTER_GROUP_LANE + N_GROUPS)
    g_max = jnp.max(jnp.where(is_g, lg, -jnp.inf), axis=-1, keepdims=True)
    g_den = jnp.sum(jnp.where(is_g, jnp.exp(lg - g_max), 0.0), axis=-1, keepdims=True)
    g_w = 1.0 / g_den
    g_lane = jnp.min(jnp.where(is_g & (lg == g_max), lane_f, big), axis=-1, keepdims=True)
    g_sel = g_lane.astype(I32) - ROUTER_GROUP_LANE
    is_e = (lane < N_EXPERTS) & (lane // EXPERTS_PER_GROUP == g_sel)
    e_max = jnp.max(jnp.where(is_e, lg, -jnp.inf), axis=-1, keepdims=True)
    pe = jnp.where(is_e, jnp.exp(lg - e_max), 0.0)
    pe = pe / jnp.sum(pe, axis=-1, keepdims=True)
    pe = jnp.where(is_e, pe, -1.0)
    p1 = jnp.max(pe, axis=-1, keepdims=True)
    i1 = jnp.min(jnp.where(pe == p1, lane_f, big), axis=-1, keepdims=True)
    pe2 = jnp.where(lane_f == i1, -1.0, pe)
    p2 = jnp.max(pe2, axis=-1, keepdims=True)
    i2 = jnp.min(jnp.where(pe2 == p2, lane_f, big), axis=-1, keepdims=True)
    tot = p1 + p2
    ids_ref[...] = jnp.where(lane == 0, i1, jnp.where(lane == 1, i2, 0.0)).astype(I32)
    gates_ref[...] = jnp.where(lane == 0, g_w * (p1 / tot), jnp.where(lane == 1, g_w * (p2 / tot), 0.0))


def _router(x1, g, whi, wlo, bias, tm):
    n = x1.shape[0]
    return pl.pallas_call(
        _router_kernel,
        grid=(n // tm,),
        in_specs=[pl.BlockSpec((tm, D_MODEL), lambda i: (i, 0)),
                  pl.BlockSpec((1, D_MODEL), lambda i: (0, 0)),
                  pl.BlockSpec((D_MODEL, LANES), lambda i: (0, 0)),
                  pl.BlockSpec((D_MODEL, LANES), lambda i: (0, 0)),
                  pl.BlockSpec((1, LANES), lambda i: (0, 0))],
        out_specs=[pl.BlockSpec((tm, D_MODEL), lambda i: (i, 0)),
                   pl.BlockSpec((tm, LANES), lambda i: (i, 0)),
                   pl.BlockSpec((tm, LANES), lambda i: (i, 0))],
        out_shape=[jax.ShapeDtypeStruct((n, D_MODEL), F32),
                   jax.ShapeDtypeStruct((n, LANES), I32),
                   jax.ShapeDtypeStruct((n, LANES), F32)],
        compiler_params=_params(("arbitrary",)),
        name="router",
    )(x1, g, whi, wlo, bias)


def _row_copy(src, dst, s_row, d_row, sem):
    return pltpu.make_async_copy(src.at[pl.ds(s_row, 1)], dst.at[pl.ds(d_row, 1)], sem)


def _dispatch_kernel(dest_ref, hn_ref, xs_in_ref, xs_ref, sem, *, ch):
    del xs_in_ref

    def issue(t, carry):
        for s in range(2):
            _row_copy(hn_ref, xs_ref, t, dest_ref[0, 0, 2 * t + s], sem).start()
        return carry

    lax.fori_loop(0, ch // 2, issue, 0, unroll=8)
    for s in range(2):
        pltpu.make_async_copy(hn_ref, xs_ref.at[pl.ds(0, ch // 2)], sem).wait()


def _dispatch(dest, hn, xs0, ch):
    m = dest.shape[0]
    return pl.pallas_call(
        functools.partial(_dispatch_kernel, ch=ch),
        grid=(m // ch,),
        in_specs=[pl.BlockSpec((1, 1, ch), lambda i: (i, 0, 0), memory_space=pltpu.SMEM),
                  pl.BlockSpec((ch // 2, D_MODEL), lambda i: (i, 0)),
                  pl.BlockSpec(memory_space=pl.ANY)],
        out_specs=pl.BlockSpec(memory_space=pl.ANY),
        out_shape=jax.ShapeDtypeStruct(xs0.shape, xs0.dtype),
        scratch_shapes=[pltpu.SemaphoreType.DMA(())],
        input_output_aliases={2: 0},
        compiler_params=_params(("arbitrary",), disable_bounds_checks=True, has_side_effects=True),
        name="dispatch",
    )(dest.reshape(m // ch, 1, ch), hn, xs0)


def _expert_kernel(be_ref, nu_ref, x_ref, wg_ref, wu_ref, wd_ref, o_ref, wg_s, wu_s, wd_s):
    b = pl.program_id(0)
    prev = be_ref[jnp.maximum(b - 1, 0)]

    @pl.when(b < nu_ref[0])
    def _():
        @pl.when((b == 0) | (be_ref[b] != prev))
        def _():
            wg_s[...] = wg_ref[...].astype(BF16)
            wu_s[...] = wu_ref[...].astype(BF16)
            wd_s[...] = wd_ref[...].astype(BF16)

        x = x_ref[...].astype(BF16)
        gate = _dot(x, wg_s[...])
        h = gate * _sigmoid(gate) * _dot(x, wu_s[...])
        o_ref[...] = _dot(h.astype(BF16), wd_s[...])

    @pl.when(b >= nu_ref[0])
    def _():
        o_ref[...] = jnp.zeros(o_ref.shape, F32)


def _experts(block_e, n_used, xs, wg, wu, wd, bm):
    rows = xs.shape[0]
    nb = rows // bm
    grid_spec = pltpu.PrefetchScalarGridSpec(
        num_scalar_prefetch=2,
        grid=(nb,),
        in_specs=[pl.BlockSpec((bm, D_MODEL), lambda b, be, nu: (b, 0)),
                  pl.BlockSpec((None, D_MODEL, D_EXPERT), lambda b, be, nu: (be[b], 0, 0)),
                  pl.BlockSpec((None, D_MODEL, D_EXPERT), lambda b, be, nu: (be[b], 0, 0)),
                  pl.BlockSpec((None, D_EXPERT, D_MODEL), lambda b, be, nu: (be[b], 0, 0))],
        out_specs=pl.BlockSpec((bm, D_MODEL), lambda b, be, nu: (b, 0)),
        scratch_shapes=[pltpu.VMEM((D_MODEL, D_EXPERT), BF16), pltpu.VMEM((D_MODEL, D_EXPERT), BF16),
                        pltpu.VMEM((D_EXPERT, D_MODEL), BF16)])
    return pl.pallas_call(
        _expert_kernel,
        grid_spec=grid_spec,
        out_shape=jax.ShapeDtypeStruct((rows, D_MODEL), F32),
        compiler_params=_params(("arbitrary",)),
        name="experts",
    )(block_e, n_used, xs, wg, wu, wd)


def _combine_kernel(dest_ref, gates_ref, x_ref, ys_ref, o_ref, buf, sem, *, tc):
    def issue(t, carry):
        for s in range(2):
            _row_copy(ys_ref, buf.at[s], dest_ref[0, 0, 2 * t + s], t, sem).start()
        return carry

    lax.fori_loop(0, tc, issue, 0, unroll=8)
    for s in range(2):
        pltpu.make_async_copy(ys_ref.at[pl.ds(0, tc)], buf.at[s], sem).wait()
    gates = gates_ref[...]
    o_ref[...] = x_ref[...] + gates[:, 0:1] * buf[0] + gates[:, 1:2] * buf[1]


def _combine(dest, gates, x1, ys, tc):
    n = x1.shape[0]
    return pl.pallas_call(
        functools.partial(_combine_kernel, tc=tc),
        grid=(n // tc,),
        in_specs=[pl.BlockSpec((1, 1, 2 * tc), lambda i: (i, 0, 0), memory_space=pltpu.SMEM),
                  pl.BlockSpec((tc, LANES), lambda i: (i, 0)),
                  pl.BlockSpec((tc, D_MODEL), lambda i: (i, 0)),
                  pl.BlockSpec(memory_space=pl.ANY)],
        out_specs=pl.BlockSpec((tc, D_MODEL), lambda i: (i, 0)),
        out_shape=jax.ShapeDtypeStruct((n, D_MODEL), F32),
        scratch_shapes=[pltpu.VMEM((2, tc, D_MODEL), F32), pltpu.SemaphoreType.DMA(())],
        compiler_params=_params(("arbitrary",), disable_bounds_checks=True),
        name="combine",
    )(dest.reshape(n // tc, 1, 2 * tc), gates, x1, ys)


def _moe(x1, p, tm, bm):
    n = x1.shape[0]
    hn, ids, gates = _router(x1, p['norm_ffn'], p['wr_hi'], p['wr_lo'], p['b_router'], tm)
    flat_e = ids[:, :2].reshape(-1)
    m = flat_e.shape[0]
    onehot = (flat_e[:, None] == jnp.arange(N_EXPERTS, dtype=I32)[None, :]).astype(I32)
    csum = jnp.cumsum(onehot, axis=0)
    counts = csum[-1]
    padded = (counts + bm - 1) // bm * bm
    pad_end = jnp.cumsum(padded)
    pad_start = pad_end - padded
    dest = jnp.sum(onehot * (csum - 1 + pad_start[None, :]), axis=1).astype(I32)
    nb = -(-(m + N_EXPERTS * (bm - 1)) // bm)
    block_start = jnp.arange(nb, dtype=I32) * bm
    block_e = jnp.minimum(jnp.sum((pad_end[None, :] <= block_start[:, None]).astype(I32), axis=1),
                          N_EXPERTS - 1).astype(I32)
    n_used = (pad_end[-1] // bm).astype(I32).reshape(1)
    xs0 = jnp.zeros((nb * bm, D_MODEL), F32)
    xs = _dispatch(dest, hn, xs0, min(m, 1024))
    ys = _experts(block_e, n_used, xs, p['w_gate'], p['w_up'], p['w_down'], bm)
    return _combine(dest, gates, x1, ys, min(n, 256))


def _prep(norm_mix, w_in, s5_a_re, s5_a_im, s5_log_dt, s5_b_re, s5_b_im, s5_c_re, s5_c_im, s5_d,
          w_glu, q_norm, k_norm, idx_k_norm, w_branch_a, w_branch_b, w_out, norm_ffn,
          w_router_group, b_router_group, w_router_expert, b_router_expert, w_gate, w_up, w_down):
    pts = []
    acc = 0
    for s in IN_SIZES[:-1]:
        acc += s
        pts.append(acc)
    w_u, w_q, w_k, w_v, w_qi, w_ki, w_wi, w_ga, w_gb = jnp.split(w_in, pts, axis=1)
    pad = PROJ_COLS - sum(IN_SIZES)
    w_proj = jnp.concatenate([w_u, w_q, w_ga, w_gb, w_qi, w_k, w_v, w_ki, w_wi,
                              jnp.zeros((D_MODEL, pad), w_in.dtype)], axis=1).astype(BF16)
    one = lambda k: jnp.ones((k,), F32)
    gain = jnp.concatenate([one(S5_WIDTH), jnp.tile(q_norm.astype(F32), N_HEADS), one(2 * D_MODEL),
                            one(N_IDX_HEADS * IDX_DIM), jnp.tile(k_norm.astype(F32), N_KV_HEADS), one(KV_WIDTH),
                            idx_k_norm.astype(F32), one(N_IDX_HEADS + pad)]).reshape(1, PROJ_COLS)
    half = 256
    wv, wg = w_glu[:, :S5_WIDTH], w_glu[:, S5_WIDTH:]
    w_glu_p = jnp.concatenate(
        [jnp.concatenate([wv[:, c * half:(c + 1) * half], wg[:, c * half:(c + 1) * half]], axis=1)
         for c in range(S5_WIDTH // half)], axis=1).astype(BF16)
    w_r = jnp.concatenate([w_router_expert.astype(F32), w_router_group.astype(F32),
                           jnp.zeros((D_MODEL, LANES - N_EXPERTS - N_GROUPS), F32)], axis=1)
    wr_hi = w_r.astype(BF16)
    wr_lo = (w_r - wr_hi.astype(F32)).astype(BF16)
    b_r = jnp.concatenate([b_router_expert.astype(F32), b_router_group.astype(F32),
                           jnp.zeros((LANES - N_EXPERTS - N_GROUPS,), F32)]).reshape(1, LANES)
    return dict(norm_mix=norm_mix.astype(F32).reshape(1, D_MODEL), w_proj=w_proj, gain=gain,
                s5=(s5_a_re, s5_a_im, s5_log_dt, s5_b_re, s5_b_im, s5_c_re, s5_c_im),
                s5_d=s5_d.astype(F32).reshape(1, S5_WIDTH), w_glu=w_glu_p,
                w_a=w_branch_a.astype(BF16), w_b=w_branch_b.astype(BF16), w_out=w_out.astype(BF16),
                norm_ffn=norm_ffn.astype(F32).reshape(1, D_MODEL), wr_hi=wr_hi, wr_lo=wr_lo, b_router=b_r,
                w_gate=w_gate, w_up=w_up, w_down=w_down)


def _layer(x, h0_re, h0_im, k_past, v_past, ki_past, p, *, tm, lc, tt, tq, kb, bm):
    bsz, t, _ = x.shape
    n = bsz * t
    x2 = x.reshape(n, D_MODEL)
    proj = _inproj(x2, p['norm_mix'], p['w_proj'], p['gain'], tm)
    proj3 = proj.reshape(bsz, t, PROJ_COLS)
    k = proj3[:, :, COL_K:COL_K + KV_WIDTH]
    v = proj3[:, :, COL_V:COL_V + KV_WIDTH]
    ki = proj3[:, :, COL_KI:COL_KI + IDX_DIM]
    n_state = S5_GROUPS * S5_STATE
    tables = _s5_tables(*p['s5'], lc)
    yg, s_re, s_im = _s5(proj3, h0_re.reshape(bsz, 1, n_state), h0_im.reshape(bsz, 1, n_state),
                         tables, p['s5_d'], lc, tt)
    y_a = _glu(yg.reshape(n, S5_WIDTH), p['w_glu'], tm)
    if k_past is None:
        pos0 = 0
        k_all, v_all, ki_all = k, v, ki
    else:
        pos0 = k_past.shape[1]
        k_all = jnp.concatenate([k_past.reshape(bsz, pos0, KV_WIDTH), k], axis=1)
        v_all = jnp.concatenate([v_past.reshape(bsz, pos0, KV_WIDTH), v], axis=1)
        ki_all = jnp.concatenate([ki_past, ki], axis=1)
    n_keys = k_all.shape[1]
    n_top = min(TOP_K_MAX, n_keys // 4)
    nkp = -(-n_keys // kb) * kb
    padk = lambda a: jnp.pad(a.astype(BF16), ((0, 0), (0, nkp - n_keys), (0, 0)))
    kit = jnp.swapaxes(padk(ki_all), 1, 2)
    kt = jnp.swapaxes(padk(k_all), 1, 2)
    y_b = _dsa(proj3, kit, kt, padk(v_all), tq=tq, kb=kb, pos0=pos0, n_keys=n_keys, n_top=n_top)
    merged = _merge(y_a, y_b.reshape(n, ATT_WIDTH), p['w_a'], p['w_b'], proj, tm)
    x1 = _outproj(merged, p['w_out'], x2, tm)
    y = _moe(x1, p, tm, bm)
    return (y.reshape(bsz, t, D_MODEL), s_re.reshape(bsz, S5_GROUPS, S5_STATE),
            s_im.reshape(bsz, S5_GROUPS, S5_STATE), k.reshape(bsz, t, N_KV_HEADS, HEAD_DIM),
            v.reshape(bsz, t, N_KV_HEADS, HEAD_DIM), ki)


def kernel(x_prompt, x_sample, state_s5_re, state_s5_im, cache_k, cache_v, cache_idx_k, norm_mix, w_in, s5_a_re, s5_a_im, s5_log_dt, s5_b_re, s5_b_im, s5_c_re, s5_c_im, s5_d, w_glu, q_norm, k_norm, idx_k_norm, w_branch_a, w_branch_b, w_out, norm_ffn, w_router_group, b_router_group, w_router_expert, b_router_expert, w_gate, w_up, w_down):
    p = _prep(norm_mix, w_in, s5_a_re, s5_a_im, s5_log_dt, s5_b_re, s5_b_im, s5_c_re, s5_c_im, s5_d,
              w_glu, q_norm, k_norm, idx_k_norm, w_branch_a, w_branch_b, w_out, norm_ffn,
              w_router_group, b_router_group, w_router_expert, b_router_expert, w_gate, w_up, w_down)
    h0 = jnp.zeros((x_prompt.shape[0], S5_GROUPS, S5_STATE), F32)
    yp, srp, sip, kp, vp, kip = _layer(x_prompt, h0, h0, None, None, None, p,
                                       tm=1024, lc=64, tt=1024, tq=128, kb=512, bm=256)
    ys, srs, sis, ks, vs, kis = _layer(x_sample, state_s5_re, state_s5_im, cache_k, cache_v, cache_idx_k, p,
                                       tm=256, lc=32, tt=32, tq=32, kb=384, bm=64)
    return (yp, ys, srp, sip, kp, vp, kip, srs, sis, ks, vs, kis)
```

```python
import functools
import math

import jax
import jax.numpy as jnp
from jax import lax
from jax.experimental import pallas as pl
from jax.experimental.pallas import tpu as pltpu

F32 = jnp.float32
BF16 = jnp.bfloat16
I32 = jnp.int32

D_MODEL = 2048
CHUNK = 64
EPS = 1e-6
S5_WIDTH = 1024
S5_GROUP = 16
S5_GROUPS = 64
S5_STATE = 64
S5_MAX_RE = -1e-4
N_HEADS = 8
N_KV_HEADS = 2
Q_PER_KV = 4
HEAD_DIM = 128
ATT_WIDTH = 1024
KV_WIDTH = 256
N_IDX_HEADS = 8
IDX_DIM = 64
TOP_K_MAX = 256
N_GROUPS = 4
EXPERTS_PER_GROUP = 8
N_EXPERTS = 32
D_EXPERT = 512
IN_SIZES = (S5_WIDTH, ATT_WIDTH, KV_WIDTH, KV_WIDTH, N_IDX_HEADS * IDX_DIM, IDX_DIM, N_IDX_HEADS, D_MODEL, D_MODEL)

LANES = 128
PROJ_TN = 512
COL_U, COL_Q, COL_GA, COL_GB, COL_QI, COL_K, COL_V, COL_KI, COL_WI = 0, 1024, 2048, 4096, 6144, 6656, 6912, 7168, 7232
PROJ_COLS = 7680
S5_LANE_BLOCKS = S5_WIDTH // LANES
S5_BLOCK_STATE = (LANES // S5_GROUP) * S5_STATE
S5_MAX_CHUNK = 64
VMEM_LIMIT = 56 * 1024 * 1024
INT_MIN = -2 ** 31
KEY_LOWEST_FINITE = -2 ** 31 + 0x00800000
NEG_BIG = -1e30
LOG2E = 1.4426950408889634
FAST_SOFTMAX_BOUND = 40.0
TOPK_PER_LANE = 12


def _dot(a, b):
    return jnp.dot(a, b, preferred_element_type=F32)


def _split_bf16(x):
    hi = x.astype(BF16)
    lo = (x - hi.astype(F32)).astype(BF16)
    return hi, lo


def _sigmoid(x):
    return 1.0 / (1.0 + jnp.exp(-x))


def _params(sem, **kw):
    return pltpu.CompilerParams(dimension_semantics=sem, vmem_limit_bytes=VMEM_LIMIT, **kw)


def _group_norm(a, gain):
    ms = jnp.mean(a * a, axis=-1, keepdims=True)
    return a * lax.rsqrt(ms + EPS) * gain


def _inproj_kernel(x_ref, g_ref, w_ref, gain_ref, o_ref, xn_ref):
    j = pl.program_id(1)

    @pl.when(j == 0)
    def _():
        xf = x_ref[...]
        ms = jnp.mean(xf * xf, axis=-1, keepdims=True)
        xn_ref[...] = (xf * lax.rsqrt(ms + EPS) * g_ref[...]).astype(BF16)

    acc = _dot(xn_ref[...], w_ref[...])
    gain = gain_ref[...]
    groups = [slice(c * LANES, (c + 1) * LANES) for c in range(PROJ_TN // LANES)]

    @pl.when((j < 2) | (j == 12))
    def _():
        o_ref[...] = acc

    @pl.when((j == 2) | (j == 3))
    def _():
        for s in groups:
            o_ref[:, s] = _group_norm(acc[:, s], gain[:, s])

    @pl.when((j >= 4) & (j < 12))
    def _():
        o_ref[...] = _sigmoid(acc)

    @pl.when(j == 13)
    def _():
        for s in groups[:2]:
            o_ref[:, s] = _group_norm(acc[:, s], gain[:, s])
        o_ref[:, 2 * LANES:] = acc[:, 2 * LANES:]

    @pl.when(j == 14)
    def _():
        a = acc[:, :LANES]
        lane = lax.broadcasted_iota(I32, a.shape, 1)
        is_ki = lane < IDX_DIM
        ms = jnp.sum(jnp.where(is_ki, a * a, 0.0), axis=-1, keepdims=True) * (1.0 / IDX_DIM)
        ki = a * lax.rsqrt(ms + EPS) * gain[:, :LANES]
        wi = a * (N_IDX_HEADS ** -0.5) * (IDX_DIM ** -0.5)
        o_ref[:, :LANES] = jnp.where(is_ki, ki, wi)
        o_ref[:, LANES:] = acc[:, LANES:]


def _inproj(x, g, w, gain, tm):
    n = x.shape[0]
    return pl.pallas_call(
        _inproj_kernel,
        grid=(n // tm, PROJ_COLS // PROJ_TN),
        in_specs=[pl.BlockSpec((tm, D_MODEL), lambda i, j: (i, 0)),
                  pl.BlockSpec((1, D_MODEL), lambda i, j: (0, 0)),
                  pl.BlockSpec((D_MODEL, PROJ_TN), lambda i, j: (0, j)),
                  pl.BlockSpec((1, PROJ_TN), lambda i, j: (0, j))],
        out_specs=pl.BlockSpec((tm, PROJ_TN), lambda i, j: (i, j)),
        out_shape=jax.ShapeDtypeStruct((n, PROJ_COLS), F32),
        scratch_shapes=[pltpu.VMEM((tm, D_MODEL), BF16)],
        compiler_params=_params(("arbitrary", "arbitrary")),
        name="inproj",
    )(x, g, w, gain)


def _gelu_tanh(y):
    return 0.5 * y * (1.0 + jnp.tanh(math.sqrt(2.0 / math.pi) * (y + 0.044715 * (y * y * y))))


def _s5_kernel(u_ref, h0r_ref, h0i_ref, bhi_ref, blo_ref, air_ref, aii_ref, apr_ref, api_ref,
               a1r_ref, a1i_ref, tri_ref, cd_ref, d_ref, yg_ref, sr_ref, si_ref,
               bu_scr, h_scr, hr_scr, hi_scr, *, lc, tt):
    t = pl.program_id(2)
    ns = S5_BLOCK_STATE

    @pl.when(t == 0)
    def _():
        hr_scr[...] = h0r_ref[...]
        hi_scr[...] = h0i_ref[...]

    u = u_ref[...]
    u_hi, u_lo = _split_bf16(u)
    bhi = bhi_ref[...]
    bu_scr[...] = _dot(u_hi, bhi) + _dot(jnp.concatenate([u_lo, u_hi], axis=1), blo_ref[...])
    tri = tri_ref[...]
    air, aii, apr, api = air_ref[...], aii_ref[...], apr_ref[...], api_ref[...]
    a1r, a1i = a1r_ref[...], a1i_ref[...]

    def chunk(s, carry):
        h_re, h_im = carry
        r0 = pl.multiple_of(s * lc, lc)
        br = bu_scr[pl.ds(r0, lc), 0:ns]
        bi = bu_scr[pl.ds(r0, lc), ns:2 * ns]
        z = jnp.concatenate([air * br - aii * bi, air * bi + aii * br], axis=1)
        z_hi, z_lo = _split_bf16(z)
        c = _dot(tri, jnp.concatenate([z_hi, z_lo], axis=0))
        cr = c[:, 0:ns] + (a1r * h_re - a1i * h_im)
        ci = c[:, ns:2 * ns] + (a1r * h_im + a1i * h_re)
        hr = apr * cr - api * ci
        hi = apr * ci + api * cr
        h_scr[pl.ds(r0, lc), 0:ns] = hr.astype(BF16)
        h_scr[pl.ds(r0, lc), ns:2 * ns] = hi.astype(BF16)
        return hr[lc - 1:lc, :], hi[lc - 1:lc, :]

    h_re, h_im = lax.fori_loop(0, tt // lc, chunk, (hr_scr[...], hi_scr[...]), unroll=min(4, tt // lc))
    hr_scr[...] = h_re
    hi_scr[...] = h_im
    sr_ref[...] = h_re
    si_ref[...] = h_im
    y = _dot(h_scr[...], cd_ref[...]) + d_ref[...] * u
    yg_ref[...] = _gelu_tanh(y).astype(BF16)


def _s5_tables(a_re, a_im, log_dt, b_re, b_im, c_re, c_im, lc):
    lr = jnp.minimum(a_re.astype(F32), S5_MAX_RE)
    li = a_im.astype(F32)
    dt = jnp.exp(log_dt.astype(F32))[:, None]
    mag = jnp.exp(lr * dt)
    lbr = mag * jnp.cos(li * dt)
    lbi = mag * jnp.sin(li * dt)
    den = lr * lr + li * li
    fr = ((lbr - 1.0) * lr + lbi * li) / den
    fi = (lbi * lr - (lbr - 1.0) * li) / den
    br = b_re.astype(F32)
    bi = b_im.astype(F32)
    bbr = fr[..., None] * br - fi[..., None] * bi
    bbi = fr[..., None] * bi + fi[..., None] * br
    j = jnp.arange(lc, dtype=F32)[:, None, None]
    ang = j * (li * dt)[None]
    lmag = j * (lr * dt)[None]
    n_state = S5_GROUPS * S5_STATE
    apr = (jnp.exp(lmag) * jnp.cos(ang)).reshape(lc, n_state)
    api = (jnp.exp(lmag) * jnp.sin(ang)).reshape(lc, n_state)
    air = (jnp.exp(-lmag) * jnp.cos(ang)).reshape(lc, n_state)
    aii = (-jnp.exp(-lmag) * jnp.sin(ang)).reshape(lc, n_state)
    a1r = lbr.reshape(1, n_state)
    a1i = lbi.reshape(1, n_state)
    gpb = LANES // S5_GROUP
    eye = jnp.eye(gpb, dtype=F32)

    def bdiag(b):
        return jnp.einsum('kgpc,gh->kgchp', b.reshape(S5_LANE_BLOCKS, gpb, S5_STATE, S5_GROUP), eye).reshape(
            S5_LANE_BLOCKS, LANES, S5_BLOCK_STATE)

    def cdiag(c):
        return jnp.einsum('kgcp,gh->kgphc', c.reshape(S5_LANE_BLOCKS, gpb, S5_GROUP, S5_STATE), eye).reshape(
            S5_LANE_BLOCKS, S5_BLOCK_STATE, LANES)

    bd = jnp.concatenate([bdiag(bbr), bdiag(bbi)], axis=-1)
    bd_hi = bd.astype(BF16)
    bd_lo = (bd - bd_hi.astype(F32)).astype(BF16)
    cd = jnp.concatenate([cdiag(c_re.astype(F32)), -cdiag(c_im.astype(F32))], axis=1).astype(BF16)
    tri = jnp.tril(jnp.ones((lc, lc), F32)).astype(BF16)
    bd_lo = jnp.concatenate([bd_hi, bd_lo], axis=1)
    return dict(bd_hi=bd_hi, bd_lo=bd_lo, air=air, aii=aii, apr=apr, api=api, a1r=a1r, a1i=a1i, tri=tri, cd=cd)


def _s5(proj3, h0_re, h0_im, tb, dvec, lc, tt):
    bsz, t, _ = proj3.shape
    ns = S5_BLOCK_STATE
    n_state = S5_GROUPS * S5_STATE
    tab = lambda: pl.BlockSpec((lc, ns), lambda b, k, i: (0, k))
    row = lambda: pl.BlockSpec((1, ns), lambda b, k, i: (0, k))
    st = lambda: pl.BlockSpec((None, 1, ns), lambda b, k, i: (b, 0, k))
    return pl.pallas_call(
        functools.partial(_s5_kernel, lc=lc, tt=tt),
        grid=(bsz, S5_LANE_BLOCKS, t // tt),
        in_specs=[pl.BlockSpec((None, tt, LANES), lambda b, k, i: (b, i, k)),
                  st(), st(),
                  pl.BlockSpec((None, LANES, 2 * ns), lambda b, k, i: (k, 0, 0)),
                  pl.BlockSpec((None, 2 * LANES, 2 * ns), lambda b, k, i: (k, 0, 0)),
                  tab(), tab(), tab(), tab(), row(), row(),
                  pl.BlockSpec((lc, 2 * lc), lambda b, k, i: (0, 0)),
                  pl.BlockSpec((None, 2 * ns, LANES), lambda b, k, i: (k, 0, 0)),
                  pl.BlockSpec((1, LANES), lambda b, k, i: (0, k))],
        out_specs=[pl.BlockSpec((None, tt, LANES), lambda b, k, i: (b, i, k)), st(), st()],
        out_shape=[jax.ShapeDtypeStruct((bsz, t, S5_WIDTH), BF16),
                   jax.ShapeDtypeStruct((bsz, 1, n_state), F32),
                   jax.ShapeDtypeStruct((bsz, 1, n_state), F32)],
        scratch_shapes=[pltpu.VMEM((tt, 2 * ns), F32), pltpu.VMEM((tt, 2 * ns), BF16),
                        pltpu.VMEM((1, ns), F32), pltpu.VMEM((1, ns), F32)],
        compiler_params=_params(("arbitrary", "arbitrary", "arbitrary")),
        name="s5",
    )(proj3, h0_re, h0_im, tb['bd_hi'], tb['bd_lo'], tb['air'], tb['aii'], tb['apr'], tb['api'],
      tb['a1r'], tb['a1i'], tb['tri'], tb['cd'], dvec)


def _glu_kernel(y_ref, w_ref, o_ref):
    acc = _dot(y_ref[...], w_ref[...])
    half = acc.shape[1] // 2
    o_ref[...] = (acc[:, :half] * _sigmoid(acc[:, half:])).astype(BF16)


def _glu(yg, w, tm):
    n = yg.shape[0]
    tn = 512
    return pl.pallas_call(
        _glu_kernel,
        grid=(n // tm, 2 * S5_WIDTH // tn),
        in_specs=[pl.BlockSpec((tm, S5_WIDTH), lambda i, j: (i, 0)),
                  pl.BlockSpec((S5_WIDTH, tn), lambda i, j: (0, j))],
        out_specs=pl.BlockSpec((tm, tn // 2), lambda i, j: (i, j)),
        out_shape=jax.ShapeDtypeStruct((n, S5_WIDTH), BF16),
        compiler_params=_params(("arbitrary", "arbitrary")),
        name="glu",
    )(yg, w)


def _dsa_kernel(q_ref, qi_ref, kiw_ref, kit_ref, kt_ref, v_ref, o_ref,
                key_scr, qs_scr, qis_scr, wis_scr, m_scr, l_scr, acc_scr,
                kmax_scr, lsum_scr, cand_scr, ckey_scr, thr_scr,
                *, tq, kb, pos0, n_keys, n_top, nt):
    i = pl.program_id(1)
    nkp = key_scr.shape[2]
    n_cand = cand_scr.shape[1]
    slot_s = i % 2
    slot_a = 1 - slot_s
    tile_s = jnp.minimum(i, nt - 1)
    tile_a = jnp.maximum(i - 1, 0)

    def n_blocks(tile):
        k_end = jnp.minimum(n_keys, (pos0 + tile * tq + tq - 1) // CHUNK * CHUNK + CHUNK)
        return (k_end + kb - 1) // kb

    nkb_s = jnp.where(i < nt, n_blocks(tile_s), 0)
    nkb_a = jnp.where(i >= 1, n_blocks(tile_a), 0)

    @pl.when(i == 0)
    def _():
        def norm_block(b, carry):
            c0 = pl.multiple_of(b * kb, kb)
            out = []
            for g in range(N_KV_HEADS):
                kk = kt_ref[g * HEAD_DIM:(g + 1) * HEAD_DIM, pl.ds(c0, kb)].astype(F32)
                out.append(jnp.maximum(carry[g], jnp.sum(kk * kk, axis=0, keepdims=True)))
            return tuple(out)

        res = lax.fori_loop(0, nkp // kb, norm_block, (jnp.zeros((1, kb), F32),) * N_KV_HEADS)
        for g in range(N_KV_HEADS):
            kmax_scr[g] = jnp.broadcast_to(jnp.max(res[g], axis=-1, keepdims=True), (1, LANES))

    q = q_ref[...] * (HEAD_DIM ** -0.5 * LOG2E)
    for h in range(N_HEADS):
        g, r = divmod(h, Q_PER_KV)
        qs_scr[g, r * tq:(r + 1) * tq, :] = q[:, h * HEAD_DIM:(h + 1) * HEAD_DIM].astype(BF16)
    qi = qi_ref[...]
    kiw = kiw_ref[...]
    for h in range(N_IDX_HEADS):
        qis_scr[h * tq:(h + 1) * tq, :] = qi[:, h * IDX_DIM:(h + 1) * IDX_DIM].astype(BF16)
        wis_scr[h * tq:(h + 1) * tq, :] = kiw[:, IDX_DIM + h:IDX_DIM + h + 1]

    q_chunk = (pos0 + tile_s * tq + lax.broadcasted_iota(I32, (tq, 1), 0)) // CHUNK

    def to_key(x):
        bits = pltpu.bitcast(x, I32)
        return jnp.where(bits < 0, bits ^ 0x7FFFFFFF, bits)

    cand_scr[slot_s] = jnp.full(cand_scr.shape[1:], -jnp.inf, F32)

    def score_block(b):
        c0 = pl.multiple_of(b * kb, kb)
        s = _dot(qis_scr[...], kit_ref[:, pl.ds(c0, kb)])
        s = jnp.maximum(s, 0.0) * wis_scr[...]
        sc = s[0:tq]
        for h in range(1, N_IDX_HEADS):
            sc = sc + s[h * tq:(h + 1) * tq]
        k_pos = c0 + lax.broadcasted_iota(I32, (1, kb), 1)
        adm = (k_pos // CHUNK <= q_chunk) & (k_pos < n_keys)
        blk = jnp.where(adm, sc + 0.0, -jnp.inf)
        key_scr[slot_s, :, pl.ds(c0, kb)] = blk
        for rg in range(tq // 8):
            rows = slice(rg * 8, rg * 8 + 8)
            top = [cand_scr[slot_s, j, rows, :] for j in range(n_cand)]
            for c in range(kb // LANES):
                x = blk[rows, c * LANES:(c + 1) * LANES]
                for j in range(n_cand):
                    hi = jnp.maximum(top[j], x)
                    x = jnp.minimum(top[j], x)
                    top[j] = hi
            for j in range(n_cand):
                cand_scr[slot_s, j, rows, :] = top[j]

    def count_all(hit_fn):
        def body(b, acc):
            c0 = pl.multiple_of(b * kb, kb)
            hit = hit_fn(key_scr[slot_a, :, pl.ds(c0, kb)])
            for c in range(kb // LANES):
                acc = acc + hit[:, c * LANES:(c + 1) * LANES]
            return acc
        acc = lax.fori_loop(0, nkb_a, body, jnp.zeros((tq, LANES), F32))
        return jnp.sum(acc, axis=-1, keepdims=True)

    def count_cand(cand):
        acc = jnp.zeros((tq, LANES), F32)
        for j in range(n_cand):
            acc = acc + jnp.where(ckey_scr[j] >= cand, 1.0, 0.0)
        return jnp.sum(acc, axis=-1, keepdims=True)

    def bisect(count_fn):
        def bit_pass(it, t_off):
            cand_off = t_off | lax.shift_left(jnp.int32(1), 31 - it)
            cnt = count_fn(cand_off ^ INT_MIN)
            return jnp.where(cnt >= n_top, cand_off, t_off)
        t_off = lax.fori_loop(0, 32, bit_pass, jnp.zeros((tq, 1), I32))
        key = jnp.maximum(t_off ^ INT_MIN, KEY_LOWEST_FINITE)
        return pltpu.bitcast(jnp.where(key < 0, key ^ 0x7FFFFFFF, key), F32)

    @pl.when(i >= 1)
    def _():
        for j in range(n_cand):
            ckey_scr[j] = to_key(cand_scr[slot_a, j])
        thr_cand = bisect(count_cand)
        thr_scr[...] = thr_cand
        over = jnp.max(count_all(lambda x: jnp.where(x >= thr_cand, 1.0, 0.0))) > n_top

        @pl.when(over)
        def _():
            thr_scr[...] = bisect(lambda cand: count_all(lambda x: jnp.where(to_key(x) >= cand, 1.0, 0.0)))

    thr = thr_scr[...]

    acc_scr[...] = jnp.zeros(acc_scr.shape, F32)
    lsum_scr[...] = jnp.zeros(lsum_scr.shape, F32)
    m_scr[...] = jnp.full(m_scr.shape, NEG_BIG, F32)
    l_scr[...] = jnp.zeros(l_scr.shape, F32)
    bound_max = jnp.float32(0.0)
    for g in range(N_KV_HEADS):
        qg = qs_scr[g].astype(F32)
        qn2 = jnp.sum(qg * qg, axis=-1, keepdims=True)
        bound = jnp.sqrt(qn2 * kmax_scr[g][:, 0:1]) * 1.002
        bound_max = jnp.maximum(bound_max, jnp.max(bound))
    fast = bound_max <= FAST_SOFTMAX_BOUND

    def attend_fast(b):
        c0 = pl.multiple_of(b * kb, kb)
        bias = jnp.where(key_scr[slot_a, :, pl.ds(c0, kb)] >= thr, 0.0, NEG_BIG)
        for g in range(N_KV_HEADS):
            lg = _dot(qs_scr[g], kt_ref[g * HEAD_DIM:(g + 1) * HEAD_DIM, pl.ds(c0, kb)])
            parts = []
            for r in range(Q_PER_KV):
                rows = slice(r * tq, (r + 1) * tq)
                e = [jnp.exp2(lg[rows, c * LANES:(c + 1) * LANES] + bias[:, c * LANES:(c + 1) * LANES])
                     for c in range(kb // LANES)]
                ls = e[0]
                for c in range(1, kb // LANES):
                    ls = ls + e[c]
                lsum_scr[g, rows, :] += ls
                parts.append(jnp.concatenate(e, axis=1).astype(BF16))
            p = jnp.concatenate(parts, axis=0)
            acc_scr[g] += _dot(p, v_ref[pl.ds(c0, kb), g * HEAD_DIM:(g + 1) * HEAD_DIM])

    def attend_general(b):
        c0 = pl.multiple_of(b * kb, kb)
        sel = key_scr[slot_a, :, pl.ds(c0, kb)] >= thr
        sel4 = jnp.concatenate([sel] * Q_PER_KV, axis=0)
        for g in range(N_KV_HEADS):
            lg = _dot(qs_scr[g], kt_ref[g * HEAD_DIM:(g + 1) * HEAD_DIM, pl.ds(c0, kb)])
            lg = jnp.where(sel4, lg, NEG_BIG)
            m_old = m_scr[g]
            m_new = jnp.maximum(m_old, jnp.max(lg, axis=-1, keepdims=True))
            p = jnp.exp2(lg - m_new)
            alpha = jnp.exp2(m_old - m_new)
            l_scr[g] = alpha * l_scr[g] + jnp.sum(p, axis=-1, keepdims=True)
            pv = _dot(p.astype(BF16), v_ref[pl.ds(c0, kb), g * HEAD_DIM:(g + 1) * HEAD_DIM])
            acc_scr[g] = alpha * acc_scr[g] + pv
            m_scr[g] = m_new

    def loop(lo, hi, *fns):
        def body(b, carry):
            for fn in fns:
                fn(b)
            return carry
        lax.fori_loop(lo, hi, body, 0)

    n_fused = jnp.where(fast, jnp.minimum(nkb_a, nkb_s), 0)
    loop(0, n_fused, score_block, attend_fast)
    loop(n_fused, nkb_s, score_block)
    loop(n_fused, jnp.where(fast, nkb_a, 0), attend_fast)
    loop(0, jnp.where(fast, 0, nkb_a), attend_general)

    @pl.when((i >= 1) & fast)
    def _():
        for h in range(N_HEADS):
            g, r = divmod(h, Q_PER_KV)
            rows = slice(r * tq, (r + 1) * tq)
            l = jnp.sum(lsum_scr[g, rows, :], axis=-1, keepdims=True)
            o_ref[:, h * HEAD_DIM:(h + 1) * HEAD_DIM] = (acc_scr[g, rows, :] / l).astype(BF16)

    @pl.when((i >= 1) & jnp.logical_not(fast))
    def _():
        for h in range(N_HEADS):
            g, r = divmod(h, Q_PER_KV)
            rows = slice(r * tq, (r + 1) * tq)
            o_ref[:, h * HEAD_DIM:(h + 1) * HEAD_DIM] = (acc_scr[g, rows, :] / l_scr[g, rows, :]).astype(BF16)


def _dsa(proj3, kit, kt, v, *, tq, kb, pos0, n_keys, n_top):
    bsz, t, _ = proj3.shape
    nkp = kit.shape[-1]
    nt = t // tq
    kern = functools.partial(_dsa_kernel, tq=tq, kb=kb, pos0=pos0, n_keys=n_keys, n_top=n_top, nt=nt)
    resident = dict(pipeline_mode=pl.Buffered(1))
    prev = lambda i: jnp.maximum(i - 1, 0)
    cur = lambda i: jnp.minimum(i, nt - 1)
    return pl.pallas_call(
        kern,
        grid=(bsz, nt + 1),
        in_specs=[pl.BlockSpec((None, tq, ATT_WIDTH), lambda b, i: (b, prev(i), COL_Q // ATT_WIDTH)),
                  pl.BlockSpec((None, tq, 512), lambda b, i: (b, cur(i), COL_QI // 512)),
                  pl.BlockSpec((None, tq, 512), lambda b, i: (b, cur(i), COL_KI // 512)),
                  pl.BlockSpec((None, IDX_DIM, nkp), lambda b, i: (b, 0, 0), **resident),
                  pl.BlockSpec((None, KV_WIDTH, nkp), lambda b, i: (b, 0, 0), **resident),
                  pl.BlockSpec((None, nkp, KV_WIDTH), lambda b, i: (b, 0, 0), **resident)],
        out_specs=pl.BlockSpec((None, tq, ATT_WIDTH), lambda b, i: (b, prev(i), 0)),
        out_shape=jax.ShapeDtypeStruct((bsz, t, ATT_WIDTH), BF16),
        scratch_shapes=[pltpu.VMEM((2, tq, nkp), F32),
                        pltpu.VMEM((N_KV_HEADS, Q_PER_KV * tq, HEAD_DIM), BF16),
                        pltpu.VMEM((N_IDX_HEADS * tq, IDX_DIM), BF16),
                        pltpu.VMEM((N_IDX_HEADS * tq, 1), F32),
                        pltpu.VMEM((N_KV_HEADS, Q_PER_KV * tq, 1), F32),
                        pltpu.VMEM((N_KV_HEADS, Q_PER_KV * tq, 1), F32),
                        pltpu.VMEM((N_KV_HEADS, Q_PER_KV * tq, HEAD_DIM), F32),
                        pltpu.VMEM((N_KV_HEADS, 1, LANES), F32),
                        pltpu.VMEM((N_KV_HEADS, Q_PER_KV * tq, LANES), F32),
                        pltpu.VMEM((2, TOPK_PER_LANE, tq, LANES), F32),
                        pltpu.VMEM((TOPK_PER_LANE, tq, LANES), I32),
                        pltpu.VMEM((tq, 1), F32)],
        compiler_params=_params(("arbitrary", "arbitrary")),
        name="dsa",
    )(proj3, proj3, proj3, kit, kt, v)


def _merge_kernel(ya_ref, yb_ref, wa_ref, wb_ref, ga_ref, gb_ref, o_ref):
    o_ref[...] = (ga_ref[...] * _dot(ya_ref[...], wa_ref[...])
                  + gb_ref[...] * _dot(yb_ref[...], wb_ref[...])).astype(BF16)


def _merge(ya, yb, wa, wb, proj, tm):
    n = ya.shape[0]
    tn = 512
    return pl.pallas_call(
        _merge_kernel,
        grid=(n // tm, D_MODEL // tn),
        in_specs=[pl.BlockSpec((tm, S5_WIDTH), lambda i, j: (i, 0)),
                  pl.BlockSpec((tm, ATT_WIDTH), lambda i, j: (i, 0)),
                  pl.BlockSpec((S5_WIDTH, tn), lambda i, j: (0, j)),
                  pl.BlockSpec((ATT_WIDTH, tn), lambda i, j: (0, j)),
                  pl.BlockSpec((tm, tn), lambda i, j: (i, COL_GA // tn + j)),
                  pl.BlockSpec((tm, tn), lambda i, j: (i, COL_GB // tn + j))],
        out_specs=pl.BlockSpec((tm, tn), lambda i, j: (i, j)),
        out_shape=jax.ShapeDtypeStruct((n, D_MODEL), BF16),
        compiler_params=_params(("arbitrary", "arbitrary")),
        name="merge",
    )(ya, yb, wa, wb, proj, proj)


def _outproj_kernel(m_ref, w_ref, x_ref, o_ref):
    o_ref[...] = x_ref[...] + _dot(m_ref[...], w_ref[...])


def _outproj(merged, w, x, tm):
    n = x.shape[0]
    tn = 512
    return pl.pallas_call(
        _outproj_kernel,
        grid=(n // tm, D_MODEL // tn),
        in_specs=[pl.BlockSpec((tm, D_MODEL), lambda i, j: (i, 0)),
                  pl.BlockSpec((D_MODEL, tn), lambda i, j: (0, j)),
                  pl.BlockSpec((tm, tn), lambda i, j: (i, j))],
        out_specs=pl.BlockSpec((tm, tn), lambda i, j: (i, j)),
        out_shape=jax.ShapeDtypeStruct((n, D_MODEL), F32),
        compiler_params=_params(("arbitrary", "arbitrary")),
        name="outproj",
    )(merged, w, x)


ROUTER_GROUP_LANE = N_EXPERTS


def _router_kernel(x_ref, g_ref, whi_ref, wlo_ref, b_ref, hn_ref, ids_ref, gates_ref):
    xf = x_ref[...]
    ms = jnp.mean(xf * xf, axis=-1, keepdims=True)
    hn = xf * lax.rsqrt(ms + EPS) * g_ref[...]
    hn_ref[...] = hn
    h_hi, h_lo = _split_bf16(hn)
    whi = whi_ref[...]
    lg = _dot(h_hi, whi) + _dot(h_lo, whi) + _dot(h_hi, wlo_ref[...]) + b_ref[...]
    lane = lax.broadcasted_iota(I32, lg.shape, 1)
    lane_f = lane.astype(F32)
    big = float(LANES)
    is_g = (lane >= ROUTER_GROUP_LANE) & (lane < ROUTER_GROUP_LANE + N_GROUPS)
    g_max = jnp.max(jnp.where(is_g, lg, -jnp.inf), axis=-1, keepdims=True)
    g_den = jnp.sum(jnp.where(is_g, jnp.exp(lg - g_max), 0.0), axis=-1, keepdims=True)
    g_w = 1.0 / g_den
    g_lane = jnp.min(jnp.where(is_g & (lg == g_max), lane_f, big), axis=-1, keepdims=True)
    g_sel = g_lane.astype(I32) - ROUTER_GROUP_LANE
    is_e = (lane < N_EXPERTS) & (lane // EXPERTS_PER_GROUP == g_sel)
    e_max = jnp.max(jnp.where(is_e, lg, -jnp.inf), axis=-1, keepdims=True)
    pe = jnp.where(is_e, jnp.exp(lg - e_max), 0.0)
    pe = pe / jnp.sum(pe, axis=-1, keepdims=True)
    pe = jnp.where(is_e, pe, -1.0)
    p1 = jnp.max(pe, axis=-1, keepdims=True)
    i1 = jnp.min(jnp.where(pe == p1, lane_f, big), axis=-1, keepdims=True)
    pe2 = jnp.where(lane_f == i1, -1.0, pe)
    p2 = jnp.max(pe2, axis=-1, keepdims=True)
    i2 = jnp.min(jnp.where(pe2 == p2, lane_f, big), axis=-1, keepdims=True)
    tot = p1 + p2
    ids_ref[...] = jnp.where(lane == 0, i1, jnp.where(lane == 1, i2, 0.0)).astype(I32)
    gates_ref[...] = jnp.where(lane == 0, g_w * (p1 / tot), jnp.where(lane == 1, g_w * (p2 / tot), 0.0))


def _router(x1, g, whi, wlo, bias, tm):
    n = x1.shape[0]
    return pl.pallas_call(
        _router_kernel,
        grid=(n // tm,),
        in_specs=[pl.BlockSpec((tm, D_MODEL), lambda i: (i, 0)),
                  pl.BlockSpec((1, D_MODEL), lambda i: (0, 0)),
                  pl.BlockSpec((D_MODEL, LANES), lambda i: (0, 0)),
                  pl.BlockSpec((D_MODEL, LANES), lambda i: (0, 0)),
                  pl.BlockSpec((1, LANES), lambda i: (0, 0))],
        out_specs=[pl.BlockSpec((tm, D_MODEL), lambda i: (i, 0)),
                   pl.BlockSpec((tm, LANES), lambda i: (i, 0)),
                   pl.BlockSpec((tm, LANES), lambda i: (i, 0))],
        out_shape=[jax.ShapeDtypeStruct((n, D_MODEL), F32),
                   jax.ShapeDtypeStruct((n, LANES), I32),
                   jax.ShapeDtypeStruct((n, LANES), F32)],
        compiler_params=_params(("arbitrary",)),
        name="router",
    )(x1, g, whi, wlo, bias)


def _row_copy(src, dst, s_row, d_row, sem):
    return pltpu.make_async_copy(src.at[pl.ds(s_row, 1)], dst.at[pl.ds(d_row, 1)], sem)


def _dispatch_kernel(dest_ref, hn_ref, xs_in_ref, xs_ref, sem, *, ch):
    del xs_in_ref

    def issue(t, carry):
        for s in range(2):
            _row_copy(hn_ref, xs_ref, t, dest_ref[0, 0, 2 * t + s], sem).start()
        return carry

    lax.fori_loop(0, ch // 2, issue, 0, unroll=8)
    for s in range(2):
        pltpu.make_async_copy(hn_ref, xs_ref.at[pl.ds(0, ch // 2)], sem).wait()


def _dispatch(dest, hn, xs0, ch):
    m = dest.shape[0]
    return pl.pallas_call(
        functools.partial(_dispatch_kernel, ch=ch),
        grid=(m // ch,),
        in_specs=[pl.BlockSpec((1, 1, ch), lambda i: (i, 0, 0), memory_space=pltpu.SMEM),
                  pl.BlockSpec((ch // 2, D_MODEL), lambda i: (i, 0)),
                  pl.BlockSpec(memory_space=pl.ANY)],
        out_specs=pl.BlockSpec(memory_space=pl.ANY),
        out_shape=jax.ShapeDtypeStruct(xs0.shape, xs0.dtype),
        scratch_shapes=[pltpu.SemaphoreType.DMA(())],
        input_output_aliases={2: 0},
        compiler_params=_params(("arbitrary",), disable_bounds_checks=True, has_side_effects=True),
        name="dispatch",
    )(dest.reshape(m // ch, 1, ch), hn, xs0)


def _expert_kernel(be_ref, nu_ref, x_ref, wg_ref, wu_ref, wd_ref, o_ref, wg_s, wu_s, wd_s):
    b = pl.program_id(0)
    prev = be_ref[jnp.maximum(b - 1, 0)]

    @pl.when(b < nu_ref[0])
    def _():
        @pl.when((b == 0) | (be_ref[b] != prev))
        def _():
            wg_s[...] = wg_ref[...].astype(BF16)
            wu_s[...] = wu_ref[...].astype(BF16)
            wd_s[...] = wd_ref[...].astype(BF16)

        x = x_ref[...].astype(BF16)
        gate = _dot(x, wg_s[...])
        h = gate * _sigmoid(gate) * _dot(x, wu_s[...])
        o_ref[...] = _dot(h.astype(BF16), wd_s[...])

    @pl.when(b >= nu_ref[0])
    def _():
        o_ref[...] = jnp.zeros(o_ref.shape, F32)


def _experts(block_e, n_used, xs, wg, wu, wd, bm):
    rows = xs.shape[0]
    nb = rows // bm
    grid_spec = pltpu.PrefetchScalarGridSpec(
        num_scalar_prefetch=2,
        grid=(nb,),
        in_specs=[pl.BlockSpec((bm, D_MODEL), lambda b, be, nu: (b, 0)),
                  pl.BlockSpec((None, D_MODEL, D_EXPERT), lambda b, be, nu: (be[b], 0, 0)),
                  pl.BlockSpec((None, D_MODEL, D_EXPERT), lambda b, be, nu: (be[b], 0, 0)),
                  pl.BlockSpec((None, D_EXPERT, D_MODEL), lambda b, be, nu: (be[b], 0, 0))],
        out_specs=pl.BlockSpec((bm, D_MODEL), lambda b, be, nu: (b, 0)),
        scratch_shapes=[pltpu.VMEM((D_MODEL, D_EXPERT), BF16), pltpu.VMEM((D_MODEL, D_EXPERT), BF16),
                        pltpu.VMEM((D_EXPERT, D_MODEL), BF16)])
    return pl.pallas_call(
        _expert_kernel,
        grid_spec=grid_spec,
        out_shape=jax.ShapeDtypeStruct((rows, D_MODEL), F32),
        compiler_params=_params(("arbitrary",)),
        name="experts",
    )(block_e, n_used, xs, wg, wu, wd)


def _combine_kernel(dest_ref, gates_ref, x_ref, ys_ref, o_ref, buf, sem, *, tc):
    def issue(t, carry):
        for s in range(2):
            _row_copy(ys_ref, buf.at[s], dest_ref[0, 0, 2 * t + s], t, sem).start()
        return carry

    lax.fori_loop(0, tc, issue, 0, unroll=8)
    for s in range(2):
        pltpu.make_async_copy(ys_ref.at[pl.ds(0, tc)], buf.at[s], sem).wait()
    gates = gates_ref[...]
    o_ref[...] = x_ref[...] + gates[:, 0:1] * buf[0] + gates[:, 1:2] * buf[1]


def _combine(dest, gates, x1, ys, tc):
    n = x1.shape[0]
    return pl.pallas_call(
        functools.partial(_combine_kernel, tc=tc),
        grid=(n // tc,),
        in_specs=[pl.BlockSpec((1, 1, 2 * tc), lambda i: (i, 0, 0), memory_space=pltpu.SMEM),
                  pl.BlockSpec((tc, LANES), lambda i: (i, 0)),
                  pl.BlockSpec((tc, D_MODEL), lambda i: (i, 0)),
                  pl.BlockSpec(memory_space=pl.ANY)],
        out_specs=pl.BlockSpec((tc, D_MODEL), lambda i: (i, 0)),
        out_shape=jax.ShapeDtypeStruct((n, D_MODEL), F32),
        scratch_shapes=[pltpu.VMEM((2, tc, D_MODEL), F32), pltpu.SemaphoreType.DMA(())],
        compiler_params=_params(("arbitrary",), disable_bounds_checks=True),
        name="combine",
    )(dest.reshape(n // tc, 1, 2 * tc), gates, x1, ys)


def _moe(x1s, p, tms, bm):
    routed = [_router(x1, p['norm_ffn'], p['wr_hi'], p['wr_lo'], p['b_router'], tm) for x1, tm in zip(x1s, tms)]
    flat_e = jnp.concatenate([ids[:, :2].reshape(-1) for _, ids, _ in routed])
    m = flat_e.shape[0]
    onehot = (flat_e[:, None] == jnp.arange(N_EXPERTS, dtype=I32)[None, :]).astype(I32)
    csum = jnp.cumsum(onehot, axis=0)
    counts = csum[-1]
    padded = (counts + bm - 1) // bm * bm
    pad_end = jnp.cumsum(padded)
    pad_start = pad_end - padded
    dest = jnp.sum(onehot * (csum - 1 + pad_start[None, :]), axis=1).astype(I32)
    nb = -(-(m + N_EXPERTS * (bm - 1)) // bm)
    block_start = jnp.arange(nb, dtype=I32) * bm
    block_e = jnp.minimum(jnp.sum((pad_end[None, :] <= block_start[:, None]).astype(I32), axis=1),
                          N_EXPERTS - 1).astype(I32)
    n_used = (pad_end[-1] // bm).astype(I32).reshape(1)
    xs = jnp.zeros((nb * bm, D_MODEL), F32)
    dests = []
    first = 0
    for hn, _, _ in routed:
        dests.append(dest[first:first + 2 * hn.shape[0]])
        first += 2 * hn.shape[0]
        xs = _dispatch(dests[-1], hn, xs, min(2 * hn.shape[0], 1024))
    ys = _experts(block_e, n_used, xs, p['w_gate'], p['w_up'], p['w_down'], bm)
    return [_combine(d, gates, x1, ys, min(x1.shape[0], 256)) for d, (_, _, gates), x1 in zip(dests, routed, x1s)]


def _prep(norm_mix, w_in, s5_a_re, s5_a_im, s5_log_dt, s5_b_re, s5_b_im, s5_c_re, s5_c_im, s5_d,
          w_glu, q_norm, k_norm, idx_k_norm, w_branch_a, w_branch_b, w_out, norm_ffn,
          w_router_group, b_router_group, w_router_expert, b_router_expert, w_gate, w_up, w_down):
    pts = []
    acc = 0
    for s in IN_SIZES[:-1]:
        acc += s
        pts.append(acc)
    w_u, w_q, w_k, w_v, w_qi, w_ki, w_wi, w_ga, w_gb = jnp.split(w_in, pts, axis=1)
    pad = PROJ_COLS - sum(IN_SIZES)
    w_proj = jnp.concatenate([w_u, w_q, w_ga, w_gb, w_qi, w_k, w_v, w_ki, w_wi,
                              jnp.zeros((D_MODEL, pad), w_in.dtype)], axis=1).astype(BF16)
    one = lambda k: jnp.ones((k,), F32)
    gain = jnp.concatenate([one(S5_WIDTH), jnp.tile(q_norm.astype(F32), N_HEADS), one(2 * D_MODEL),
                            one(N_IDX_HEADS * IDX_DIM), jnp.tile(k_norm.astype(F32), N_KV_HEADS), one(KV_WIDTH),
                            idx_k_norm.astype(F32), one(N_IDX_HEADS + pad)]).reshape(1, PROJ_COLS)
    half = 256
    wv, wg = w_glu[:, :S5_WIDTH], w_glu[:, S5_WIDTH:]
    w_glu_p = jnp.concatenate(
        [jnp.concatenate([wv[:, c * half:(c + 1) * half], wg[:, c * half:(c + 1) * half]], axis=1)
         for c in range(S5_WIDTH // half)], axis=1).astype(BF16)
    w_r = jnp.concatenate([w_router_expert.astype(F32), w_router_group.astype(F32),
                           jnp.zeros((D_MODEL, LANES - N_EXPERTS - N_GROUPS), F32)], axis=1)
    wr_hi = w_r.astype(BF16)
    wr_lo = (w_r - wr_hi.astype(F32)).astype(BF16)
    b_r = jnp.concatenate([b_router_expert.astype(F32), b_router_group.astype(F32),
                           jnp.zeros((LANES - N_EXPERTS - N_GROUPS,), F32)]).reshape(1, LANES)
    return dict(norm_mix=norm_mix.astype(F32).reshape(1, D_MODEL), w_proj=w_proj, gain=gain,
                s5=_s5_tables(s5_a_re, s5_a_im, s5_log_dt, s5_b_re, s5_b_im, s5_c_re, s5_c_im, S5_MAX_CHUNK),
                s5_d=s5_d.astype(F32).reshape(1, S5_WIDTH), w_glu=w_glu_p,
                w_a=w_branch_a.astype(BF16), w_b=w_branch_b.astype(BF16), w_out=w_out.astype(BF16),
                norm_ffn=norm_ffn.astype(F32).reshape(1, D_MODEL), wr_hi=wr_hi, wr_lo=wr_lo, b_router=b_r,
                w_gate=w_gate, w_up=w_up, w_down=w_down)


def _mixer(x, h0_re, h0_im, k_past, v_past, ki_past, p, *, tm, lc, tt, tq, kb):
    bsz, t, _ = x.shape
    n = bsz * t
    x2 = x.reshape(n, D_MODEL)
    proj = _inproj(x2, p['norm_mix'], p['w_proj'], p['gain'], tm)
    proj3 = proj.reshape(bsz, t, PROJ_COLS)
    k = proj3[:, :, COL_K:COL_K + KV_WIDTH]
    v = proj3[:, :, COL_V:COL_V + KV_WIDTH]
    ki = proj3[:, :, COL_KI:COL_KI + IDX_DIM]
    n_state = S5_GROUPS * S5_STATE
    tables = dict(p['s5'])
    for name in ('air', 'aii', 'apr', 'api'):
        tables[name] = tables[name][:lc]
    tables['tri'] = jnp.tile(tables['tri'][:lc, :lc], (1, 2))
    yg, s_re, s_im = _s5(proj3, h0_re.reshape(bsz, 1, n_state), h0_im.reshape(bsz, 1, n_state),
                         tables, p['s5_d'], lc, tt)
    y_a = _glu(yg.reshape(n, S5_WIDTH), p['w_glu'], tm)
    if k_past is None:
        pos0 = 0
        k_all, v_all, ki_all = k, v, ki
    else:
        pos0 = k_past.shape[1]
        k_all = jnp.concatenate([k_past.reshape(bsz, pos0, KV_WIDTH), k], axis=1)
        v_all = jnp.concatenate([v_past.reshape(bsz, pos0, KV_WIDTH), v], axis=1)
        ki_all = jnp.concatenate([ki_past, ki], axis=1)
    n_keys = k_all.shape[1]
    n_top = min(TOP_K_MAX, n_keys // 4)
    nkp = -(-n_keys // kb) * kb
    padk = lambda a: jnp.pad(a.astype(BF16), ((0, 0), (0, nkp - n_keys), (0, 0)))
    kit = jnp.swapaxes(padk(ki_all), 1, 2)
    kt = jnp.swapaxes(padk(k_all), 1, 2)
    y_b = _dsa(proj3, kit, kt, padk(v_all), tq=tq, kb=kb, pos0=pos0, n_keys=n_keys, n_top=n_top)
    merged = _merge(y_a, y_b.reshape(n, ATT_WIDTH), p['w_a'], p['w_b'], proj, tm)
    x1 = _outproj(merged, p['w_out'], x2, tm)
    return (x1, s_re.reshape(bsz, S5_GROUPS, S5_STATE),
            s_im.reshape(bsz, S5_GROUPS, S5_STATE), k.reshape(bsz, t, N_KV_HEADS, HEAD_DIM),
            v.reshape(bsz, t, N_KV_HEADS, HEAD_DIM), ki)


def kernel(x_prompt, x_sample, state_s5_re, state_s5_im, cache_k, cache_v, cache_idx_k, norm_mix, w_in, s5_a_re, s5_a_im, s5_log_dt, s5_b_re, s5_b_im, s5_c_re, s5_c_im, s5_d, w_glu, q_norm, k_norm, idx_k_norm, w_branch_a, w_branch_b, w_out, norm_ffn, w_router_group, b_router_group, w_router_expert, b_router_expert, w_gate, w_up, w_down):
    p = _prep(norm_mix, w_in, s5_a_re, s5_a_im, s5_log_dt, s5_b_re, s5_b_im, s5_c_re, s5_c_im, s5_d,
              w_glu, q_norm, k_norm, idx_k_norm, w_branch_a, w_branch_b, w_out, norm_ffn,
              w_router_group, b_router_group, w_router_expert, b_router_expert, w_gate, w_up, w_down)
    h0 = jnp.zeros((x_prompt.shape[0], S5_GROUPS, S5_STATE), F32)
    tm_p, tm_s = 1024, 256
    xp, srp, sip, kp, vp, kip = _mixer(x_prompt, h0, h0, None, None, None, p,
                                       tm=tm_p, lc=64, tt=1024, tq=128, kb=512)
    xs, srs, sis, ks, vs, kis = _mixer(x_sample, state_s5_re, state_s5_im, cache_k, cache_v, cache_idx_k, p,
                                       tm=tm_s, lc=32, tt=32, tq=32, kb=384)
    yp, ys = _moe([xp, xs], p, [tm_p, tm_s], 256)
    return (yp.reshape(x_prompt.shape), ys.reshape(x_sample.shape), srp, sip, kp, vp, kip, srs, sis, ks, vs, kis)
```

```python
import functools
import math

import jax
import jax.numpy as jnp
from jax import lax
from jax.experimental import pallas as pl
from jax.experimental.pallas import tpu as pltpu

F32 = jnp.float32
BF16 = jnp.bfloat16
I32 = jnp.int32

D_MODEL = 2048
CHUNK = 64
EPS = 1e-6
S5_WIDTH = 1024
S5_GROUP = 16
S5_GROUPS = 64
S5_STATE = 64
S5_MAX_RE = -1e-4
N_HEADS = 8
N_KV_HEADS = 2
Q_PER_KV = 4
HEAD_DIM = 128
ATT_WIDTH = 1024
KV_WIDTH = 256
N_IDX_HEADS = 8
IDX_DIM = 64
TOP_K_MAX = 256
N_GROUPS = 4
EXPERTS_PER_GROUP = 8
N_EXPERTS = 32
D_EXPERT = 512
IN_SIZES = (S5_WIDTH, ATT_WIDTH, KV_WIDTH, KV_WIDTH, N_IDX_HEADS * IDX_DIM, IDX_DIM, N_IDX_HEADS, D_MODEL, D_MODEL)

LANES = 128
PROJ_TN = 512
COL_U, COL_Q, COL_GA, COL_GB, COL_QI, COL_K, COL_V, COL_KI, COL_WI = 0, 1024, 2048, 4096, 6144, 6656, 6912, 7168, 7232
PROJ_COLS = 7680
S5_LANE_BLOCKS = S5_WIDTH // LANES
S5_BLOCK_STATE = (LANES // S5_GROUP) * S5_STATE
S5_MAX_CHUNK = 64
VMEM_LIMIT = 56 * 1024 * 1024
INT_MIN = -2 ** 31
KEY_LOWEST_FINITE = -2 ** 31 + 0x00800000
NEG_BIG = -1e30
LOG2E = 1.4426950408889634
FAST_SOFTMAX_BOUND = 40.0
TOPK_PER_LANE = 12


def _dot(a, b):
    return jnp.dot(a, b, preferred_element_type=F32)


def _split_bf16(x):
    hi = x.astype(BF16)
    lo = (x - hi.astype(F32)).astype(BF16)
    return hi, lo


def _sigmoid(x):
    return 1.0 / (1.0 + jnp.exp(-x))


def _params(sem, **kw):
    return pltpu.CompilerParams(dimension_semantics=sem, vmem_limit_bytes=VMEM_LIMIT, **kw)


def _group_norm(a, gain):
    ms = jnp.mean(a * a, axis=-1, keepdims=True)
    return a * lax.rsqrt(ms + EPS) * gain


def _inproj_kernel(x_ref, g_ref, w_ref, gain_ref, o_ref, xn_ref):
    j = pl.program_id(1)

    @pl.when(j == 0)
    def _():
        xf = x_ref[...]
        ms = jnp.mean(xf * xf, axis=-1, keepdims=True)
        xn_ref[...] = (xf * lax.rsqrt(ms + EPS) * g_ref[...]).astype(BF16)

    acc = _dot(xn_ref[...], w_ref[...])
    gain = gain_ref[...]
    groups = [slice(c * LANES, (c + 1) * LANES) for c in range(PROJ_TN // LANES)]

    @pl.when((j < 2) | (j == 12))
    def _():
        o_ref[...] = acc

    @pl.when((j == 2) | (j == 3))
    def _():
        for s in groups:
            o_ref[:, s] = _group_norm(acc[:, s], gain[:, s])

    @pl.when((j >= 4) & (j < 12))
    def _():
        o_ref[...] = _sigmoid(acc)

    @pl.when(j == 13)
    def _():
        for s in groups[:2]:
            o_ref[:, s] = _group_norm(acc[:, s], gain[:, s])
        o_ref[:, 2 * LANES:] = acc[:, 2 * LANES:]

    @pl.when(j == 14)
    def _():
        a = acc[:, :LANES]
        lane = lax.broadcasted_iota(I32, a.shape, 1)
        is_ki = lane < IDX_DIM
        ms = jnp.sum(jnp.where(is_ki, a * a, 0.0), axis=-1, keepdims=True) * (1.0 / IDX_DIM)
        ki = a * lax.rsqrt(ms + EPS) * gain[:, :LANES]
        wi = a * (N_IDX_HEADS ** -0.5) * (IDX_DIM ** -0.5)
        o_ref[:, :LANES] = jnp.where(is_ki, ki, wi)
        o_ref[:, LANES:] = acc[:, LANES:]


def _inproj(x, g, w, gain, tm):
    n = x.shape[0]
    return pl.pallas_call(
        _inproj_kernel,
        grid=(n // tm, PROJ_COLS // PROJ_TN),
        in_specs=[pl.BlockSpec((tm, D_MODEL), lambda i, j: (i, 0)),
                  pl.BlockSpec((1, D_MODEL), lambda i, j: (0, 0)),
                  pl.BlockSpec((D_MODEL, PROJ_TN), lambda i, j: (0, j)),
                  pl.BlockSpec((1, PROJ_TN), lambda i, j: (0, j))],
        out_specs=pl.BlockSpec((tm, PROJ_TN), lambda i, j: (i, j)),
        out_shape=jax.ShapeDtypeStruct((n, PROJ_COLS), F32),
        scratch_shapes=[pltpu.VMEM((tm, D_MODEL), BF16)],
        compiler_params=_params(("arbitrary", "arbitrary")),
        name="inproj",
    )(x, g, w, gain)


def _gelu_tanh(y):
    return 0.5 * y * (1.0 + jnp.tanh(math.sqrt(2.0 / math.pi) * (y + 0.044715 * (y * y * y))))


def _s5_kernel(u_ref, h0r_ref, h0i_ref, bhi_ref, blo_ref, air_ref, aii_ref, apr_ref, api_ref,
               a1r_ref, a1i_ref, tri_ref, cd_ref, d_ref, yg_ref, sr_ref, si_ref,
               bu_scr, h_scr, hr_scr, hi_scr, *, lc, tt):
    t = pl.program_id(2)
    ns = S5_BLOCK_STATE

    @pl.when(t == 0)
    def _():
        hr_scr[...] = h0r_ref[...]
        hi_scr[...] = h0i_ref[...]

    u = u_ref[...]
    u_hi, u_lo = _split_bf16(u)
    bhi = bhi_ref[...]
    bu_scr[...] = _dot(u_hi, bhi) + _dot(jnp.concatenate([u_lo, u_hi], axis=1), blo_ref[...])
    tri = tri_ref[...]
    air, aii, apr, api = air_ref[...], aii_ref[...], apr_ref[...], api_ref[...]
    a1r, a1i = a1r_ref[...], a1i_ref[...]

    def chunk(s, carry):
        h_re, h_im = carry
        r0 = pl.multiple_of(s * lc, lc)
        br = bu_scr[pl.ds(r0, lc), 0:ns]
        bi = bu_scr[pl.ds(r0, lc), ns:2 * ns]
        z = jnp.concatenate([air * br - aii * bi, air * bi + aii * br], axis=1)
        z_hi, z_lo = _split_bf16(z)
        c = _dot(tri, jnp.concatenate([z_hi, z_lo], axis=0))
        cr = c[:, 0:ns] + (a1r * h_re - a1i * h_im)
        ci = c[:, ns:2 * ns] + (a1r * h_im + a1i * h_re)
        hr = apr * cr - api * ci
        hi = apr * ci + api * cr
        h_scr[pl.ds(r0, lc), 0:ns] = hr.astype(BF16)
        h_scr[pl.ds(r0, lc), ns:2 * ns] = hi.astype(BF16)
        return hr[lc - 1:lc, :], hi[lc - 1:lc, :]

    h_re, h_im = lax.fori_loop(0, tt // lc, chunk, (hr_scr[...], hi_scr[...]), unroll=min(4, tt // lc))
    hr_scr[...] = h_re
    hi_scr[...] = h_im
    sr_ref[...] = h_re
    si_ref[...] = h_im
    y = _dot(h_scr[...], cd_ref[...]) + d_ref[...] * u
    yg_ref[...] = _gelu_tanh(y).astype(BF16)


def _s5_tables(a_re, a_im, log_dt, b_re, b_im, c_re, c_im, lc):
    lr = jnp.minimum(a_re.astype(F32), S5_MAX_RE)
    li = a_im.astype(F32)
    dt = jnp.exp(log_dt.astype(F32))[:, None]
    mag = jnp.exp(lr * dt)
    lbr = mag * jnp.cos(li * dt)
    lbi = mag * jnp.sin(li * dt)
    den = lr * lr + li * li
    fr = ((lbr - 1.0) * lr + lbi * li) / den
    fi = (lbi * lr - (lbr - 1.0) * li) / den
    br = b_re.astype(F32)
    bi = b_im.astype(F32)
    bbr = fr[..., None] * br - fi[..., None] * bi
    bbi = fr[..., None] * bi + fi[..., None] * br
    j = jnp.arange(lc, dtype=F32)[:, None, None]
    ang = j * (li * dt)[None]
    lmag = j * (lr * dt)[None]
    n_state = S5_GROUPS * S5_STATE
    apr = (jnp.exp(lmag) * jnp.cos(ang)).reshape(lc, n_state)
    api = (jnp.exp(lmag) * jnp.sin(ang)).reshape(lc, n_state)
    air = (jnp.exp(-lmag) * jnp.cos(ang)).reshape(lc, n_state)
    aii = (-jnp.exp(-lmag) * jnp.sin(ang)).reshape(lc, n_state)
    a1r = lbr.reshape(1, n_state)
    a1i = lbi.reshape(1, n_state)
    gpb = LANES // S5_GROUP
    eye = jnp.eye(gpb, dtype=F32)

    def bdiag(b):
        return jnp.einsum('kgpc,gh->kgchp', b.reshape(S5_LANE_BLOCKS, gpb, S5_STATE, S5_GROUP), eye).reshape(
            S5_LANE_BLOCKS, LANES, S5_BLOCK_STATE)

    def cdiag(c):
        return jnp.einsum('kgcp,gh->kgphc', c.reshape(S5_LANE_BLOCKS, gpb, S5_GROUP, S5_STATE), eye).reshape(
            S5_LANE_BLOCKS, S5_BLOCK_STATE, LANES)

    bd = jnp.concatenate([bdiag(bbr), bdiag(bbi)], axis=-1)
    bd_hi = bd.astype(BF16)
    bd_lo = (bd - bd_hi.astype(F32)).astype(BF16)
    cd = jnp.concatenate([cdiag(c_re.astype(F32)), -cdiag(c_im.astype(F32))], axis=1).astype(BF16)
    tri = jnp.tril(jnp.ones((lc, lc), F32)).astype(BF16)
    bd_lo = jnp.concatenate([bd_hi, bd_lo], axis=1)
    return dict(bd_hi=bd_hi, bd_lo=bd_lo, air=air, aii=aii, apr=apr, api=api, a1r=a1r, a1i=a1i, tri=tri, cd=cd)


def _s5(proj3, h0_re, h0_im, tb, dvec, lc, tt):
    bsz, t, _ = proj3.shape
    ns = S5_BLOCK_STATE
    n_state = S5_GROUPS * S5_STATE
    tab = lambda: pl.BlockSpec((lc, ns), lambda b, k, i: (0, k))
    row = lambda: pl.BlockSpec((1, ns), lambda b, k, i: (0, k))
    st = lambda: pl.BlockSpec((None, 1, ns), lambda b, k, i: (b, 0, k))
    return pl.pallas_call(
        functools.partial(_s5_kernel, lc=lc, tt=tt),
        grid=(bsz, S5_LANE_BLOCKS, t // tt),
        in_specs=[pl.BlockSpec((None, tt, LANES), lambda b, k, i: (b, i, k)),
                  st(), st(),
                  pl.BlockSpec((None, LANES, 2 * ns), lambda b, k, i: (k, 0, 0)),
                  pl.BlockSpec((None, 2 * LANES, 2 * ns), lambda b, k, i: (k, 0, 0)),
                  tab(), tab(), tab(), tab(), row(), row(),
                  pl.BlockSpec((lc, 2 * lc), lambda b, k, i: (0, 0)),
                  pl.BlockSpec((None, 2 * ns, LANES), lambda b, k, i: (k, 0, 0)),
                  pl.BlockSpec((1, LANES), lambda b, k, i: (0, k))],
        out_specs=[pl.BlockSpec((None, tt, LANES), lambda b, k, i: (b, i, k)), st(), st()],
        out_shape=[jax.ShapeDtypeStruct((bsz, t, S5_WIDTH), BF16),
                   jax.ShapeDtypeStruct((bsz, 1, n_state), F32),
                   jax.ShapeDtypeStruct((bsz, 1, n_state), F32)],
        scratch_shapes=[pltpu.VMEM((tt, 2 * ns), F32), pltpu.VMEM((tt, 2 * ns), BF16),
                        pltpu.VMEM((1, ns), F32), pltpu.VMEM((1, ns), F32)],
        compiler_params=_params(("arbitrary", "arbitrary", "arbitrary")),
        name="s5",
    )(proj3, h0_re, h0_im, tb['bd_hi'], tb['bd_lo'], tb['air'], tb['aii'], tb['apr'], tb['api'],
      tb['a1r'], tb['a1i'], tb['tri'], tb['cd'], dvec)


def _glu_kernel(y_ref, w_ref, o_ref):
    acc = _dot(y_ref[...], w_ref[...])
    half = acc.shape[1] // 2
    o_ref[...] = (acc[:, :half] * _sigmoid(acc[:, half:])).astype(BF16)


def _glu(yg, w, tm):
    n = yg.shape[0]
    tn = 512
    return pl.pallas_call(
        _glu_kernel,
        grid=(n // tm, 2 * S5_WIDTH // tn),
        in_specs=[pl.BlockSpec((tm, S5_WIDTH), lambda i, j: (i, 0)),
                  pl.BlockSpec((S5_WIDTH, tn), lambda i, j: (0, j))],
        out_specs=pl.BlockSpec((tm, tn // 2), lambda i, j: (i, j)),
        out_shape=jax.ShapeDtypeStruct((n, S5_WIDTH), BF16),
        compiler_params=_params(("arbitrary", "arbitrary")),
        name="glu",
    )(yg, w)


def _dsa_kernel(q_ref, qi_ref, kiw_ref, kit_ref, kt_ref, v_ref, ut_ref, o_ref,
                key_scr, qs_scr, qis_scr, wis_scr, m_scr, l_scr, acc_scr,
                kmax_scr, lsum_scr, cand_scr, ckey_scr, thr_scr, cnt_scr,
                *, tq, kb, pos0, n_keys, n_top, nt):
    i = pl.program_id(1)
    nkp = key_scr.shape[2]
    n_cand = cand_scr.shape[1]
    slot_s = i % 2
    slot_a = 1 - slot_s
    tile_s = jnp.minimum(i, nt - 1)
    tile_a = jnp.maximum(i - 1, 0)

    def n_blocks(tile):
        k_end = jnp.minimum(n_keys, (pos0 + tile * tq + tq - 1) // CHUNK * CHUNK + CHUNK)
        return (k_end + kb - 1) // kb

    nkb_s = jnp.where(i < nt, n_blocks(tile_s), 0)
    nkb_a = jnp.where(i >= 1, n_blocks(tile_a), 0)

    @pl.when(i == 0)
    def _():
        def norm_block(b, carry):
            c0 = pl.multiple_of(b * kb, kb)
            out = []
            for g in range(N_KV_HEADS):
                kk = kt_ref[g * HEAD_DIM:(g + 1) * HEAD_DIM, pl.ds(c0, kb)].astype(F32)
                out.append(jnp.maximum(carry[g], jnp.sum(kk * kk, axis=0, keepdims=True)))
            return tuple(out)

        res = lax.fori_loop(0, nkp // kb, norm_block, (jnp.zeros((1, kb), F32),) * N_KV_HEADS)
        for g in range(N_KV_HEADS):
            kmax_scr[g] = jnp.broadcast_to(jnp.max(res[g], axis=-1, keepdims=True), (1, LANES))

    q = q_ref[...] * (HEAD_DIM ** -0.5 * LOG2E)
    for h in range(N_HEADS):
        g, r = divmod(h, Q_PER_KV)
        qs_scr[g, r * tq:(r + 1) * tq, :] = q[:, h * HEAD_DIM:(h + 1) * HEAD_DIM].astype(BF16)
    qi = qi_ref[...]
    kiw = kiw_ref[...]
    for h in range(N_IDX_HEADS):
        qis_scr[h * tq:(h + 1) * tq, :] = qi[:, h * IDX_DIM:(h + 1) * IDX_DIM].astype(BF16)
        wis_scr[h * tq:(h + 1) * tq, :] = kiw[:, IDX_DIM + h:IDX_DIM + h + 1]

    q_chunk = (pos0 + tile_s * tq + lax.broadcasted_iota(I32, (tq, 1), 0)) // CHUNK

    def to_key(x):
        bits = pltpu.bitcast(x, I32)
        return jnp.where(bits < 0, bits ^ 0x7FFFFFFF, bits)

    cand_scr[slot_s] = jnp.full(cand_scr.shape[1:], -jnp.inf, F32)

    def score_block(b):
        c0 = pl.multiple_of(b * kb, kb)
        s = _dot(qis_scr[...], kit_ref[:, pl.ds(c0, kb)])
        s = jnp.maximum(s, 0.0) * wis_scr[...]
        sc = s[0:tq]
        for h in range(1, N_IDX_HEADS):
            sc = sc + s[h * tq:(h + 1) * tq]
        k_pos = c0 + lax.broadcasted_iota(I32, (1, kb), 1)
        adm = (k_pos // CHUNK <= q_chunk) & (k_pos < n_keys)
        blk = jnp.where(adm, sc + 0.0, -jnp.inf)
        key_scr[slot_s, :, pl.ds(c0, kb)] = blk
        for rg in range(tq // 8):
            rows = slice(rg * 8, rg * 8 + 8)
            top = [cand_scr[slot_s, j, rows, :] for j in range(n_cand)]
            for c in range(kb // LANES):
                x = blk[rows, c * LANES:(c + 1) * LANES]
                for j in range(n_cand):
                    hi = jnp.maximum(top[j], x)
                    x = jnp.minimum(top[j], x)
                    top[j] = hi
            for j in range(n_cand):
                cand_scr[slot_s, j, rows, :] = top[j]

    def count_all(hit_fn):
        def body(b, acc):
            c0 = pl.multiple_of(b * kb, kb)
            hit = hit_fn(key_scr[slot_a, :, pl.ds(c0, kb)])
            for c in range(kb // LANES):
                acc = acc + hit[:, c * LANES:(c + 1) * LANES]
            return acc
        acc = lax.fori_loop(0, nkb_a, body, jnp.zeros((tq, LANES), F32))
        return jnp.sum(acc, axis=-1, keepdims=True)

    def count_cand(cand):
        acc = jnp.zeros((tq, LANES), F32)
        for j in range(n_cand):
            acc = acc + jnp.where(ckey_scr[j] >= cand, 1.0, 0.0)
        return jnp.sum(acc, axis=-1, keepdims=True)

    def bisect(count_fn):
        def bit_pass(it, t_off):
            cand_off = t_off | lax.shift_left(jnp.int32(1), 31 - it)
            cnt = count_fn(cand_off ^ INT_MIN)
            return jnp.where(cnt >= n_top, cand_off, t_off)
        t_off = lax.fori_loop(0, 32, bit_pass, jnp.zeros((tq, 1), I32))
        key = jnp.maximum(t_off ^ INT_MIN, KEY_LOWEST_FINITE)
        return pltpu.bitcast(jnp.where(key < 0, key ^ 0x7FFFFFFF, key), F32)

    @pl.when(i >= 1)
    def _():
        for j in range(n_cand):
            ckey_scr[j] = to_key(cand_scr[slot_a, j])
        thr_cand = bisect(count_cand)
        thr_scr[...] = thr_cand
        n_all = count_all(lambda x: jnp.where(x >= thr_cand, 1.0, 0.0))
        n_listed = jnp.zeros((tq, LANES), F32)
        for j in range(n_cand):
            n_listed = n_listed + jnp.where(cand_scr[slot_a, j] >= thr_cand, 1.0, 0.0)
        cnt_scr[...] = n_all
        overflow = jnp.max(n_all - jnp.sum(n_listed, axis=-1, keepdims=True)) > 0.0

        @pl.when(overflow)
        def _():
            thr_all = bisect(lambda cand: count_all(lambda x: jnp.where(to_key(x) >= cand, 1.0, 0.0)))
            thr_scr[...] = thr_all
            cnt_scr[...] = count_all(lambda x: jnp.where(x >= thr_all, 1.0, 0.0))

        @pl.when(jnp.max(cnt_scr[...]) > n_top)
        def _():
            thr_t = thr_scr[...]
            need = n_top - count_all(lambda x: jnp.where(x > thr_t, 1.0, 0.0))

            def drop_surplus(b, seen):
                c0 = pl.multiple_of(b * kb, kb)
                x = key_scr[slot_a, :, pl.ds(c0, kb)]
                tied = x == thr_t
                rank = seen + _dot(jnp.where(tied, 1.0, 0.0).astype(BF16), ut_ref[...])
                key_scr[slot_a, :, pl.ds(c0, kb)] = jnp.where(tied & (rank > need), -jnp.inf, x)
                return rank[:, kb - 1:kb]

            lax.fori_loop(0, nkb_a, drop_surplus, jnp.zeros((tq, 1), F32))

    thr = thr_scr[...]

    acc_scr[...] = jnp.zeros(acc_scr.shape, F32)
    lsum_scr[...] = jnp.zeros(lsum_scr.shape, F32)
    m_scr[...] = jnp.full(m_scr.shape, NEG_BIG, F32)
    l_scr[...] = jnp.zeros(l_scr.shape, F32)
    bound_max = jnp.float32(0.0)
    for g in range(N_KV_HEADS):
        qg = qs_scr[g].astype(F32)
        qn2 = jnp.sum(qg * qg, axis=-1, keepdims=True)
        bound = jnp.sqrt(qn2 * kmax_scr[g][:, 0:1]) * 1.002
        bound_max = jnp.maximum(bound_max, jnp.max(bound))
    fast = bound_max <= FAST_SOFTMAX_BOUND

    def attend_fast(b):
        c0 = pl.multiple_of(b * kb, kb)
        bias = jnp.where(key_scr[slot_a, :, pl.ds(c0, kb)] >= thr, 0.0, NEG_BIG)
        for g in range(N_KV_HEADS):
            lg = _dot(qs_scr[g], kt_ref[g * HEAD_DIM:(g + 1) * HEAD_DIM, pl.ds(c0, kb)])
            parts = []
            for r in range(Q_PER_KV):
                rows = slice(r * tq, (r + 1) * tq)
                e = [jnp.exp2(lg[rows, c * LANES:(c + 1) * LANES] + bias[:, c * LANES:(c + 1) * LANES])
                     for c in range(kb // LANES)]
                ls = e[0]
                for c in range(1, kb // LANES):
                    ls = ls + e[c]
                lsum_scr[g, rows, :] += ls
                parts.append(jnp.concatenate(e, axis=1).astype(BF16))
            p = jnp.concatenate(parts, axis=0)
            acc_scr[g] += _dot(p, v_ref[pl.ds(c0, kb), g * HEAD_DIM:(g + 1) * HEAD_DIM])

    def attend_general(b):
        c0 = pl.multiple_of(b * kb, kb)
        sel = key_scr[slot_a, :, pl.ds(c0, kb)] >= thr
        sel4 = jnp.concatenate([sel] * Q_PER_KV, axis=0)
        for g in range(N_KV_HEADS):
            lg = _dot(qs_scr[g], kt_ref[g * HEAD_DIM:(g + 1) * HEAD_DIM, pl.ds(c0, kb)])
            lg = jnp.where(sel4, lg, NEG_BIG)
            m_old = m_scr[g]
            m_new = jnp.maximum(m_old, jnp.max(lg, axis=-1, keepdims=True))
            p = jnp.exp2(lg - m_new)
            alpha = jnp.exp2(m_old - m_new)
            l_scr[g] = alpha * l_scr[g] + jnp.sum(p, axis=-1, keepdims=True)
            pv = _dot(p.astype(BF16), v_ref[pl.ds(c0, kb), g * HEAD_DIM:(g + 1) * HEAD_DIM])
            acc_scr[g] = alpha * acc_scr[g] + pv
            m_scr[g] = m_new

    def loop(lo, hi, *fns):
        def body(b, carry):
            for fn in fns:
                fn(b)
            return carry
        lax.fori_loop(lo, hi, body, 0)

    n_fused = jnp.where(fast, jnp.minimum(nkb_a, nkb_s), 0)
    loop(0, n_fused, score_block, attend_fast)
    loop(n_fused, nkb_s, score_block)
    loop(n_fused, jnp.where(fast, nkb_a, 0), attend_fast)
    loop(0, jnp.where(fast, 0, nkb_a), attend_general)

    @pl.when((i >= 1) & fast)
    def _():
        for h in range(N_HEADS):
            g, r = divmod(h, Q_PER_KV)
            rows = slice(r * tq, (r + 1) * tq)
            l = jnp.sum(lsum_scr[g, rows, :], axis=-1, keepdims=True)
            o_ref[:, h * HEAD_DIM:(h + 1) * HEAD_DIM] = (acc_scr[g, rows, :] / l).astype(BF16)

    @pl.when((i >= 1) & jnp.logical_not(fast))
    def _():
        for h in range(N_HEADS):
            g, r = divmod(h, Q_PER_KV)
            rows = slice(r * tq, (r + 1) * tq)
            o_ref[:, h * HEAD_DIM:(h + 1) * HEAD_DIM] = (acc_scr[g, rows, :] / l_scr[g, rows, :]).astype(BF16)


def _dsa(proj3, kit, kt, v, *, tq, kb, pos0, n_keys, n_top):
    bsz, t, _ = proj3.shape
    nkp = kit.shape[-1]
    nt = t // tq
    kern = functools.partial(_dsa_kernel, tq=tq, kb=kb, pos0=pos0, n_keys=n_keys, n_top=n_top, nt=nt)
    resident = dict(pipeline_mode=pl.Buffered(1))
    prev = lambda i: jnp.maximum(i - 1, 0)
    cur = lambda i: jnp.minimum(i, nt - 1)
    return pl.pallas_call(
        kern,
        grid=(bsz, nt + 1),
        in_specs=[pl.BlockSpec((None, tq, ATT_WIDTH), lambda b, i: (b, prev(i), COL_Q // ATT_WIDTH)),
                  pl.BlockSpec((None, tq, 512), lambda b, i: (b, cur(i), COL_QI // 512)),
                  pl.BlockSpec((None, tq, 512), lambda b, i: (b, cur(i), COL_KI // 512)),
                  pl.BlockSpec((None, IDX_DIM, nkp), lambda b, i: (b, 0, 0), **resident),
                  pl.BlockSpec((None, KV_WIDTH, nkp), lambda b, i: (b, 0, 0), **resident),
                  pl.BlockSpec((None, nkp, KV_WIDTH), lambda b, i: (b, 0, 0), **resident),
                  pl.BlockSpec((kb, kb), lambda b, i: (0, 0), **resident)],
        out_specs=pl.BlockSpec((None, tq, ATT_WIDTH), lambda b, i: (b, prev(i), 0)),
        out_shape=jax.ShapeDtypeStruct((bsz, t, ATT_WIDTH), BF16),
        scratch_shapes=[pltpu.VMEM((2, tq, nkp), F32),
                        pltpu.VMEM((N_KV_HEADS, Q_PER_KV * tq, HEAD_DIM), BF16),
                        pltpu.VMEM((N_IDX_HEADS * tq, IDX_DIM), BF16),
                        pltpu.VMEM((N_IDX_HEADS * tq, 1), F32),
                        pltpu.VMEM((N_KV_HEADS, Q_PER_KV * tq, 1), F32),
                        pltpu.VMEM((N_KV_HEADS, Q_PER_KV * tq, 1), F32),
                        pltpu.VMEM((N_KV_HEADS, Q_PER_KV * tq, HEAD_DIM), F32),
                        pltpu.VMEM((N_KV_HEADS, 1, LANES), F32),
                        pltpu.VMEM((N_KV_HEADS, Q_PER_KV * tq, LANES), F32),
                        pltpu.VMEM((2, TOPK_PER_LANE, tq, LANES), F32),
                        pltpu.VMEM((TOPK_PER_LANE, tq, LANES), I32),
                        pltpu.VMEM((tq, 1), F32),
                        pltpu.VMEM((tq, 1), F32)],
        compiler_params=_params(("arbitrary", "arbitrary")),
        name="dsa",
    )(proj3, proj3, proj3, kit, kt, v, jnp.triu(jnp.ones((kb, kb), F32)).astype(BF16))


def _merge_kernel(ya_ref, yb_ref, wa_ref, wb_ref, ga_ref, gb_ref, o_ref):
    o_ref[...] = (ga_ref[...] * _dot(ya_ref[...], wa_ref[...])
                  + gb_ref[...] * _dot(yb_ref[...], wb_ref[...])).astype(BF16)


def _merge(ya, yb, wa, wb, proj, tm):
    n = ya.shape[0]
    tn = 512
    return pl.pallas_call(
        _merge_kernel,
        grid=(n // tm, D_MODEL // tn),
        in_specs=[pl.BlockSpec((tm, S5_WIDTH), lambda i, j: (i, 0)),
                  pl.BlockSpec((tm, ATT_WIDTH), lambda i, j: (i, 0)),
                  pl.BlockSpec((S5_WIDTH, tn), lambda i, j: (0, j)),
                  pl.BlockSpec((ATT_WIDTH, tn), lambda i, j: (0, j)),
                  pl.BlockSpec((tm, tn), lambda i, j: (i, COL_GA // tn + j)),
                  pl.BlockSpec((tm, tn), lambda i, j: (i, COL_GB // tn + j))],
        out_specs=pl.BlockSpec((tm, tn), lambda i, j: (i, j)),
        out_shape=jax.ShapeDtypeStruct((n, D_MODEL), BF16),
        compiler_params=_params(("arbitrary", "arbitrary")),
        name="merge",
    )(ya, yb, wa, wb, proj, proj)


def _outproj_kernel(m_ref, w_ref, x_ref, o_ref):
    o_ref[...] = x_ref[...] + _dot(m_ref[...], w_ref[...])


def _outproj(merged, w, x, tm):
    n = x.shape[0]
    tn = 512
    return pl.pallas_call(
        _outproj_kernel,
        grid=(n // tm, D_MODEL // tn),
        in_specs=[pl.BlockSpec((tm, D_MODEL), lambda i, j: (i, 0)),
                  pl.BlockSpec((D_MODEL, tn), lambda i, j: (0, j)),
                  pl.BlockSpec((tm, tn), lambda i, j: (i, j))],
        out_specs=pl.BlockSpec((tm, tn), lambda i, j: (i, j)),
        out_shape=jax.ShapeDtypeStruct((n, D_MODEL), F32),
        compiler_params=_params(("arbitrary", "arbitrary")),
        name="outproj",
    )(merged, w, x)


ROUTER_GROUP_LANE = N_EXPERTS


def _router_kernel(x_ref, g_ref, whi_ref, wlo_ref, b_ref, hn_ref, ids_ref, gates_ref):
    xf = x_ref[...]
    ms = jnp.mean(xf * xf, axis=-1, keepdims=True)
    hn = xf * lax.rsqrt(ms + EPS) * g_ref[...]
    hn_ref[...] = hn
    h_hi, h_lo = _split_bf16(hn)
    whi = whi_ref[...]
    lg = _dot(h_hi, whi) + _dot(h_lo, whi) + _dot(h_hi, wlo_ref[...]) + b_ref[...]
    lane = lax.broadcasted_iota(I32, lg.shape, 1)
    lane_f = lane.astype(F32)
    big = float(LANES)
    is_g = (lane >= ROUTER_GROUP_LANE) & (lane < ROUTER_GROUP_LANE + N_GROUPS)
    g_max = jnp.max(jnp.where(is_g, lg, -jnp.inf), axis=-1, keepdims=True)
    g_den = jnp.sum(jnp.where(is_g, jnp.exp(lg - g_max), 0.0), axis=-1, keepdims=True)
    g_w = 1.0 / g_den
    g_lane = jnp.min(jnp.where(is_g & (lg == g_max), lane_f, big), axis=-1, keepdims=True)
    g_sel = g_lane.astype(I32) - ROUTER_GROUP_LANE
    is_e = (lane < N_EXPERTS) & (lane // EXPERTS_PER_GROUP == g_sel)
    e_max = jnp.max(jnp.where(is_e, lg, -jnp.inf), axis=-1, keepdims=True)
    pe = jnp.where(is_e, jnp.exp(lg - e_max), 0.0)
    pe = pe / jnp.sum(pe, axis=-1, keepdims=True)
    pe = jnp.where(is_e, pe, -1.0)
    p1 = jnp.max(pe, axis=-1, keepdims=True)
    i1 = jnp.min(jnp.where(pe == p1, lane_f, big), axis=-1, keepdims=True)
    pe2 = jnp.where(lane_f == i1, -1.0, pe)
    p2 = jnp.max(pe2, axis=-1, keepdims=True)
    i2 = jnp.min(jnp.where(pe2 == p2, lane_f, big), axis=-1, keepdims=True)
    tot = p1 + p2
    ids_ref[...] = jnp.where(lane == 0, i1, jnp.where(lane == 1, i2, 0.0)).astype(I32)
    gates_ref[...] = jnp.where(lane == 0, g_w * (p1 / tot), jnp.where(lane == 1, g_w * (p2 / tot), 0.0))


def _router(x1, g, whi, wlo, bias, tm):
    n = x1.shape[0]
    return pl.pallas_call(
        _router_kernel,
        grid=(n // tm,),
        in_specs=[pl.BlockSpec((tm, D_MODEL), lambda i: (i, 0)),
                  pl.BlockSpec((1, D_MODEL), lambda i: (0, 0)),
                  pl.BlockSpec((D_MODEL, LANES), lambda i: (0, 0)),
                  pl.BlockSpec((D_MODEL, LANES), lambda i: (0, 0)),
                  pl.BlockSpec((1, LANES), lambda i: (0, 0))],
        out_specs=[pl.BlockSpec((tm, D_MODEL), lambda i: (i, 0)),
                   pl.BlockSpec((tm, LANES), lambda i: (i, 0)),
                   pl.BlockSpec((tm, LANES), lambda i: (i, 0))],
        out_shape=[jax.ShapeDtypeStruct((n, D_MODEL), F32),
                   jax.ShapeDtypeStruct((n, LANES), I32),
                   jax.ShapeDtypeStruct((n, LANES), F32)],
        compiler_params=_params(("arbitrary",)),
        name="router",
    )(x1, g, whi, wlo, bias)


def _row_copy(src, dst, s_row, d_row, sem):
    return pltpu.make_async_copy(src.at[pl.ds(s_row, 1)], dst.at[pl.ds(d_row, 1)], sem)


def _dispatch_kernel(dest_ref, hn_ref, xs_in_ref, xs_ref, sem, *, ch):
    del xs_in_ref

    def issue(t, carry):
        for s in range(2):
            _row_copy(hn_ref, xs_ref, t, dest_ref[0, 0, 2 * t + s], sem).start()
        return carry

    lax.fori_loop(0, ch // 2, issue, 0, unroll=8)
    for s in range(2):
        pltpu.make_async_copy(hn_ref, xs_ref.at[pl.ds(0, ch // 2)], sem).wait()


def _dispatch(dest, hn, xs0, ch):
    m = dest.shape[0]
    return pl.pallas_call(
        functools.partial(_dispatch_kernel, ch=ch),
        grid=(m // ch,),
        in_specs=[pl.BlockSpec((1, 1, ch), lambda i: (i, 0, 0), memory_space=pltpu.SMEM),
                  pl.BlockSpec((ch // 2, D_MODEL), lambda i: (i, 0)),
                  pl.BlockSpec(memory_space=pl.ANY)],
        out_specs=pl.BlockSpec(memory_space=pl.ANY),
        out_shape=jax.ShapeDtypeStruct(xs0.shape, xs0.dtype),
        scratch_shapes=[pltpu.SemaphoreType.DMA(())],
        input_output_aliases={2: 0},
        compiler_params=_params(("arbitrary",), disable_bounds_checks=True, has_side_effects=True),
        name="dispatch",
    )(dest.reshape(m // ch, 1, ch), hn, xs0)


def _expert_kernel(be_ref, nu_ref, x_ref, wg_ref, wu_ref, wd_ref, o_ref, wg_s, wu_s, wd_s):
    b = pl.program_id(0)
    prev = be_ref[jnp.maximum(b - 1, 0)]

    @pl.when(b < nu_ref[0])
    def _():
        @pl.when((b == 0) | (be_ref[b] != prev))
        def _():
            wg_s[...] = wg_ref[...].astype(BF16)
            wu_s[...] = wu_ref[...].astype(BF16)
            wd_s[...] = wd_ref[...].astype(BF16)

        x = x_ref[...].astype(BF16)
        gate = _dot(x, wg_s[...])
        h = gate * _sigmoid(gate) * _dot(x, wu_s[...])
        o_ref[...] = _dot(h.astype(BF16), wd_s[...])

    @pl.when(b >= nu_ref[0])
    def _():
        o_ref[...] = jnp.zeros(o_ref.shape, F32)


def _experts(block_e, n_used, xs, wg, wu, wd, bm):
    rows = xs.shape[0]
    nb = rows // bm
    grid_spec = pltpu.PrefetchScalarGridSpec(
        num_scalar_prefetch=2,
        grid=(nb,),
        in_specs=[pl.BlockSpec((bm, D_MODEL), lambda b, be, nu: (b, 0)),
                  pl.BlockSpec((None, D_MODEL, D_EXPERT), lambda b, be, nu: (be[b], 0, 0)),
                  pl.BlockSpec((None, D_MODEL, D_EXPERT), lambda b, be, nu: (be[b], 0, 0)),
                  pl.BlockSpec((None, D_EXPERT, D_MODEL), lambda b, be, nu: (be[b], 0, 0))],
        out_specs=pl.BlockSpec((bm, D_MODEL), lambda b, be, nu: (b, 0)),
        scratch_shapes=[pltpu.VMEM((D_MODEL, D_EXPERT), BF16), pltpu.VMEM((D_MODEL, D_EXPERT), BF16),
                        pltpu.VMEM((D_EXPERT, D_MODEL), BF16)])
    return pl.pallas_call(
        _expert_kernel,
        grid_spec=grid_spec,
        out_shape=jax.ShapeDtypeStruct((rows, D_MODEL), F32),
        compiler_params=_params(("arbitrary",)),
        name="experts",
    )(block_e, n_used, xs, wg, wu, wd)


def _combine_kernel(dest_ref, gates_ref, x_ref, ys_ref, o_ref, buf, sem, *, tc):
    def issue(t, carry):
        for s in range(2):
            _row_copy(ys_ref, buf.at[s], dest_ref[0, 0, 2 * t + s], t, sem).start()
        return carry

    lax.fori_loop(0, tc, issue, 0, unroll=8)
    for s in range(2):
        pltpu.make_async_copy(ys_ref.at[pl.ds(0, tc)], buf.at[s], sem).wait()
    gates = gates_ref[...]
    o_ref[...] = x_ref[...] + gates[:, 0:1] * buf[0] + gates[:, 1:2] * buf[1]


def _combine(dest, gates, x1, ys, tc):
    n = x1.shape[0]
    return pl.pallas_call(
        functools.partial(_combine_kernel, tc=tc),
        grid=(n // tc,),
        in_specs=[pl.BlockSpec((1, 1, 2 * tc), lambda i: (i, 0, 0), memory_space=pltpu.SMEM),
                  pl.BlockSpec((tc, LANES), lambda i: (i, 0)),
                  pl.BlockSpec((tc, D_MODEL), lambda i: (i, 0)),
                  pl.BlockSpec(memory_space=pl.ANY)],
        out_specs=pl.BlockSpec((tc, D_MODEL), lambda i: (i, 0)),
        out_shape=jax.ShapeDtypeStruct((n, D_MODEL), F32),
        scratch_shapes=[pltpu.VMEM((2, tc, D_MODEL), F32), pltpu.SemaphoreType.DMA(())],
        compiler_params=_params(("arbitrary",), disable_bounds_checks=True),
        name="combine",
    )(dest.reshape(n // tc, 1, 2 * tc), gates, x1, ys)


def _moe(x1s, p, tms, bm):
    routed = [_router(x1, p['norm_ffn'], p['wr_hi'], p['wr_lo'], p['b_router'], tm) for x1, tm in zip(x1s, tms)]
    flat_e = jnp.concatenate([ids[:, :2].reshape(-1) for _, ids, _ in routed])
    m = flat_e.shape[0]
    onehot = (flat_e[:, None] == jnp.arange(N_EXPERTS, dtype=I32)[None, :]).astype(I32)
    csum = jnp.cumsum(onehot, axis=0)
    counts = csum[-1]
    padded = (counts + bm - 1) // bm * bm
    pad_end = jnp.cumsum(padded)
    pad_start = pad_end - padded
    dest = jnp.sum(onehot * (csum - 1 + pad_start[None, :]), axis=1).astype(I32)
    nb = -(-(m + N_EXPERTS * (bm - 1)) // bm)
    block_start = jnp.arange(nb, dtype=I32) * bm
    block_e = jnp.minimum(jnp.sum((pad_end[None, :] <= block_start[:, None]).astype(I32), axis=1),
                          N_EXPERTS - 1).astype(I32)
    n_used = (pad_end[-1] // bm).astype(I32).reshape(1)
    xs = jnp.zeros((nb * bm, D_MODEL), F32)
    dests = []
    first = 0
    for hn, _, _ in routed:
        dests.append(dest[first:first + 2 * hn.shape[0]])
        first += 2 * hn.shape[0]
        xs = _dispatch(dests[-1], hn, xs, min(2 * hn.shape[0], 1024))
    ys = _experts(block_e, n_used, xs, p['w_gate'], p['w_up'], p['w_down'], bm)
    return [_combine(d, gates, x1, ys, min(x1.shape[0], 256)) for d, (_, _, gates), x1 in zip(dests, routed, x1s)]


def _prep(norm_mix, w_in, s5_a_re, s5_a_im, s5_log_dt, s5_b_re, s5_b_im, s5_c_re, s5_c_im, s5_d,
          w_glu, q_norm, k_norm, idx_k_norm, w_branch_a, w_branch_b, w_out, norm_ffn,
          w_router_group, b_router_group, w_router_expert, b_router_expert, w_gate, w_up, w_down):
    pts = []
    acc = 0
    for s in IN_SIZES[:-1]:
        acc += s
        pts.append(acc)
    w_u, w_q, w_k, w_v, w_qi, w_ki, w_wi, w_ga, w_gb = jnp.split(w_in, pts, axis=1)
    pad = PROJ_COLS - sum(IN_SIZES)
    w_proj = jnp.concatenate([w_u, w_q, w_ga, w_gb, w_qi, w_k, w_v, w_ki, w_wi,
                              jnp.zeros((D_MODEL, pad), w_in.dtype)], axis=1).astype(BF16)
    one = lambda k: jnp.ones((k,), F32)
    gain = jnp.concatenate([one(S5_WIDTH), jnp.tile(q_norm.astype(F32), N_HEADS), one(2 * D_MODEL),
                            one(N_IDX_HEADS * IDX_DIM), jnp.tile(k_norm.astype(F32), N_KV_HEADS), one(KV_WIDTH),
                            idx_k_norm.astype(F32), one(N_IDX_HEADS + pad)]).reshape(1, PROJ_COLS)
    half = 256
    wv, wg = w_glu[:, :S5_WIDTH], w_glu[:, S5_WIDTH:]
    w_glu_p = jnp.concatenate(
        [jnp.concatenate([wv[:, c * half:(c + 1) * half], wg[:, c * half:(c + 1) * half]], axis=1)
         for c in range(S5_WIDTH // half)], axis=1).astype(BF16)
    w_r = jnp.concatenate([w_router_expert.astype(F32), w_router_group.astype(F32),
                           jnp.zeros((D_MODEL, LANES - N_EXPERTS - N_GROUPS), F32)], axis=1)
    wr_hi = w_r.astype(BF16)
    wr_lo = (w_r - wr_hi.astype(F32)).astype(BF16)
    b_r = jnp.concatenate([b_router_expert.astype(F32), b_router_group.astype(F32),
                           jnp.zeros((LANES - N_EXPERTS - N_GROUPS,), F32)]).reshape(1, LANES)
    return dict(norm_mix=norm_mix.astype(F32).reshape(1, D_MODEL), w_proj=w_proj, gain=gain,
                s5=_s5_tables(s5_a_re, s5_a_im, s5_log_dt, s5_b_re, s5_b_im, s5_c_re, s5_c_im, S5_MAX_CHUNK),
                s5_d=s5_d.astype(F32).reshape(1, S5_WIDTH), w_glu=w_glu_p,
                w_a=w_branch_a.astype(BF16), w_b=w_branch_b.astype(BF16), w_out=w_out.astype(BF16),
                norm_ffn=norm_ffn.astype(F32).reshape(1, D_MODEL), wr_hi=wr_hi, wr_lo=wr_lo, b_router=b_r,
                w_gate=w_gate, w_up=w_up, w_down=w_down)


def _mixer(x, h0_re, h0_im, k_past, v_past, ki_past, p, *, tm, lc, tt, tq, kb):
    bsz, t, _ = x.shape
    n = bsz * t
    x2 = x.reshape(n, D_MODEL)
    proj = _inproj(x2, p['norm_mix'], p['w_proj'], p['gain'], tm)
    proj3 = proj.reshape(bsz, t, PROJ_COLS)
    k = proj3[:, :, COL_K:COL_K + KV_WIDTH]
    v = proj3[:, :, COL_V:COL_V + KV_WIDTH]
    ki = proj3[:, :, COL_KI:COL_KI + IDX_DIM]
    n_state = S5_GROUPS * S5_STATE
    tables = dict(p['s5'])
    for name in ('air', 'aii', 'apr', 'api'):
        tables[name] = tables[name][:lc]
    tables['tri'] = jnp.tile(tables['tri'][:lc, :lc], (1, 2))
    yg, s_re, s_im = _s5(proj3, h0_re.reshape(bsz, 1, n_state), h0_im.reshape(bsz, 1, n_state),
                         tables, p['s5_d'], lc, tt)
    y_a = _glu(yg.reshape(n, S5_WIDTH), p['w_glu'], tm)
    if k_past is None:
        pos0 = 0
        k_all, v_all, ki_all = k, v, ki
    else:
        pos0 = k_past.shape[1]
        k_all = jnp.concatenate([k_past.reshape(bsz, pos0, KV_WIDTH), k], axis=1)
        v_all = jnp.concatenate([v_past.reshape(bsz, pos0, KV_WIDTH), v], axis=1)
        ki_all = jnp.concatenate([ki_past, ki], axis=1)
    n_keys = k_all.shape[1]
    n_top = min(TOP_K_MAX, n_keys // 4)
    nkp = -(-n_keys // kb) * kb
    padk = lambda a: jnp.pad(a.astype(BF16), ((0, 0), (0, nkp - n_keys), (0, 0)))
    kit = jnp.swapaxes(padk(ki_all), 1, 2)
    kt = jnp.swapaxes(padk(k_all), 1, 2)
    y_b = _dsa(proj3, kit, kt, padk(v_all), tq=tq, kb=kb, pos0=pos0, n_keys=n_keys, n_top=n_top)
    merged = _merge(y_a, y_b.reshape(n, ATT_WIDTH), p['w_a'], p['w_b'], proj, tm)
    x1 = _outproj(merged, p['w_out'], x2, tm)
    return (x1, s_re.reshape(bsz, S5_GROUPS, S5_STATE),
            s_im.reshape(bsz, S5_GROUPS, S5_STATE), k.reshape(bsz, t, N_KV_HEADS, HEAD_DIM),
            v.reshape(bsz, t, N_KV_HEADS, HEAD_DIM), ki)


def kernel(x_prompt, x_sample, state_s5_re, state_s5_im, cache_k, cache_v, cache_idx_k, norm_mix, w_in, s5_a_re, s5_a_im, s5_log_dt, s5_b_re, s5_b_im, s5_c_re, s5_c_im, s5_d, w_glu, q_norm, k_norm, idx_k_norm, w_branch_a, w_branch_b, w_out, norm_ffn, w_router_group, b_router_group, w_router_expert, b_router_expert, w_gate, w_up, w_down):
    p = _prep(norm_mix, w_in, s5_a_re, s5_a_im, s5_log_dt, s5_b_re, s5_b_im, s5_c_re, s5_c_im, s5_d,
              w_glu, q_norm, k_norm, idx_k_norm, w_branch_a, w_branch_b, w_out, norm_ffn,
              w_router_group, b_router_group, w_router_expert, b_router_expert, w_gate, w_up, w_down)
    h0 = jnp.zeros((x_prompt.shape[0], S5_GROUPS, S5_STATE), F32)
    tm_p, tm_s = 1024, 256
    xp, srp, sip, kp, vp, kip = _mixer(x_prompt, h0, h0, None, None, None, p,
                                       tm=tm_p, lc=64, tt=1024, tq=128, kb=512)
    xs, srs, sis, ks, vs, kis = _mixer(x_sample, state_s5_re, state_s5_im, cache_k, cache_v, cache_idx_k, p,
                                       tm=tm_s, lc=32, tt=32, tq=32, kb=384)
    yp, ys = _moe([xp, xs], p, [tm_p, tm_s], 256)
    return (yp.reshape(x_prompt.shape), ys.reshape(x_sample.shape), srp, sip, kp, vp, kip, srs, sis, ks, vs, kis)
```

```python
import functools
import math

import jax
import jax.numpy as jnp
from jax import lax
from jax.experimental import pallas as pl
from jax.experimental.pallas import tpu as pltpu

F32 = jnp.float32
BF16 = jnp.bfloat16
I32 = jnp.int32

D_MODEL = 2048
CHUNK = 64
EPS = 1e-6
S5_WIDTH = 1024
S5_GROUP = 16
S5_GROUPS = 64
S5_STATE = 64
S5_MAX_RE = -1e-4
N_HEADS = 8
N_KV_HEADS = 2
Q_PER_KV = 4
HEAD_DIM = 128
ATT_WIDTH = 1024
KV_WIDTH = 256
N_IDX_HEADS = 8
IDX_DIM = 64
TOP_K_MAX = 256
N_GROUPS = 4
EXPERTS_PER_GROUP = 8
N_EXPERTS = 32
D_EXPERT = 512
IN_SIZES = (S5_WIDTH, ATT_WIDTH, KV_WIDTH, KV_WIDTH, N_IDX_HEADS * IDX_DIM, IDX_DIM, N_IDX_HEADS, D_MODEL, D_MODEL)

LANES = 128
PROJ_TN = 512
COL_U, COL_Q, COL_GA, COL_GB, COL_QI, COL_K, COL_V, COL_KI, COL_WI = 0, 1024, 2048, 4096, 6144, 6656, 6912, 7168, 7232
PROJ_COLS = 7680
S5_LANE_BLOCKS = S5_WIDTH // LANES
S5_BLOCK_STATE = (LANES // S5_GROUP) * S5_STATE
S5_MAX_CHUNK = 64
VMEM_LIMIT = 56 * 1024 * 1024
INT_MIN = -2 ** 31
KEY_LOWEST_FINITE = -2 ** 31 + 0x00800000
NEG_BIG = -1e30
LOG2E = 1.4426950408889634
FAST_SOFTMAX_BOUND = 40.0
TOPK_PER_LANE = 12


def _dot(a, b):
    return jnp.dot(a, b, preferred_element_type=F32)


def _split_bf16(x):
    hi = x.astype(BF16)
    lo = (x - hi.astype(F32)).astype(BF16)
    return hi, lo


def _sigmoid(x):
    return 0.5 * jnp.tanh(0.5 * x) + 0.5


def _params(sem, **kw):
    return pltpu.CompilerParams(dimension_semantics=sem, vmem_limit_bytes=VMEM_LIMIT, **kw)


def _group_norm(a, gain):
    ms = jnp.mean(a * a, axis=-1, keepdims=True)
    return a * lax.rsqrt(ms + EPS) * gain


def _inproj_kernel(x_ref, g_ref, w_ref, gain_ref, o_ref, xn_ref):
    j = pl.program_id(1)

    @pl.when(j == 0)
    def _():
        xf = x_ref[...]
        ms = jnp.mean(xf * xf, axis=-1, keepdims=True)
        xn_ref[...] = (xf * lax.rsqrt(ms + EPS) * g_ref[...]).astype(BF16)

    acc = _dot(xn_ref[...], w_ref[...])
    gain = gain_ref[...]
    groups = [slice(c * LANES, (c + 1) * LANES) for c in range(PROJ_TN // LANES)]

    @pl.when((j < 2) | (j == 12))
    def _():
        o_ref[...] = acc

    @pl.when((j == 2) | (j == 3))
    def _():
        for s in groups:
            o_ref[:, s] = _group_norm(acc[:, s], gain[:, s])

    @pl.when((j >= 4) & (j < 12))
    def _():
        o_ref[...] = _sigmoid(acc)

    @pl.when(j == 13)
    def _():
        for s in groups[:2]:
            o_ref[:, s] = _group_norm(acc[:, s], gain[:, s])
        o_ref[:, 2 * LANES:] = acc[:, 2 * LANES:]

    @pl.when(j == 14)
    def _():
        a = acc[:, :LANES]
        lane = lax.broadcasted_iota(I32, a.shape, 1)
        is_ki = lane < IDX_DIM
        ms = jnp.sum(jnp.where(is_ki, a * a, 0.0), axis=-1, keepdims=True) * (1.0 / IDX_DIM)
        ki = a * lax.rsqrt(ms + EPS) * gain[:, :LANES]
        wi = a * (N_IDX_HEADS ** -0.5) * (IDX_DIM ** -0.5)
        o_ref[:, :LANES] = jnp.where(is_ki, ki, wi)
        o_ref[:, LANES:] = acc[:, LANES:]


def _inproj(x, g, w, gain, tm):
    n = x.shape[0]
    return pl.pallas_call(
        _inproj_kernel,
        grid=(n // tm, PROJ_COLS // PROJ_TN),
        in_specs=[pl.BlockSpec((tm, D_MODEL), lambda i, j: (i, 0)),
                  pl.BlockSpec((1, D_MODEL), lambda i, j: (0, 0)),
                  pl.BlockSpec((D_MODEL, PROJ_TN), lambda i, j: (0, j)),
                  pl.BlockSpec((1, PROJ_TN), lambda i, j: (0, j))],
        out_specs=pl.BlockSpec((tm, PROJ_TN), lambda i, j: (i, j)),
        out_shape=jax.ShapeDtypeStruct((n, PROJ_COLS), F32),
        scratch_shapes=[pltpu.VMEM((tm, D_MODEL), BF16)],
        compiler_params=_params(("arbitrary", "arbitrary")),
        name="inproj",
    )(x, g, w, gain)


def _gelu_tanh(y):
    return 0.5 * y * (1.0 + jnp.tanh(math.sqrt(2.0 / math.pi) * (y + 0.044715 * (y * y * y))))


def _s5_kernel(u_ref, h0r_ref, h0i_ref, bhi_ref, blo_ref, air_ref, aii_ref, apr_ref, api_ref,
               a1r_ref, a1i_ref, tri_ref, cd_ref, d_ref, yg_ref, sr_ref, si_ref,
               bu_scr, h_scr, hr_scr, hi_scr, *, lc, tt):
    t = pl.program_id(2)
    ns = S5_BLOCK_STATE

    @pl.when(t == 0)
    def _():
        hr_scr[...] = h0r_ref[...]
        hi_scr[...] = h0i_ref[...]

    u = u_ref[...]
    u_hi, u_lo = _split_bf16(u)
    bhi = bhi_ref[...]
    bu_scr[...] = _dot(u_hi, bhi) + _dot(jnp.concatenate([u_lo, u_hi], axis=1), blo_ref[...])
    tri = tri_ref[...]
    air, aii, apr, api = air_ref[...], aii_ref[...], apr_ref[...], api_ref[...]
    a1r, a1i = a1r_ref[...], a1i_ref[...]

    def chunk(s, carry):
        h_re, h_im = carry
        r0 = pl.multiple_of(s * lc, lc)
        br = bu_scr[pl.ds(r0, lc), 0:ns]
        bi = bu_scr[pl.ds(r0, lc), ns:2 * ns]
        z = jnp.concatenate([air * br - aii * bi, air * bi + aii * br], axis=1)
        z_hi, z_lo = _split_bf16(z)
        c = _dot(tri, jnp.concatenate([z_hi, z_lo], axis=0))
        cr = c[:, 0:ns] + (a1r * h_re - a1i * h_im)
        ci = c[:, ns:2 * ns] + (a1r * h_im + a1i * h_re)
        hr = apr * cr - api * ci
        hi = apr * ci + api * cr
        h_scr[pl.ds(r0, lc), 0:ns] = hr.astype(BF16)
        h_scr[pl.ds(r0, lc), ns:2 * ns] = hi.astype(BF16)
        return hr[lc - 1:lc, :], hi[lc - 1:lc, :]

    h_re, h_im = lax.fori_loop(0, tt // lc, chunk, (hr_scr[...], hi_scr[...]), unroll=min(4, tt // lc))
    hr_scr[...] = h_re
    hi_scr[...] = h_im
    sr_ref[...] = h_re
    si_ref[...] = h_im
    y = _dot(h_scr[...], cd_ref[...]) + d_ref[...] * u
    yg_ref[...] = _gelu_tanh(y).astype(BF16)


def _s5_tables(a_re, a_im, log_dt, b_re, b_im, c_re, c_im, lc):
    lr = jnp.minimum(a_re.astype(F32), S5_MAX_RE)
    li = a_im.astype(F32)
    dt = jnp.exp(log_dt.astype(F32))[:, None]
    mag = jnp.exp(lr * dt)
    lbr = mag * jnp.cos(li * dt)
    lbi = mag * jnp.sin(li * dt)
    den = lr * lr + li * li
    fr = ((lbr - 1.0) * lr + lbi * li) / den
    fi = (lbi * lr - (lbr - 1.0) * li) / den
    br = b_re.astype(F32)
    bi = b_im.astype(F32)
    bbr = fr[..., None] * br - fi[..., None] * bi
    bbi = fr[..., None] * bi + fi[..., None] * br
    j = jnp.arange(lc, dtype=F32)[:, None, None]
    ang = j * (li * dt)[None]
    lmag = j * (lr * dt)[None]
    n_state = S5_GROUPS * S5_STATE
    apr = (jnp.exp(lmag) * jnp.cos(ang)).reshape(lc, n_state)
    api = (jnp.exp(lmag) * jnp.sin(ang)).reshape(lc, n_state)
    air = (jnp.exp(-lmag) * jnp.cos(ang)).reshape(lc, n_state)
    aii = (-jnp.exp(-lmag) * jnp.sin(ang)).reshape(lc, n_state)
    a1r = lbr.reshape(1, n_state)
    a1i = lbi.reshape(1, n_state)
    gpb = LANES // S5_GROUP
    eye = jnp.eye(gpb, dtype=F32)

    def bdiag(b):
        return jnp.einsum('kgpc,gh->kgchp', b.reshape(S5_LANE_BLOCKS, gpb, S5_STATE, S5_GROUP), eye).reshape(
            S5_LANE_BLOCKS, LANES, S5_BLOCK_STATE)

    def cdiag(c):
        return jnp.einsum('kgcp,gh->kgphc', c.reshape(S5_LANE_BLOCKS, gpb, S5_GROUP, S5_STATE), eye).reshape(
            S5_LANE_BLOCKS, S5_BLOCK_STATE, LANES)

    bd = jnp.concatenate([bdiag(bbr), bdiag(bbi)], axis=-1)
    bd_hi = bd.astype(BF16)
    bd_lo = (bd - bd_hi.astype(F32)).astype(BF16)
    cd = jnp.concatenate([cdiag(c_re.astype(F32)), -cdiag(c_im.astype(F32))], axis=1).astype(BF16)
    tri = jnp.tril(jnp.ones((lc, lc), F32)).astype(BF16)
    bd_lo = jnp.concatenate([bd_hi, bd_lo], axis=1)
    return dict(bd_hi=bd_hi, bd_lo=bd_lo, air=air, aii=aii, apr=apr, api=api, a1r=a1r, a1i=a1i, tri=tri, cd=cd)


def _s5(proj3, h0_re, h0_im, tb, dvec, lc, tt):
    bsz, t, _ = proj3.shape
    ns = S5_BLOCK_STATE
    n_state = S5_GROUPS * S5_STATE
    tab = lambda: pl.BlockSpec((lc, ns), lambda b, k, i: (0, k))
    row = lambda: pl.BlockSpec((1, ns), lambda b, k, i: (0, k))
    st = lambda: pl.BlockSpec((None, 1, ns), lambda b, k, i: (b, 0, k))
    return pl.pallas_call(
        functools.partial(_s5_kernel, lc=lc, tt=tt),
        grid=(bsz, S5_LANE_BLOCKS, t // tt),
        in_specs=[pl.BlockSpec((None, tt, LANES), lambda b, k, i: (b, i, k)),
                  st(), st(),
                  pl.BlockSpec((None, LANES, 2 * ns), lambda b, k, i: (k, 0, 0)),
                  pl.BlockSpec((None, 2 * LANES, 2 * ns), lambda b, k, i: (k, 0, 0)),
                  tab(), tab(), tab(), tab(), row(), row(),
                  pl.BlockSpec((lc, 2 * lc), lambda b, k, i: (0, 0)),
                  pl.BlockSpec((None, 2 * ns, LANES), lambda b, k, i: (k, 0, 0)),
                  pl.BlockSpec((1, LANES), lambda b, k, i: (0, k))],
        out_specs=[pl.BlockSpec((None, tt, LANES), lambda b, k, i: (b, i, k)), st(), st()],
        out_shape=[jax.ShapeDtypeStruct((bsz, t, S5_WIDTH), BF16),
                   jax.ShapeDtypeStruct((bsz, 1, n_state), F32),
                   jax.ShapeDtypeStruct((bsz, 1, n_state), F32)],
        scratch_shapes=[pltpu.VMEM((tt, 2 * ns), F32), pltpu.VMEM((tt, 2 * ns), BF16),
                        pltpu.VMEM((1, ns), F32), pltpu.VMEM((1, ns), F32)],
        compiler_params=_params(("arbitrary", "arbitrary", "arbitrary")),
        name="s5",
    )(proj3, h0_re, h0_im, tb['bd_hi'], tb['bd_lo'], tb['air'], tb['aii'], tb['apr'], tb['api'],
      tb['a1r'], tb['a1i'], tb['tri'], tb['cd'], dvec)


def _glu_kernel(y_ref, w_ref, o_ref):
    acc = _dot(y_ref[...], w_ref[...])
    half = acc.shape[1] // 2
    o_ref[...] = (acc[:, :half] * _sigmoid(acc[:, half:])).astype(BF16)


def _glu(yg, w, tm):
    n = yg.shape[0]
    tn = 512
    return pl.pallas_call(
        _glu_kernel,
        grid=(n // tm, 2 * S5_WIDTH // tn),
        in_specs=[pl.BlockSpec((tm, S5_WIDTH), lambda i, j: (i, 0)),
                  pl.BlockSpec((S5_WIDTH, tn), lambda i, j: (0, j))],
        out_specs=pl.BlockSpec((tm, tn // 2), lambda i, j: (i, j)),
        out_shape=jax.ShapeDtypeStruct((n, S5_WIDTH), BF16),
        compiler_params=_params(("arbitrary", "arbitrary")),
        name="glu",
    )(yg, w)


def _dsa_kernel(q_ref, qi_ref, kiw_ref, kit_ref, kt_ref, v_ref, ut_ref, o_ref,
                key_scr, qs_scr, qis_scr, wis_scr, m_scr, l_scr, acc_scr,
                kmax_scr, lsum_scr, cand_scr, ckey_scr, thr_scr, cnt_scr,
                *, tq, kb, pos0, n_keys, n_top, nt):
    i = pl.program_id(1)
    nkp = key_scr.shape[2]
    n_cand = cand_scr.shape[1]
    slot_s = i % 2
    slot_a = 1 - slot_s
    tile_s = jnp.minimum(i, nt - 1)
    tile_a = jnp.maximum(i - 1, 0)

    def n_blocks(tile):
        k_end = jnp.minimum(n_keys, (pos0 + tile * tq + tq - 1) // CHUNK * CHUNK + CHUNK)
        return (k_end + kb - 1) // kb

    nkb_s = jnp.where(i < nt, n_blocks(tile_s), 0)
    nkb_a = jnp.where(i >= 1, n_blocks(tile_a), 0)

    @pl.when(i == 0)
    def _():
        def norm_block(b, carry):
            c0 = pl.multiple_of(b * kb, kb)
            out = []
            for g in range(N_KV_HEADS):
                kk = kt_ref[g * HEAD_DIM:(g + 1) * HEAD_DIM, pl.ds(c0, kb)].astype(F32)
                out.append(jnp.maximum(carry[g], jnp.sum(kk * kk, axis=0, keepdims=True)))
            return tuple(out)

        res = lax.fori_loop(0, nkp // kb, norm_block, (jnp.zeros((1, kb), F32),) * N_KV_HEADS)
        for g in range(N_KV_HEADS):
            kmax_scr[g] = jnp.broadcast_to(jnp.max(res[g], axis=-1, keepdims=True), (1, LANES))

    q = q_ref[...] * (HEAD_DIM ** -0.5 * LOG2E)
    for h in range(N_HEADS):
        g, r = divmod(h, Q_PER_KV)
        qs_scr[g, r * tq:(r + 1) * tq, :] = q[:, h * HEAD_DIM:(h + 1) * HEAD_DIM].astype(BF16)
    qi = qi_ref[...]
    kiw = kiw_ref[...]
    for h in range(N_IDX_HEADS):
        qis_scr[h * tq:(h + 1) * tq, :] = qi[:, h * IDX_DIM:(h + 1) * IDX_DIM].astype(BF16)
        wis_scr[h * tq:(h + 1) * tq, :] = kiw[:, IDX_DIM + h:IDX_DIM + h + 1]

    q_chunk = (pos0 + tile_s * tq + lax.broadcasted_iota(I32, (tq, 1), 0)) // CHUNK

    def to_key(x):
        bits = pltpu.bitcast(x, I32)
        return jnp.where(bits < 0, bits ^ 0x7FFFFFFF, bits)

    cand_scr[slot_s] = jnp.full(cand_scr.shape[1:], -jnp.inf, F32)

    def score_block(b):
        c0 = pl.multiple_of(b * kb, kb)
        s = _dot(qis_scr[...], kit_ref[:, pl.ds(c0, kb)])
        s = jnp.maximum(s, 0.0) * wis_scr[...]
        sc = s[0:tq]
        for h in range(1, N_IDX_HEADS):
            sc = sc + s[h * tq:(h + 1) * tq]
        k_pos = c0 + lax.broadcasted_iota(I32, (1, kb), 1)
        adm = (k_pos // CHUNK <= q_chunk) & (k_pos < n_keys)
        blk = jnp.where(adm, sc + 0.0, -jnp.inf)
        key_scr[slot_s, :, pl.ds(c0, kb)] = blk
        for rg in range(tq // 8):
            rows = slice(rg * 8, rg * 8 + 8)
            top = [cand_scr[slot_s, j, rows, :] for j in range(n_cand)]
            for c in range(kb // LANES):
                x = blk[rows, c * LANES:(c + 1) * LANES]
                for j in range(n_cand):
                    hi = jnp.maximum(top[j], x)
                    x = jnp.minimum(top[j], x)
                    top[j] = hi
            for j in range(n_cand):
                cand_scr[slot_s, j, rows, :] = top[j]

    def count_all(hit_fn):
        def body(b, acc):
            c0 = pl.multiple_of(b * kb, kb)
            hit = hit_fn(key_scr[slot_a, :, pl.ds(c0, kb)])
            for c in range(kb // LANES):
                acc = acc + hit[:, c * LANES:(c + 1) * LANES]
            return acc
        acc = lax.fori_loop(0, nkb_a, body, jnp.zeros((tq, LANES), F32))
        return jnp.sum(acc, axis=-1, keepdims=True)

    def count_cand(cand):
        acc = jnp.zeros((tq, LANES), F32)
        for j in range(n_cand):
            acc = acc + jnp.where(ckey_scr[j] >= cand, 1.0, 0.0)
        return jnp.sum(acc, axis=-1, keepdims=True)

    def bisect(count_fn, first_bit=0, prefix=None):
        def bit_pass(it, t_off):
            cand_off = t_off | lax.shift_left(jnp.int32(1), 31 - it)
            cnt = count_fn(cand_off ^ INT_MIN)
            return jnp.where(cnt >= n_top, cand_off, t_off)
        t_off = lax.fori_loop(first_bit, 32, bit_pass, jnp.zeros((tq, 1), I32) if prefix is None else prefix)
        key = jnp.maximum(t_off ^ INT_MIN, KEY_LOWEST_FINITE)
        return pltpu.bitcast(jnp.where(key < 0, key ^ 0x7FFFFFFF, key), F32)

    def candidate_bracket():
        hi = to_key(jnp.max(cand_scr[slot_a, 0], axis=-1, keepdims=True)) ^ INT_MIN
        lo = to_key(jnp.min(cand_scr[slot_a, 1], axis=-1, keepdims=True)) ^ INT_MIN
        n_common = jnp.min(lax.clz(hi ^ lo).astype(F32)).astype(I32)
        mask = jnp.where(n_common > 0, lax.shift_left(jnp.int32(-1), 32 - jnp.maximum(n_common, 1)), 0)
        return n_common, hi & mask

    @pl.when(i >= 1)
    def _():
        for j in range(n_cand):
            ckey_scr[j] = to_key(cand_scr[slot_a, j])
        thr_cand = bisect(count_cand, *candidate_bracket()) if n_top <= 2 * LANES else bisect(count_cand)
        thr_scr[...] = thr_cand
        n_all = count_all(lambda x: jnp.where(x >= thr_cand, 1.0, 0.0))
        n_listed = jnp.zeros((tq, LANES), F32)
        for j in range(n_cand):
            n_listed = n_listed + jnp.where(cand_scr[slot_a, j] >= thr_cand, 1.0, 0.0)
        cnt_scr[...] = n_all
        overflow = jnp.max(n_all - jnp.sum(n_listed, axis=-1, keepdims=True)) > 0.0

        @pl.when(overflow)
        def _():
            thr_all = bisect(lambda cand: count_all(lambda x: jnp.where(to_key(x) >= cand, 1.0, 0.0)))
            thr_scr[...] = thr_all
            cnt_scr[...] = count_all(lambda x: jnp.where(x >= thr_all, 1.0, 0.0))

        @pl.when(jnp.max(cnt_scr[...]) > n_top)
        def _():
            thr_t = thr_scr[...]
            need = n_top - count_all(lambda x: jnp.where(x > thr_t, 1.0, 0.0))

            def drop_surplus(b, seen):
                c0 = pl.multiple_of(b * kb, kb)
                x = key_scr[slot_a, :, pl.ds(c0, kb)]
                tied = x == thr_t
                rank = seen + _dot(jnp.where(tied, 1.0, 0.0).astype(BF16), ut_ref[...])
                key_scr[slot_a, :, pl.ds(c0, kb)] = jnp.where(tied & (rank > need), -jnp.inf, x)
                return rank[:, kb - 1:kb]

            lax.fori_loop(0, nkb_a, drop_surplus, jnp.zeros((tq, 1), F32))

    thr = thr_scr[...]

    acc_scr[...] = jnp.zeros(acc_scr.shape, F32)
    lsum_scr[...] = jnp.zeros(lsum_scr.shape, F32)
    m_scr[...] = jnp.full(m_scr.shape, NEG_BIG, F32)
    l_scr[...] = jnp.zeros(l_scr.shape, F32)
    bound_max = jnp.float32(0.0)
    for g in range(N_KV_HEADS):
        qg = qs_scr[g].astype(F32)
        qn2 = jnp.sum(qg * qg, axis=-1, keepdims=True)
        bound = jnp.sqrt(qn2 * kmax_scr[g][:, 0:1]) * 1.002
        bound_max = jnp.maximum(bound_max, jnp.max(bound))
    fast = bound_max <= FAST_SOFTMAX_BOUND

    def attend_fast(b):
        c0 = pl.multiple_of(b * kb, kb)
        bias = jnp.where(key_scr[slot_a, :, pl.ds(c0, kb)] >= thr, 0.0, NEG_BIG)
        for g in range(N_KV_HEADS):
            lg = _dot(qs_scr[g], kt_ref[g * HEAD_DIM:(g + 1) * HEAD_DIM, pl.ds(c0, kb)])
            parts = []
            for r in range(Q_PER_KV):
                rows = slice(r * tq, (r + 1) * tq)
                e = [jnp.exp2(lg[rows, c * LANES:(c + 1) * LANES] + bias[:, c * LANES:(c + 1) * LANES])
                     for c in range(kb // LANES)]
                ls = e[0]
                for c in range(1, kb // LANES):
                    ls = ls + e[c]
                lsum_scr[g, rows, :] += ls
                parts.append(jnp.concatenate(e, axis=1).astype(BF16))
            p = jnp.concatenate(parts, axis=0)
            acc_scr[g] += _dot(p, v_ref[pl.ds(c0, kb), g * HEAD_DIM:(g + 1) * HEAD_DIM])

    def attend_general(b):
        c0 = pl.multiple_of(b * kb, kb)
        sel = key_scr[slot_a, :, pl.ds(c0, kb)] >= thr
        sel4 = jnp.concatenate([sel] * Q_PER_KV, axis=0)
        for g in range(N_KV_HEADS):
            lg = _dot(qs_scr[g], kt_ref[g * HEAD_DIM:(g + 1) * HEAD_DIM, pl.ds(c0, kb)])
            lg = jnp.where(sel4, lg, NEG_BIG)
            m_old = m_scr[g]
            m_new = jnp.maximum(m_old, jnp.max(lg, axis=-1, keepdims=True))
            p = jnp.exp2(lg - m_new)
            alpha = jnp.exp2(m_old - m_new)
            l_scr[g] = alpha * l_scr[g] + jnp.sum(p, axis=-1, keepdims=True)
            pv = _dot(p.astype(BF16), v_ref[pl.ds(c0, kb), g * HEAD_DIM:(g + 1) * HEAD_DIM])
            acc_scr[g] = alpha * acc_scr[g] + pv
            m_scr[g] = m_new

    def loop(lo, hi, *fns):
        def body(b, carry):
            for fn in fns:
                fn(b)
            return carry
        lax.fori_loop(lo, hi, body, 0)

    n_fused = jnp.where(fast, jnp.minimum(nkb_a, nkb_s), 0)
    loop(0, n_fused, score_block, attend_fast)
    loop(n_fused, nkb_s, score_block)
    loop(n_fused, jnp.where(fast, nkb_a, 0), attend_fast)
    loop(0, jnp.where(fast, 0, nkb_a), attend_general)

    @pl.when((i >= 1) & fast)
    def _():
        for h in range(N_HEADS):
            g, r = divmod(h, Q_PER_KV)
            rows = slice(r * tq, (r + 1) * tq)
            l = jnp.sum(lsum_scr[g, rows, :], axis=-1, keepdims=True)
            o_ref[:, h * HEAD_DIM:(h + 1) * HEAD_DIM] = (acc_scr[g, rows, :] / l).astype(BF16)

    @pl.when((i >= 1) & jnp.logical_not(fast))
    def _():
        for h in range(N_HEADS):
            g, r = divmod(h, Q_PER_KV)
            rows = slice(r * tq, (r + 1) * tq)
            o_ref[:, h * HEAD_DIM:(h + 1) * HEAD_DIM] = (acc_scr[g, rows, :] / l_scr[g, rows, :]).astype(BF16)


def _dsa(proj3, kit, kt, v, *, tq, kb, pos0, n_keys, n_top):
    bsz, t, _ = proj3.shape
    nkp = kit.shape[-1]
    nt = t // tq
    kern = functools.partial(_dsa_kernel, tq=tq, kb=kb, pos0=pos0, n_keys=n_keys, n_top=n_top, nt=nt)
    resident = dict(pipeline_mode=pl.Buffered(1))
    prev = lambda i: jnp.maximum(i - 1, 0)
    cur = lambda i: jnp.minimum(i, nt - 1)
    return pl.pallas_call(
        kern,
        grid=(bsz, nt + 1),
        in_specs=[pl.BlockSpec((None, tq, ATT_WIDTH), lambda b, i: (b, prev(i), COL_Q // ATT_WIDTH)),
                  pl.BlockSpec((None, tq, 512), lambda b, i: (b, cur(i), COL_QI // 512)),
                  pl.BlockSpec((None, tq, 512), lambda b, i: (b, cur(i), COL_KI // 512)),
                  pl.BlockSpec((None, IDX_DIM, nkp), lambda b, i: (b, 0, 0), **resident),
                  pl.BlockSpec((None, KV_WIDTH, nkp), lambda b, i: (b, 0, 0), **resident),
                  pl.BlockSpec((None, nkp, KV_WIDTH), lambda b, i: (b, 0, 0), **resident),
                  pl.BlockSpec((kb, kb), lambda b, i: (0, 0), **resident)],
        out_specs=pl.BlockSpec((None, tq, ATT_WIDTH), lambda b, i: (b, prev(i), 0)),
        out_shape=jax.ShapeDtypeStruct((bsz, t, ATT_WIDTH), BF16),
        scratch_shapes=[pltpu.VMEM((2, tq, nkp), F32),
                        pltpu.VMEM((N_KV_HEADS, Q_PER_KV * tq, HEAD_DIM), BF16),
                        pltpu.VMEM((N_IDX_HEADS * tq, IDX_DIM), BF16),
                        pltpu.VMEM((N_IDX_HEADS * tq, 1), F32),
                        pltpu.VMEM((N_KV_HEADS, Q_PER_KV * tq, 1), F32),
                        pltpu.VMEM((N_KV_HEADS, Q_PER_KV * tq, 1), F32),
                        pltpu.VMEM((N_KV_HEADS, Q_PER_KV * tq, HEAD_DIM), F32),
                        pltpu.VMEM((N_KV_HEADS, 1, LANES), F32),
                        pltpu.VMEM((N_KV_HEADS, Q_PER_KV * tq, LANES), F32),
                        pltpu.VMEM((2, TOPK_PER_LANE, tq, LANES), F32),
                        pltpu.VMEM((TOPK_PER_LANE, tq, LANES), I32),
                        pltpu.VMEM((tq, 1), F32),
                        pltpu.VMEM((tq, 1), F32)],
        compiler_params=_params(("arbitrary", "arbitrary")),
        name="dsa",
    )(proj3, proj3, proj3, kit, kt, v, jnp.triu(jnp.ones((kb, kb), F32)).astype(BF16))


def _merge_kernel(ya_ref, yb_ref, wa_ref, wb_ref, ga_ref, gb_ref, o_ref):
    o_ref[...] = (ga_ref[...] * _dot(ya_ref[...], wa_ref[...])
                  + gb_ref[...] * _dot(yb_ref[...], wb_ref[...])).astype(BF16)


def _merge(ya, yb, wa, wb, proj, tm):
    n = ya.shape[0]
    tn = 512
    return pl.pallas_call(
        _merge_kernel,
        grid=(n // tm, D_MODEL // tn),
        in_specs=[pl.BlockSpec((tm, S5_WIDTH), lambda i, j: (i, 0)),
                  pl.BlockSpec((tm, ATT_WIDTH), lambda i, j: (i, 0)),
                  pl.BlockSpec((S5_WIDTH, tn), lambda i, j: (0, j)),
                  pl.BlockSpec((ATT_WIDTH, tn), lambda i, j: (0, j)),
                  pl.BlockSpec((tm, tn), lambda i, j: (i, COL_GA // tn + j)),
                  pl.BlockSpec((tm, tn), lambda i, j: (i, COL_GB // tn + j))],
        out_specs=pl.BlockSpec((tm, tn), lambda i, j: (i, j)),
        out_shape=jax.ShapeDtypeStruct((n, D_MODEL), BF16),
        compiler_params=_params(("arbitrary", "arbitrary")),
        name="merge",
    )(ya, yb, wa, wb, proj, proj)


def _outproj_kernel(m_ref, w_ref, x_ref, o_ref):
    o_ref[...] = x_ref[...] + _dot(m_ref[...], w_ref[...])


def _outproj(merged, w, x, tm):
    n = x.shape[0]
    tn = 512
    return pl.pallas_call(
        _outproj_kernel,
        grid=(n // tm, D_MODEL // tn),
        in_specs=[pl.BlockSpec((tm, D_MODEL), lambda i, j: (i, 0)),
                  pl.BlockSpec((D_MODEL, tn), lambda i, j: (0, j)),
                  pl.BlockSpec((tm, tn), lambda i, j: (i, j))],
        out_specs=pl.BlockSpec((tm, tn), lambda i, j: (i, j)),
        out_shape=jax.ShapeDtypeStruct((n, D_MODEL), F32),
        compiler_params=_params(("arbitrary", "arbitrary")),
        name="outproj",
    )(merged, w, x)


ROUTER_GROUP_LANE = N_EXPERTS


def _router_kernel(x_ref, g_ref, whi_ref, wlo_ref, b_ref, hn_ref, ids_ref, gates_ref):
    xf = x_ref[...]
    ms = jnp.mean(xf * xf, axis=-1, keepdims=True)
    hn = xf * lax.rsqrt(ms + EPS) * g_ref[...]
    hn_ref[...] = hn
    h_hi, h_lo = _split_bf16(hn)
    whi = whi_ref[...]
    lg = _dot(h_hi, whi) + _dot(h_lo, whi) + _dot(h_hi, wlo_ref[...]) + b_ref[...]
    lane = lax.broadcasted_iota(I32, lg.shape, 1)
    lane_f = lane.astype(F32)
    big = float(LANES)
    is_g = (lane >= ROUTER_GROUP_LANE) & (lane < ROUTER_GROUP_LANE + N_GROUPS)
    g_max = jnp.max(jnp.where(is_g, lg, -jnp.inf), axis=-1, keepdims=True)
    g_den = jnp.sum(jnp.where(is_g, jnp.exp(lg - g_max), 0.0), axis=-1, keepdims=True)
    g_w = 1.0 / g_den
    g_lane = jnp.min(jnp.where(is_g & (lg == g_max), lane_f, big), axis=-1, keepdims=True)
    g_sel = g_lane.astype(I32) - ROUTER_GROUP_LANE
    is_e = (lane < N_EXPERTS) & (lane // EXPERTS_PER_GROUP == g_sel)
    e_max = jnp.max(jnp.where(is_e, lg, -jnp.inf), axis=-1, keepdims=True)
    pe = jnp.where(is_e, jnp.exp(lg - e_max), 0.0)
    pe = pe / jnp.sum(pe, axis=-1, keepdims=True)
    pe = jnp.where(is_e, pe, -1.0)
    p1 = jnp.max(pe, axis=-1, keepdims=True)
    i1 = jnp.min(jnp.where(pe == p1, lane_f, big), axis=-1, keepdims=True)
    pe2 = jnp.where(lane_f == i1, -1.0, pe)
    p2 = jnp.max(pe2, axis=-1, keepdims=True)
    i2 = jnp.min(jnp.where(pe2 == p2, lane_f, big), axis=-1, keepdims=True)
    tot = p1 + p2
    ids_ref[...] = jnp.where(lane == 0, i1, jnp.where(lane == 1, i2, 0.0)).astype(I32)
    gates_ref[...] = jnp.where(lane == 0, g_w * (p1 / tot), jnp.where(lane == 1, g_w * (p2 / tot), 0.0))


def _router(x1, g, whi, wlo, bias, tm):
    n = x1.shape[0]
    return pl.pallas_call(
        _router_kernel,
        grid=(n // tm,),
        in_specs=[pl.BlockSpec((tm, D_MODEL), lambda i: (i, 0)),
                  pl.BlockSpec((1, D_MODEL), lambda i: (0, 0)),
                  pl.BlockSpec((D_MODEL, LANES), lambda i: (0, 0)),
                  pl.BlockSpec((D_MODEL, LANES), lambda i: (0, 0)),
                  pl.BlockSpec((1, LANES), lambda i: (0, 0))],
        out_specs=[pl.BlockSpec((tm, D_MODEL), lambda i: (i, 0)),
                   pl.BlockSpec((tm, LANES), lambda i: (i, 0)),
                   pl.BlockSpec((tm, LANES), lambda i: (i, 0))],
        out_shape=[jax.ShapeDtypeStruct((n, D_MODEL), F32),
                   jax.ShapeDtypeStruct((n, LANES), I32),
                   jax.ShapeDtypeStruct((n, LANES), F32)],
        compiler_params=_params(("arbitrary",)),
        name="router",
    )(x1, g, whi, wlo, bias)


def _row_copy(src, dst, s_row, d_row, sem):
    return pltpu.make_async_copy(src.at[pl.ds(s_row, 1)], dst.at[pl.ds(d_row, 1)], sem)


def _dispatch_kernel(dest_ref, hn_ref, xs_in_ref, xs_ref, sem, *, ch):
    del xs_in_ref

    def issue(t, carry):
        for s in range(2):
            _row_copy(hn_ref, xs_ref, t, dest_ref[0, 0, 2 * t + s], sem).start()
        return carry

    lax.fori_loop(0, ch // 2, issue, 0, unroll=8)
    for s in range(2):
        pltpu.make_async_copy(hn_ref, xs_ref.at[pl.ds(0, ch // 2)], sem).wait()


def _dispatch(dest, hn, xs0, ch):
    m = dest.shape[0]
    return pl.pallas_call(
        functools.partial(_dispatch_kernel, ch=ch),
        grid=(m // ch,),
        in_specs=[pl.BlockSpec((1, 1, ch), lambda i: (i, 0, 0), memory_space=pltpu.SMEM),
                  pl.BlockSpec((ch // 2, D_MODEL), lambda i: (i, 0)),
                  pl.BlockSpec(memory_space=pl.ANY)],
        out_specs=pl.BlockSpec(memory_space=pl.ANY),
        out_shape=jax.ShapeDtypeStruct(xs0.shape, xs0.dtype),
        scratch_shapes=[pltpu.SemaphoreType.DMA(())],
        input_output_aliases={2: 0},
        compiler_params=_params(("arbitrary",), disable_bounds_checks=True, has_side_effects=True),
        name="dispatch",
    )(dest.reshape(m // ch, 1, ch), hn, xs0)


def _expert_kernel(be_ref, nu_ref, x_ref, wg_ref, wu_ref, wd_ref, o_ref, wg_s, wu_s, wd_s):
    b = pl.program_id(0)
    prev = be_ref[jnp.maximum(b - 1, 0)]

    @pl.when(b < nu_ref[0])
    def _():
        @pl.when((b == 0) | (be_ref[b] != prev))
        def _():
            wg_s[...] = wg_ref[...].astype(BF16)
            wu_s[...] = wu_ref[...].astype(BF16)
            wd_s[...] = wd_ref[...].astype(BF16)

        x = x_ref[...].astype(BF16)
        gate = _dot(x, wg_s[...])
        h = gate * _sigmoid(gate) * _dot(x, wu_s[...])
        o_ref[...] = _dot(h.astype(BF16), wd_s[...])

    @pl.when(b >= nu_ref[0])
    def _():
        o_ref[...] = jnp.zeros(o_ref.shape, F32)


def _experts(block_e, n_used, xs, wg, wu, wd, bm):
    rows = xs.shape[0]
    nb = rows // bm
    grid_spec = pltpu.PrefetchScalarGridSpec(
        num_scalar_prefetch=2,
        grid=(nb,),
        in_specs=[pl.BlockSpec((bm, D_MODEL), lambda b, be, nu: (b, 0)),
                  pl.BlockSpec((None, D_MODEL, D_EXPERT), lambda b, be, nu: (be[b], 0, 0)),
                  pl.BlockSpec((None, D_MODEL, D_EXPERT), lambda b, be, nu: (be[b], 0, 0)),
                  pl.BlockSpec((None, D_EXPERT, D_MODEL), lambda b, be, nu: (be[b], 0, 0))],
        out_specs=pl.BlockSpec((bm, D_MODEL), lambda b, be, nu: (b, 0)),
        scratch_shapes=[pltpu.VMEM((D_MODEL, D_EXPERT), BF16), pltpu.VMEM((D_MODEL, D_EXPERT), BF16),
                        pltpu.VMEM((D_EXPERT, D_MODEL), BF16)])
    return pl.pallas_call(
        _expert_kernel,
        grid_spec=grid_spec,
        out_shape=jax.ShapeDtypeStruct((rows, D_MODEL), F32),
        compiler_params=_params(("arbitrary",)),
        name="experts",
    )(block_e, n_used, xs, wg, wu, wd)


def _combine_kernel(dest_ref, gates_ref, x_ref, ys_ref, o_ref, buf, sem, *, tc):
    def issue(t, carry):
        for s in range(2):
            _row_copy(ys_ref, buf.at[s], dest_ref[0, 0, 2 * t + s], t, sem).start()
        return carry

    lax.fori_loop(0, tc, issue, 0, unroll=8)
    for s in range(2):
        pltpu.make_async_copy(ys_ref.at[pl.ds(0, tc)], buf.at[s], sem).wait()
    gates = gates_ref[...]
    o_ref[...] = x_ref[...] + gates[:, 0:1] * buf[0] + gates[:, 1:2] * buf[1]


def _combine(dest, gates, x1, ys, tc):
    n = x1.shape[0]
    return pl.pallas_call(
        functools.partial(_combine_kernel, tc=tc),
        grid=(n // tc,),
        in_specs=[pl.BlockSpec((1, 1, 2 * tc), lambda i: (i, 0, 0), memory_space=pltpu.SMEM),
                  pl.BlockSpec((tc, LANES), lambda i: (i, 0)),
                  pl.BlockSpec((tc, D_MODEL), lambda i: (i, 0)),
                  pl.BlockSpec(memory_space=pl.ANY)],
        out_specs=pl.BlockSpec((tc, D_MODEL), lambda i: (i, 0)),
        out_shape=jax.ShapeDtypeStruct((n, D_MODEL), F32),
        scratch_shapes=[pltpu.VMEM((2, tc, D_MODEL), F32), pltpu.SemaphoreType.DMA(())],
        compiler_params=_params(("arbitrary",), disable_bounds_checks=True),
        name="combine",
    )(dest.reshape(n // tc, 1, 2 * tc), gates, x1, ys)


def _moe(x1s, p, tms, bm):
    routed = [_router(x1, p['norm_ffn'], p['wr_hi'], p['wr_lo'], p['b_router'], tm) for x1, tm in zip(x1s, tms)]
    flat_e = jnp.concatenate([ids[:, :2].reshape(-1) for _, ids, _ in routed])
    m = flat_e.shape[0]
    onehot = (flat_e[:, None] == jnp.arange(N_EXPERTS, dtype=I32)[None, :]).astype(I32)
    csum = jnp.cumsum(onehot, axis=0)
    counts = csum[-1]
    padded = (counts + bm - 1) // bm * bm
    pad_end = jnp.cumsum(padded)
    pad_start = pad_end - padded
    dest = jnp.sum(onehot * (csum - 1 + pad_start[None, :]), axis=1).astype(I32)
    nb = -(-(m + N_EXPERTS * (bm - 1)) // bm)
    block_start = jnp.arange(nb, dtype=I32) * bm
    block_e = jnp.minimum(jnp.sum((pad_end[None, :] <= block_start[:, None]).astype(I32), axis=1),
                          N_EXPERTS - 1).astype(I32)
    n_used = (pad_end[-1] // bm).astype(I32).reshape(1)
    xs = jnp.zeros((nb * bm, D_MODEL), F32)
    dests = []
    first = 0
    for hn, _, _ in routed:
        dests.append(dest[first:first + 2 * hn.shape[0]])
        first += 2 * hn.shape[0]
        xs = _dispatch(dests[-1], hn, xs, min(2 * hn.shape[0], 1024))
    ys = _experts(block_e, n_used, xs, p['w_gate'], p['w_up'], p['w_down'], bm)
    return [_combine(d, gates, x1, ys, min(x1.shape[0], 256)) for d, (_, _, gates), x1 in zip(dests, routed, x1s)]


def _prep(norm_mix, w_in, s5_a_re, s5_a_im, s5_log_dt, s5_b_re, s5_b_im, s5_c_re, s5_c_im, s5_d,
          w_glu, q_norm, k_norm, idx_k_norm, w_branch_a, w_branch_b, w_out, norm_ffn,
          w_router_group, b_router_group, w_router_expert, b_router_expert, w_gate, w_up, w_down):
    pts = []
    acc = 0
    for s in IN_SIZES[:-1]:
        acc += s
        pts.append(acc)
    w_u, w_q, w_k, w_v, w_qi, w_ki, w_wi, w_ga, w_gb = jnp.split(w_in, pts, axis=1)
    pad = PROJ_COLS - sum(IN_SIZES)
    w_proj = jnp.concatenate([w_u, w_q, w_ga, w_gb, w_qi, w_k, w_v, w_ki, w_wi,
                              jnp.zeros((D_MODEL, pad), w_in.dtype)], axis=1).astype(BF16)
    one = lambda k: jnp.ones((k,), F32)
    gain = jnp.concatenate([one(S5_WIDTH), jnp.tile(q_norm.astype(F32), N_HEADS), one(2 * D_MODEL),
                            one(N_IDX_HEADS * IDX_DIM), jnp.tile(k_norm.astype(F32), N_KV_HEADS), one(KV_WIDTH),
                            idx_k_norm.astype(F32), one(N_IDX_HEADS + pad)]).reshape(1, PROJ_COLS)
    half = 256
    wv, wg = w_glu[:, :S5_WIDTH], w_glu[:, S5_WIDTH:]
    w_glu_p = jnp.concatenate(
        [jnp.concatenate([wv[:, c * half:(c + 1) * half], wg[:, c * half:(c + 1) * half]], axis=1)
         for c in range(S5_WIDTH // half)], axis=1).astype(BF16)
    w_r = jnp.concatenate([w_router_expert.astype(F32), w_router_group.astype(F32),
                           jnp.zeros((D_MODEL, LANES - N_EXPERTS - N_GROUPS), F32)], axis=1)
    wr_hi = w_r.astype(BF16)
    wr_lo = (w_r - wr_hi.astype(F32)).astype(BF16)
    b_r = jnp.concatenate([b_router_expert.astype(F32), b_router_group.astype(F32),
                           jnp.zeros((LANES - N_EXPERTS - N_GROUPS,), F32)]).reshape(1, LANES)
    return dict(norm_mix=norm_mix.astype(F32).reshape(1, D_MODEL), w_proj=w_proj, gain=gain,
                s5=_s5_tables(s5_a_re, s5_a_im, s5_log_dt, s5_b_re, s5_b_im, s5_c_re, s5_c_im, S5_MAX_CHUNK),
                s5_d=s5_d.astype(F32).reshape(1, S5_WIDTH), w_glu=w_glu_p,
                w_a=w_branch_a.astype(BF16), w_b=w_branch_b.astype(BF16), w_out=w_out.astype(BF16),
                norm_ffn=norm_ffn.astype(F32).reshape(1, D_MODEL), wr_hi=wr_hi, wr_lo=wr_lo, b_router=b_r,
                w_gate=w_gate, w_up=w_up, w_down=w_down)


def _mixer(x, h0_re, h0_im, k_past, v_past, ki_past, p, *, tm, lc, tt, tq, kb):
    bsz, t, _ = x.shape
    n = bsz * t
    x2 = x.reshape(n, D_MODEL)
    proj = _inproj(x2, p['norm_mix'], p['w_proj'], p['gain'], tm)
    proj3 = proj.reshape(bsz, t, PROJ_COLS)
    k = proj3[:, :, COL_K:COL_K + KV_WIDTH]
    v = proj3[:, :, COL_V:COL_V + KV_WIDTH]
    ki = proj3[:, :, COL_KI:COL_KI + IDX_DIM]
    n_state = S5_GROUPS * S5_STATE
    tables = dict(p['s5'])
    for name in ('air', 'aii', 'apr', 'api'):
        tables[name] = tables[name][:lc]
    tables['tri'] = jnp.tile(tables['tri'][:lc, :lc], (1, 2))
    yg, s_re, s_im = _s5(proj3, h0_re.reshape(bsz, 1, n_state), h0_im.reshape(bsz, 1, n_state),
                         tables, p['s5_d'], lc, tt)
    y_a = _glu(yg.reshape(n, S5_WIDTH), p['w_glu'], tm)
    if k_past is None:
        pos0 = 0
        k_all, v_all, ki_all = k, v, ki
    else:
        pos0 = k_past.shape[1]
        k_all = jnp.concatenate([k_past.reshape(bsz, pos0, KV_WIDTH), k], axis=1)
        v_all = jnp.concatenate([v_past.reshape(bsz, pos0, KV_WIDTH), v], axis=1)
        ki_all = jnp.concatenate([ki_past, ki], axis=1)
    n_keys = k_all.shape[1]
    n_top = min(TOP_K_MAX, n_keys // 4)
    nkp = -(-n_keys // kb) * kb
    padk = lambda a: jnp.pad(a.astype(BF16), ((0, 0), (0, nkp - n_keys), (0, 0)))
    kit = jnp.swapaxes(padk(ki_all), 1, 2)
    kt = jnp.swapaxes(padk(k_all), 1, 2)
    y_b = _dsa(proj3, kit, kt, padk(v_all), tq=tq, kb=kb, pos0=pos0, n_keys=n_keys, n_top=n_top)
    merged = _merge(y_a, y_b.reshape(n, ATT_WIDTH), p['w_a'], p['w_b'], proj, tm)
    x1 = _outproj(merged, p['w_out'], x2, tm)
    return (x1, s_re.reshape(bsz, S5_GROUPS, S5_STATE),
            s_im.reshape(bsz, S5_GROUPS, S5_STATE), k.reshape(bsz, t, N_KV_HEADS, HEAD_DIM),
            v.reshape(bsz, t, N_KV_HEADS, HEAD_DIM), ki)


def kernel(x_prompt, x_sample, state_s5_re, state_s5_im, cache_k, cache_v, cache_idx_k, norm_mix, w_in, s5_a_re, s5_a_im, s5_log_dt, s5_b_re, s5_b_im, s5_c_re, s5_c_im, s5_d, w_glu, q_norm, k_norm, idx_k_norm, w_branch_a, w_branch_b, w_out, norm_ffn, w_router_group, b_router_group, w_router_expert, b_router_expert, w_gate, w_up, w_down):
    p = _prep(norm_mix, w_in, s5_a_re, s5_a_im, s5_log_dt, s5_b_re, s5_b_im, s5_c_re, s5_c_im, s5_d,
              w_glu, q_norm, k_norm, idx_k_norm, w_branch_a, w_branch_b, w_out, norm_ffn,
              w_router_group, b_router_group, w_router_expert, b_router_expert, w_gate, w_up, w_down)
    h0 = jnp.zeros((x_prompt.shape[0], S5_GROUPS, S5_STATE), F32)
    tm_p, tm_s = 1024, 256
    xp, srp, sip, kp, vp, kip = _mixer(x_prompt, h0, h0, None, None, None, p,
                                       tm=tm_p, lc=64, tt=1024, tq=128, kb=512)
    xs, srs, sis, ks, vs, kis = _mixer(x_sample, state_s5_re, state_s5_im, cache_k, cache_v, cache_idx_k, p,
                                       tm=tm_s, lc=32, tt=32, tq=32, kb=384)
    yp, ys = _moe([xp, xs], p, [tm_p, tm_s], 256)
    return (yp.reshape(x_prompt.shape), ys.reshape(x_sample.shape), srp, sip, kp, vp, kip, srs, sis, ks, vs, kis)
```

```python
import functools
import math

import jax
import jax.numpy as jnp
from jax import lax
from jax.experimental import pallas as pl
from jax.experimental.pallas import tpu as pltpu

F32 = jnp.float32
BF16 = jnp.bfloat16
I32 = jnp.int32

D_MODEL = 2048
CHUNK = 64
EPS = 1e-6
S5_WIDTH = 1024
S5_GROUP = 16
S5_GROUPS = 64
S5_STATE = 64
S5_MAX_RE = -1e-4
N_HEADS = 8
N_KV_HEADS = 2
Q_PER_KV = 4
HEAD_DIM = 128
ATT_WIDTH = 1024
KV_WIDTH = 256
N_IDX_HEADS = 8
IDX_DIM = 64
TOP_K_MAX = 256
N_GROUPS = 4
EXPERTS_PER_GROUP = 8
N_EXPERTS = 32
D_EXPERT = 512
IN_SIZES = (S5_WIDTH, ATT_WIDTH, KV_WIDTH, KV_WIDTH, N_IDX_HEADS * IDX_DIM, IDX_DIM, N_IDX_HEADS, D_MODEL, D_MODEL)

LANES = 128
PROJ_TN = 512
COL_U, COL_Q, COL_GA, COL_GB, COL_QI, COL_K, COL_V, COL_KI, COL_WI = 0, 1024, 2048, 4096, 6144, 6656, 6912, 7168, 7232
PROJ_COLS = 7680
S5_LANE_BLOCKS = S5_WIDTH // LANES
S5_BLOCK_STATE = (LANES // S5_GROUP) * S5_STATE
S5_MAX_CHUNK = 64
VMEM_LIMIT = 56 * 1024 * 1024
INT_MIN = -2 ** 31
KEY_LOWEST_FINITE = -2 ** 31 + 0x00800000
NEG_BIG = -1e30
LOG2E = 1.4426950408889634
FAST_SOFTMAX_BOUND = 40.0
TOPK_PER_LANE = 12


def _dot(a, b):
    return jnp.dot(a, b, preferred_element_type=F32)


def _split_bf16(x):
    hi = x.astype(BF16)
    lo = (x - hi.astype(F32)).astype(BF16)
    return hi, lo


def _sigmoid(x):
    return 0.5 * jnp.tanh(0.5 * x) + 0.5


def _params(sem, **kw):
    return pltpu.CompilerParams(dimension_semantics=sem, vmem_limit_bytes=VMEM_LIMIT, **kw)


def _group_norm(a, gain):
    ms = jnp.mean(a * a, axis=-1, keepdims=True)
    return a * lax.rsqrt(ms + EPS) * gain


def _inproj_kernel(x_ref, g_ref, w_ref, gain_ref, o_ref, xn_ref):
    j = pl.program_id(1)

    @pl.when(j == 0)
    def _():
        xf = x_ref[...]
        ms = jnp.mean(xf * xf, axis=-1, keepdims=True)
        xn_ref[...] = (xf * lax.rsqrt(ms + EPS) * g_ref[...]).astype(BF16)

    acc = _dot(xn_ref[...], w_ref[...])
    gain = gain_ref[...]
    groups = [slice(c * LANES, (c + 1) * LANES) for c in range(PROJ_TN // LANES)]

    @pl.when((j < 2) | (j == 12))
    def _():
        o_ref[...] = acc

    @pl.when((j == 2) | (j == 3))
    def _():
        for s in groups:
            o_ref[:, s] = _group_norm(acc[:, s], gain[:, s])

    @pl.when((j >= 4) & (j < 12))
    def _():
        o_ref[...] = _sigmoid(acc)

    @pl.when(j == 13)
    def _():
        for s in groups[:2]:
            o_ref[:, s] = _group_norm(acc[:, s], gain[:, s])
        o_ref[:, 2 * LANES:] = acc[:, 2 * LANES:]

    @pl.when(j == 14)
    def _():
        a = acc[:, :LANES]
        lane = lax.broadcasted_iota(I32, a.shape, 1)
        is_ki = lane < IDX_DIM
        ms = jnp.sum(jnp.where(is_ki, a * a, 0.0), axis=-1, keepdims=True) * (1.0 / IDX_DIM)
        ki = a * lax.rsqrt(ms + EPS) * gain[:, :LANES]
        wi = a * (N_IDX_HEADS ** -0.5) * (IDX_DIM ** -0.5)
        o_ref[:, :LANES] = jnp.where(is_ki, ki, wi)
        o_ref[:, LANES:] = acc[:, LANES:]


def _inproj(x, g, w, gain, tm):
    n = x.shape[0]
    return pl.pallas_call(
        _inproj_kernel,
        grid=(n // tm, PROJ_COLS // PROJ_TN),
        in_specs=[pl.BlockSpec((tm, D_MODEL), lambda i, j: (i, 0)),
                  pl.BlockSpec((1, D_MODEL), lambda i, j: (0, 0)),
                  pl.BlockSpec((D_MODEL, PROJ_TN), lambda i, j: (0, j)),
                  pl.BlockSpec((1, PROJ_TN), lambda i, j: (0, j))],
        out_specs=pl.BlockSpec((tm, PROJ_TN), lambda i, j: (i, j)),
        out_shape=jax.ShapeDtypeStruct((n, PROJ_COLS), F32),
        scratch_shapes=[pltpu.VMEM((tm, D_MODEL), BF16)],
        compiler_params=_params(("arbitrary", "arbitrary")),
        name="inproj",
    )(x, g, w, gain)


def _gelu_tanh(y):
    return 0.5 * y * (1.0 + jnp.tanh(math.sqrt(2.0 / math.pi) * (y + 0.044715 * (y * y * y))))


def _s5_kernel(u_ref, h0r_ref, h0i_ref, bhi_ref, blo_ref, air_ref, aii_ref, apr_ref, api_ref,
               a1r_ref, a1i_ref, tri_ref, cd_ref, d_ref, yg_ref, sr_ref, si_ref,
               bu_scr, h_scr, hr_scr, hi_scr, *, lc, tt):
    t = pl.program_id(2)
    ns = S5_BLOCK_STATE

    @pl.when(t == 0)
    def _():
        hr_scr[...] = h0r_ref[...]
        hi_scr[...] = h0i_ref[...]

    u = u_ref[...]
    u_hi, u_lo = _split_bf16(u)
    bhi = bhi_ref[...]
    bu_scr[...] = _dot(u_hi, bhi) + _dot(jnp.concatenate([u_lo, u_hi], axis=1), blo_ref[...])
    tri = tri_ref[...]
    air, aii, apr, api = air_ref[...], aii_ref[...], apr_ref[...], api_ref[...]
    a1r, a1i = a1r_ref[...], a1i_ref[...]

    def chunk(s, carry):
        h_re, h_im = carry
        r0 = pl.multiple_of(s * lc, lc)
        br = bu_scr[pl.ds(r0, lc), 0:ns]
        bi = bu_scr[pl.ds(r0, lc), ns:2 * ns]
        z = jnp.concatenate([air * br - aii * bi, air * bi + aii * br], axis=1)
        z_hi, z_lo = _split_bf16(z)
        c = _dot(tri, jnp.concatenate([z_hi, z_lo], axis=0))
        cr = c[:, 0:ns] + (a1r * h_re - a1i * h_im)
        ci = c[:, ns:2 * ns] + (a1r * h_im + a1i * h_re)
        hr = apr * cr - api * ci
        hi = apr * ci + api * cr
        h_scr[pl.ds(r0, lc), 0:ns] = hr.astype(BF16)
        h_scr[pl.ds(r0, lc), ns:2 * ns] = hi.astype(BF16)
        return hr[lc - 1:lc, :], hi[lc - 1:lc, :]

    h_re, h_im = lax.fori_loop(0, tt // lc, chunk, (hr_scr[...], hi_scr[...]), unroll=min(4, tt // lc))
    hr_scr[...] = h_re
    hi_scr[...] = h_im
    sr_ref[...] = h_re
    si_ref[...] = h_im
    y = _dot(h_scr[...], cd_ref[...]) + d_ref[...] * u
    yg_ref[...] = _gelu_tanh(y).astype(BF16)


def _s5_tables(a_re, a_im, log_dt, b_re, b_im, c_re, c_im, lc):
    lr = jnp.minimum(a_re.astype(F32), S5_MAX_RE)
    li = a_im.astype(F32)
    dt = jnp.exp(log_dt.astype(F32))[:, None]
    mag = jnp.exp(lr * dt)
    lbr = mag * jnp.cos(li * dt)
    lbi = mag * jnp.sin(li * dt)
    den = lr * lr + li * li
    fr = ((lbr - 1.0) * lr + lbi * li) / den
    fi = (lbi * lr - (lbr - 1.0) * li) / den
    br = b_re.astype(F32)
    bi = b_im.astype(F32)
    bbr = fr[..., None] * br - fi[..., None] * bi
    bbi = fr[..., None] * bi + fi[..., None] * br
    j = jnp.arange(lc, dtype=F32)[:, None, None]
    ang = j * (li * dt)[None]
    lmag = j * (lr * dt)[None]
    n_state = S5_GROUPS * S5_STATE
    apr = (jnp.exp(lmag) * jnp.cos(ang)).reshape(lc, n_state)
    api = (jnp.exp(lmag) * jnp.sin(ang)).reshape(lc, n_state)
    air = (jnp.exp(-lmag) * jnp.cos(ang)).reshape(lc, n_state)
    aii = (-jnp.exp(-lmag) * jnp.sin(ang)).reshape(lc, n_state)
    a1r = lbr.reshape(1, n_state)
    a1i = lbi.reshape(1, n_state)
    gpb = LANES // S5_GROUP
    eye = jnp.eye(gpb, dtype=F32)

    def bdiag(b):
        return jnp.einsum('kgpc,gh->kgchp', b.reshape(S5_LANE_BLOCKS, gpb, S5_STATE, S5_GROUP), eye).reshape(
            S5_LANE_BLOCKS, LANES, S5_BLOCK_STATE)

    def cdiag(c):
        return jnp.einsum('kgcp,gh->kgphc', c.reshape(S5_LANE_BLOCKS, gpb, S5_GROUP, S5_STATE), eye).reshape(
            S5_LANE_BLOCKS, S5_BLOCK_STATE, LANES)

    bd = jnp.concatenate([bdiag(bbr), bdiag(bbi)], axis=-1)
    bd_hi = bd.astype(BF16)
    bd_lo = (bd - bd_hi.astype(F32)).astype(BF16)
    cd = jnp.concatenate([cdiag(c_re.astype(F32)), -cdiag(c_im.astype(F32))], axis=1).astype(BF16)
    tri = jnp.tril(jnp.ones((lc, lc), F32)).astype(BF16)
    bd_lo = jnp.concatenate([bd_hi, bd_lo], axis=1)
    return dict(bd_hi=bd_hi, bd_lo=bd_lo, air=air, aii=aii, apr=apr, api=api, a1r=a1r, a1i=a1i, tri=tri, cd=cd)


def _s5(proj3, h0_re, h0_im, tb, dvec, lc, tt):
    bsz, t, _ = proj3.shape
    ns = S5_BLOCK_STATE
    n_state = S5_GROUPS * S5_STATE
    tab = lambda: pl.BlockSpec((lc, ns), lambda b, k, i: (0, k))
    row = lambda: pl.BlockSpec((1, ns), lambda b, k, i: (0, k))
    st = lambda: pl.BlockSpec((None, 1, ns), lambda b, k, i: (b, 0, k))
    return pl.pallas_call(
        functools.partial(_s5_kernel, lc=lc, tt=tt),
        grid=(bsz, S5_LANE_BLOCKS, t // tt),
        in_specs=[pl.BlockSpec((None, tt, LANES), lambda b, k, i: (b, i, k)),
                  st(), st(),
                  pl.BlockSpec((None, LANES, 2 * ns), lambda b, k, i: (k, 0, 0)),
                  pl.BlockSpec((None, 2 * LANES, 2 * ns), lambda b, k, i: (k, 0, 0)),
                  tab(), tab(), tab(), tab(), row(), row(),
                  pl.BlockSpec((lc, 2 * lc), lambda b, k, i: (0, 0)),
                  pl.BlockSpec((None, 2 * ns, LANES), lambda b, k, i: (k, 0, 0)),
                  pl.BlockSpec((1, LANES), lambda b, k, i: (0, k))],
        out_specs=[pl.BlockSpec((None, tt, LANES), lambda b, k, i: (b, i, k)), st(), st()],
        out_shape=[jax.ShapeDtypeStruct((bsz, t, S5_WIDTH), BF16),
                   jax.ShapeDtypeStruct((bsz, 1, n_state), F32),
                   jax.ShapeDtypeStruct((bsz, 1, n_state), F32)],
        scratch_shapes=[pltpu.VMEM((tt, 2 * ns), F32), pltpu.VMEM((tt, 2 * ns), BF16),
                        pltpu.VMEM((1, ns), F32), pltpu.VMEM((1, ns), F32)],
        compiler_params=_params(("arbitrary", "arbitrary", "arbitrary")),
        name="s5",
    )(proj3, h0_re, h0_im, tb['bd_hi'], tb['bd_lo'], tb['air'], tb['aii'], tb['apr'], tb['api'],
      tb['a1r'], tb['a1i'], tb['tri'], tb['cd'], dvec)


def _glu_kernel(y_ref, w_ref, o_ref):
    acc = _dot(y_ref[...], w_ref[...])
    half = acc.shape[1] // 2
    o_ref[...] = (acc[:, :half] * _sigmoid(acc[:, half:])).astype(BF16)


def _glu(yg, w, tm):
    n = yg.shape[0]
    tn = 512
    return pl.pallas_call(
        _glu_kernel,
        grid=(n // tm, 2 * S5_WIDTH // tn),
        in_specs=[pl.BlockSpec((tm, S5_WIDTH), lambda i, j: (i, 0)),
                  pl.BlockSpec((S5_WIDTH, tn), lambda i, j: (0, j))],
        out_specs=pl.BlockSpec((tm, tn // 2), lambda i, j: (i, j)),
        out_shape=jax.ShapeDtypeStruct((n, S5_WIDTH), BF16),
        compiler_params=_params(("arbitrary", "arbitrary")),
        name="glu",
    )(yg, w)


def _dsa_kernel(q_ref, qi_ref, kiw_ref, kit_ref, kt_ref, v_ref, ut_ref, o_ref,
                key_scr, qs_scr, qis_scr, wis_scr, m_scr, l_scr, acc_scr,
                kmax_scr, lsum_scr, cand_scr, ckey_scr, thr_scr, cnt_scr,
                *, tq, kb, pos0, n_keys, n_top, nt):
    i = pl.program_id(1)
    nkp = key_scr.shape[2]
    n_cand = cand_scr.shape[1]
    slot_s = i % 2
    slot_a = 1 - slot_s
    tile_s = jnp.minimum(i, nt - 1)
    tile_a = jnp.maximum(i - 1, 0)

    def n_blocks(tile):
        k_end = jnp.minimum(n_keys, (pos0 + tile * tq + tq - 1) // CHUNK * CHUNK + CHUNK)
        return (k_end + kb - 1) // kb

    nkb_s = jnp.where(i < nt, n_blocks(tile_s), 0)
    nkb_a = jnp.where(i >= 1, n_blocks(tile_a), 0)

    @pl.when(i == 0)
    def _():
        def norm_block(b, carry):
            c0 = pl.multiple_of(b * kb, kb)
            out = []
            for g in range(N_KV_HEADS):
                kk = kt_ref[g * HEAD_DIM:(g + 1) * HEAD_DIM, pl.ds(c0, kb)].astype(F32)
                out.append(jnp.maximum(carry[g], jnp.sum(kk * kk, axis=0, keepdims=True)))
            return tuple(out)

        res = lax.fori_loop(0, nkp // kb, norm_block, (jnp.zeros((1, kb), F32),) * N_KV_HEADS)
        for g in range(N_KV_HEADS):
            kmax_scr[g] = jnp.broadcast_to(jnp.max(res[g], axis=-1, keepdims=True), (1, LANES))

    q = q_ref[...] * (HEAD_DIM ** -0.5 * LOG2E)
    for h in range(N_HEADS):
        g, r = divmod(h, Q_PER_KV)
        qs_scr[g, r * tq:(r + 1) * tq, :] = q[:, h * HEAD_DIM:(h + 1) * HEAD_DIM].astype(BF16)
    qi = qi_ref[...]
    kiw = kiw_ref[...]
    for h in range(N_IDX_HEADS):
        qis_scr[h * tq:(h + 1) * tq, :] = qi[:, h * IDX_DIM:(h + 1) * IDX_DIM].astype(BF16)
        wis_scr[h * tq:(h + 1) * tq, :] = kiw[:, IDX_DIM + h:IDX_DIM + h + 1]

    q_chunk = (pos0 + tile_s * tq + lax.broadcasted_iota(I32, (tq, 1), 0)) // CHUNK

    def to_key(x):
        bits = pltpu.bitcast(x, I32)
        return jnp.where(bits < 0, bits ^ 0x7FFFFFFF, bits)

    cand_scr[slot_s] = jnp.full(cand_scr.shape[1:], -jnp.inf, F32)

    def score_block(b):
        c0 = pl.multiple_of(b * kb, kb)
        s = _dot(qis_scr[...], kit_ref[:, pl.ds(c0, kb)])
        s = jnp.maximum(s, 0.0) * wis_scr[...]
        sc = s[0:tq]
        for h in range(1, N_IDX_HEADS):
            sc = sc + s[h * tq:(h + 1) * tq]
        k_pos = c0 + lax.broadcasted_iota(I32, (1, kb), 1)
        adm = (k_pos // CHUNK <= q_chunk) & (k_pos < n_keys)
        blk = jnp.where(adm, sc + 0.0, -jnp.inf)
        key_scr[slot_s, :, pl.ds(c0, kb)] = blk
        for rg in range(tq // 8):
            rows = slice(rg * 8, rg * 8 + 8)
            top = [cand_scr[slot_s, j, rows, :] for j in range(n_cand)]
            for c in range(kb // LANES):
                x = blk[rows, c * LANES:(c + 1) * LANES]
                for j in range(n_cand):
                    hi = jnp.maximum(top[j], x)
                    x = jnp.minimum(top[j], x)
                    top[j] = hi
            for j in range(n_cand):
                cand_scr[slot_s, j, rows, :] = top[j]

    def count_all(hit_fn):
        def body(b, acc):
            c0 = pl.multiple_of(b * kb, kb)
            hit = hit_fn(key_scr[slot_a, :, pl.ds(c0, kb)])
            for c in range(kb // LANES):
                acc = acc + hit[:, c * LANES:(c + 1) * LANES]
            return acc
        acc = lax.fori_loop(0, nkb_a, body, jnp.zeros((tq, LANES), F32))
        return jnp.sum(acc, axis=-1, keepdims=True)

    def count_cand(cand):
        acc = jnp.zeros((tq, LANES), F32)
        for j in range(n_cand):
            acc = acc + jnp.where(ckey_scr[j] >= cand, 1.0, 0.0)
        return jnp.sum(acc, axis=-1, keepdims=True)

    def bisect(count_fn, first_bit=0, prefix=None):
        def bit_pass(it, t_off):
            cand_off = t_off | lax.shift_left(jnp.int32(1), 31 - it)
            cnt = count_fn(cand_off ^ INT_MIN)
            return jnp.where(cnt >= n_top, cand_off, t_off)
        t_off = lax.fori_loop(first_bit, 32, bit_pass, jnp.zeros((tq, 1), I32) if prefix is None else prefix)
        key = jnp.maximum(t_off ^ INT_MIN, KEY_LOWEST_FINITE)
        return pltpu.bitcast(jnp.where(key < 0, key ^ 0x7FFFFFFF, key), F32)

    def candidate_bracket():
        hi = to_key(jnp.max(cand_scr[slot_a, 0], axis=-1, keepdims=True)) ^ INT_MIN
        lo = to_key(jnp.min(cand_scr[slot_a, 1], axis=-1, keepdims=True)) ^ INT_MIN
        n_common = jnp.min(lax.clz(hi ^ lo).astype(F32)).astype(I32)
        mask = jnp.where(n_common > 0, lax.shift_left(jnp.int32(-1), 32 - jnp.maximum(n_common, 1)), 0)
        return n_common, hi & mask

    @pl.when(i >= 1)
    def _():
        for j in range(n_cand):
            ckey_scr[j] = to_key(cand_scr[slot_a, j])
        thr_cand = bisect(count_cand, *candidate_bracket()) if n_top <= 2 * LANES else bisect(count_cand)
        thr_scr[...] = thr_cand
        n_all = count_all(lambda x: jnp.where(x >= thr_cand, 1.0, 0.0))
        n_listed = jnp.zeros((tq, LANES), F32)
        for j in range(n_cand):
            n_listed = n_listed + jnp.where(cand_scr[slot_a, j] >= thr_cand, 1.0, 0.0)
        cnt_scr[...] = n_all
        overflow = jnp.max(n_all - jnp.sum(n_listed, axis=-1, keepdims=True)) > 0.0

        @pl.when(overflow)
        def _():
            thr_all = bisect(lambda cand: count_all(lambda x: jnp.where(to_key(x) >= cand, 1.0, 0.0)))
            thr_scr[...] = thr_all
            cnt_scr[...] = count_all(lambda x: jnp.where(x >= thr_all, 1.0, 0.0))

        @pl.when(jnp.max(cnt_scr[...]) > n_top)
        def _():
            thr_t = thr_scr[...]
            need = n_top - count_all(lambda x: jnp.where(x > thr_t, 1.0, 0.0))

            def drop_surplus(b, seen):
                c0 = pl.multiple_of(b * kb, kb)
                x = key_scr[slot_a, :, pl.ds(c0, kb)]
                tied = x == thr_t
                rank = seen + _dot(jnp.where(tied, 1.0, 0.0).astype(BF16), ut_ref[...])
                key_scr[slot_a, :, pl.ds(c0, kb)] = jnp.where(tied & (rank > need), -jnp.inf, x)
                return rank[:, kb - 1:kb]

            lax.fori_loop(0, nkb_a, drop_surplus, jnp.zeros((tq, 1), F32))

    thr = thr_scr[...]

    acc_scr[...] = jnp.zeros(acc_scr.shape, F32)
    lsum_scr[...] = jnp.zeros(lsum_scr.shape, F32)
    m_scr[...] = jnp.full(m_scr.shape, NEG_BIG, F32)
    l_scr[...] = jnp.zeros(l_scr.shape, F32)
    bound_max = jnp.float32(0.0)
    for g in range(N_KV_HEADS):
        qg = qs_scr[g].astype(F32)
        qn2 = jnp.sum(qg * qg, axis=-1, keepdims=True)
        bound = jnp.sqrt(qn2 * kmax_scr[g][:, 0:1]) * 1.002
        bound_max = jnp.maximum(bound_max, jnp.max(bound))
    fast = bound_max <= FAST_SOFTMAX_BOUND

    def attend_fast(b):
        c0 = pl.multiple_of(b * kb, kb)
        bias = jnp.where(key_scr[slot_a, :, pl.ds(c0, kb)] >= thr, 0.0, NEG_BIG)
        for g in range(N_KV_HEADS):
            lg = _dot(qs_scr[g], kt_ref[g * HEAD_DIM:(g + 1) * HEAD_DIM, pl.ds(c0, kb)])
            parts = []
            for r in range(Q_PER_KV):
                rows = slice(r * tq, (r + 1) * tq)
                e = [jnp.exp2(lg[rows, c * LANES:(c + 1) * LANES] + bias[:, c * LANES:(c + 1) * LANES])
                     for c in range(kb // LANES)]
                ls = e[0]
                for c in range(1, kb // LANES):
                    ls = ls + e[c]
                lsum_scr[g, rows, :] += ls
                parts.append(jnp.concatenate(e, axis=1).astype(BF16))
            p = jnp.concatenate(parts, axis=0)
            acc_scr[g] += _dot(p, v_ref[pl.ds(c0, kb), g * HEAD_DIM:(g + 1) * HEAD_DIM])

    def attend_general(b):
        c0 = pl.multiple_of(b * kb, kb)
        sel = key_scr[slot_a, :, pl.ds(c0, kb)] >= thr
        sel4 = jnp.concatenate([sel] * Q_PER_KV, axis=0)
        for g in range(N_KV_HEADS):
            lg = _dot(qs_scr[g], kt_ref[g * HEAD_DIM:(g + 1) * HEAD_DIM, pl.ds(c0, kb)])
            lg = jnp.where(sel4, lg, NEG_BIG)
            m_old = m_scr[g]
            m_new = jnp.maximum(m_old, jnp.max(lg, axis=-1, keepdims=True))
            p = jnp.exp2(lg - m_new)
            alpha = jnp.exp2(m_old - m_new)
            l_scr[g] = alpha * l_scr[g] + jnp.sum(p, axis=-1, keepdims=True)
            pv = _dot(p.astype(BF16), v_ref[pl.ds(c0, kb), g * HEAD_DIM:(g + 1) * HEAD_DIM])
            acc_scr[g] = alpha * acc_scr[g] + pv
            m_scr[g] = m_new

    def loop(lo, hi, *fns):
        def body(b, carry):
            for fn in fns:
                fn(b)
            return carry
        lax.fori_loop(lo, hi, body, 0)

    n_fused = jnp.where(fast, jnp.minimum(nkb_a, nkb_s), 0)
    loop(0, n_fused, score_block, attend_fast)
    loop(n_fused, nkb_s, score_block)
    loop(n_fused, jnp.where(fast, nkb_a, 0), attend_fast)
    loop(0, jnp.where(fast, 0, nkb_a), attend_general)

    @pl.when((i >= 1) & fast)
    def _():
        for h in range(N_HEADS):
            g, r = divmod(h, Q_PER_KV)
            rows = slice(r * tq, (r + 1) * tq)
            l = jnp.sum(lsum_scr[g, rows, :], axis=-1, keepdims=True)
            o_ref[:, h * HEAD_DIM:(h + 1) * HEAD_DIM] = (acc_scr[g, rows, :] / l).astype(BF16)

    @pl.when((i >= 1) & jnp.logical_not(fast))
    def _():
        for h in range(N_HEADS):
            g, r = divmod(h, Q_PER_KV)
            rows = slice(r * tq, (r + 1) * tq)
            o_ref[:, h * HEAD_DIM:(h + 1) * HEAD_DIM] = (acc_scr[g, rows, :] / l_scr[g, rows, :]).astype(BF16)


def _dsa(proj3, kit, kt, v, *, tq, kb, pos0, n_keys, n_top):
    bsz, t, _ = proj3.shape
    nkp = kit.shape[-1]
    nt = t // tq
    kern = functools.partial(_dsa_kernel, tq=tq, kb=kb, pos0=pos0, n_keys=n_keys, n_top=n_top, nt=nt)
    resident = dict(pipeline_mode=pl.Buffered(1))
    prev = lambda i: jnp.maximum(i - 1, 0)
    cur = lambda i: jnp.minimum(i, nt - 1)
    return pl.pallas_call(
        kern,
        grid=(bsz, nt + 1),
        in_specs=[pl.BlockSpec((None, tq, ATT_WIDTH), lambda b, i: (b, prev(i), COL_Q // ATT_WIDTH)),
                  pl.BlockSpec((None, tq, 512), lambda b, i: (b, cur(i), COL_QI // 512)),
                  pl.BlockSpec((None, tq, 512), lambda b, i: (b, cur(i), COL_KI // 512)),
                  pl.BlockSpec((None, IDX_DIM, nkp), lambda b, i: (b, 0, 0), **resident),
                  pl.BlockSpec((None, KV_WIDTH, nkp), lambda b, i: (b, 0, 0), **resident),
                  pl.BlockSpec((None, nkp, KV_WIDTH), lambda b, i: (b, 0, 0), **resident),
                  pl.BlockSpec((kb, kb), lambda b, i: (0, 0), **resident)],
        out_specs=pl.BlockSpec((None, tq, ATT_WIDTH), lambda b, i: (b, prev(i), 0)),
        out_shape=jax.ShapeDtypeStruct((bsz, t, ATT_WIDTH), BF16),
        scratch_shapes=[pltpu.VMEM((2, tq, nkp), F32),
                        pltpu.VMEM((N_KV_HEADS, Q_PER_KV * tq, HEAD_DIM), BF16),
                        pltpu.VMEM((N_IDX_HEADS * tq, IDX_DIM), BF16),
                        pltpu.VMEM((N_IDX_HEADS * tq, 1), F32),
                        pltpu.VMEM((N_KV_HEADS, Q_PER_KV * tq, 1), F32),
                        pltpu.VMEM((N_KV_HEADS, Q_PER_KV * tq, 1), F32),
                        pltpu.VMEM((N_KV_HEADS, Q_PER_KV * tq, HEAD_DIM), F32),
                        pltpu.VMEM((N_KV_HEADS, 1, LANES), F32),
                        pltpu.VMEM((N_KV_HEADS, Q_PER_KV * tq, LANES), F32),
                        pltpu.VMEM((2, TOPK_PER_LANE, tq, LANES), F32),
                        pltpu.VMEM((TOPK_PER_LANE, tq, LANES), I32),
                        pltpu.VMEM((tq, 1), F32),
                        pltpu.VMEM((tq, 1), F32)],
        compiler_params=_params(("arbitrary", "arbitrary")),
        name="dsa",
    )(proj3, proj3, proj3, kit, kt, v, jnp.triu(jnp.ones((kb, kb), F32)).astype(BF16))


def _merge_kernel(ya_ref, yb_ref, wa_ref, wb_ref, ga_ref, gb_ref, o_ref):
    o_ref[...] = (ga_ref[...] * _dot(ya_ref[...], wa_ref[...])
                  + gb_ref[...] * _dot(yb_ref[...], wb_ref[...])).astype(BF16)


def _merge(ya, yb, wa, wb, proj, tm):
    n = ya.shape[0]
    tn = 512
    return pl.pallas_call(
        _merge_kernel,
        grid=(n // tm, D_MODEL // tn),
        in_specs=[pl.BlockSpec((tm, S5_WIDTH), lambda i, j: (i, 0)),
                  pl.BlockSpec((tm, ATT_WIDTH), lambda i, j: (i, 0)),
                  pl.BlockSpec((S5_WIDTH, tn), lambda i, j: (0, j)),
                  pl.BlockSpec((ATT_WIDTH, tn), lambda i, j: (0, j)),
                  pl.BlockSpec((tm, tn), lambda i, j: (i, COL_GA // tn + j)),
                  pl.BlockSpec((tm, tn), lambda i, j: (i, COL_GB // tn + j))],
        out_specs=pl.BlockSpec((tm, tn), lambda i, j: (i, j)),
        out_shape=jax.ShapeDtypeStruct((n, D_MODEL), BF16),
        compiler_params=_params(("arbitrary", "arbitrary")),
        name="merge",
    )(ya, yb, wa, wb, proj, proj)


def _outproj_kernel(m_ref, w_ref, x_ref, o_ref):
    o_ref[...] = x_ref[...] + _dot(m_ref[...], w_ref[...])


def _outproj(merged, w, x, tm):
    n = x.shape[0]
    tn = 512
    return pl.pallas_call(
        _outproj_kernel,
        grid=(n // tm, D_MODEL // tn),
        in_specs=[pl.BlockSpec((tm, D_MODEL), lambda i, j: (i, 0)),
                  pl.BlockSpec((D_MODEL, tn), lambda i, j: (0, j)),
                  pl.BlockSpec((tm, tn), lambda i, j: (i, j))],
        out_specs=pl.BlockSpec((tm, tn), lambda i, j: (i, j)),
        out_shape=jax.ShapeDtypeStruct((n, D_MODEL), F32),
        compiler_params=_params(("arbitrary", "arbitrary")),
        name="outproj",
    )(merged, w, x)


ROUTER_GROUP_LANE = N_EXPERTS


def _router_kernel(x_ref, g_ref, whi_ref, wlo_ref, b_ref, hn_ref, ids_ref, gates_ref):
    xf = x_ref[...]
    ms = jnp.mean(xf * xf, axis=-1, keepdims=True)
    hn = xf * lax.rsqrt(ms + EPS) * g_ref[...]
    hn_ref[...] = hn
    h_hi, h_lo = _split_bf16(hn)
    whi = whi_ref[...]
    lg = _dot(h_hi, whi) + _dot(h_lo, whi) + _dot(h_hi, wlo_ref[...]) + b_ref[...]
    lane = lax.broadcasted_iota(I32, lg.shape, 1)
    lane_f = lane.astype(F32)
    big = float(LANES)
    is_g = (lane >= ROUTER_GROUP_LANE) & (lane < ROUTER_GROUP_LANE + N_GROUPS)
    g_max = jnp.max(jnp.where(is_g, lg, -jnp.inf), axis=-1, keepdims=True)
    g_den = jnp.sum(jnp.where(is_g, jnp.exp(lg - g_max), 0.0), axis=-1, keepdims=True)
    g_w = 1.0 / g_den
    g_lane = jnp.min(jnp.where(is_g & (lg == g_max), lane_f, big), axis=-1, keepdims=True)
    g_sel = g_lane.astype(I32) - ROUTER_GROUP_LANE
    is_e = (lane < N_EXPERTS) & (lane // EXPERTS_PER_GROUP == g_sel)
    e_max = jnp.max(jnp.where(is_e, lg, -jnp.inf), axis=-1, keepdims=True)
    pe = jnp.where(is_e, jnp.exp(lg - e_max), 0.0)
    pe = pe / jnp.sum(pe, axis=-1, keepdims=True)
    pe = jnp.where(is_e, pe, -1.0)
    p1 = jnp.max(pe, axis=-1, keepdims=True)
    i1 = jnp.min(jnp.where(pe == p1, lane_f, big), axis=-1, keepdims=True)
    pe2 = jnp.where(lane_f == i1, -1.0, pe)
    p2 = jnp.max(pe2, axis=-1, keepdims=True)
    i2 = jnp.min(jnp.where(pe2 == p2, lane_f, big), axis=-1, keepdims=True)
    tot = p1 + p2
    ids_ref[...] = jnp.where(lane == 0, i1, jnp.where(lane == 1, i2, 0.0)).astype(I32)
    gates_ref[...] = jnp.where(lane == 0, g_w * (p1 / tot), jnp.where(lane == 1, g_w * (p2 / tot), 0.0))


def _router(x1, g, whi, wlo, bias, tm):
    n = x1.shape[0]
    return pl.pallas_call(
        _router_kernel,
        grid=(n // tm,),
        in_specs=[pl.BlockSpec((tm, D_MODEL), lambda i: (i, 0)),
                  pl.BlockSpec((1, D_MODEL), lambda i: (0, 0)),
                  pl.BlockSpec((D_MODEL, LANES), lambda i: (0, 0)),
                  pl.BlockSpec((D_MODEL, LANES), lambda i: (0, 0)),
                  pl.BlockSpec((1, LANES), lambda i: (0, 0))],
        out_specs=[pl.BlockSpec((tm, D_MODEL), lambda i: (i, 0)),
                   pl.BlockSpec((tm, LANES), lambda i: (i, 0)),
                   pl.BlockSpec((tm, LANES), lambda i: (i, 0))],
        out_shape=[jax.ShapeDtypeStruct((n, D_MODEL), F32),
                   jax.ShapeDtypeStruct((n, LANES), I32),
                   jax.ShapeDtypeStruct((n, LANES), F32)],
        compiler_params=_params(("arbitrary",)),
        name="router",
    )(x1, g, whi, wlo, bias)


def _row_copy(src, dst, s_row, d_row, sem):
    return pltpu.make_async_copy(src.at[pl.ds(s_row, 1)], dst.at[pl.ds(d_row, 1)], sem)


def _dispatch_kernel(dest_ref, hn_ref, xs_in_ref, xs_ref, sem, *, ch):
    del xs_in_ref

    def issue(t, carry):
        for s in range(2):
            _row_copy(hn_ref, xs_ref, t, dest_ref[0, 0, 2 * t + s], sem).start()
        return carry

    lax.fori_loop(0, ch // 2, issue, 0, unroll=8)
    for s in range(2):
        pltpu.make_async_copy(hn_ref, xs_ref.at[pl.ds(0, ch // 2)], sem).wait()


def _dispatch(dest, hn, xs0, ch):
    m = dest.shape[0]
    return pl.pallas_call(
        functools.partial(_dispatch_kernel, ch=ch),
        grid=(m // ch,),
        in_specs=[pl.BlockSpec((1, 1, ch), lambda i: (i, 0, 0), memory_space=pltpu.SMEM),
                  pl.BlockSpec((ch // 2, D_MODEL), lambda i: (i, 0)),
                  pl.BlockSpec(memory_space=pl.ANY)],
        out_specs=pl.BlockSpec(memory_space=pl.ANY),
        out_shape=jax.ShapeDtypeStruct(xs0.shape, xs0.dtype),
        scratch_shapes=[pltpu.SemaphoreType.DMA(())],
        input_output_aliases={2: 0},
        compiler_params=_params(("arbitrary",), disable_bounds_checks=True, has_side_effects=True),
        name="dispatch",
    )(dest.reshape(m // ch, 1, ch), hn, xs0)


def _expert_kernel(be_ref, first_ref, next_ref, slot_ref, nu_ref, x_ref, wg_hbm, wu_hbm, wd_hbm, o_ref,
                   wg_f, wu_f, wd_f, wg_s, wu_s, wd_s, sems):
    b = pl.program_id(0)

    def weight_copies(e, slot):
        return [pltpu.make_async_copy(src.at[e], dst.at[slot], sems.at[slot, k])
                for k, (src, dst) in enumerate(((wg_hbm, wg_f), (wu_hbm, wu_f), (wd_hbm, wd_f)))]

    @pl.when(b < nu_ref[0])
    def _():
        @pl.when(first_ref[b] == 1)
        def _():
            slot = slot_ref[b]

            @pl.when(b == 0)
            def _():
                for c in weight_copies(be_ref[b], slot):
                    c.start()

            for c in weight_copies(be_ref[b], slot):
                c.wait()
            for s in range(2):
                @pl.when(slot == s)
                def _(s=s):
                    wg_s[...] = wg_f[s].astype(BF16)
                    wu_s[...] = wu_f[s].astype(BF16)
                    wd_s[...] = wd_f[s].astype(BF16)

            @pl.when(next_ref[b] >= 0)
            def _():
                for c in weight_copies(next_ref[b], 1 - slot):
                    c.start()

        x = x_ref[...].astype(BF16)
        gate = _dot(x, wg_s[...])
        h = gate * _sigmoid(gate) * _dot(x, wu_s[...])
        o_ref[...] = _dot(h.astype(BF16), wd_s[...])

    @pl.when(b >= nu_ref[0])
    def _():
        o_ref[...] = jnp.zeros(o_ref.shape, F32)


def _experts(block_e, n_used, next_first, xs, wg, wu, wd, bm):
    rows = xs.shape[0]
    nb = rows // bm
    idx = jnp.arange(nb, dtype=I32)
    first = ((idx < n_used[0]) & ((idx == 0) | (block_e != jnp.roll(block_e, 1)))).astype(I32)
    slot = ((jnp.cumsum(first) - 1) % 2).astype(I32)
    next_e = jnp.where(next_first < n_used[0], block_e[jnp.minimum(next_first, nb - 1)], -1).astype(I32)
    any_space = pl.BlockSpec(memory_space=pl.ANY)
    grid_spec = pltpu.PrefetchScalarGridSpec(
        num_scalar_prefetch=5,
        grid=(nb,),
        in_specs=[pl.BlockSpec((bm, D_MODEL), lambda b, *_: (b, 0)), any_space, any_space, any_space],
        out_specs=pl.BlockSpec((bm, D_MODEL), lambda b, *_: (b, 0)),
        scratch_shapes=[pltpu.VMEM((2, D_MODEL, D_EXPERT), F32), pltpu.VMEM((2, D_MODEL, D_EXPERT), F32),
                        pltpu.VMEM((2, D_EXPERT, D_MODEL), F32),
                        pltpu.VMEM((D_MODEL, D_EXPERT), BF16), pltpu.VMEM((D_MODEL, D_EXPERT), BF16),
                        pltpu.VMEM((D_EXPERT, D_MODEL), BF16),
                        pltpu.SemaphoreType.DMA((2, 3))])
    return pl.pallas_call(
        _expert_kernel,
        grid_spec=grid_spec,
        out_shape=jax.ShapeDtypeStruct((rows, D_MODEL), F32),
        compiler_params=_params(("arbitrary",)),
        name="experts",
    )(block_e, first, next_e, slot, n_used, xs, wg, wu, wd)


def _combine_kernel(dest_ref, gates_ref, x_ref, ys_ref, o_ref, buf, sem, *, tc):
    def issue(t, carry):
        for s in range(2):
            _row_copy(ys_ref, buf.at[s], dest_ref[0, 0, 2 * t + s], t, sem).start()
        return carry

    lax.fori_loop(0, tc, issue, 0, unroll=8)
    for s in range(2):
        pltpu.make_async_copy(ys_ref.at[pl.ds(0, tc)], buf.at[s], sem).wait()
    gates = gates_ref[...]
    o_ref[...] = x_ref[...] + gates[:, 0:1] * buf[0] + gates[:, 1:2] * buf[1]


def _combine(dest, gates, x1, ys, tc):
    n = x1.shape[0]
    return pl.pallas_call(
        functools.partial(_combine_kernel, tc=tc),
        grid=(n // tc,),
        in_specs=[pl.BlockSpec((1, 1, 2 * tc), lambda i: (i, 0, 0), memory_space=pltpu.SMEM),
                  pl.BlockSpec((tc, LANES), lambda i: (i, 0)),
                  pl.BlockSpec((tc, D_MODEL), lambda i: (i, 0)),
                  pl.BlockSpec(memory_space=pl.ANY)],
        out_specs=pl.BlockSpec((tc, D_MODEL), lambda i: (i, 0)),
        out_shape=jax.ShapeDtypeStruct((n, D_MODEL), F32),
        scratch_shapes=[pltpu.VMEM((2, tc, D_MODEL), F32), pltpu.SemaphoreType.DMA(())],
        compiler_params=_params(("arbitrary",), disable_bounds_checks=True),
        name="combine",
    )(dest.reshape(n // tc, 1, 2 * tc), gates, x1, ys)


def _moe(x1s, p, tms, bm):
    routed = [_router(x1, p['norm_ffn'], p['wr_hi'], p['wr_lo'], p['b_router'], tm) for x1, tm in zip(x1s, tms)]
    flat_e = jnp.concatenate([ids[:, :2].reshape(-1) for _, ids, _ in routed])
    m = flat_e.shape[0]
    onehot = (flat_e[:, None] == jnp.arange(N_EXPERTS, dtype=I32)[None, :]).astype(I32)
    csum = jnp.cumsum(onehot, axis=0)
    counts = csum[-1]
    padded = (counts + bm - 1) // bm * bm
    pad_end = jnp.cumsum(padded)
    pad_start = pad_end - padded
    dest = jnp.sum(onehot * (csum - 1 + pad_start[None, :]), axis=1).astype(I32)
    nb = -(-(m + N_EXPERTS * (bm - 1)) // bm)
    block_start = jnp.arange(nb, dtype=I32) * bm
    block_e = jnp.minimum(jnp.sum((pad_end[None, :] <= block_start[:, None]).astype(I32), axis=1),
                          N_EXPERTS - 1).astype(I32)
    n_used = (pad_end[-1] // bm).astype(I32).reshape(1)
    xs = jnp.zeros((nb * bm, D_MODEL), F32)
    dests = []
    first = 0
    for hn, _, _ in routed:
        dests.append(dest[first:first + 2 * hn.shape[0]])
        first += 2 * hn.shape[0]
        xs = _dispatch(dests[-1], hn, xs, min(2 * hn.shape[0], 1024))
    next_first = (pad_end[block_e] // bm).astype(I32)
    ys = _experts(block_e, n_used, next_first, xs, p['w_gate'], p['w_up'], p['w_down'], bm)
    return [_combine(d, gates, x1, ys, min(x1.shape[0], 256)) for d, (_, _, gates), x1 in zip(dests, routed, x1s)]


def _prep(norm_mix, w_in, s5_a_re, s5_a_im, s5_log_dt, s5_b_re, s5_b_im, s5_c_re, s5_c_im, s5_d,
          w_glu, q_norm, k_norm, idx_k_norm, w_branch_a, w_branch_b, w_out, norm_ffn,
          w_router_group, b_router_group, w_router_expert, b_router_expert, w_gate, w_up, w_down):
    pts = []
    acc = 0
    for s in IN_SIZES[:-1]:
        acc += s
        pts.append(acc)
    w_u, w_q, w_k, w_v, w_qi, w_ki, w_wi, w_ga, w_gb = jnp.split(w_in, pts, axis=1)
    pad = PROJ_COLS - sum(IN_SIZES)
    w_proj = jnp.concatenate([w_u, w_q, w_ga, w_gb, w_qi, w_k, w_v, w_ki, w_wi,
                              jnp.zeros((D_MODEL, pad), w_in.dtype)], axis=1).astype(BF16)
    one = lambda k: jnp.ones((k,), F32)
    gain = jnp.concatenate([one(S5_WIDTH), jnp.tile(q_norm.astype(F32), N_HEADS), one(2 * D_MODEL),
                            one(N_IDX_HEADS * IDX_DIM), jnp.tile(k_norm.astype(F32), N_KV_HEADS), one(KV_WIDTH),
                            idx_k_norm.astype(F32), one(N_IDX_HEADS + pad)]).reshape(1, PROJ_COLS)
    half = 256
    wv, wg = w_glu[:, :S5_WIDTH], w_glu[:, S5_WIDTH:]
    w_glu_p = jnp.concatenate(
        [jnp.concatenate([wv[:, c * half:(c + 1) * half], wg[:, c * half:(c + 1) * half]], axis=1)
         for c in range(S5_WIDTH // half)], axis=1).astype(BF16)
    w_r = jnp.concatenate([w_router_expert.astype(F32), w_router_group.astype(F32),
                           jnp.zeros((D_MODEL, LANES - N_EXPERTS - N_GROUPS), F32)], axis=1)
    wr_hi = w_r.astype(BF16)
    wr_lo = (w_r - wr_hi.astype(F32)).astype(BF16)
    b_r = jnp.concatenate([b_router_expert.astype(F32), b_router_group.astype(F32),
                           jnp.zeros((LANES - N_EXPERTS - N_GROUPS,), F32)]).reshape(1, LANES)
    return dict(norm_mix=norm_mix.astype(F32).reshape(1, D_MODEL), w_proj=w_proj, gain=gain,
                s5=_s5_tables(s5_a_re, s5_a_im, s5_log_dt, s5_b_re, s5_b_im, s5_c_re, s5_c_im, S5_MAX_CHUNK),
                s5_d=s5_d.astype(F32).reshape(1, S5_WIDTH), w_glu=w_glu_p,
                w_a=w_branch_a.astype(BF16), w_b=w_branch_b.astype(BF16), w_out=w_out.astype(BF16),
                norm_ffn=norm_ffn.astype(F32).reshape(1, D_MODEL), wr_hi=wr_hi, wr_lo=wr_lo, b_router=b_r,
                w_gate=w_gate, w_up=w_up, w_down=w_down)


def _mixer(x, h0_re, h0_im, k_past, v_past, ki_past, p, *, tm, lc, tt, tq, kb):
    bsz, t, _ = x.shape
    n = bsz * t
    x2 = x.reshape(n, D_MODEL)
    proj = _inproj(x2, p['norm_mix'], p['w_proj'], p['gain'], tm)
    proj3 = proj.reshape(bsz, t, PROJ_COLS)
    k = proj3[:, :, COL_K:COL_K + KV_WIDTH]
    v = proj3[:, :, COL_V:COL_V + KV_WIDTH]
    ki = proj3[:, :, COL_KI:COL_KI + IDX_DIM]
    n_state = S5_GROUPS * S5_STATE
    tables = dict(p['s5'])
    for name in ('air', 'aii', 'apr', 'api'):
        tables[name] = tables[name][:lc]
    tables['tri'] = jnp.tile(tables['tri'][:lc, :lc], (1, 2))
    yg, s_re, s_im = _s5(proj3, h0_re.reshape(bsz, 1, n_state), h0_im.reshape(bsz, 1, n_state),
                         tables, p['s5_d'], lc, tt)
    y_a = _glu(yg.reshape(n, S5_WIDTH), p['w_glu'], tm)
    if k_past is None:
        pos0 = 0
        k_all, v_all, ki_all = k, v, ki
    else:
        pos0 = k_past.shape[1]
        k_all = jnp.concatenate([k_past.reshape(bsz, pos0, KV_WIDTH), k], axis=1)
        v_all = jnp.concatenate([v_past.reshape(bsz, pos0, KV_WIDTH), v], axis=1)
        ki_all = jnp.concatenate([ki_past, ki], axis=1)
    n_keys = k_all.shape[1]
    n_top = min(TOP_K_MAX, n_keys // 4)
    nkp = -(-n_keys // kb) * kb
    padk = lambda a: jnp.pad(a.astype(BF16), ((0, 0), (0, nkp - n_keys), (0, 0)))
    kit = jnp.swapaxes(padk(ki_all), 1, 2)
    kt = jnp.swapaxes(padk(k_all), 1, 2)
    y_b = _dsa(proj3, kit, kt, padk(v_all), tq=tq, kb=kb, pos0=pos0, n_keys=n_keys, n_top=n_top)
    merged = _merge(y_a, y_b.reshape(n, ATT_WIDTH), p['w_a'], p['w_b'], proj, tm)
    x1 = _outproj(merged, p['w_out'], x2, tm)
    return (x1, s_re.reshape(bsz, S5_GROUPS, S5_STATE),
            s_im.reshape(bsz, S5_GROUPS, S5_STATE), k.reshape(bsz, t, N_KV_HEADS, HEAD_DIM),
            v.reshape(bsz, t, N_KV_HEADS, HEAD_DIM), ki)


def kernel(x_prompt, x_sample, state_s5_re, state_s5_im, cache_k, cache_v, cache_idx_k, norm_mix, w_in, s5_a_re, s5_a_im, s5_log_dt, s5_b_re, s5_b_im, s5_c_re, s5_c_im, s5_d, w_glu, q_norm, k_norm, idx_k_norm, w_branch_a, w_branch_b, w_out, norm_ffn, w_router_group, b_router_group, w_router_expert, b_router_expert, w_gate, w_up, w_down):
    p = _prep(norm_mix, w_in, s5_a_re, s5_a_im, s5_log_dt, s5_b_re, s5_b_im, s5_c_re, s5_c_im, s5_d,
              w_glu, q_norm, k_norm, idx_k_norm, w_branch_a, w_branch_b, w_out, norm_ffn,
              w_router_group, b_router_group, w_router_expert, b_router_expert, w_gate, w_up, w_down)
    h0 = jnp.zeros((x_prompt.shape[0], S5_GROUPS, S5_STATE), F32)
    tm_p, tm_s = 1024, 256
    xp, srp, sip, kp, vp, kip = _mixer(x_prompt, h0, h0, None, None, None, p,
                                       tm=tm_p, lc=64, tt=1024, tq=128, kb=512)
    xs, srs, sis, ks, vs, kis = _mixer(x_sample, state_s5_re, state_s5_im, cache_k, cache_v, cache_idx_k, p,
                                       tm=tm_s, lc=32, tt=32, tq=32, kb=384)
    yp, ys = _moe([xp, xs], p, [tm_p, tm_s], 256)
    return (yp.reshape(x_prompt.shape), ys.reshape(x_sample.shape), srp, sip, kp, vp, kip, srs, sis, ks, vs, kis)
```

```python
import functools
import math

import jax
import jax.numpy as jnp
from jax import lax
from jax.experimental import pallas as pl
from jax.experimental.pallas import tpu as pltpu

F32 = jnp.float32
BF16 = jnp.bfloat16
I32 = jnp.int32

D_MODEL = 2048
CHUNK = 64
EPS = 1e-6
S5_WIDTH = 1024
S5_GROUP = 16
S5_GROUPS = 64
S5_STATE = 64
S5_MAX_RE = -1e-4
N_HEADS = 8
N_KV_HEADS = 2
Q_PER_KV = 4
HEAD_DIM = 128
ATT_WIDTH = 1024
KV_WIDTH = 256
N_IDX_HEADS = 8
IDX_DIM = 64
TOP_K_MAX = 256
N_GROUPS = 4
EXPERTS_PER_GROUP = 8
N_EXPERTS = 32
D_EXPERT = 512
IN_SIZES = (S5_WIDTH, ATT_WIDTH, KV_WIDTH, KV_WIDTH, N_IDX_HEADS * IDX_DIM, IDX_DIM, N_IDX_HEADS, D_MODEL, D_MODEL)

LANES = 128
PROJ_TN = 512
COL_U, COL_Q, COL_GA, COL_GB, COL_QI, COL_K, COL_V, COL_KI, COL_WI = 0, 1024, 2048, 4096, 6144, 6656, 6912, 7168, 7232
PROJ_COLS = 7680
S5_LANE_BLOCKS = S5_WIDTH // LANES
S5_BLOCK_STATE = (LANES // S5_GROUP) * S5_STATE
S5_MAX_CHUNK = 64
VMEM_LIMIT = 56 * 1024 * 1024
INT_MIN = -2 ** 31
KEY_LOWEST_FINITE = -2 ** 31 + 0x00800000
NEG_BIG = -1e30
LOG2E = 1.4426950408889634
FAST_SOFTMAX_BOUND = 40.0
TOPK_PER_LANE = 12


def _dot(a, b):
    return jnp.dot(a, b, preferred_element_type=F32)


def _split_bf16(x):
    hi = x.astype(BF16)
    lo = (x - hi.astype(F32)).astype(BF16)
    return hi, lo


def _sigmoid(x):
    return 0.5 * jnp.tanh(0.5 * x) + 0.5


def _params(sem, **kw):
    return pltpu.CompilerParams(dimension_semantics=sem, vmem_limit_bytes=VMEM_LIMIT, **kw)


def _group_norm(a, gain):
    ms = jnp.mean(a * a, axis=-1, keepdims=True)
    return a * lax.rsqrt(ms + EPS) * gain


def _inproj_kernel(x_ref, g_ref, w_ref, gain_ref, o_ref, xn_ref):
    j = pl.program_id(1)

    @pl.when(j == 0)
    def _():
        xf = x_ref[...]
        ms = jnp.mean(xf * xf, axis=-1, keepdims=True)
        xn_ref[...] = (xf * lax.rsqrt(ms + EPS) * g_ref[...]).astype(BF16)

    acc = _dot(xn_ref[...], w_ref[...])
    gain = gain_ref[...]
    groups = [slice(c * LANES, (c + 1) * LANES) for c in range(PROJ_TN // LANES)]

    @pl.when((j < 2) | (j == 12))
    def _():
        o_ref[...] = acc

    @pl.when((j == 2) | (j == 3))
    def _():
        for s in groups:
            o_ref[:, s] = _group_norm(acc[:, s], gain[:, s])

    @pl.when((j >= 4) & (j < 12))
    def _():
        o_ref[...] = _sigmoid(acc)

    @pl.when(j == 13)
    def _():
        for s in groups[:2]:
            o_ref[:, s] = _group_norm(acc[:, s], gain[:, s])
        o_ref[:, 2 * LANES:] = acc[:, 2 * LANES:]

    @pl.when(j == 14)
    def _():
        a = acc[:, :LANES]
        lane = lax.broadcasted_iota(I32, a.shape, 1)
        is_ki = lane < IDX_DIM
        ms = jnp.sum(jnp.where(is_ki, a * a, 0.0), axis=-1, keepdims=True) * (1.0 / IDX_DIM)
        ki = a * lax.rsqrt(ms + EPS) * gain[:, :LANES]
        wi = a * (N_IDX_HEADS ** -0.5) * (IDX_DIM ** -0.5)
        o_ref[:, :LANES] = jnp.where(is_ki, ki, wi)
        o_ref[:, LANES:] = acc[:, LANES:]


def _inproj(x, g, w, gain, tm):
    n = x.shape[0]
    return pl.pallas_call(
        _inproj_kernel,
        grid=(n // tm, PROJ_COLS // PROJ_TN),
        in_specs=[pl.BlockSpec((tm, D_MODEL), lambda i, j: (i, 0)),
                  pl.BlockSpec((1, D_MODEL), lambda i, j: (0, 0)),
                  pl.BlockSpec((D_MODEL, PROJ_TN), lambda i, j: (0, j)),
                  pl.BlockSpec((1, PROJ_TN), lambda i, j: (0, j))],
        out_specs=pl.BlockSpec((tm, PROJ_TN), lambda i, j: (i, j)),
        out_shape=jax.ShapeDtypeStruct((n, PROJ_COLS), F32),
        scratch_shapes=[pltpu.VMEM((tm, D_MODEL), BF16)],
        compiler_params=_params(("arbitrary", "arbitrary")),
        name="inproj",
    )(x, g, w, gain)


def _gelu_tanh(y):
    return 0.5 * y * (1.0 + jnp.tanh(math.sqrt(2.0 / math.pi) * (y + 0.044715 * (y * y * y))))


def _s5_kernel(u_ref, h0r_ref, h0i_ref, bhi_ref, blo_ref, air_ref, aii_ref, apr_ref, api_ref,
               a1r_ref, a1i_ref, tri_ref, cd_ref, d_ref, yg_ref, sr_ref, si_ref,
               bu_scr, h_scr, hr_scr, hi_scr, *, lc, tt):
    t = pl.program_id(2)
    ns = S5_BLOCK_STATE

    @pl.when(t == 0)
    def _():
        hr_scr[...] = h0r_ref[...]
        hi_scr[...] = h0i_ref[...]

    u = u_ref[...]
    u_hi, u_lo = _split_bf16(u)
    bhi = bhi_ref[...]
    bu_scr[...] = _dot(u_hi, bhi) + _dot(jnp.concatenate([u_lo, u_hi], axis=1), blo_ref[...])
    tri = tri_ref[...]
    air, aii, apr, api = air_ref[...], aii_ref[...], apr_ref[...], api_ref[...]
    a1r, a1i = a1r_ref[...], a1i_ref[...]

    def chunk(s, carry):
        h_re, h_im = carry
        r0 = pl.multiple_of(s * lc, lc)
        br = bu_scr[pl.ds(r0, lc), 0:ns]
        bi = bu_scr[pl.ds(r0, lc), ns:2 * ns]
        z = jnp.concatenate([air * br - aii * bi, air * bi + aii * br], axis=1)
        z_hi, z_lo = _split_bf16(z)
        c = _dot(tri, jnp.concatenate([z_hi, z_lo], axis=0))
        cr = c[:, 0:ns] + (a1r * h_re - a1i * h_im)
        ci = c[:, ns:2 * ns] + (a1r * h_im + a1i * h_re)
        hr = apr * cr - api * ci
        hi = apr * ci + api * cr
        h_scr[pl.ds(r0, lc), 0:ns] = hr.astype(BF16)
        h_scr[pl.ds(r0, lc), ns:2 * ns] = hi.astype(BF16)
        return hr[lc - 1:lc, :], hi[lc - 1:lc, :]

    h_re, h_im = lax.fori_loop(0, tt // lc, chunk, (hr_scr[...], hi_scr[...]), unroll=min(4, tt // lc))
    hr_scr[...] = h_re
    hi_scr[...] = h_im
    sr_ref[...] = h_re
    si_ref[...] = h_im
    y = _dot(h_scr[...], cd_ref[...]) + d_ref[...] * u
    yg_ref[...] = _gelu_tanh(y).astype(BF16)


def _s5_tables(a_re, a_im, log_dt, b_re, b_im, c_re, c_im, lc):
    lr = jnp.minimum(a_re.astype(F32), S5_MAX_RE)
    li = a_im.astype(F32)
    dt = jnp.exp(log_dt.astype(F32))[:, None]
    mag = jnp.exp(lr * dt)
    lbr = mag * jnp.cos(li * dt)
    lbi = mag * jnp.sin(li * dt)
    den = lr * lr + li * li
    fr = ((lbr - 1.0) * lr + lbi * li) / den
    fi = (lbi * lr - (lbr - 1.0) * li) / den
    br = b_re.astype(F32)
    bi = b_im.astype(F32)
    bbr = fr[..., None] * br - fi[..., None] * bi
    bbi = fr[..., None] * bi + fi[..., None] * br
    j = jnp.arange(lc, dtype=F32)[:, None, None]
    ang = j * (li * dt)[None]
    lmag = j * (lr * dt)[None]
    n_state = S5_GROUPS * S5_STATE
    apr = (jnp.exp(lmag) * jnp.cos(ang)).reshape(lc, n_state)
    api = (jnp.exp(lmag) * jnp.sin(ang)).reshape(lc, n_state)
    air = (jnp.exp(-lmag) * jnp.cos(ang)).reshape(lc, n_state)
    aii = (-jnp.exp(-lmag) * jnp.sin(ang)).reshape(lc, n_state)
    a1r = lbr.reshape(1, n_state)
    a1i = lbi.reshape(1, n_state)
    gpb = LANES // S5_GROUP
    eye = jnp.eye(gpb, dtype=F32)

    def bdiag(b):
        return jnp.einsum('kgpc,gh->kgchp', b.reshape(S5_LANE_BLOCKS, gpb, S5_STATE, S5_GROUP), eye).reshape(
            S5_LANE_BLOCKS, LANES, S5_BLOCK_STATE)

    def cdiag(c):
        return jnp.einsum('kgcp,gh->kgphc', c.reshape(S5_LANE_BLOCKS, gpb, S5_GROUP, S5_STATE), eye).reshape(
            S5_LANE_BLOCKS, S5_BLOCK_STATE, LANES)

    bd = jnp.concatenate([bdiag(bbr), bdiag(bbi)], axis=-1)
    bd_hi = bd.astype(BF16)
    bd_lo = (bd - bd_hi.astype(F32)).astype(BF16)
    cd = jnp.concatenate([cdiag(c_re.astype(F32)), -cdiag(c_im.astype(F32))], axis=1).astype(BF16)
    tri = jnp.tril(jnp.ones((lc, lc), F32)).astype(BF16)
    bd_lo = jnp.concatenate([bd_hi, bd_lo], axis=1)
    return dict(bd_hi=bd_hi, bd_lo=bd_lo, air=air, aii=aii, apr=apr, api=api, a1r=a1r, a1i=a1i, tri=tri, cd=cd)


def _s5(proj3, h0_re, h0_im, tb, dvec, lc, tt):
    bsz, t, _ = proj3.shape
    ns = S5_BLOCK_STATE
    n_state = S5_GROUPS * S5_STATE
    tab = lambda: pl.BlockSpec((lc, ns), lambda b, k, i: (0, k))
    row = lambda: pl.BlockSpec((1, ns), lambda b, k, i: (0, k))
    st = lambda: pl.BlockSpec((None, 1, ns), lambda b, k, i: (b, 0, k))
    return pl.pallas_call(
        functools.partial(_s5_kernel, lc=lc, tt=tt),
        grid=(bsz, S5_LANE_BLOCKS, t // tt),
        in_specs=[pl.BlockSpec((None, tt, LANES), lambda b, k, i: (b, i, k)),
                  st(), st(),
                  pl.BlockSpec((None, LANES, 2 * ns), lambda b, k, i: (k, 0, 0)),
                  pl.BlockSpec((None, 2 * LANES, 2 * ns), lambda b, k, i: (k, 0, 0)),
                  tab(), tab(), tab(), tab(), row(), row(),
                  pl.BlockSpec((lc, 2 * lc), lambda b, k, i: (0, 0)),
                  pl.BlockSpec((None, 2 * ns, LANES), lambda b, k, i: (k, 0, 0)),
                  pl.BlockSpec((1, LANES), lambda b, k, i: (0, k))],
        out_specs=[pl.BlockSpec((None, tt, LANES), lambda b, k, i: (b, i, k)), st(), st()],
        out_shape=[jax.ShapeDtypeStruct((bsz, t, S5_WIDTH), BF16),
                   jax.ShapeDtypeStruct((bsz, 1, n_state), F32),
                   jax.ShapeDtypeStruct((bsz, 1, n_state), F32)],
        scratch_shapes=[pltpu.VMEM((tt, 2 * ns), F32), pltpu.VMEM((tt, 2 * ns), BF16),
                        pltpu.VMEM((1, ns), F32), pltpu.VMEM((1, ns), F32)],
        compiler_params=_params(("arbitrary", "arbitrary", "arbitrary")),
        name="s5",
    )(proj3, h0_re, h0_im, tb['bd_hi'], tb['bd_lo'], tb['air'], tb['aii'], tb['apr'], tb['api'],
      tb['a1r'], tb['a1i'], tb['tri'], tb['cd'], dvec)


def _glu_kernel(y_ref, w_ref, o_ref):
    acc = _dot(y_ref[...], w_ref[...])
    half = acc.shape[1] // 2
    o_ref[...] = (acc[:, :half] * _sigmoid(acc[:, half:])).astype(BF16)


def _glu(yg, w, tm):
    n = yg.shape[0]
    tn = 512
    return pl.pallas_call(
        _glu_kernel,
        grid=(n // tm, 2 * S5_WIDTH // tn),
        in_specs=[pl.BlockSpec((tm, S5_WIDTH), lambda i, j: (i, 0)),
                  pl.BlockSpec((S5_WIDTH, tn), lambda i, j: (0, j))],
        out_specs=pl.BlockSpec((tm, tn // 2), lambda i, j: (i, j)),
        out_shape=jax.ShapeDtypeStruct((n, S5_WIDTH), BF16),
        compiler_params=_params(("arbitrary", "arbitrary")),
        name="glu",
    )(yg, w)


def _dsa_kernel(q_ref, qi_ref, kiw_ref, kit_ref, kt_ref, v_ref, ut_ref, o_ref,
                key_scr, qs_scr, qis_scr, wis_scr, m_scr, l_scr, acc_scr,
                kmax_scr, lsum_scr, cand_scr, ckey_scr, thr_scr, cnt_scr,
                *, tq, kb, pos0, n_keys, n_top, nt):
    i = pl.program_id(1)
    nkp = key_scr.shape[2]
    n_cand = cand_scr.shape[1]
    slot_s = i % 2
    slot_a = 1 - slot_s
    tile_s = jnp.minimum(i, nt - 1)
    tile_a = jnp.maximum(i - 1, 0)

    def n_blocks(tile):
        k_end = jnp.minimum(n_keys, (pos0 + tile * tq + tq - 1) // CHUNK * CHUNK + CHUNK)
        return (k_end + kb - 1) // kb

    nkb_s = jnp.where(i < nt, n_blocks(tile_s), 0)
    nkb_a = jnp.where(i >= 1, n_blocks(tile_a), 0)

    @pl.when(i == 0)
    def _():
        def norm_block(b, carry):
            c0 = pl.multiple_of(b * kb, kb)
            out = []
            for g in range(N_KV_HEADS):
                kk = kt_ref[g * HEAD_DIM:(g + 1) * HEAD_DIM, pl.ds(c0, kb)].astype(F32)
                out.append(jnp.maximum(carry[g], jnp.sum(kk * kk, axis=0, keepdims=True)))
            return tuple(out)

        res = lax.fori_loop(0, nkp // kb, norm_block, (jnp.zeros((1, kb), F32),) * N_KV_HEADS)
        for g in range(N_KV_HEADS):
            kmax_scr[g] = jnp.broadcast_to(jnp.max(res[g], axis=-1, keepdims=True), (1, LANES))

    q = q_ref[...] * (HEAD_DIM ** -0.5 * LOG2E)
    for h in range(N_HEADS):
        g, r = divmod(h, Q_PER_KV)
        qs_scr[g, r * tq:(r + 1) * tq, :] = q[:, h * HEAD_DIM:(h + 1) * HEAD_DIM].astype(BF16)
    qi = qi_ref[...]
    kiw = kiw_ref[...]
    for h in range(N_IDX_HEADS):
        qis_scr[h * tq:(h + 1) * tq, :] = qi[:, h * IDX_DIM:(h + 1) * IDX_DIM].astype(BF16)
        wis_scr[h * tq:(h + 1) * tq, :] = kiw[:, IDX_DIM + h:IDX_DIM + h + 1]

    q_chunk = (pos0 + tile_s * tq + lax.broadcasted_iota(I32, (tq, 1), 0)) // CHUNK

    def to_key(x):
        bits = pltpu.bitcast(x, I32)
        return jnp.where(bits < 0, bits ^ 0x7FFFFFFF, bits)

    cand_scr[slot_s] = jnp.full(cand_scr.shape[1:], -jnp.inf, F32)

    def score_block(b, masked=True):
        c0 = pl.multiple_of(b * kb, kb)
        s = _dot(qis_scr[...], kit_ref[:, pl.ds(c0, kb)])
        s = jnp.maximum(s, 0.0) * wis_scr[...]
        sc = s[0:tq]
        for h in range(1, N_IDX_HEADS):
            sc = sc + s[h * tq:(h + 1) * tq]
        blk = sc + 0.0
        if masked:
            k_pos = c0 + lax.broadcasted_iota(I32, (1, kb), 1)
            adm = (k_pos // CHUNK <= q_chunk) & (k_pos < n_keys)
            blk = jnp.where(adm, blk, -jnp.inf)
        key_scr[slot_s, :, pl.ds(c0, kb)] = blk
        for rg in range(tq // 8):
            rows = slice(rg * 8, rg * 8 + 8)
            top = [cand_scr[slot_s, j, rows, :] for j in range(n_cand)]
            for c in range(kb // LANES):
                x = blk[rows, c * LANES:(c + 1) * LANES]
                for j in range(n_cand):
                    hi = jnp.maximum(top[j], x)
                    x = jnp.minimum(top[j], x)
                    top[j] = hi
            for j in range(n_cand):
                cand_scr[slot_s, j, rows, :] = top[j]

    def count_all(hit_fn):
        def body(b, acc):
            c0 = pl.multiple_of(b * kb, kb)
            hit = hit_fn(key_scr[slot_a, :, pl.ds(c0, kb)])
            for c in range(kb // LANES):
                acc = acc + hit[:, c * LANES:(c + 1) * LANES]
            return acc
        acc = lax.fori_loop(0, nkb_a, body, jnp.zeros((tq, LANES), F32))
        return jnp.sum(acc, axis=-1, keepdims=True)

    def count_cand(cand):
        acc = jnp.zeros((tq, LANES), F32)
        for j in range(n_cand):
            acc = acc + jnp.where(ckey_scr[j] >= cand, 1.0, 0.0)
        return jnp.sum(acc, axis=-1, keepdims=True)

    def bisect(count_fn, first_bit=0, prefix=None):
        def bit_pass(it, t_off):
            cand_off = t_off | lax.shift_left(jnp.int32(1), 31 - it)
            cnt = count_fn(cand_off ^ INT_MIN)
            return jnp.where(cnt >= n_top, cand_off, t_off)
        t_off = lax.fori_loop(first_bit, 32, bit_pass, jnp.zeros((tq, 1), I32) if prefix is None else prefix)
        key = jnp.maximum(t_off ^ INT_MIN, KEY_LOWEST_FINITE)
        return pltpu.bitcast(jnp.where(key < 0, key ^ 0x7FFFFFFF, key), F32)

    def candidate_bracket():
        hi = to_key(jnp.max(cand_scr[slot_a, 0], axis=-1, keepdims=True)) ^ INT_MIN
        lo = to_key(jnp.min(cand_scr[slot_a, 1], axis=-1, keepdims=True)) ^ INT_MIN
        n_common = jnp.min(lax.clz(hi ^ lo).astype(F32)).astype(I32)
        mask = jnp.where(n_common > 0, lax.shift_left(jnp.int32(-1), 32 - jnp.maximum(n_common, 1)), 0)
        return n_common, hi & mask

    @pl.when(i >= 1)
    def _():
        for j in range(n_cand):
            ckey_scr[j] = to_key(cand_scr[slot_a, j])
        thr_cand = bisect(count_cand, *candidate_bracket()) if n_top <= 2 * LANES else bisect(count_cand)
        thr_scr[...] = thr_cand
        n_all = count_all(lambda x: jnp.where(x >= thr_cand, 1.0, 0.0))
        n_listed = jnp.zeros((tq, LANES), F32)
        for j in range(n_cand):
            n_listed = n_listed + jnp.where(cand_scr[slot_a, j] >= thr_cand, 1.0, 0.0)
        cnt_scr[...] = n_all
        overflow = jnp.max(n_all - jnp.sum(n_listed, axis=-1, keepdims=True)) > 0.0

        @pl.when(overflow)
        def _():
            thr_all = bisect(lambda cand: count_all(lambda x: jnp.where(to_key(x) >= cand, 1.0, 0.0)))
            thr_scr[...] = thr_all
            cnt_scr[...] = count_all(lambda x: jnp.where(x >= thr_all, 1.0, 0.0))

        @pl.when(jnp.max(cnt_scr[...]) > n_top)
        def _():
            thr_t = thr_scr[...]
            need = n_top - count_all(lambda x: jnp.where(x > thr_t, 1.0, 0.0))

            def drop_surplus(b, seen):
                c0 = pl.multiple_of(b * kb, kb)
                x = key_scr[slot_a, :, pl.ds(c0, kb)]
                tied = x == thr_t
                rank = seen + _dot(jnp.where(tied, 1.0, 0.0).astype(BF16), ut_ref[...])
                key_scr[slot_a, :, pl.ds(c0, kb)] = jnp.where(tied & (rank > need), -jnp.inf, x)
                return rank[:, kb - 1:kb]

            lax.fori_loop(0, nkb_a, drop_surplus, jnp.zeros((tq, 1), F32))

    thr = thr_scr[...]

    acc_scr[...] = jnp.zeros(acc_scr.shape, F32)
    lsum_scr[...] = jnp.zeros(lsum_scr.shape, F32)
    bound_max = jnp.float32(0.0)
    for g in range(N_KV_HEADS):
        qg = qs_scr[g].astype(F32)
        qn2 = jnp.sum(qg * qg, axis=-1, keepdims=True)
        bound = jnp.sqrt(qn2 * kmax_scr[g][:, 0:1]) * 1.002
        bound_max = jnp.maximum(bound_max, jnp.max(bound))
    fast = bound_max <= FAST_SOFTMAX_BOUND

    @pl.when(jnp.logical_not(fast))
    def _():
        m_scr[...] = jnp.full(m_scr.shape, NEG_BIG, F32)
        l_scr[...] = jnp.zeros(l_scr.shape, F32)

    def attend_fast(b):
        c0 = pl.multiple_of(b * kb, kb)
        bias = jnp.where(key_scr[slot_a, :, pl.ds(c0, kb)] >= thr, 0.0, NEG_BIG)
        for g in range(N_KV_HEADS):
            lg = _dot(qs_scr[g], kt_ref[g * HEAD_DIM:(g + 1) * HEAD_DIM, pl.ds(c0, kb)])
            parts = []
            for r in range(Q_PER_KV):
                rows = slice(r * tq, (r + 1) * tq)
                e = [jnp.exp2(lg[rows, c * LANES:(c + 1) * LANES] + bias[:, c * LANES:(c + 1) * LANES])
                     for c in range(kb // LANES)]
                ls = e[0]
                for c in range(1, kb // LANES):
                    ls = ls + e[c]
                lsum_scr[g, rows, :] += ls
                parts.append(jnp.concatenate(e, axis=1).astype(BF16))
            p = jnp.concatenate(parts, axis=0)
            acc_scr[g] += _dot(p, v_ref[pl.ds(c0, kb), g * HEAD_DIM:(g + 1) * HEAD_DIM])

    def attend_general(b):
        c0 = pl.multiple_of(b * kb, kb)
        sel = key_scr[slot_a, :, pl.ds(c0, kb)] >= thr
        sel4 = jnp.concatenate([sel] * Q_PER_KV, axis=0)
        for g in range(N_KV_HEADS):
            lg = _dot(qs_scr[g], kt_ref[g * HEAD_DIM:(g + 1) * HEAD_DIM, pl.ds(c0, kb)])
            lg = jnp.where(sel4, lg, NEG_BIG)
            m_old = m_scr[g]
            m_new = jnp.maximum(m_old, jnp.max(lg, axis=-1, keepdims=True))
            p = jnp.exp2(lg - m_new)
            alpha = jnp.exp2(m_old - m_new)
            l_scr[g] = alpha * l_scr[g] + jnp.sum(p, axis=-1, keepdims=True)
            pv = _dot(p.astype(BF16), v_ref[pl.ds(c0, kb), g * HEAD_DIM:(g + 1) * HEAD_DIM])
            acc_scr[g] = alpha * acc_scr[g] + pv
            m_scr[g] = m_new

    def loop(lo, hi, *fns):
        def body(b, carry):
            for fn in fns:
                fn(b)
            return carry
        lax.fori_loop(lo, hi, body, 0)

    n_fused = jnp.where(fast, jnp.minimum(nkb_a, nkb_s), 0)
    first_row_end = (pos0 + tile_s * tq) // CHUNK * CHUNK + CHUNK
    n_plain = jnp.minimum(n_fused, jnp.minimum(first_row_end, n_keys) // kb)
    loop(0, n_plain, functools.partial(score_block, masked=False), attend_fast)
    loop(n_plain, n_fused, score_block, attend_fast)
    loop(n_fused, nkb_s, score_block)
    loop(n_fused, jnp.where(fast, nkb_a, 0), attend_fast)
    loop(0, jnp.where(fast, 0, nkb_a), attend_general)

    @pl.when((i >= 1) & fast)
    def _():
        for h in range(N_HEADS):
            g, r = divmod(h, Q_PER_KV)
            rows = slice(r * tq, (r + 1) * tq)
            l = jnp.sum(lsum_scr[g, rows, :], axis=-1, keepdims=True)
            o_ref[:, h * HEAD_DIM:(h + 1) * HEAD_DIM] = (acc_scr[g, rows, :] / l).astype(BF16)

    @pl.when((i >= 1) & jnp.logical_not(fast))
    def _():
        for h in range(N_HEADS):
            g, r = divmod(h, Q_PER_KV)
            rows = slice(r * tq, (r + 1) * tq)
            o_ref[:, h * HEAD_DIM:(h + 1) * HEAD_DIM] = (acc_scr[g, rows, :] / l_scr[g, rows, :]).astype(BF16)


def _dsa(proj3, kit, kt, v, *, tq, kb, pos0, n_keys, n_top):
    bsz, t, _ = proj3.shape
    nkp = kit.shape[-1]
    nt = t // tq
    kern = functools.partial(_dsa_kernel, tq=tq, kb=kb, pos0=pos0, n_keys=n_keys, n_top=n_top, nt=nt)
    resident = dict(pipeline_mode=pl.Buffered(1))
    prev = lambda i: jnp.maximum(i - 1, 0)
    cur = lambda i: jnp.minimum(i, nt - 1)
    return pl.pallas_call(
        kern,
        grid=(bsz, nt + 1),
        in_specs=[pl.BlockSpec((None, tq, ATT_WIDTH), lambda b, i: (b, prev(i), COL_Q // ATT_WIDTH)),
                  pl.BlockSpec((None, tq, 512), lambda b, i: (b, cur(i), COL_QI // 512)),
                  pl.BlockSpec((None, tq, 512), lambda b, i: (b, cur(i), COL_KI // 512)),
                  pl.BlockSpec((None, IDX_DIM, nkp), lambda b, i: (b, 0, 0), **resident),
                  pl.BlockSpec((None, KV_WIDTH, nkp), lambda b, i: (b, 0, 0), **resident),
                  pl.BlockSpec((None, nkp, KV_WIDTH), lambda b, i: (b, 0, 0), **resident),
                  pl.BlockSpec((kb, kb), lambda b, i: (0, 0), **resident)],
        out_specs=pl.BlockSpec((None, tq, ATT_WIDTH), lambda b, i: (b, prev(i), 0)),
        out_shape=jax.ShapeDtypeStruct((bsz, t, ATT_WIDTH), BF16),
        scratch_shapes=[pltpu.VMEM((2, tq, nkp), F32),
                        pltpu.VMEM((N_KV_HEADS, Q_PER_KV * tq, HEAD_DIM), BF16),
                        pltpu.VMEM((N_IDX_HEADS * tq, IDX_DIM), BF16),
                        pltpu.VMEM((N_IDX_HEADS * tq, 1), F32),
                        pltpu.VMEM((N_KV_HEADS, Q_PER_KV * tq, 1), F32),
                        pltpu.VMEM((N_KV_HEADS, Q_PER_KV * tq, 1), F32),
                        pltpu.VMEM((N_KV_HEADS, Q_PER_KV * tq, HEAD_DIM), F32),
                        pltpu.VMEM((N_KV_HEADS, 1, LANES), F32),
                        pltpu.VMEM((N_KV_HEADS, Q_PER_KV * tq, LANES), F32),
                        pltpu.VMEM((2, TOPK_PER_LANE, tq, LANES), F32),
                        pltpu.VMEM((TOPK_PER_LANE, tq, LANES), I32),
                        pltpu.VMEM((tq, 1), F32),
                        pltpu.VMEM((tq, 1), F32)],
        compiler_params=_params(("arbitrary", "arbitrary")),
        name="dsa",
    )(proj3, proj3, proj3, kit, kt, v, jnp.triu(jnp.ones((kb, kb), F32)).astype(BF16))


def _merge_kernel(ya_ref, yb_ref, wa_ref, wb_ref, ga_ref, gb_ref, o_ref):
    o_ref[...] = (ga_ref[...] * _dot(ya_ref[...], wa_ref[...])
                  + gb_ref[...] * _dot(yb_ref[...], wb_ref[...])).astype(BF16)


def _merge(ya, yb, wa, wb, proj, tm):
    n = ya.shape[0]
    tn = 512
    return pl.pallas_call(
        _merge_kernel,
        grid=(n // tm, D_MODEL // tn),
        in_specs=[pl.BlockSpec((tm, S5_WIDTH), lambda i, j: (i, 0)),
                  pl.BlockSpec((tm, ATT_WIDTH), lambda i, j: (i, 0)),
                  pl.BlockSpec((S5_WIDTH, tn), lambda i, j: (0, j)),
                  pl.BlockSpec((ATT_WIDTH, tn), lambda i, j: (0, j)),
                  pl.BlockSpec((tm, tn), lambda i, j: (i, COL_GA // tn + j)),
                  pl.BlockSpec((tm, tn), lambda i, j: (i, COL_GB // tn + j))],
        out_specs=pl.BlockSpec((tm, tn), lambda i, j: (i, j)),
        out_shape=jax.ShapeDtypeStruct((n, D_MODEL), BF16),
        compiler_params=_params(("arbitrary", "arbitrary")),
        name="merge",
    )(ya, yb, wa, wb, proj, proj)


def _outproj_kernel(m_ref, w_ref, x_ref, o_ref):
    o_ref[...] = x_ref[...] + _dot(m_ref[...], w_ref[...])


def _outproj(merged, w, x, tm):
    n = x.shape[0]
    tn = 512
    return pl.pallas_call(
        _outproj_kernel,
        grid=(n // tm, D_MODEL // tn),
        in_specs=[pl.BlockSpec((tm, D_MODEL), lambda i, j: (i, 0)),
                  pl.BlockSpec((D_MODEL, tn), lambda i, j: (0, j)),
                  pl.BlockSpec((tm, tn), lambda i, j: (i, j))],
        out_specs=pl.BlockSpec((tm, tn), lambda i, j: (i, j)),
        out_shape=jax.ShapeDtypeStruct((n, D_MODEL), F32),
        compiler_params=_params(("arbitrary", "arbitrary")),
        name="outproj",
    )(merged, w, x)


ROUTER_GROUP_LANE = N_EXPERTS


def _router_kernel(x_ref, g_ref, whi_ref, wlo_ref, b_ref, hn_ref, ids_ref, gates_ref):
    xf = x_ref[...]
    ms = jnp.mean(xf * xf, axis=-1, keepdims=True)
    hn = xf * lax.rsqrt(ms + EPS) * g_ref[...]
    hn_ref[...] = hn
    h_hi, h_lo = _split_bf16(hn)
    whi = whi_ref[...]
    lg = _dot(h_hi, whi) + _dot(h_lo, whi) + _dot(h_hi, wlo_ref[...]) + b_ref[...]
    lane = lax.broadcasted_iota(I32, lg.shape, 1)
    lane_f = lane.astype(F32)
    big = float(LANES)
    is_g = (lane >= ROUTER_GROUP_LANE) & (lane < ROUTER_GROUP_LANE + N_GROUPS)
    g_max = jnp.max(jnp.where(is_g, lg, -jnp.inf), axis=-1, keepdims=True)
    g_den = jnp.sum(jnp.where(is_g, jnp.exp(lg - g_max), 0.0), axis=-1, keepdims=True)
    g_w = 1.0 / g_den
    g_lane = jnp.min(jnp.where(is_g & (lg == g_max), lane_f, big), axis=-1, keepdims=True)
    g_sel = g_lane.astype(I32) - ROUTER_GROUP_LANE
    is_e = (lane < N_EXPERTS) & (lane // EXPERTS_PER_GROUP == g_sel)
    e_max = jnp.max(jnp.where(is_e, lg, -jnp.inf), axis=-1, keepdims=True)
    pe = jnp.where(is_e, jnp.exp(lg - e_max), 0.0)
    pe = pe / jnp.sum(pe, axis=-1, keepdims=True)
    pe = jnp.where(is_e, pe, -1.0)
    p1 = jnp.max(pe, axis=-1, keepdims=True)
    i1 = jnp.min(jnp.where(pe == p1, lane_f, big), axis=-1, keepdims=True)
    pe2 = jnp.where(lane_f == i1, -1.0, pe)
    p2 = jnp.max(pe2, axis=-1, keepdims=True)
    i2 = jnp.min(jnp.where(pe2 == p2, lane_f, big), axis=-1, keepdims=True)
    tot = p1 + p2
    ids_ref[...] = jnp.where(lane == 0, i1, jnp.where(lane == 1, i2, 0.0)).astype(I32)
    gates_ref[...] = jnp.where(lane == 0, g_w * (p1 / tot), jnp.where(lane == 1, g_w * (p2 / tot), 0.0))


def _router(x1, g, whi, wlo, bias, tm):
    n = x1.shape[0]
    return pl.pallas_call(
        _router_kernel,
        grid=(n // tm,),
        in_specs=[pl.BlockSpec((tm, D_MODEL), lambda i: (i, 0)),
                  pl.BlockSpec((1, D_MODEL), lambda i: (0, 0)),
                  pl.BlockSpec((D_MODEL, LANES), lambda i: (0, 0)),
                  pl.BlockSpec((D_MODEL, LANES), lambda i: (0, 0)),
                  pl.BlockSpec((1, LANES), lambda i: (0, 0))],
        out_specs=[pl.BlockSpec((tm, D_MODEL), lambda i: (i, 0)),
                   pl.BlockSpec((tm, LANES), lambda i: (i, 0)),
                   pl.BlockSpec((tm, LANES), lambda i: (i, 0))],
        out_shape=[jax.ShapeDtypeStruct((n, D_MODEL), F32),
                   jax.ShapeDtypeStruct((n, LANES), I32),
                   jax.ShapeDtypeStruct((n, LANES), F32)],
        compiler_params=_params(("arbitrary",)),
        name="router",
    )(x1, g, whi, wlo, bias)


def _row_copy(src, dst, s_row, d_row, sem):
    return pltpu.make_async_copy(src.at[pl.ds(s_row, 1)], dst.at[pl.ds(d_row, 1)], sem)


def _dispatch_kernel(dest_ref, hn_ref, xs_in_ref, xs_ref, sem, *, ch):
    del xs_in_ref

    def issue(t, carry):
        for s in range(2):
            _row_copy(hn_ref, xs_ref, t, dest_ref[0, 0, 2 * t + s], sem).start()
        return carry

    lax.fori_loop(0, ch // 2, issue, 0, unroll=8)
    for s in range(2):
        pltpu.make_async_copy(hn_ref, xs_ref.at[pl.ds(0, ch // 2)], sem).wait()


def _dispatch(dest, hn, xs0, ch):
    m = dest.shape[0]
    return pl.pallas_call(
        functools.partial(_dispatch_kernel, ch=ch),
        grid=(m // ch,),
        in_specs=[pl.BlockSpec((1, 1, ch), lambda i: (i, 0, 0), memory_space=pltpu.SMEM),
                  pl.BlockSpec((ch // 2, D_MODEL), lambda i: (i, 0)),
                  pl.BlockSpec(memory_space=pl.ANY)],
        out_specs=pl.BlockSpec(memory_space=pl.ANY),
        out_shape=jax.ShapeDtypeStruct(xs0.shape, xs0.dtype),
        scratch_shapes=[pltpu.SemaphoreType.DMA(())],
        input_output_aliases={2: 0},
        compiler_params=_params(("arbitrary",), disable_bounds_checks=True, has_side_effects=True),
        name="dispatch",
    )(dest.reshape(m // ch, 1, ch), hn, xs0)


def _expert_kernel(be_ref, nu_ref, x_ref, wg_hbm, wu_hbm, wd_hbm, o_ref,
                   wg_f, wu_f, wd_f, wg_s, wu_s, wd_s, ord_ref, sems):
    b = pl.program_id(0)
    n_used = nu_ref[0]
    e_here = be_ref[b]

    def weight_copies(e, slot):
        return [pltpu.make_async_copy(src.at[e], dst.at[slot], sems.at[slot, k])
                for k, (src, dst) in enumerate(((wg_hbm, wg_f), (wu_hbm, wu_f), (wd_hbm, wd_f)))]

    @pl.when(b < n_used)
    def _():
        @pl.when((b == 0) | (e_here != be_ref[jnp.maximum(b - 1, 0)]))
        def _():
            ordinal = jnp.where(b == 0, 0, ord_ref[0] + 1)
            ord_ref[0] = ordinal
            slot = ordinal % 2
            nxt = lax.while_loop(lambda j: (j < n_used) & (be_ref[jnp.minimum(j, be_ref.shape[0] - 1)] == e_here),
                                 lambda j: j + 1, b + 1)

            @pl.when(b == 0)
            def _():
                for c in weight_copies(e_here, slot):
                    c.start()

            for c in weight_copies(e_here, slot):
                c.wait()
            for s in range(2):
                @pl.when(slot == s)
                def _(s=s):
                    wg_s[...] = wg_f[s].astype(BF16)
                    wu_s[...] = wu_f[s].astype(BF16)
                    wd_s[...] = wd_f[s].astype(BF16)

            @pl.when(nxt < n_used)
            def _():
                for c in weight_copies(be_ref[jnp.minimum(nxt, be_ref.shape[0] - 1)], 1 - slot):
                    c.start()

        x = x_ref[...].astype(BF16)
        gate = _dot(x, wg_s[...])
        h = gate * _sigmoid(gate) * _dot(x, wu_s[...])
        o_ref[...] = _dot(h.astype(BF16), wd_s[...])

    @pl.when(b >= nu_ref[0])
    def _():
        o_ref[...] = jnp.zeros(o_ref.shape, F32)


def _experts(block_e, n_used, xs, wg, wu, wd, bm):
    rows = xs.shape[0]
    nb = rows // bm
    any_space = pl.BlockSpec(memory_space=pl.ANY)
    grid_spec = pltpu.PrefetchScalarGridSpec(
        num_scalar_prefetch=2,
        grid=(nb,),
        in_specs=[pl.BlockSpec((bm, D_MODEL), lambda b, *_: (b, 0)), any_space, any_space, any_space],
        out_specs=pl.BlockSpec((bm, D_MODEL), lambda b, *_: (b, 0)),
        scratch_shapes=[pltpu.VMEM((2, D_MODEL, D_EXPERT), F32), pltpu.VMEM((2, D_MODEL, D_EXPERT), F32),
                        pltpu.VMEM((2, D_EXPERT, D_MODEL), F32),
                        pltpu.VMEM((D_MODEL, D_EXPERT), BF16), pltpu.VMEM((D_MODEL, D_EXPERT), BF16),
                        pltpu.VMEM((D_EXPERT, D_MODEL), BF16),
                        pltpu.SMEM((1,), I32),
                        pltpu.SemaphoreType.DMA((2, 3))])
    return pl.pallas_call(
        _expert_kernel,
        grid_spec=grid_spec,
        out_shape=jax.ShapeDtypeStruct((rows, D_MODEL), F32),
        compiler_params=_params(("arbitrary",)),
        name="experts",
    )(block_e, n_used, xs, wg, wu, wd)


def _combine_kernel(dest_ref, gates_ref, x_ref, ys_ref, o_ref, buf, sem, *, tc):
    def issue(t, carry):
        for s in range(2):
            _row_copy(ys_ref, buf.at[s], dest_ref[0, 0, 2 * t + s], t, sem).start()
        return carry

    lax.fori_loop(0, tc, issue, 0, unroll=8)
    for s in range(2):
        pltpu.make_async_copy(ys_ref.at[pl.ds(0, tc)], buf.at[s], sem).wait()
    gates = gates_ref[...]
    o_ref[...] = x_ref[...] + gates[:, 0:1] * buf[0] + gates[:, 1:2] * buf[1]


def _combine(dest, gates, x1, ys, tc):
    n = x1.shape[0]
    return pl.pallas_call(
        functools.partial(_combine_kernel, tc=tc),
        grid=(n // tc,),
        in_specs=[pl.BlockSpec((1, 1, 2 * tc), lambda i: (i, 0, 0), memory_space=pltpu.SMEM),
                  pl.BlockSpec((tc, LANES), lambda i: (i, 0)),
                  pl.BlockSpec((tc, D_MODEL), lambda i: (i, 0)),
                  pl.BlockSpec(memory_space=pl.ANY)],
        out_specs=pl.BlockSpec((tc, D_MODEL), lambda i: (i, 0)),
        out_shape=jax.ShapeDtypeStruct((n, D_MODEL), F32),
        scratch_shapes=[pltpu.VMEM((2, tc, D_MODEL), F32), pltpu.SemaphoreType.DMA(())],
        compiler_params=_params(("arbitrary",), disable_bounds_checks=True),
        name="combine",
    )(dest.reshape(n // tc, 1, 2 * tc), gates, x1, ys)


def _moe(x1s, p, tms, bm):
    routed = [_router(x1, p['norm_ffn'], p['wr_hi'], p['wr_lo'], p['b_router'], tm) for x1, tm in zip(x1s, tms)]
    flat_e = jnp.concatenate([ids[:, :2].reshape(-1) for _, ids, _ in routed])
    m = flat_e.shape[0]
    onehot = (flat_e[:, None] == jnp.arange(N_EXPERTS, dtype=I32)[None, :]).astype(I32)
    csum = jnp.cumsum(onehot, axis=0)
    counts = csum[-1]
    padded = (counts + bm - 1) // bm * bm
    pad_end = jnp.cumsum(padded)
    pad_start = pad_end - padded
    dest = jnp.sum(onehot * (csum - 1 + pad_start[None, :]), axis=1).astype(I32)
    nb = -(-(m + N_EXPERTS * (bm - 1)) // bm)
    block_start = jnp.arange(nb, dtype=I32) * bm
    block_e = jnp.minimum(jnp.sum((pad_end[None, :] <= block_start[:, None]).astype(I32), axis=1),
                          N_EXPERTS - 1).astype(I32)
    n_used = (pad_end[-1] // bm).astype(I32).reshape(1)
    xs = jnp.zeros((nb * bm, D_MODEL), F32)
    dests = []
    first = 0
    for hn, _, _ in routed:
        dests.append(dest[first:first + 2 * hn.shape[0]])
        first += 2 * hn.shape[0]
        xs = _dispatch(dests[-1], hn, xs, min(2 * hn.shape[0], 1024))
    ys = _experts(block_e, n_used, xs, p['w_gate'], p['w_up'], p['w_down'], bm)
    return [_combine(d, gates, x1, ys, min(x1.shape[0], 256)) for d, (_, _, gates), x1 in zip(dests, routed, x1s)]


def _prep(norm_mix, w_in, s5_a_re, s5_a_im, s5_log_dt, s5_b_re, s5_b_im, s5_c_re, s5_c_im, s5_d,
          w_glu, q_norm, k_norm, idx_k_norm, w_branch_a, w_branch_b, w_out, norm_ffn,
          w_router_group, b_router_group, w_router_expert, b_router_expert, w_gate, w_up, w_down):
    pts = []
    acc = 0
    for s in IN_SIZES[:-1]:
        acc += s
        pts.append(acc)
    w_u, w_q, w_k, w_v, w_qi, w_ki, w_wi, w_ga, w_gb = jnp.split(w_in, pts, axis=1)
    pad = PROJ_COLS - sum(IN_SIZES)
    w_proj = jnp.concatenate([w_u, w_q, w_ga, w_gb, w_qi, w_k, w_v, w_ki, w_wi,
                              jnp.zeros((D_MODEL, pad), w_in.dtype)], axis=1).astype(BF16)
    one = lambda k: jnp.ones((k,), F32)
    gain = jnp.concatenate([one(S5_WIDTH), jnp.tile(q_norm.astype(F32), N_HEADS), one(2 * D_MODEL),
                            one(N_IDX_HEADS * IDX_DIM), jnp.tile(k_norm.astype(F32), N_KV_HEADS), one(KV_WIDTH),
                            idx_k_norm.astype(F32), one(N_IDX_HEADS + pad)]).reshape(1, PROJ_COLS)
    half = 256
    wv, wg = w_glu[:, :S5_WIDTH], w_glu[:, S5_WIDTH:]
    w_glu_p = jnp.concatenate(
        [jnp.concatenate([wv[:, c * half:(c + 1) * half], wg[:, c * half:(c + 1) * half]], axis=1)
         for c in range(S5_WIDTH // half)], axis=1).astype(BF16)
    w_r = jnp.concatenate([w_router_expert.astype(F32), w_router_group.astype(F32),
                           jnp.zeros((D_MODEL, LANES - N_EXPERTS - N_GROUPS), F32)], axis=1)
    wr_hi = w_r.astype(BF16)
    wr_lo = (w_r - wr_hi.astype(F32)).astype(BF16)
    b_r = jnp.concatenate([b_router_expert.astype(F32), b_router_group.astype(F32),
                           jnp.zeros((LANES - N_EXPERTS - N_GROUPS,), F32)]).reshape(1, LANES)
    return dict(norm_mix=norm_mix.astype(F32).reshape(1, D_MODEL), w_proj=w_proj, gain=gain,
                s5=_s5_tables(s5_a_re, s5_a_im, s5_log_dt, s5_b_re, s5_b_im, s5_c_re, s5_c_im, S5_MAX_CHUNK),
                s5_d=s5_d.astype(F32).reshape(1, S5_WIDTH), w_glu=w_glu_p,
                w_a=w_branch_a.astype(BF16), w_b=w_branch_b.astype(BF16), w_out=w_out.astype(BF16),
                norm_ffn=norm_ffn.astype(F32).reshape(1, D_MODEL), wr_hi=wr_hi, wr_lo=wr_lo, b_router=b_r,
                w_gate=w_gate, w_up=w_up, w_down=w_down)


def _mixer(x, h0_re, h0_im, k_past, v_past, ki_past, p, *, tm, lc, tt, tq, kb):
    bsz, t, _ = x.shape
    n = bsz * t
    x2 = x.reshape(n, D_MODEL)
    proj = _inproj(x2, p['norm_mix'], p['w_proj'], p['gain'], tm)
    proj3 = proj.reshape(bsz, t, PROJ_COLS)
    k = proj3[:, :, COL_K:COL_K + KV_WIDTH]
    v = proj3[:, :, COL_V:COL_V + KV_WIDTH]
    ki = proj3[:, :, COL_KI:COL_KI + IDX_DIM]
    n_state = S5_GROUPS * S5_STATE
    tables = dict(p['s5'])
    for name in ('air', 'aii', 'apr', 'api'):
        tables[name] = tables[name][:lc]
    tables['tri'] = jnp.tile(tables['tri'][:lc, :lc], (1, 2))
    yg, s_re, s_im = _s5(proj3, h0_re.reshape(bsz, 1, n_state), h0_im.reshape(bsz, 1, n_state),
                         tables, p['s5_d'], lc, tt)
    tm_wide = 2 * tm if n % (2 * tm) == 0 else tm
    y_a = _glu(yg.reshape(n, S5_WIDTH), p['w_glu'], tm_wide)
    if k_past is None:
        pos0 = 0
        k_all, v_all, ki_all = k, v, ki
    else:
        pos0 = k_past.shape[1]
        k_all = jnp.concatenate([k_past.reshape(bsz, pos0, KV_WIDTH), k], axis=1)
        v_all = jnp.concatenate([v_past.reshape(bsz, pos0, KV_WIDTH), v], axis=1)
        ki_all = jnp.concatenate([ki_past, ki], axis=1)
    n_keys = k_all.shape[1]
    n_top = min(TOP_K_MAX, n_keys // 4)
    nkp = -(-n_keys // kb) * kb
    padk = lambda a: jnp.pad(a.astype(BF16), ((0, 0), (0, nkp - n_keys), (0, 0)))
    kit = jnp.swapaxes(padk(ki_all), 1, 2)
    kt = jnp.swapaxes(padk(k_all), 1, 2)
    y_b = _dsa(proj3, kit, kt, padk(v_all), tq=tq, kb=kb, pos0=pos0, n_keys=n_keys, n_top=n_top)
    merged = _merge(y_a, y_b.reshape(n, ATT_WIDTH), p['w_a'], p['w_b'], proj, tm_wide)
    x1 = _outproj(merged, p['w_out'], x2, tm_wide)
    return (x1, s_re.reshape(bsz, S5_GROUPS, S5_STATE),
            s_im.reshape(bsz, S5_GROUPS, S5_STATE), k.reshape(bsz, t, N_KV_HEADS, HEAD_DIM),
            v.reshape(bsz, t, N_KV_HEADS, HEAD_DIM), ki)


def kernel(x_prompt, x_sample, state_s5_re, state_s5_im, cache_k, cache_v, cache_idx_k, norm_mix, w_in, s5_a_re, s5_a_im, s5_log_dt, s5_b_re, s5_b_im, s5_c_re, s5_c_im, s5_d, w_glu, q_norm, k_norm, idx_k_norm, w_branch_a, w_branch_b, w_out, norm_ffn, w_router_group, b_router_group, w_router_expert, b_router_expert, w_gate, w_up, w_down):
    p = _prep(norm_mix, w_in, s5_a_re, s5_a_im, s5_log_dt, s5_b_re, s5_b_im, s5_c_re, s5_c_im, s5_d,
              w_glu, q_norm, k_norm, idx_k_norm, w_branch_a, w_branch_b, w_out, norm_ffn,
              w_router_group, b_router_group, w_router_expert, b_router_expert, w_gate, w_up, w_down)
    h0 = jnp.zeros((x_prompt.shape[0], S5_GROUPS, S5_STATE), F32)
    tm_p, tm_s = 1024, 256
    xp, srp, sip, kp, vp, kip = _mixer(x_prompt, h0, h0, None, None, None, p,
                                       tm=tm_p, lc=64, tt=1024, tq=128, kb=512)
    xs, srs, sis, ks, vs, kis = _mixer(x_sample, state_s5_re, state_s5_im, cache_k, cache_v, cache_idx_k, p,
                                       tm=tm_s, lc=32, tt=32, tq=32, kb=384)
    yp, ys = _moe([xp, xs], p, [tm_p, tm_s], 256)
    return (yp.reshape(x_prompt.shape), ys.reshape(x_sample.shape), srp, sip, kp, vp, kip, srs, sis, ks, vs, kis)
```

```python
import functools
import math

import jax
import jax.numpy as jnp
from jax import lax
from jax.experimental import pallas as pl
from jax.experimental.pallas import tpu as pltpu

F32 = jnp.float32
BF16 = jnp.bfloat16
I32 = jnp.int32

D_MODEL = 2048
CHUNK = 64
EPS = 1e-6
S5_WIDTH = 1024
S5_GROUP = 16
S5_GROUPS = 64
S5_STATE = 64
S5_MAX_RE = -1e-4
N_HEADS = 8
N_KV_HEADS = 2
Q_PER_KV = 4
HEAD_DIM = 128
ATT_WIDTH = 1024
KV_WIDTH = 256
N_IDX_HEADS = 8
IDX_DIM = 64
TOP_K_MAX = 256
N_GROUPS = 4
EXPERTS_PER_GROUP = 8
N_EXPERTS = 32
D_EXPERT = 512
IN_SIZES = (S5_WIDTH, ATT_WIDTH, KV_WIDTH, KV_WIDTH, N_IDX_HEADS * IDX_DIM, IDX_DIM, N_IDX_HEADS, D_MODEL, D_MODEL)

LANES = 128
PROJ_TN = 512
COL_U, COL_Q, COL_GA, COL_GB, COL_QI, COL_K, COL_V, COL_KI, COL_WI = 0, 1024, 2048, 4096, 6144, 6656, 6912, 7168, 7232
PROJ_COLS = 7680
S5_LANE_BLOCKS = S5_WIDTH // LANES
S5_BLOCK_STATE = (LANES // S5_GROUP) * S5_STATE
S5_MAX_CHUNK = 64
VMEM_LIMIT = 56 * 1024 * 1024
INT_MIN = -2 ** 31
KEY_LOWEST_FINITE = -2 ** 31 + 0x00800000
NEG_BIG = -1e30
LOG2E = 1.4426950408889634
FAST_SOFTMAX_BOUND = 40.0
TOPK_PER_LANE = 12


def _dot(a, b):
    return jnp.dot(a, b, preferred_element_type=F32)


def _split_bf16(x):
    hi = x.astype(BF16)
    lo = (x - hi.astype(F32)).astype(BF16)
    return hi, lo


def _sigmoid(x):
    return 0.5 * jnp.tanh(0.5 * x) + 0.5


def _params(sem, **kw):
    return pltpu.CompilerParams(dimension_semantics=sem, vmem_limit_bytes=VMEM_LIMIT, **kw)


def _group_norm(a, gain):
    ms = jnp.mean(a * a, axis=-1, keepdims=True)
    return a * lax.rsqrt(ms + EPS) * gain


def _inproj_kernel(x_ref, g_ref, w_ref, gain_ref, o_ref, xn_ref):
    j = pl.program_id(1)

    @pl.when(j == 0)
    def _():
        xf = x_ref[...]
        ms = jnp.mean(xf * xf, axis=-1, keepdims=True)
        xn_ref[...] = (xf * lax.rsqrt(ms + EPS) * g_ref[...]).astype(BF16)

    acc = _dot(xn_ref[...], w_ref[...])
    gain = gain_ref[...]
    groups = [slice(c * LANES, (c + 1) * LANES) for c in range(PROJ_TN // LANES)]

    @pl.when((j < 2) | (j == 12))
    def _():
        o_ref[...] = acc

    @pl.when((j == 2) | (j == 3))
    def _():
        for s in groups:
            o_ref[:, s] = _group_norm(acc[:, s], gain[:, s])

    @pl.when((j >= 4) & (j < 12))
    def _():
        o_ref[...] = _sigmoid(acc)

    @pl.when(j == 13)
    def _():
        for s in groups[:2]:
            o_ref[:, s] = _group_norm(acc[:, s], gain[:, s])
        o_ref[:, 2 * LANES:] = acc[:, 2 * LANES:]

    @pl.when(j == 14)
    def _():
        a = acc[:, :LANES]
        lane = lax.broadcasted_iota(I32, a.shape, 1)
        is_ki = lane < IDX_DIM
        ms = jnp.sum(jnp.where(is_ki, a * a, 0.0), axis=-1, keepdims=True) * (1.0 / IDX_DIM)
        ki = a * lax.rsqrt(ms + EPS) * gain[:, :LANES]
        wi = a * (N_IDX_HEADS ** -0.5) * (IDX_DIM ** -0.5)
        o_ref[:, :LANES] = jnp.where(is_ki, ki, wi)
        o_ref[:, LANES:] = acc[:, LANES:]


def _inproj(x, g, w, gain, tm):
    n = x.shape[0]
    return pl.pallas_call(
        _inproj_kernel,
        grid=(n // tm, PROJ_COLS // PROJ_TN),
        in_specs=[pl.BlockSpec((tm, D_MODEL), lambda i, j: (i, 0)),
                  pl.BlockSpec((1, D_MODEL), lambda i, j: (0, 0)),
                  pl.BlockSpec((D_MODEL, PROJ_TN), lambda i, j: (0, j)),
                  pl.BlockSpec((1, PROJ_TN), lambda i, j: (0, j))],
        out_specs=pl.BlockSpec((tm, PROJ_TN), lambda i, j: (i, j)),
        out_shape=jax.ShapeDtypeStruct((n, PROJ_COLS), F32),
        scratch_shapes=[pltpu.VMEM((tm, D_MODEL), BF16)],
        compiler_params=_params(("arbitrary", "arbitrary")),
        name="inproj",
    )(x, g, w, gain)


def _gelu_tanh(y):
    return 0.5 * y * (1.0 + jnp.tanh(math.sqrt(2.0 / math.pi) * (y + 0.044715 * (y * y * y))))


def _s5_kernel(u_ref, h0r_ref, h0i_ref, bhi_ref, blo_ref, air_ref, aii_ref, apr_ref, api_ref,
               a1r_ref, a1i_ref, tri_ref, cd_ref, d_ref, yg_ref, sr_ref, si_ref,
               bu_scr, h_scr, hr_scr, hi_scr, *, lc, tt):
    t = pl.program_id(2)
    ns = S5_BLOCK_STATE

    @pl.when(t == 0)
    def _():
        hr_scr[...] = h0r_ref[...]
        hi_scr[...] = h0i_ref[...]

    u = u_ref[...]
    u_hi, u_lo = _split_bf16(u)
    bhi = bhi_ref[...]
    bu_scr[...] = _dot(u_hi, bhi) + _dot(jnp.concatenate([u_lo, u_hi], axis=1), blo_ref[...])
    tri = tri_ref[...]
    air, aii, apr, api = air_ref[...], aii_ref[...], apr_ref[...], api_ref[...]
    a1r, a1i = a1r_ref[...], a1i_ref[...]

    def chunk(s, carry):
        h_re, h_im = carry
        r0 = pl.multiple_of(s * lc, lc)
        br = bu_scr[pl.ds(r0, lc), 0:ns]
        bi = bu_scr[pl.ds(r0, lc), ns:2 * ns]
        z = jnp.concatenate([air * br - aii * bi, air * bi + aii * br], axis=1)
        z_hi, z_lo = _split_bf16(z)
        c = _dot(tri, jnp.concatenate([z_hi, z_lo], axis=0))
        cr = c[:, 0:ns] + (a1r * h_re - a1i * h_im)
        ci = c[:, ns:2 * ns] + (a1r * h_im + a1i * h_re)
        hr = apr * cr - api * ci
        hi = apr * ci + api * cr
        h_scr[pl.ds(r0, lc), 0:ns] = hr.astype(BF16)
        h_scr[pl.ds(r0, lc), ns:2 * ns] = hi.astype(BF16)
        return hr[lc - 1:lc, :], hi[lc - 1:lc, :]

    h_re, h_im = lax.fori_loop(0, tt // lc, chunk, (hr_scr[...], hi_scr[...]), unroll=min(4, tt // lc))
    hr_scr[...] = h_re
    hi_scr[...] = h_im
    sr_ref[...] = h_re
    si_ref[...] = h_im
    y = _dot(h_scr[...], cd_ref[...]) + d_ref[...] * u
    yg_ref[...] = _gelu_tanh(y).astype(BF16)


def _s5_tables(a_re, a_im, log_dt, b_re, b_im, c_re, c_im, lc):
    lr = jnp.minimum(a_re.astype(F32), S5_MAX_RE)
    li = a_im.astype(F32)
    dt = jnp.exp(log_dt.astype(F32))[:, None]
    mag = jnp.exp(lr * dt)
    lbr = mag * jnp.cos(li * dt)
    lbi = mag * jnp.sin(li * dt)
    den = lr * lr + li * li
    fr = ((lbr - 1.0) * lr + lbi * li) / den
    fi = (lbi * lr - (lbr - 1.0) * li) / den
    br = b_re.astype(F32)
    bi = b_im.astype(F32)
    bbr = fr[..., None] * br - fi[..., None] * bi
    bbi = fr[..., None] * bi + fi[..., None] * br
    j = jnp.arange(lc, dtype=F32)[:, None, None]
    ang = j * (li * dt)[None]
    lmag = j * (lr * dt)[None]
    n_state = S5_GROUPS * S5_STATE
    apr = (jnp.exp(lmag) * jnp.cos(ang)).reshape(lc, n_state)
    api = (jnp.exp(lmag) * jnp.sin(ang)).reshape(lc, n_state)
    air = (jnp.exp(-lmag) * jnp.cos(ang)).reshape(lc, n_state)
    aii = (-jnp.exp(-lmag) * jnp.sin(ang)).reshape(lc, n_state)
    a1r = lbr.reshape(1, n_state)
    a1i = lbi.reshape(1, n_state)
    gpb = LANES // S5_GROUP
    eye = jnp.eye(gpb, dtype=F32)

    def bdiag(b):
        return jnp.einsum('kgpc,gh->kgchp', b.reshape(S5_LANE_BLOCKS, gpb, S5_STATE, S5_GROUP), eye).reshape(
            S5_LANE_BLOCKS, LANES, S5_BLOCK_STATE)

    def cdiag(c):
        return jnp.einsum('kgcp,gh->kgphc', c.reshape(S5_LANE_BLOCKS, gpb, S5_GROUP, S5_STATE), eye).reshape(
            S5_LANE_BLOCKS, S5_BLOCK_STATE, LANES)

    bd = jnp.concatenate([bdiag(bbr), bdiag(bbi)], axis=-1)
    bd_hi = bd.astype(BF16)
    bd_lo = (bd - bd_hi.astype(F32)).astype(BF16)
    cd = jnp.concatenate([cdiag(c_re.astype(F32)), -cdiag(c_im.astype(F32))], axis=1).astype(BF16)
    tri = jnp.tril(jnp.ones((lc, lc), F32)).astype(BF16)
    bd_lo = jnp.concatenate([bd_hi, bd_lo], axis=1)
    return dict(bd_hi=bd_hi, bd_lo=bd_lo, air=air, aii=aii, apr=apr, api=api, a1r=a1r, a1i=a1i, tri=tri, cd=cd)


def _s5(proj3, h0_re, h0_im, tb, dvec, lc, tt):
    bsz, t, _ = proj3.shape
    ns = S5_BLOCK_STATE
    n_state = S5_GROUPS * S5_STATE
    tab = lambda: pl.BlockSpec((lc, ns), lambda b, k, i: (0, k))
    row = lambda: pl.BlockSpec((1, ns), lambda b, k, i: (0, k))
    st = lambda: pl.BlockSpec((None, 1, ns), lambda b, k, i: (b, 0, k))
    return pl.pallas_call(
        functools.partial(_s5_kernel, lc=lc, tt=tt),
        grid=(bsz, S5_LANE_BLOCKS, t // tt),
        in_specs=[pl.BlockSpec((None, tt, LANES), lambda b, k, i: (b, i, k)),
                  st(), st(),
                  pl.BlockSpec((None, LANES, 2 * ns), lambda b, k, i: (k, 0, 0)),
                  pl.BlockSpec((None, 2 * LANES, 2 * ns), lambda b, k, i: (k, 0, 0)),
                  tab(), tab(), tab(), tab(), row(), row(),
                  pl.BlockSpec((lc, 2 * lc), lambda b, k, i: (0, 0)),
                  pl.BlockSpec((None, 2 * ns, LANES), lambda b, k, i: (k, 0, 0)),
                  pl.BlockSpec((1, LANES), lambda b, k, i: (0, k))],
        out_specs=[pl.BlockSpec((None, tt, LANES), lambda b, k, i: (b, i, k)), st(), st()],
        out_shape=[jax.ShapeDtypeStruct((bsz, t, S5_WIDTH), BF16),
                   jax.ShapeDtypeStruct((bsz, 1, n_state), F32),
                   jax.ShapeDtypeStruct((bsz, 1, n_state), F32)],
        scratch_shapes=[pltpu.VMEM((tt, 2 * ns), F32), pltpu.VMEM((tt, 2 * ns), BF16),
                        pltpu.VMEM((1, ns), F32), pltpu.VMEM((1, ns), F32)],
        compiler_params=_params(("arbitrary", "arbitrary", "arbitrary")),
        name="s5",
    )(proj3, h0_re, h0_im, tb['bd_hi'], tb['bd_lo'], tb['air'], tb['aii'], tb['apr'], tb['api'],
      tb['a1r'], tb['a1i'], tb['tri'], tb['cd'], dvec)


def _glu_kernel(y_ref, w_ref, o_ref):
    acc = _dot(y_ref[...], w_ref[...])
    half = acc.shape[1] // 2
    o_ref[...] = (acc[:, :half] * _sigmoid(acc[:, half:])).astype(BF16)


def _glu(yg, w, tm):
    n = yg.shape[0]
    tn = 512
    return pl.pallas_call(
        _glu_kernel,
        grid=(n // tm, 2 * S5_WIDTH // tn),
        in_specs=[pl.BlockSpec((tm, S5_WIDTH), lambda i, j: (i, 0)),
                  pl.BlockSpec((S5_WIDTH, tn), lambda i, j: (0, j))],
        out_specs=pl.BlockSpec((tm, tn // 2), lambda i, j: (i, j)),
        out_shape=jax.ShapeDtypeStruct((n, S5_WIDTH), BF16),
        compiler_params=_params(("arbitrary", "arbitrary")),
        name="glu",
    )(yg, w)


def _dsa_kernel(q_ref, qi_ref, kiw_ref, kit_ref, kt_ref, v_ref, ut_ref, o_ref,
                key_scr, qs_scr, qis_scr, wis_scr, m_scr, l_scr, acc_scr,
                kmax_scr, lsum_scr, cand_scr, ckey_scr, thr_scr, cnt_scr,
                *, tq, kb, pos0, n_keys, n_top, nt):
    i = pl.program_id(1)
    nkp = key_scr.shape[2]
    n_cand = cand_scr.shape[1]
    slot_s = i % 2
    slot_a = 1 - slot_s
    tile_s = jnp.minimum(i, nt - 1)
    tile_a = jnp.maximum(i - 1, 0)

    def n_blocks(tile):
        k_end = jnp.minimum(n_keys, (pos0 + tile * tq + tq - 1) // CHUNK * CHUNK + CHUNK)
        return (k_end + kb - 1) // kb

    nkb_s = jnp.where(i < nt, n_blocks(tile_s), 0)
    nkb_a = jnp.where(i >= 1, n_blocks(tile_a), 0)

    @pl.when(i == 0)
    def _():
        def norm_block(b, carry):
            c0 = pl.multiple_of(b * kb, kb)
            out = []
            for g in range(N_KV_HEADS):
                kk = kt_ref[g * HEAD_DIM:(g + 1) * HEAD_DIM, pl.ds(c0, kb)].astype(F32)
                out.append(jnp.maximum(carry[g], jnp.sum(kk * kk, axis=0, keepdims=True)))
            return tuple(out)

        res = lax.fori_loop(0, nkp // kb, norm_block, (jnp.zeros((1, kb), F32),) * N_KV_HEADS)
        for g in range(N_KV_HEADS):
            kmax_scr[g] = jnp.broadcast_to(jnp.max(res[g], axis=-1, keepdims=True), (1, LANES))

    q = q_ref[...] * (HEAD_DIM ** -0.5 * LOG2E)
    for h in range(N_HEADS):
        g, r = divmod(h, Q_PER_KV)
        qs_scr[g, r * tq:(r + 1) * tq, :] = q[:, h * HEAD_DIM:(h + 1) * HEAD_DIM].astype(BF16)
    qi = qi_ref[...]
    kiw = kiw_ref[...]
    for h in range(N_IDX_HEADS):
        qis_scr[h * tq:(h + 1) * tq, :] = qi[:, h * IDX_DIM:(h + 1) * IDX_DIM].astype(BF16)
        wis_scr[h * tq:(h + 1) * tq, :] = kiw[:, IDX_DIM + h:IDX_DIM + h + 1]

    q_chunk = (pos0 + tile_s * tq + lax.broadcasted_iota(I32, (tq, 1), 0)) // CHUNK

    def to_key(x):
        bits = pltpu.bitcast(x, I32)
        return jnp.where(bits < 0, bits ^ 0x7FFFFFFF, bits)

    cand_scr[slot_s] = jnp.full(cand_scr.shape[1:], -jnp.inf, F32)

    def score_block(b, masked=True):
        c0 = pl.multiple_of(b * kb, kb)
        s = _dot(qis_scr[...], kit_ref[:, pl.ds(c0, kb)])
        s = jnp.maximum(s, 0.0) * wis_scr[...]
        sc = s[0:tq]
        for h in range(1, N_IDX_HEADS):
            sc = sc + s[h * tq:(h + 1) * tq]
        blk = sc + 0.0
        if masked:
            k_pos = c0 + lax.broadcasted_iota(I32, (1, kb), 1)
            adm = (k_pos // CHUNK <= q_chunk) & (k_pos < n_keys)
            blk = jnp.where(adm, blk, -jnp.inf)
        key_scr[slot_s, :, pl.ds(c0, kb)] = blk
        for rg in range(tq // 8):
            rows = slice(rg * 8, rg * 8 + 8)
            top = [cand_scr[slot_s, j, rows, :] for j in range(n_cand)]
            for c in range(kb // LANES):
                x = blk[rows, c * LANES:(c + 1) * LANES]
                for j in range(n_cand):
                    hi = jnp.maximum(top[j], x)
                    x = jnp.minimum(top[j], x)
                    top[j] = hi
            for j in range(n_cand):
                cand_scr[slot_s, j, rows, :] = top[j]

    def count_all(hit_fn):
        def body(b, acc):
            c0 = pl.multiple_of(b * kb, kb)
            hit = hit_fn(key_scr[slot_a, :, pl.ds(c0, kb)])
            for c in range(kb // LANES):
                acc = acc + hit[:, c * LANES:(c + 1) * LANES]
            return acc
        acc = lax.fori_loop(0, nkb_a, body, jnp.zeros((tq, LANES), F32))
        return jnp.sum(acc, axis=-1, keepdims=True)

    def count_cand(cand):
        acc = jnp.zeros((tq, LANES), F32)
        for j in range(n_cand):
            acc = acc + jnp.where(ckey_scr[j] >= cand, 1.0, 0.0)
        return jnp.sum(acc, axis=-1, keepdims=True)

    def bisect(count_fn, first_bit=0, prefix=None):
        def bit_pass(it, t_off):
            cand_off = t_off | lax.shift_left(jnp.int32(1), 31 - it)
            cnt = count_fn(cand_off ^ INT_MIN)
            return jnp.where(cnt >= n_top, cand_off, t_off)
        t_off = lax.fori_loop(first_bit, 32, bit_pass, jnp.zeros((tq, 1), I32) if prefix is None else prefix)
        key = jnp.maximum(t_off ^ INT_MIN, KEY_LOWEST_FINITE)
        return pltpu.bitcast(jnp.where(key < 0, key ^ 0x7FFFFFFF, key), F32)

    def candidate_bracket():
        hi = to_key(jnp.max(cand_scr[slot_a, 0], axis=-1, keepdims=True)) ^ INT_MIN
        lo = to_key(jnp.min(cand_scr[slot_a, 1], axis=-1, keepdims=True)) ^ INT_MIN
        n_common = jnp.min(lax.clz(hi ^ lo).astype(F32)).astype(I32)
        mask = jnp.where(n_common > 0, lax.shift_left(jnp.int32(-1), 32 - jnp.maximum(n_common, 1)), 0)
        return n_common, hi & mask

    @pl.when(i >= 1)
    def _():
        for j in range(n_cand):
            ckey_scr[j] = to_key(cand_scr[slot_a, j])
        thr_cand = bisect(count_cand, *candidate_bracket()) if n_top <= 2 * LANES else bisect(count_cand)
        thr_scr[...] = thr_cand
        n_all = count_all(lambda x: jnp.where(x >= thr_cand, 1.0, 0.0))
        n_listed = jnp.zeros((tq, LANES), F32)
        for j in range(n_cand):
            n_listed = n_listed + jnp.where(cand_scr[slot_a, j] >= thr_cand, 1.0, 0.0)
        cnt_scr[...] = n_all
        overflow = jnp.max(n_all - jnp.sum(n_listed, axis=-1, keepdims=True)) > 0.0

        @pl.when(overflow)
        def _():
            thr_all = bisect(lambda cand: count_all(lambda x: jnp.where(to_key(x) >= cand, 1.0, 0.0)))
            thr_scr[...] = thr_all
            cnt_scr[...] = count_all(lambda x: jnp.where(x >= thr_all, 1.0, 0.0))

        @pl.when(jnp.max(cnt_scr[...]) > n_top)
        def _():
            thr_t = thr_scr[...]
            need = n_top - count_all(lambda x: jnp.where(x > thr_t, 1.0, 0.0))

            def drop_surplus(b, seen):
                c0 = pl.multiple_of(b * kb, kb)
                x = key_scr[slot_a, :, pl.ds(c0, kb)]
                tied = x == thr_t
                rank = seen + _dot(jnp.where(tied, 1.0, 0.0).astype(BF16), ut_ref[...])
                key_scr[slot_a, :, pl.ds(c0, kb)] = jnp.where(tied & (rank > need), -jnp.inf, x)
                return rank[:, kb - 1:kb]

            lax.fori_loop(0, nkb_a, drop_surplus, jnp.zeros((tq, 1), F32))

    thr = thr_scr[...]

    acc_scr[...] = jnp.zeros(acc_scr.shape, F32)
    lsum_scr[...] = jnp.zeros(lsum_scr.shape, F32)
    bound_max = jnp.float32(0.0)
    for g in range(N_KV_HEADS):
        qg = qs_scr[g].astype(F32)
        qn2 = jnp.sum(qg * qg, axis=-1, keepdims=True)
        bound = jnp.sqrt(qn2 * kmax_scr[g][:, 0:1]) * 1.002
        bound_max = jnp.maximum(bound_max, jnp.max(bound))
    fast = bound_max <= FAST_SOFTMAX_BOUND

    @pl.when(jnp.logical_not(fast))
    def _():
        m_scr[...] = jnp.full(m_scr.shape, NEG_BIG, F32)
        l_scr[...] = jnp.zeros(l_scr.shape, F32)

    def attend_fast(b):
        c0 = pl.multiple_of(b * kb, kb)
        bias = jnp.where(key_scr[slot_a, :, pl.ds(c0, kb)] >= thr, 0.0, NEG_BIG)
        for g in range(N_KV_HEADS):
            lg = _dot(qs_scr[g], kt_ref[g * HEAD_DIM:(g + 1) * HEAD_DIM, pl.ds(c0, kb)])
            parts = []
            for r in range(Q_PER_KV):
                rows = slice(r * tq, (r + 1) * tq)
                e = [jnp.exp2(lg[rows, c * LANES:(c + 1) * LANES] + bias[:, c * LANES:(c + 1) * LANES])
                     for c in range(kb // LANES)]
                ls = e[0]
                for c in range(1, kb // LANES):
                    ls = ls + e[c]
                lsum_scr[g, rows, :] += ls
                parts.append(jnp.concatenate(e, axis=1).astype(BF16))
            p = jnp.concatenate(parts, axis=0)
            acc_scr[g] += _dot(p, v_ref[pl.ds(c0, kb), g * HEAD_DIM:(g + 1) * HEAD_DIM])

    def attend_general(b):
        c0 = pl.multiple_of(b * kb, kb)
        sel = key_scr[slot_a, :, pl.ds(c0, kb)] >= thr
        sel4 = jnp.concatenate([sel] * Q_PER_KV, axis=0)
        for g in range(N_KV_HEADS):
            lg = _dot(qs_scr[g], kt_ref[g * HEAD_DIM:(g + 1) * HEAD_DIM, pl.ds(c0, kb)])
            lg = jnp.where(sel4, lg, NEG_BIG)
            m_old = m_scr[g]
            m_new = jnp.maximum(m_old, jnp.max(lg, axis=-1, keepdims=True))
            p = jnp.exp2(lg - m_new)
            alpha = jnp.exp2(m_old - m_new)
            l_scr[g] = alpha * l_scr[g] + jnp.sum(p, axis=-1, keepdims=True)
            pv = _dot(p.astype(BF16), v_ref[pl.ds(c0, kb), g * HEAD_DIM:(g + 1) * HEAD_DIM])
            acc_scr[g] = alpha * acc_scr[g] + pv
            m_scr[g] = m_new

    def loop(lo, hi, *fns):
        def body(b, carry):
            for fn in fns:
                fn(b)
            return carry
        lax.fori_loop(lo, hi, body, 0)

    n_fused = jnp.where(fast, jnp.minimum(nkb_a, nkb_s), 0)
    first_row_end = (pos0 + tile_s * tq) // CHUNK * CHUNK + CHUNK
    n_plain = jnp.minimum(n_fused, jnp.minimum(first_row_end, n_keys) // kb)
    loop(0, n_plain, functools.partial(score_block, masked=False), attend_fast)
    loop(n_plain, n_fused, score_block, attend_fast)
    loop(n_fused, nkb_s, score_block)
    loop(n_fused, jnp.where(fast, nkb_a, 0), attend_fast)
    loop(0, jnp.where(fast, 0, nkb_a), attend_general)

    @pl.when((i >= 1) & fast)
    def _():
        for h in range(N_HEADS):
            g, r = divmod(h, Q_PER_KV)
            rows = slice(r * tq, (r + 1) * tq)
            l = jnp.sum(lsum_scr[g, rows, :], axis=-1, keepdims=True)
            o_ref[:, h * HEAD_DIM:(h + 1) * HEAD_DIM] = (acc_scr[g, rows, :] / l).astype(BF16)

    @pl.when((i >= 1) & jnp.logical_not(fast))
    def _():
        for h in range(N_HEADS):
            g, r = divmod(h, Q_PER_KV)
            rows = slice(r * tq, (r + 1) * tq)
            o_ref[:, h * HEAD_DIM:(h + 1) * HEAD_DIM] = (acc_scr[g, rows, :] / l_scr[g, rows, :]).astype(BF16)


def _dsa(proj3, kit, kt, v, *, tq, kb, pos0, n_keys, n_top):
    bsz, t, _ = proj3.shape
    nkp = kit.shape[-1]
    nt = t // tq
    kern = functools.partial(_dsa_kernel, tq=tq, kb=kb, pos0=pos0, n_keys=n_keys, n_top=n_top, nt=nt)
    resident = dict(pipeline_mode=pl.Buffered(1))
    prev = lambda i: jnp.maximum(i - 1, 0)
    cur = lambda i: jnp.minimum(i, nt - 1)
    return pl.pallas_call(
        kern,
        grid=(bsz, nt + 1),
        in_specs=[pl.BlockSpec((None, tq, ATT_WIDTH), lambda b, i: (b, prev(i), COL_Q // ATT_WIDTH)),
                  pl.BlockSpec((None, tq, 512), lambda b, i: (b, cur(i), COL_QI // 512)),
                  pl.BlockSpec((None, tq, 512), lambda b, i: (b, cur(i), COL_KI // 512)),
                  pl.BlockSpec((None, IDX_DIM, nkp), lambda b, i: (b, 0, 0), **resident),
                  pl.BlockSpec((None, KV_WIDTH, nkp), lambda b, i: (b, 0, 0), **resident),
                  pl.BlockSpec((None, nkp, KV_WIDTH), lambda b, i: (b, 0, 0), **resident),
                  pl.BlockSpec((kb, kb), lambda b, i: (0, 0), **resident)],
        out_specs=pl.BlockSpec((None, tq, ATT_WIDTH), lambda b, i: (b, prev(i), 0)),
        out_shape=jax.ShapeDtypeStruct((bsz, t, ATT_WIDTH), BF16),
        scratch_shapes=[pltpu.VMEM((2, tq, nkp), F32),
                        pltpu.VMEM((N_KV_HEADS, Q_PER_KV * tq, HEAD_DIM), BF16),
                        pltpu.VMEM((N_IDX_HEADS * tq, IDX_DIM), BF16),
                        pltpu.VMEM((N_IDX_HEADS * tq, 1), F32),
                        pltpu.VMEM((N_KV_HEADS, Q_PER_KV * tq, 1), F32),
                        pltpu.VMEM((N_KV_HEADS, Q_PER_KV * tq, 1), F32),
                        pltpu.VMEM((N_KV_HEADS, Q_PER_KV * tq, HEAD_DIM), F32),
                        pltpu.VMEM((N_KV_HEADS, 1, LANES), F32),
                        pltpu.VMEM((N_KV_HEADS, Q_PER_KV * tq, LANES), F32),
                        pltpu.VMEM((2, TOPK_PER_LANE, tq, LANES), F32),
                        pltpu.VMEM((TOPK_PER_LANE, tq, LANES), I32),
                        pltpu.VMEM((tq, 1), F32),
                        pltpu.VMEM((tq, 1), F32)],
        compiler_params=_params(("arbitrary", "arbitrary")),
        name="dsa",
    )(proj3, proj3, proj3, kit, kt, v, jnp.triu(jnp.ones((kb, kb), F32)).astype(BF16))


def _merge_kernel(ya_ref, yb_ref, wa_ref, wb_ref, ga_ref, gb_ref, o_ref):
    o_ref[...] = (ga_ref[...] * _dot(ya_ref[...], wa_ref[...])
                  + gb_ref[...] * _dot(yb_ref[...], wb_ref[...])).astype(BF16)


def _merge(ya, yb, wa, wb, proj, tm):
    n = ya.shape[0]
    tn = 512
    return pl.pallas_call(
        _merge_kernel,
        grid=(n // tm, D_MODEL // tn),
        in_specs=[pl.BlockSpec((tm, S5_WIDTH), lambda i, j: (i, 0)),
                  pl.BlockSpec((tm, ATT_WIDTH), lambda i, j: (i, 0)),
                  pl.BlockSpec((S5_WIDTH, tn), lambda i, j: (0, j)),
                  pl.BlockSpec((ATT_WIDTH, tn), lambda i, j: (0, j)),
                  pl.BlockSpec((tm, tn), lambda i, j: (i, COL_GA // tn + j)),
                  pl.BlockSpec((tm, tn), lambda i, j: (i, COL_GB // tn + j))],
        out_specs=pl.BlockSpec((tm, tn), lambda i, j: (i, j)),
        out_shape=jax.ShapeDtypeStruct((n, D_MODEL), BF16),
        compiler_params=_params(("arbitrary", "arbitrary")),
        name="merge",
    )(ya, yb, wa, wb, proj, proj)


def _outproj_kernel(m_ref, w_ref, x_ref, o_ref):
    o_ref[...] = x_ref[...] + _dot(m_ref[...], w_ref[...])


def _outproj(merged, w, x, tm):
    n = x.shape[0]
    tn = 512
    return pl.pallas_call(
        _outproj_kernel,
        grid=(n // tm, D_MODEL // tn),
        in_specs=[pl.BlockSpec((tm, D_MODEL), lambda i, j: (i, 0)),
                  pl.BlockSpec((D_MODEL, tn), lambda i, j: (0, j)),
                  pl.BlockSpec((tm, tn), lambda i, j: (i, j))],
        out_specs=pl.BlockSpec((tm, tn), lambda i, j: (i, j)),
        out_shape=jax.ShapeDtypeStruct((n, D_MODEL), F32),
        compiler_params=_params(("arbitrary", "arbitrary")),
        name="outproj",
    )(merged, w, x)


ROUTER_GROUP_LANE = N_EXPERTS


def _router_kernel(x_ref, g_ref, whi_ref, wlo_ref, b_ref, hn_ref, ids_ref, gates_ref):
    xf = x_ref[...]
    ms = jnp.mean(xf * xf, axis=-1, keepdims=True)
    hn = xf * lax.rsqrt(ms + EPS) * g_ref[...]
    hn_ref[...] = hn
    h_hi, h_lo = _split_bf16(hn)
    whi = whi_ref[...]
    lg = _dot(h_hi, whi) + _dot(h_lo, whi) + _dot(h_hi, wlo_ref[...]) + b_ref[...]
    lane = lax.broadcasted_iota(I32, lg.shape, 1)
    lane_f = lane.astype(F32)
    big = float(LANES)
    is_g = (lane >= ROUTER_GROUP_LANE) & (lane < ROUTER_GROUP_LANE + N_GROUPS)
    g_max = jnp.max(jnp.where(is_g, lg, -jnp.inf), axis=-1, keepdims=True)
    g_den = jnp.sum(jnp.where(is_g, jnp.exp(lg - g_max), 0.0), axis=-1, keepdims=True)
    g_w = 1.0 / g_den
    g_lane = jnp.min(jnp.where(is_g & (lg == g_max), lane_f, big), axis=-1, keepdims=True)
    g_sel = g_lane.astype(I32) - ROUTER_GROUP_LANE
    is_e = (lane < N_EXPERTS) & (lane // EXPERTS_PER_GROUP == g_sel)
    e_max = jnp.max(jnp.where(is_e, lg, -jnp.inf), axis=-1, keepdims=True)
    pe = jnp.where(is_e, jnp.exp(lg - e_max), 0.0)
    pe = pe / jnp.sum(pe, axis=-1, keepdims=True)
    pe = jnp.where(is_e, pe, -1.0)
    p1 = jnp.max(pe, axis=-1, keepdims=True)
    i1 = jnp.min(jnp.where(pe == p1, lane_f, big), axis=-1, keepdims=True)
    pe2 = jnp.where(lane_f == i1, -1.0, pe)
    p2 = jnp.max(pe2, axis=-1, keepdims=True)
    i2 = jnp.min(jnp.where(pe2 == p2, lane_f, big), axis=-1, keepdims=True)
    tot = p1 + p2
    ids_ref[...] = jnp.where(lane == 0, i1, jnp.where(lane == 1, i2, 0.0)).astype(I32)
    gates_ref[...] = jnp.where(lane == 0, g_w * (p1 / tot), jnp.where(lane == 1, g_w * (p2 / tot), 0.0))


def _router(x1, g, whi, wlo, bias, tm):
    n = x1.shape[0]
    return pl.pallas_call(
        _router_kernel,
        grid=(n // tm,),
        in_specs=[pl.BlockSpec((tm, D_MODEL), lambda i: (i, 0)),
                  pl.BlockSpec((1, D_MODEL), lambda i: (0, 0)),
                  pl.BlockSpec((D_MODEL, LANES), lambda i: (0, 0)),
                  pl.BlockSpec((D_MODEL, LANES), lambda i: (0, 0)),
                  pl.BlockSpec((1, LANES), lambda i: (0, 0))],
        out_specs=[pl.BlockSpec((tm, D_MODEL), lambda i: (i, 0)),
                   pl.BlockSpec((tm, LANES), lambda i: (i, 0)),
                   pl.BlockSpec((tm, LANES), lambda i: (i, 0))],
        out_shape=[jax.ShapeDtypeStruct((n, D_MODEL), F32),
                   jax.ShapeDtypeStruct((n, LANES), I32),
                   jax.ShapeDtypeStruct((n, LANES), F32)],
        compiler_params=_params(("arbitrary",)),
        name="router",
    )(x1, g, whi, wlo, bias)


def _row_copy(src, dst, s_row, d_row, sem):
    return pltpu.make_async_copy(src.at[pl.ds(s_row, 1)], dst.at[pl.ds(d_row, 1)], sem)


def _dispatch_kernel(dest_ref, hn_ref, xs_in_ref, xs_ref, sem, *, ch):
    del xs_in_ref

    def issue(t, carry):
        for s in range(2):
            _row_copy(hn_ref, xs_ref, t, dest_ref[0, 0, 2 * t + s], sem).start()
        return carry

    lax.fori_loop(0, ch // 2, issue, 0, unroll=8)
    for s in range(2):
        pltpu.make_async_copy(hn_ref, xs_ref.at[pl.ds(0, ch // 2)], sem).wait()


def _dispatch(dest, hn, xs0, ch):
    m = dest.shape[0]
    return pl.pallas_call(
        functools.partial(_dispatch_kernel, ch=ch),
        grid=(m // ch,),
        in_specs=[pl.BlockSpec((1, 1, ch), lambda i: (i, 0, 0), memory_space=pltpu.SMEM),
                  pl.BlockSpec((ch // 2, D_MODEL), lambda i: (i, 0)),
                  pl.BlockSpec(memory_space=pl.ANY)],
        out_specs=pl.BlockSpec(memory_space=pl.ANY),
        out_shape=jax.ShapeDtypeStruct(xs0.shape, xs0.dtype),
        scratch_shapes=[pltpu.SemaphoreType.DMA(())],
        input_output_aliases={2: 0},
        compiler_params=_params(("arbitrary",), disable_bounds_checks=True, has_side_effects=True),
        name="dispatch",
    )(dest.reshape(m // ch, 1, ch), hn, xs0)


def _expert_kernel(be_ref, nu_ref, x_ref, wg_hbm, wu_hbm, wd_hbm, o_ref,
                   wg_f, wu_f, wd_f, wg_s, wu_s, wd_s, ord_ref, sems):
    b = pl.program_id(0)
    n_used = nu_ref[0]
    e_here = be_ref[b]

    def weight_copies(e, slot):
        return [pltpu.make_async_copy(src.at[e], dst.at[slot], sems.at[slot, k])
                for k, (src, dst) in enumerate(((wg_hbm, wg_f), (wu_hbm, wu_f), (wd_hbm, wd_f)))]

    @pl.when(b < n_used)
    def _():
        @pl.when((b == 0) | (e_here != be_ref[jnp.maximum(b - 1, 0)]))
        def _():
            ordinal = jnp.where(b == 0, 0, ord_ref[0] + 1)
            ord_ref[0] = ordinal
            slot = ordinal % 2
            nxt = lax.while_loop(lambda j: (j < n_used) & (be_ref[jnp.minimum(j, be_ref.shape[0] - 1)] == e_here),
                                 lambda j: j + 1, b + 1)

            @pl.when(b == 0)
            def _():
                for c in weight_copies(e_here, slot):
                    c.start()

            for c in weight_copies(e_here, slot):
                c.wait()
            for s in range(2):
                @pl.when(slot == s)
                def _(s=s):
                    wg_s[...] = wg_f[s].astype(BF16)
                    wu_s[...] = wu_f[s].astype(BF16)
                    wd_s[...] = wd_f[s].astype(BF16)

            @pl.when(nxt < n_used)
            def _():
                for c in weight_copies(be_ref[jnp.minimum(nxt, be_ref.shape[0] - 1)], 1 - slot):
                    c.start()

        x = x_ref[...].astype(BF16)
        gate = _dot(x, wg_s[...])
        h = gate * _sigmoid(gate) * _dot(x, wu_s[...])
        o_ref[...] = _dot(h.astype(BF16), wd_s[...])

    @pl.when(b >= nu_ref[0])
    def _():
        o_ref[...] = jnp.zeros(o_ref.shape, F32)


def _experts(block_e, n_used, xs, wg, wu, wd, bm):
    rows = xs.shape[0]
    nb = rows // bm
    any_space = pl.BlockSpec(memory_space=pl.ANY)
    grid_spec = pltpu.PrefetchScalarGridSpec(
        num_scalar_prefetch=2,
        grid=(nb,),
        in_specs=[pl.BlockSpec((bm, D_MODEL), lambda b, *_: (b, 0)), any_space, any_space, any_space],
        out_specs=pl.BlockSpec((bm, D_MODEL), lambda b, *_: (b, 0)),
        scratch_shapes=[pltpu.VMEM((2, D_MODEL, D_EXPERT), F32), pltpu.VMEM((2, D_MODEL, D_EXPERT), F32),
                        pltpu.VMEM((2, D_EXPERT, D_MODEL), F32),
                        pltpu.VMEM((D_MODEL, D_EXPERT), BF16), pltpu.VMEM((D_MODEL, D_EXPERT), BF16),
                        pltpu.VMEM((D_EXPERT, D_MODEL), BF16),
                        pltpu.SMEM((1,), I32),
                        pltpu.SemaphoreType.DMA((2, 3))])
    return pl.pallas_call(
        _expert_kernel,
        grid_spec=grid_spec,
        out_shape=jax.ShapeDtypeStruct((rows, D_MODEL), F32),
        compiler_params=_params(("arbitrary",)),
        name="experts",
    )(block_e, n_used, xs, wg, wu, wd)


def _combine_kernel(dest_ref, gates_ref, x_ref, ys_ref, o_ref, buf, sem, *, tc):
    def issue(t, carry):
        for s in range(2):
            _row_copy(ys_ref, buf.at[s], dest_ref[0, 0, 2 * t + s], t, sem).start()
        return carry

    lax.fori_loop(0, tc, issue, 0, unroll=8)
    for s in range(2):
        pltpu.make_async_copy(ys_ref.at[pl.ds(0, tc)], buf.at[s], sem).wait()
    gates = gates_ref[...]
    o_ref[...] = x_ref[...] + gates[:, 0:1] * buf[0] + gates[:, 1:2] * buf[1]


def _combine(dest, gates, x1, ys, tc):
    n = x1.shape[0]
    return pl.pallas_call(
        functools.partial(_combine_kernel, tc=tc),
        grid=(n // tc,),
        in_specs=[pl.BlockSpec((1, 1, 2 * tc), lambda i: (i, 0, 0), memory_space=pltpu.SMEM),
                  pl.BlockSpec((tc, LANES), lambda i: (i, 0)),
                  pl.BlockSpec((tc, D_MODEL), lambda i: (i, 0)),
                  pl.BlockSpec(memory_space=pl.ANY)],
        out_specs=pl.BlockSpec((tc, D_MODEL), lambda i: (i, 0)),
        out_shape=jax.ShapeDtypeStruct((n, D_MODEL), F32),
        scratch_shapes=[pltpu.VMEM((2, tc, D_MODEL), F32), pltpu.SemaphoreType.DMA(())],
        compiler_params=_params(("arbitrary",), disable_bounds_checks=True),
        name="combine",
    )(dest.reshape(n // tc, 1, 2 * tc), gates, x1, ys)


def _moe(x1s, p, tms, bm):
    routed = [_router(x1, p['norm_ffn'], p['wr_hi'], p['wr_lo'], p['b_router'], tm) for x1, tm in zip(x1s, tms)]
    flat_e = jnp.concatenate([ids[:, :2].reshape(-1) for _, ids, _ in routed])
    m = flat_e.shape[0]
    onehot = (flat_e[:, None] == jnp.arange(N_EXPERTS, dtype=I32)[None, :]).astype(I32)
    csum = jnp.cumsum(onehot, axis=0)
    counts = csum[-1]
    padded = (counts + bm - 1) // bm * bm
    pad_end = jnp.cumsum(padded)
    pad_start = pad_end - padded
    dest = jnp.sum(onehot * (csum - 1 + pad_start[None, :]), axis=1).astype(I32)
    nb = -(-(m + N_EXPERTS * (bm - 1)) // bm)
    block_start = jnp.arange(nb, dtype=I32) * bm
    block_e = jnp.minimum(jnp.sum((pad_end[None, :] <= block_start[:, None]).astype(I32), axis=1),
                          N_EXPERTS - 1).astype(I32)
    n_used = (pad_end[-1] // bm).astype(I32).reshape(1)
    xs = jnp.zeros((nb * bm, D_MODEL), F32)
    dests = []
    first = 0
    for hn, _, _ in routed:
        dests.append(dest[first:first + 2 * hn.shape[0]])
        first += 2 * hn.shape[0]
        xs = _dispatch(dests[-1], hn, xs, min(2 * hn.shape[0], 2048))
    ys = _experts(block_e, n_used, xs, p['w_gate'], p['w_up'], p['w_down'], bm)
    return [_combine(d, gates, x1, ys, min(x1.shape[0], 512)) for d, (_, _, gates), x1 in zip(dests, routed, x1s)]


def _prep(norm_mix, w_in, s5_a_re, s5_a_im, s5_log_dt, s5_b_re, s5_b_im, s5_c_re, s5_c_im, s5_d,
          w_glu, q_norm, k_norm, idx_k_norm, w_branch_a, w_branch_b, w_out, norm_ffn,
          w_router_group, b_router_group, w_router_expert, b_router_expert, w_gate, w_up, w_down):
    pts = []
    acc = 0
    for s in IN_SIZES[:-1]:
        acc += s
        pts.append(acc)
    w_u, w_q, w_k, w_v, w_qi, w_ki, w_wi, w_ga, w_gb = jnp.split(w_in, pts, axis=1)
    pad = PROJ_COLS - sum(IN_SIZES)
    w_proj = jnp.concatenate([w_u, w_q, w_ga, w_gb, w_qi, w_k, w_v, w_ki, w_wi,
                              jnp.zeros((D_MODEL, pad), w_in.dtype)], axis=1).astype(BF16)
    one = lambda k: jnp.ones((k,), F32)
    gain = jnp.concatenate([one(S5_WIDTH), jnp.tile(q_norm.astype(F32), N_HEADS), one(2 * D_MODEL),
                            one(N_IDX_HEADS * IDX_DIM), jnp.tile(k_norm.astype(F32), N_KV_HEADS), one(KV_WIDTH),
                            idx_k_norm.astype(F32), one(N_IDX_HEADS + pad)]).reshape(1, PROJ_COLS)
    half = 256
    wv, wg = w_glu[:, :S5_WIDTH], w_glu[:, S5_WIDTH:]
    w_glu_p = jnp.concatenate(
        [jnp.concatenate([wv[:, c * half:(c + 1) * half], wg[:, c * half:(c + 1) * half]], axis=1)
         for c in range(S5_WIDTH // half)], axis=1).astype(BF16)
    w_r = jnp.concatenate([w_router_expert.astype(F32), w_router_group.astype(F32),
                           jnp.zeros((D_MODEL, LANES - N_EXPERTS - N_GROUPS), F32)], axis=1)
    wr_hi = w_r.astype(BF16)
    wr_lo = (w_r - wr_hi.astype(F32)).astype(BF16)
    b_r = jnp.concatenate([b_router_expert.astype(F32), b_router_group.astype(F32),
                           jnp.zeros((LANES - N_EXPERTS - N_GROUPS,), F32)]).reshape(1, LANES)
    return dict(norm_mix=norm_mix.astype(F32).reshape(1, D_MODEL), w_proj=w_proj, gain=gain,
                s5=_s5_tables(s5_a_re, s5_a_im, s5_log_dt, s5_b_re, s5_b_im, s5_c_re, s5_c_im, S5_MAX_CHUNK),
                s5_d=s5_d.astype(F32).reshape(1, S5_WIDTH), w_glu=w_glu_p,
                w_a=w_branch_a.astype(BF16), w_b=w_branch_b.astype(BF16), w_out=w_out.astype(BF16),
                norm_ffn=norm_ffn.astype(F32).reshape(1, D_MODEL), wr_hi=wr_hi, wr_lo=wr_lo, b_router=b_r,
                w_gate=w_gate, w_up=w_up, w_down=w_down)


MOE_BLOCK_ROWS = 256


def _tile_plan(bsz, t, n_keys):
    n = bsz * t
    long_seq = t % 128 == 0
    kb = 512 if long_seq else 384
    assert n % 256 == 0 and (long_seq or t % 16 == 0) and n_keys > 0
    return dict(tm=1024 if n % 1024 == 0 else 256,
                lc=S5_MAX_CHUNK if t % S5_MAX_CHUNK == 0 else t,
                tt=2048 if t % 2048 == 0 else t,
                tq=128 if long_seq else t,
                kb=kb)


def _mixer(x, h0_re, h0_im, k_past, v_past, ki_past, p, *, tm, lc, tt, tq, kb):
    bsz, t, _ = x.shape
    n = bsz * t
    x2 = x.reshape(n, D_MODEL)
    proj = _inproj(x2, p['norm_mix'], p['w_proj'], p['gain'], tm)
    proj3 = proj.reshape(bsz, t, PROJ_COLS)
    k = proj3[:, :, COL_K:COL_K + KV_WIDTH]
    v = proj3[:, :, COL_V:COL_V + KV_WIDTH]
    ki = proj3[:, :, COL_KI:COL_KI + IDX_DIM]
    n_state = S5_GROUPS * S5_STATE
    tables = dict(p['s5'])
    for name in ('air', 'aii', 'apr', 'api'):
        tables[name] = tables[name][:lc]
    tables['tri'] = jnp.tile(tables['tri'][:lc, :lc], (1, 2))
    yg, s_re, s_im = _s5(proj3, h0_re.reshape(bsz, 1, n_state), h0_im.reshape(bsz, 1, n_state),
                         tables, p['s5_d'], lc, tt)
    tm_wide = 2 * tm if n % (2 * tm) == 0 else tm
    y_a = _glu(yg.reshape(n, S5_WIDTH), p['w_glu'], tm_wide)
    if k_past is None:
        pos0 = 0
        k_all, v_all, ki_all = k, v, ki
    else:
        pos0 = k_past.shape[1]
        k_all = jnp.concatenate([k_past.reshape(bsz, pos0, KV_WIDTH), k], axis=1)
        v_all = jnp.concatenate([v_past.reshape(bsz, pos0, KV_WIDTH), v], axis=1)
        ki_all = jnp.concatenate([ki_past, ki], axis=1)
    n_keys = k_all.shape[1]
    n_top = min(TOP_K_MAX, n_keys // 4)
    nkp = -(-n_keys // kb) * kb
    padk = lambda a: jnp.pad(a.astype(BF16), ((0, 0), (0, nkp - n_keys), (0, 0)))
    kit = jnp.swapaxes(padk(ki_all), 1, 2)
    kt = jnp.swapaxes(padk(k_all), 1, 2)
    y_b = _dsa(proj3, kit, kt, padk(v_all), tq=tq, kb=kb, pos0=pos0, n_keys=n_keys, n_top=n_top)
    merged = _merge(y_a, y_b.reshape(n, ATT_WIDTH), p['w_a'], p['w_b'], proj, tm_wide)
    x1 = _outproj(merged, p['w_out'], x2, tm_wide)
    return (x1, s_re.reshape(bsz, S5_GROUPS, S5_STATE),
            s_im.reshape(bsz, S5_GROUPS, S5_STATE), k.reshape(bsz, t, N_KV_HEADS, HEAD_DIM),
            v.reshape(bsz, t, N_KV_HEADS, HEAD_DIM), ki)


def kernel(x_prompt, x_sample, state_s5_re, state_s5_im, cache_k, cache_v, cache_idx_k, norm_mix, w_in, s5_a_re, s5_a_im, s5_log_dt, s5_b_re, s5_b_im, s5_c_re, s5_c_im, s5_d, w_glu, q_norm, k_norm, idx_k_norm, w_branch_a, w_branch_b, w_out, norm_ffn, w_router_group, b_router_group, w_router_expert, b_router_expert, w_gate, w_up, w_down):
    p = _prep(norm_mix, w_in, s5_a_re, s5_a_im, s5_log_dt, s5_b_re, s5_b_im, s5_c_re, s5_c_im, s5_d,
              w_glu, q_norm, k_norm, idx_k_norm, w_branch_a, w_branch_b, w_out, norm_ffn,
              w_router_group, b_router_group, w_router_expert, b_router_expert, w_gate, w_up, w_down)
    h0 = jnp.zeros((x_prompt.shape[0], S5_GROUPS, S5_STATE), F32)
    plan_p = _tile_plan(x_prompt.shape[0], x_prompt.shape[1], x_prompt.shape[1])
    plan_s = _tile_plan(x_sample.shape[0], x_sample.shape[1], cache_k.shape[1] + x_sample.shape[1])
    xp, srp, sip, kp, vp, kip = _mixer(x_prompt, h0, h0, None, None, None, p, **plan_p)
    xs, srs, sis, ks, vs, kis = _mixer(x_sample, state_s5_re, state_s5_im, cache_k, cache_v, cache_idx_k, p, **plan_s)
    yp, ys = _moe([xp, xs], p, [plan_p['tm'], plan_s['tm']], MOE_BLOCK_ROWS)
    return (yp.reshape(x_prompt.shape), ys.reshape(x_sample.shape), srp, sip, kp, vp, kip, srs, sis, ks, vs, kis)
```

```python
import functools
import math

import jax
import jax.numpy as jnp
from jax import lax
from jax.experimental import pallas as pl
from jax.experimental.pallas import tpu as pltpu

F32 = jnp.float32
BF16 = jnp.bfloat16
I32 = jnp.int32

D_MODEL = 2048
CHUNK = 64
EPS = 1e-6
S5_WIDTH = 1024
S5_GROUP = 16
S5_GROUPS = 64
S5_STATE = 64
S5_MAX_RE = -1e-4
N_HEADS = 8
N_KV_HEADS = 2
Q_PER_KV = 4
HEAD_DIM = 128
ATT_WIDTH = 1024
KV_WIDTH = 256
N_IDX_HEADS = 8
IDX_DIM = 64
TOP_K_MAX = 256
N_GROUPS = 4
EXPERTS_PER_GROUP = 8
N_EXPERTS = 32
D_EXPERT = 512
IN_SIZES = (S5_WIDTH, ATT_WIDTH, KV_WIDTH, KV_WIDTH, N_IDX_HEADS * IDX_DIM, IDX_DIM, N_IDX_HEADS, D_MODEL, D_MODEL)

LANES = 128
PROJ_TN = 512
COL_U, COL_Q, COL_GA, COL_GB, COL_QI, COL_K, COL_V, COL_KI, COL_WI = 0, 1024, 2048, 4096, 6144, 6656, 6912, 7168, 7232
PROJ_COLS = 7680
S5_LANE_BLOCKS = S5_WIDTH // LANES
S5_BLOCK_STATE = (LANES // S5_GROUP) * S5_STATE
S5_MAX_CHUNK = 64
VMEM_LIMIT = 56 * 1024 * 1024
INT_MIN = -2 ** 31
KEY_LOWEST_FINITE = -2 ** 31 + 0x00800000
NEG_BIG = -1e30
LOG2E = 1.4426950408889634
FAST_SOFTMAX_BOUND = 40.0
TOPK_PER_LANE = 12


def _dot(a, b):
    return jnp.dot(a, b, preferred_element_type=F32)


def _split_bf16(x):
    hi = x.astype(BF16)
    lo = (x - hi.astype(F32)).astype(BF16)
    return hi, lo


def _sigmoid(x):
    return 0.5 * jnp.tanh(0.5 * x) + 0.5


def _params(sem, **kw):
    return pltpu.CompilerParams(dimension_semantics=sem, vmem_limit_bytes=VMEM_LIMIT, **kw)


def _group_norm(a, gain):
    ms = jnp.mean(a * a, axis=-1, keepdims=True)
    return a * lax.rsqrt(ms + EPS) * gain


def _inproj_kernel(x_ref, g_ref, w_ref, gain_ref, o_ref, xn_ref):
    j = pl.program_id(1)

    @pl.when(j == 0)
    def _():
        xf = x_ref[...]
        ms = jnp.mean(xf * xf, axis=-1, keepdims=True)
        xn_ref[...] = (xf * lax.rsqrt(ms + EPS) * g_ref[...]).astype(BF16)

    acc = _dot(xn_ref[...], w_ref[...])
    gain = gain_ref[...]
    groups = [slice(c * LANES, (c + 1) * LANES) for c in range(PROJ_TN // LANES)]

    @pl.when((j < 2) | (j == 12))
    def _():
        o_ref[...] = acc

    @pl.when((j == 2) | (j == 3))
    def _():
        for s in groups:
            o_ref[:, s] = _group_norm(acc[:, s], gain[:, s])

    @pl.when((j >= 4) & (j < 12))
    def _():
        o_ref[...] = _sigmoid(acc)

    @pl.when(j == 13)
    def _():
        for s in groups[:2]:
            o_ref[:, s] = _group_norm(acc[:, s], gain[:, s])
        o_ref[:, 2 * LANES:] = acc[:, 2 * LANES:]

    @pl.when(j == 14)
    def _():
        a = acc[:, :LANES]
        lane = lax.broadcasted_iota(I32, a.shape, 1)
        is_ki = lane < IDX_DIM
        ms = jnp.sum(jnp.where(is_ki, a * a, 0.0), axis=-1, keepdims=True) * (1.0 / IDX_DIM)
        ki = a * lax.rsqrt(ms + EPS) * gain[:, :LANES]
        wi = a * (N_IDX_HEADS ** -0.5) * (IDX_DIM ** -0.5)
        o_ref[:, :LANES] = jnp.where(is_ki, ki, wi)
        o_ref[:, LANES:] = acc[:, LANES:]


def _inproj(x, g, w, gain, tm):
    n = x.shape[0]
    return pl.pallas_call(
        _inproj_kernel,
        grid=(n // tm, PROJ_COLS // PROJ_TN),
        in_specs=[pl.BlockSpec((tm, D_MODEL), lambda i, j: (i, 0)),
                  pl.BlockSpec((1, D_MODEL), lambda i, j: (0, 0)),
                  pl.BlockSpec((D_MODEL, PROJ_TN), lambda i, j: (0, j)),
                  pl.BlockSpec((1, PROJ_TN), lambda i, j: (0, j))],
        out_specs=pl.BlockSpec((tm, PROJ_TN), lambda i, j: (i, j)),
        out_shape=jax.ShapeDtypeStruct((n, PROJ_COLS), F32),
        scratch_shapes=[pltpu.VMEM((tm, D_MODEL), BF16)],
        compiler_params=_params(("arbitrary", "arbitrary")),
        name="inproj",
    )(x, g, w, gain)


def _gelu_tanh(y):
    return 0.5 * y * (1.0 + jnp.tanh(math.sqrt(2.0 / math.pi) * (y + 0.044715 * (y * y * y))))


def _s5_kernel(u_ref, h0r_ref, h0i_ref, bhi_ref, blo_ref, air_ref, aii_ref, apr_ref, api_ref,
               a1r_ref, a1i_ref, tri_ref, cd_ref, d_ref, yg_ref, sr_ref, si_ref,
               bu_scr, h_scr, hr_scr, hi_scr, *, lc, tt):
    t = pl.program_id(2)
    ns = S5_BLOCK_STATE

    @pl.when(t == 0)
    def _():
        hr_scr[...] = h0r_ref[...]
        hi_scr[...] = h0i_ref[...]

    u = u_ref[...]
    u_hi, u_lo = _split_bf16(u)
    bhi = bhi_ref[...]
    bu_scr[...] = _dot(u_hi, bhi) + _dot(jnp.concatenate([u_lo, u_hi], axis=1), blo_ref[...])
    tri = tri_ref[...]
    air, aii, apr, api = air_ref[...], aii_ref[...], apr_ref[...], api_ref[...]
    a1r, a1i = a1r_ref[...], a1i_ref[...]

    def chunk(s, carry):
        h_re, h_im = carry
        r0 = pl.multiple_of(s * lc, lc)
        br = bu_scr[pl.ds(r0, lc), 0:ns]
        bi = bu_scr[pl.ds(r0, lc), ns:2 * ns]
        z = jnp.concatenate([air * br - aii * bi, air * bi + aii * br], axis=1)
        z_hi, z_lo = _split_bf16(z)
        c = _dot(tri, jnp.concatenate([z_hi, z_lo], axis=0))
        cr = c[:, 0:ns] + (a1r * h_re - a1i * h_im)
        ci = c[:, ns:2 * ns] + (a1r * h_im + a1i * h_re)
        hr = apr * cr - api * ci
        hi = apr * ci + api * cr
        h_scr[pl.ds(r0, lc), 0:ns] = hr.astype(BF16)
        h_scr[pl.ds(r0, lc), ns:2 * ns] = hi.astype(BF16)
        return hr[lc - 1:lc, :], hi[lc - 1:lc, :]

    h_re, h_im = lax.fori_loop(0, tt // lc, chunk, (hr_scr[...], hi_scr[...]), unroll=min(4, tt // lc))
    hr_scr[...] = h_re
    hi_scr[...] = h_im
    sr_ref[...] = h_re
    si_ref[...] = h_im
    y = _dot(h_scr[...], cd_ref[...]) + d_ref[...] * u
    yg_ref[...] = _gelu_tanh(y).astype(BF16)


def _s5_tables(a_re, a_im, log_dt, b_re, b_im, c_re, c_im, lc):
    lr = jnp.minimum(a_re.astype(F32), S5_MAX_RE)
    li = a_im.astype(F32)
    dt = jnp.exp(log_dt.astype(F32))[:, None]
    mag = jnp.exp(lr * dt)
    lbr = mag * jnp.cos(li * dt)
    lbi = mag * jnp.sin(li * dt)
    den = lr * lr + li * li
    fr = ((lbr - 1.0) * lr + lbi * li) / den
    fi = (lbi * lr - (lbr - 1.0) * li) / den
    br = b_re.astype(F32)
    bi = b_im.astype(F32)
    bbr = fr[..., None] * br - fi[..., None] * bi
    bbi = fr[..., None] * bi + fi[..., None] * br
    j = jnp.arange(lc, dtype=F32)[:, None, None]
    ang = j * (li * dt)[None]
    lmag = j * (lr * dt)[None]
    n_state = S5_GROUPS * S5_STATE
    apr = (jnp.exp(lmag) * jnp.cos(ang)).reshape(lc, n_state)
    api = (jnp.exp(lmag) * jnp.sin(ang)).reshape(lc, n_state)
    air = (jnp.exp(-lmag) * jnp.cos(ang)).reshape(lc, n_state)
    aii = (-jnp.exp(-lmag) * jnp.sin(ang)).reshape(lc, n_state)
    a1r = lbr.reshape(1, n_state)
    a1i = lbi.reshape(1, n_state)
    gpb = LANES // S5_GROUP
    eye = jnp.eye(gpb, dtype=F32)

    def bdiag(b):
        return jnp.einsum('kgpc,gh->kgchp', b.reshape(S5_LANE_BLOCKS, gpb, S5_STATE, S5_GROUP), eye).reshape(
            S5_LANE_BLOCKS, LANES, S5_BLOCK_STATE)

    def cdiag(c):
        return jnp.einsum('kgcp,gh->kgphc', c.reshape(S5_LANE_BLOCKS, gpb, S5_GROUP, S5_STATE), eye).reshape(
            S5_LANE_BLOCKS, S5_BLOCK_STATE, LANES)

    bd = jnp.concatenate([bdiag(bbr), bdiag(bbi)], axis=-1)
    bd_hi = bd.astype(BF16)
    bd_lo = (bd - bd_hi.astype(F32)).astype(BF16)
    cd = jnp.concatenate([cdiag(c_re.astype(F32)), -cdiag(c_im.astype(F32))], axis=1).astype(BF16)
    tri = jnp.tril(jnp.ones((lc, lc), F32)).astype(BF16)
    bd_lo = jnp.concatenate([bd_hi, bd_lo], axis=1)
    return dict(bd_hi=bd_hi, bd_lo=bd_lo, air=air, aii=aii, apr=apr, api=api, a1r=a1r, a1i=a1i, tri=tri, cd=cd)


def _s5(proj3, h0_re, h0_im, tb, dvec, lc, tt):
    bsz, t, _ = proj3.shape
    ns = S5_BLOCK_STATE
    n_state = S5_GROUPS * S5_STATE
    tab = lambda: pl.BlockSpec((lc, ns), lambda k, b, i: (0, k))
    row = lambda: pl.BlockSpec((1, ns), lambda k, b, i: (0, k))
    st = lambda: pl.BlockSpec((None, 1, ns), lambda k, b, i: (b, 0, k))
    return pl.pallas_call(
        functools.partial(_s5_kernel, lc=lc, tt=tt),
        grid=(S5_LANE_BLOCKS, bsz, t // tt),
        in_specs=[pl.BlockSpec((None, tt, LANES), lambda k, b, i: (b, i, k)),
                  st(), st(),
                  pl.BlockSpec((None, LANES, 2 * ns), lambda k, b, i: (k, 0, 0)),
                  pl.BlockSpec((None, 2 * LANES, 2 * ns), lambda k, b, i: (k, 0, 0)),
                  tab(), tab(), tab(), tab(), row(), row(),
                  pl.BlockSpec((lc, 2 * lc), lambda k, b, i: (0, 0)),
                  pl.BlockSpec((None, 2 * ns, LANES), lambda k, b, i: (k, 0, 0)),
                  pl.BlockSpec((1, LANES), lambda k, b, i: (0, k))],
        out_specs=[pl.BlockSpec((None, tt, LANES), lambda k, b, i: (b, i, k)), st(), st()],
        out_shape=[jax.ShapeDtypeStruct((bsz, t, S5_WIDTH), BF16),
                   jax.ShapeDtypeStruct((bsz, 1, n_state), F32),
                   jax.ShapeDtypeStruct((bsz, 1, n_state), F32)],
        scratch_shapes=[pltpu.VMEM((tt, 2 * ns), F32), pltpu.VMEM((tt, 2 * ns), BF16),
                        pltpu.VMEM((1, ns), F32), pltpu.VMEM((1, ns), F32)],
        compiler_params=_params(("arbitrary", "arbitrary", "arbitrary")),
        name="s5",
    )(proj3, h0_re, h0_im, tb['bd_hi'], tb['bd_lo'], tb['air'], tb['aii'], tb['apr'], tb['api'],
      tb['a1r'], tb['a1i'], tb['tri'], tb['cd'], dvec)


def _glu_kernel(y_ref, w_ref, o_ref):
    acc = _dot(y_ref[...], w_ref[...])
    half = acc.shape[1] // 2
    o_ref[...] = (acc[:, :half] * _sigmoid(acc[:, half:])).astype(BF16)


def _glu(yg, w, tm):
    n = yg.shape[0]
    tn = 512
    return pl.pallas_call(
        _glu_kernel,
        grid=(n // tm, 2 * S5_WIDTH // tn),
        in_specs=[pl.BlockSpec((tm, S5_WIDTH), lambda i, j: (i, 0)),
                  pl.BlockSpec((S5_WIDTH, tn), lambda i, j: (0, j))],
        out_specs=pl.BlockSpec((tm, tn // 2), lambda i, j: (i, j)),
        out_shape=jax.ShapeDtypeStruct((n, S5_WIDTH), BF16),
        compiler_params=_params(("arbitrary", "arbitrary")),
        name="glu",
    )(yg, w)


def _dsa_kernel(q_ref, qi_ref, kiw_ref, kit_ref, kt_ref, v_ref, ut_ref, o_ref,
                key_scr, qs_scr, qis_scr, wis_scr, m_scr, l_scr, acc_scr,
                kmax_scr, lsum_scr, cand_scr, ckey_scr, thr_scr, cnt_scr,
                *, tq, kb, pos0, n_keys, n_top, nt):
    i = pl.program_id(1)
    nkp = key_scr.shape[2]
    n_cand = cand_scr.shape[1]
    slot_s = i % 2
    slot_a = 1 - slot_s
    tile_s = jnp.minimum(i, nt - 1)
    tile_a = jnp.maximum(i - 1, 0)

    def n_blocks(tile):
        k_end = jnp.minimum(n_keys, (pos0 + tile * tq + tq - 1) // CHUNK * CHUNK + CHUNK)
        return (k_end + kb - 1) // kb

    nkb_s = jnp.where(i < nt, n_blocks(tile_s), 0)
    nkb_a = jnp.where(i >= 1, n_blocks(tile_a), 0)

    @pl.when(i == 0)
    def _():
        def norm_block(b, carry):
            c0 = pl.multiple_of(b * kb, kb)
            out = []
            for g in range(N_KV_HEADS):
                kk = kt_ref[g * HEAD_DIM:(g + 1) * HEAD_DIM, pl.ds(c0, kb)].astype(F32)
                out.append(jnp.maximum(carry[g], jnp.sum(kk * kk, axis=0, keepdims=True)))
            return tuple(out)

        res = lax.fori_loop(0, nkp // kb, norm_block, (jnp.zeros((1, kb), F32),) * N_KV_HEADS)
        for g in range(N_KV_HEADS):
            kmax_scr[g] = jnp.broadcast_to(jnp.max(res[g], axis=-1, keepdims=True), (1, LANES))

    q = q_ref[...] * (HEAD_DIM ** -0.5 * LOG2E)
    for h in range(N_HEADS):
        g, r = divmod(h, Q_PER_KV)
        qs_scr[g, r * tq:(r + 1) * tq, :] = q[:, h * HEAD_DIM:(h + 1) * HEAD_DIM].astype(BF16)
    qi = qi_ref[...]
    kiw = kiw_ref[...]
    for h in range(N_IDX_HEADS):
        qis_scr[h * tq:(h + 1) * tq, :] = qi[:, h * IDX_DIM:(h + 1) * IDX_DIM].astype(BF16)
        wis_scr[h * tq:(h + 1) * tq, :] = kiw[:, IDX_DIM + h:IDX_DIM + h + 1]

    q_chunk = (pos0 + tile_s * tq + lax.broadcasted_iota(I32, (tq, 1), 0)) // CHUNK

    def to_key(x):
        bits = pltpu.bitcast(x, I32)
        return jnp.where(bits < 0, bits ^ 0x7FFFFFFF, bits)

    cand_scr[slot_s] = jnp.full(cand_scr.shape[1:], -jnp.inf, F32)

    def score_block(b, masked=True):
        c0 = pl.multiple_of(b * kb, kb)
        s = _dot(qis_scr[...], kit_ref[:, pl.ds(c0, kb)])
        s = jnp.maximum(s, 0.0) * wis_scr[...]
        sc = s[0:tq]
        for h in range(1, N_IDX_HEADS):
            sc = sc + s[h * tq:(h + 1) * tq]
        blk = sc + 0.0
        if masked:
            k_pos = c0 + lax.broadcasted_iota(I32, (1, kb), 1)
            adm = (k_pos // CHUNK <= q_chunk) & (k_pos < n_keys)
            blk = jnp.where(adm, blk, -jnp.inf)
        key_scr[slot_s, :, pl.ds(c0, kb)] = blk
        for rg in range(tq // 8):
            rows = slice(rg * 8, rg * 8 + 8)
            top = [cand_scr[slot_s, j, rows, :] for j in range(n_cand)]
            for c in range(kb // LANES):
                x = blk[rows, c * LANES:(c + 1) * LANES]
                for j in range(n_cand):
                    hi = jnp.maximum(top[j], x)
                    x = jnp.minimum(top[j], x)
                    top[j] = hi
            for j in range(n_cand):
                cand_scr[slot_s, j, rows, :] = top[j]

    def count_all(hit_fn):
        def body(b, acc):
            c0 = pl.multiple_of(b * kb, kb)
            hit = hit_fn(key_scr[slot_a, :, pl.ds(c0, kb)])
            for c in range(kb // LANES):
                acc = acc + hit[:, c * LANES:(c + 1) * LANES]
            return acc
        acc = lax.fori_loop(0, nkb_a, body, jnp.zeros((tq, LANES), F32))
        return jnp.sum(acc, axis=-1, keepdims=True)

    def count_cand(cand):
        acc = jnp.zeros((tq, LANES), F32)
        for j in range(n_cand):
            acc = acc + jnp.where(ckey_scr[j] >= cand, 1.0, 0.0)
        return jnp.sum(acc, axis=-1, keepdims=True)

    def bisect(count_fn, first_bit=0, prefix=None):
        def bit_pass(it, t_off):
            cand_off = t_off | lax.shift_left(jnp.int32(1), 31 - it)
            cnt = count_fn(cand_off ^ INT_MIN)
            return jnp.where(cnt >= n_top, cand_off, t_off)
        t_off = lax.fori_loop(first_bit, 32, bit_pass, jnp.zeros((tq, 1), I32) if prefix is None else prefix)
        key = jnp.maximum(t_off ^ INT_MIN, KEY_LOWEST_FINITE)
        return pltpu.bitcast(jnp.where(key < 0, key ^ 0x7FFFFFFF, key), F32)

    def candidate_bracket():
        hi = to_key(jnp.max(cand_scr[slot_a, 0], axis=-1, keepdims=True)) ^ INT_MIN
        lo = to_key(jnp.min(cand_scr[slot_a, 1], axis=-1, keepdims=True)) ^ INT_MIN
        n_common = jnp.min(lax.clz(hi ^ lo).astype(F32)).astype(I32)
        mask = jnp.where(n_common > 0, lax.shift_left(jnp.int32(-1), 32 - jnp.maximum(n_common, 1)), 0)
        return n_common, hi & mask

    @pl.when(i >= 1)
    def _():
        for j in range(n_cand):
            ckey_scr[j] = to_key(cand_scr[slot_a, j])
        thr_cand = bisect(count_cand, *candidate_bracket()) if n_top <= 2 * LANES else bisect(count_cand)
        thr_scr[...] = thr_cand
        n_all = count_all(lambda x: jnp.where(x >= thr_cand, 1.0, 0.0))
        n_listed = jnp.zeros((tq, LANES), F32)
        for j in range(n_cand):
            n_listed = n_listed + jnp.where(cand_scr[slot_a, j] >= thr_cand, 1.0, 0.0)
        cnt_scr[...] = n_all
        overflow = jnp.max(n_all - jnp.sum(n_listed, axis=-1, keepdims=True)) > 0.0

        @pl.when(overflow)
        def _():
            thr_all = bisect(lambda cand: count_all(lambda x: jnp.where(to_key(x) >= cand, 1.0, 0.0)))
            thr_scr[...] = thr_all
            cnt_scr[...] = count_all(lambda x: jnp.where(x >= thr_all, 1.0, 0.0))

        @pl.when(jnp.max(cnt_scr[...]) > n_top)
        def _():
            thr_t = thr_scr[...]
            need = n_top - count_all(lambda x: jnp.where(x > thr_t, 1.0, 0.0))

            def drop_surplus(b, seen):
                c0 = pl.multiple_of(b * kb, kb)
                x = key_scr[slot_a, :, pl.ds(c0, kb)]
                tied = x == thr_t
                rank = seen + _dot(jnp.where(tied, 1.0, 0.0).astype(BF16), ut_ref[...])
                key_scr[slot_a, :, pl.ds(c0, kb)] = jnp.where(tied & (rank > need), -jnp.inf, x)
                return rank[:, kb - 1:kb]

            lax.fori_loop(0, nkb_a, drop_surplus, jnp.zeros((tq, 1), F32))

    thr = thr_scr[...]

    acc_scr[...] = jnp.zeros(acc_scr.shape, F32)
    lsum_scr[...] = jnp.zeros(lsum_scr.shape, F32)
    bound_max = jnp.float32(0.0)
    for g in range(N_KV_HEADS):
        qg = qs_scr[g].astype(F32)
        qn2 = jnp.sum(qg * qg, axis=-1, keepdims=True)
        bound = jnp.sqrt(qn2 * kmax_scr[g][:, 0:1]) * 1.002
        bound_max = jnp.maximum(bound_max, jnp.max(bound))
    fast = bound_max <= FAST_SOFTMAX_BOUND

    @pl.when(jnp.logical_not(fast))
    def _():
        m_scr[...] = jnp.full(m_scr.shape, NEG_BIG, F32)
        l_scr[...] = jnp.zeros(l_scr.shape, F32)

    def attend_fast(b):
        c0 = pl.multiple_of(b * kb, kb)
        bias = jnp.where(key_scr[slot_a, :, pl.ds(c0, kb)] >= thr, 0.0, NEG_BIG)
        for g in range(N_KV_HEADS):
            lg = _dot(qs_scr[g], kt_ref[g * HEAD_DIM:(g + 1) * HEAD_DIM, pl.ds(c0, kb)])
            parts = []
            for r in range(Q_PER_KV):
                rows = slice(r * tq, (r + 1) * tq)
                e = [jnp.exp2(lg[rows, c * LANES:(c + 1) * LANES] + bias[:, c * LANES:(c + 1) * LANES])
                     for c in range(kb // LANES)]
                ls = e[0]
                for c in range(1, kb // LANES):
                    ls = ls + e[c]
                lsum_scr[g, rows, :] += ls
                parts.append(jnp.concatenate(e, axis=1).astype(BF16))
            p = jnp.concatenate(parts, axis=0)
            acc_scr[g] += _dot(p, v_ref[pl.ds(c0, kb), g * HEAD_DIM:(g + 1) * HEAD_DIM])

    def attend_general(b):
        c0 = pl.multiple_of(b * kb, kb)
        sel = key_scr[slot_a, :, pl.ds(c0, kb)] >= thr
        sel4 = jnp.concatenate([sel] * Q_PER_KV, axis=0)
        for g in range(N_KV_HEADS):
            lg = _dot(qs_scr[g], kt_ref[g * HEAD_DIM:(g + 1) * HEAD_DIM, pl.ds(c0, kb)])
            lg = jnp.where(sel4, lg, NEG_BIG)
            m_old = m_scr[g]
            m_new = jnp.maximum(m_old, jnp.max(lg, axis=-1, keepdims=True))
            p = jnp.exp2(lg - m_new)
            alpha = jnp.exp2(m_old - m_new)
            l_scr[g] = alpha * l_scr[g] + jnp.sum(p, axis=-1, keepdims=True)
            pv = _dot(p.astype(BF16), v_ref[pl.ds(c0, kb), g * HEAD_DIM:(g + 1) * HEAD_DIM])
            acc_scr[g] = alpha * acc_scr[g] + pv
            m_scr[g] = m_new

    def loop(lo, hi, *fns):
        def body(b, carry):
            for fn in fns:
                fn(b)
            return carry
        lax.fori_loop(lo, hi, body, 0)

    n_fused = jnp.where(fast, jnp.minimum(nkb_a, nkb_s), 0)
    first_row_end = (pos0 + tile_s * tq) // CHUNK * CHUNK + CHUNK
    n_plain = jnp.minimum(n_fused, jnp.minimum(first_row_end, n_keys) // kb)
    loop(0, n_plain, functools.partial(score_block, masked=False), attend_fast)
    loop(n_plain, n_fused, score_block, attend_fast)
    loop(n_fused, nkb_s, score_block)
    loop(n_fused, jnp.where(fast, nkb_a, 0), attend_fast)
    loop(0, jnp.where(fast, 0, nkb_a), attend_general)

    @pl.when((i >= 1) & fast)
    def _():
        for h in range(N_HEADS):
            g, r = divmod(h, Q_PER_KV)
            rows = slice(r * tq, (r + 1) * tq)
            l = jnp.sum(lsum_scr[g, rows, :], axis=-1, keepdims=True)
            o_ref[:, h * HEAD_DIM:(h + 1) * HEAD_DIM] = (acc_scr[g, rows, :] / l).astype(BF16)

    @pl.when((i >= 1) & jnp.logical_not(fast))
    def _():
        for h in range(N_HEADS):
            g, r = divmod(h, Q_PER_KV)
            rows = slice(r * tq, (r + 1) * tq)
            o_ref[:, h * HEAD_DIM:(h + 1) * HEAD_DIM] = (acc_scr[g, rows, :] / l_scr[g, rows, :]).astype(BF16)


def _dsa(proj3, kit, kt, v, *, tq, kb, pos0, n_keys, n_top):
    bsz, t, _ = proj3.shape
    nkp = kit.shape[-1]
    nt = t // tq
    kern = functools.partial(_dsa_kernel, tq=tq, kb=kb, pos0=pos0, n_keys=n_keys, n_top=n_top, nt=nt)
    resident = dict(pipeline_mode=pl.Buffered(1))
    prev = lambda i: jnp.maximum(i - 1, 0)
    cur = lambda i: jnp.minimum(i, nt - 1)
    return pl.pallas_call(
        kern,
        grid=(bsz, nt + 1),
        in_specs=[pl.BlockSpec((None, tq, ATT_WIDTH), lambda b, i: (b, prev(i), COL_Q // ATT_WIDTH)),
                  pl.BlockSpec((None, tq, 512), lambda b, i: (b, cur(i), COL_QI // 512)),
                  pl.BlockSpec((None, tq, 512), lambda b, i: (b, cur(i), COL_KI // 512)),
                  pl.BlockSpec((None, IDX_DIM, nkp), lambda b, i: (b, 0, 0), **resident),
                  pl.BlockSpec((None, KV_WIDTH, nkp), lambda b, i: (b, 0, 0), **resident),
                  pl.BlockSpec((None, nkp, KV_WIDTH), lambda b, i: (b, 0, 0), **resident),
                  pl.BlockSpec((kb, kb), lambda b, i: (0, 0), **resident)],
        out_specs=pl.BlockSpec((None, tq, ATT_WIDTH), lambda b, i: (b, prev(i), 0)),
        out_shape=jax.ShapeDtypeStruct((bsz, t, ATT_WIDTH), BF16),
        scratch_shapes=[pltpu.VMEM((2, tq, nkp), F32),
                        pltpu.VMEM((N_KV_HEADS, Q_PER_KV * tq, HEAD_DIM), BF16),
                        pltpu.VMEM((N_IDX_HEADS * tq, IDX_DIM), BF16),
                        pltpu.VMEM((N_IDX_HEADS * tq, 1), F32),
                        pltpu.VMEM((N_KV_HEADS, Q_PER_KV * tq, 1), F32),
                        pltpu.VMEM((N_KV_HEADS, Q_PER_KV * tq, 1), F32),
                        pltpu.VMEM((N_KV_HEADS, Q_PER_KV * tq, HEAD_DIM), F32),
                        pltpu.VMEM((N_KV_HEADS, 1, LANES), F32),
                        pltpu.VMEM((N_KV_HEADS, Q_PER_KV * tq, LANES), F32),
                        pltpu.VMEM((2, TOPK_PER_LANE, tq, LANES), F32),
                        pltpu.VMEM((TOPK_PER_LANE, tq, LANES), I32),
                        pltpu.VMEM((tq, 1), F32),
                        pltpu.VMEM((tq, 1), F32)],
        compiler_params=_params(("arbitrary", "arbitrary")),
        name="dsa",
    )(proj3, proj3, proj3, kit, kt, v, jnp.triu(jnp.ones((kb, kb), F32)).astype(BF16))


def _merge_kernel(ya_ref, yb_ref, wa_ref, wb_ref, ga_ref, gb_ref, o_ref):
    o_ref[...] = (ga_ref[...] * _dot(ya_ref[...], wa_ref[...])
                  + gb_ref[...] * _dot(yb_ref[...], wb_ref[...])).astype(BF16)


def _merge(ya, yb, wa, wb, proj, tm):
    n = ya.shape[0]
    tn = 512
    return pl.pallas_call(
        _merge_kernel,
        grid=(n // tm, D_MODEL // tn),
        in_specs=[pl.BlockSpec((tm, S5_WIDTH), lambda i, j: (i, 0)),
                  pl.BlockSpec((tm, ATT_WIDTH), lambda i, j: (i, 0)),
                  pl.BlockSpec((S5_WIDTH, tn), lambda i, j: (0, j)),
                  pl.BlockSpec((ATT_WIDTH, tn), lambda i, j: (0, j)),
                  pl.BlockSpec((tm, tn), lambda i, j: (i, COL_GA // tn + j)),
                  pl.BlockSpec((tm, tn), lambda i, j: (i, COL_GB // tn + j))],
        out_specs=pl.BlockSpec((tm, tn), lambda i, j: (i, j)),
        out_shape=jax.ShapeDtypeStruct((n, D_MODEL), BF16),
        compiler_params=_params(("arbitrary", "arbitrary")),
        name="merge",
    )(ya, yb, wa, wb, proj, proj)


def _outproj_kernel(m_ref, w_ref, x_ref, o_ref):
    o_ref[...] = x_ref[...] + _dot(m_ref[...], w_ref[...])


def _outproj(merged, w, x, tm):
    n = x.shape[0]
    tn = 512
    return pl.pallas_call(
        _outproj_kernel,
        grid=(n // tm, D_MODEL // tn),
        in_specs=[pl.BlockSpec((tm, D_MODEL), lambda i, j: (i, 0)),
                  pl.BlockSpec((D_MODEL, tn), lambda i, j: (0, j)),
                  pl.BlockSpec((tm, tn), lambda i, j: (i, j))],
        out_specs=pl.BlockSpec((tm, tn), lambda i, j: (i, j)),
        out_shape=jax.ShapeDtypeStruct((n, D_MODEL), F32),
        compiler_params=_params(("arbitrary", "arbitrary")),
        name="outproj",
    )(merged, w, x)


ROUTER_GROUP_LANE = N_EXPERTS


def _router_kernel(x_ref, g_ref, whi_ref, wlo_ref, b_ref, hn_ref, ids_ref, gates_ref):
    xf = x_ref[...]
    ms = jnp.mean(xf * xf, axis=-1, keepdims=True)
    hn = xf * lax.rsqrt(ms + EPS) * g_ref[...]
    hn_ref[...] = hn
    h_hi, h_lo = _split_bf16(hn)
    whi = whi_ref[...]
    lg = _dot(h_hi, whi) + _dot(h_lo, whi) + _dot(h_hi, wlo_ref[...]) + b_ref[...]
    lane = lax.broadcasted_iota(I32, lg.shape, 1)
    lane_f = lane.astype(F32)
    big = float(LANES)
    is_g = (lane >= ROUTER_GROUP_LANE) & (lane < ROUTER_GROUP_LANE + N_GROUPS)
    g_max = jnp.max(jnp.where(is_g, lg, -jnp.inf), axis=-1, keepdims=True)
    g_den = jnp.sum(jnp.where(is_g, jnp.exp(lg - g_max), 0.0), axis=-1, keepdims=True)
    g_w = 1.0 / g_den
    g_lane = jnp.min(jnp.where(is_g & (lg == g_max), lane_f, big), axis=-1, keepdims=True)
    g_sel = g_lane.astype(I32) - ROUTER_GROUP_LANE
    is_e = (lane < N_EXPERTS) & (lane // EXPERTS_PER_GROUP == g_sel)
    e_max = jnp.max(jnp.where(is_e, lg, -jnp.inf), axis=-1, keepdims=True)
    pe = jnp.where(is_e, jnp.exp(lg - e_max), 0.0)
    pe = pe / jnp.sum(pe, axis=-1, keepdims=True)
    pe = jnp.where(is_e, pe, -1.0)
    p1 = jnp.max(pe, axis=-1, keepdims=True)
    i1 = jnp.min(jnp.where(pe == p1, lane_f, big), axis=-1, keepdims=True)
    pe2 = jnp.where(lane_f == i1, -1.0, pe)
    p2 = jnp.max(pe2, axis=-1, keepdims=True)
    i2 = jnp.min(jnp.where(pe2 == p2, lane_f, big), axis=-1, keepdims=True)
    tot = p1 + p2
    ids_ref[...] = jnp.where(lane == 0, i1, jnp.where(lane == 1, i2, 0.0)).astype(I32)
    gates_ref[...] = jnp.where(lane == 0, g_w * (p1 / tot), jnp.where(lane == 1, g_w * (p2 / tot), 0.0))


def _router(x1, g, whi, wlo, bias, tm):
    n = x1.shape[0]
    return pl.pallas_call(
        _router_kernel,
        grid=(n // tm,),
        in_specs=[pl.BlockSpec((tm, D_MODEL), lambda i: (i, 0)),
                  pl.BlockSpec((1, D_MODEL), lambda i: (0, 0)),
                  pl.BlockSpec((D_MODEL, LANES), lambda i: (0, 0)),
                  pl.BlockSpec((D_MODEL, LANES), lambda i: (0, 0)),
                  pl.BlockSpec((1, LANES), lambda i: (0, 0))],
        out_specs=[pl.BlockSpec((tm, D_MODEL), lambda i: (i, 0)),
                   pl.BlockSpec((tm, LANES), lambda i: (i, 0)),
                   pl.BlockSpec((tm, LANES), lambda i: (i, 0))],
        out_shape=[jax.ShapeDtypeStruct((n, D_MODEL), F32),
                   jax.ShapeDtypeStruct((n, LANES), I32),
                   jax.ShapeDtypeStruct((n, LANES), F32)],
        compiler_params=_params(("arbitrary",)),
        name="router",
    )(x1, g, whi, wlo, bias)


def _row_copy(src, dst, s_row, d_row, sem):
    return pltpu.make_async_copy(src.at[pl.ds(s_row, 1)], dst.at[pl.ds(d_row, 1)], sem)


def _dispatch_kernel(dest_ref, hn_ref, xs_in_ref, xs_ref, sem, *, ch):
    del xs_in_ref

    def issue(t, carry):
        for s in range(2):
            _row_copy(hn_ref, xs_ref, t, dest_ref[0, 0, 2 * t + s], sem).start()
        return carry

    lax.fori_loop(0, ch // 2, issue, 0, unroll=8)
    for s in range(2):
        pltpu.make_async_copy(hn_ref, xs_ref.at[pl.ds(0, ch // 2)], sem).wait()


def _dispatch(dest, hn, xs0, ch):
    m = dest.shape[0]
    return pl.pallas_call(
        functools.partial(_dispatch_kernel, ch=ch),
        grid=(m // ch,),
        in_specs=[pl.BlockSpec((1, 1, ch), lambda i: (i, 0, 0), memory_space=pltpu.SMEM),
                  pl.BlockSpec((ch // 2, D_MODEL), lambda i: (i, 0)),
                  pl.BlockSpec(memory_space=pl.ANY)],
        out_specs=pl.BlockSpec(memory_space=pl.ANY),
        out_shape=jax.ShapeDtypeStruct(xs0.shape, xs0.dtype),
        scratch_shapes=[pltpu.SemaphoreType.DMA(())],
        input_output_aliases={2: 0},
        compiler_params=_params(("arbitrary",), disable_bounds_checks=True, has_side_effects=True),
        name="dispatch",
    )(dest.reshape(m // ch, 1, ch), hn, xs0)


def _expert_kernel(be_ref, nu_ref, x_ref, wg_hbm, wu_hbm, wd_hbm, o_ref,
                   wg_f, wu_f, wd_f, wg_s, wu_s, wd_s, ord_ref, sems):
    b = pl.program_id(0)
    n_used = nu_ref[0]
    e_here = be_ref[b]

    def weight_copies(e, slot):
        return [pltpu.make_async_copy(src.at[e], dst.at[slot], sems.at[slot, k])
                for k, (src, dst) in enumerate(((wg_hbm, wg_f), (wu_hbm, wu_f), (wd_hbm, wd_f)))]

    @pl.when(b < n_used)
    def _():
        @pl.when((b == 0) | (e_here != be_ref[jnp.maximum(b - 1, 0)]))
        def _():
            ordinal = jnp.where(b == 0, 0, ord_ref[0] + 1)
            ord_ref[0] = ordinal
            slot = ordinal % 2
            nxt = lax.while_loop(lambda j: (j < n_used) & (be_ref[jnp.minimum(j, be_ref.shape[0] - 1)] == e_here),
                                 lambda j: j + 1, b + 1)

            @pl.when(b == 0)
            def _():
                for c in weight_copies(e_here, slot):
                    c.start()

            for c in weight_copies(e_here, slot):
                c.wait()
            for s in range(2):
                @pl.when(slot == s)
                def _(s=s):
                    wg_s[...] = wg_f[s].astype(BF16)
                    wu_s[...] = wu_f[s].astype(BF16)
                    wd_s[...] = wd_f[s].astype(BF16)

            @pl.when(nxt < n_used)
            def _():
                for c in weight_copies(be_ref[jnp.minimum(nxt, be_ref.shape[0] - 1)], 1 - slot):
                    c.start()

        x = x_ref[...].astype(BF16)
        gate = _dot(x, wg_s[...])
        h = gate * _sigmoid(gate) * _dot(x, wu_s[...])
        o_ref[...] = _dot(h.astype(BF16), wd_s[...])

    @pl.when(b >= nu_ref[0])
    def _():
        o_ref[...] = jnp.zeros(o_ref.shape, F32)


def _experts(block_e, n_used, xs, wg, wu, wd, bm):
    rows = xs.shape[0]
    nb = rows // bm
    any_space = pl.BlockSpec(memory_space=pl.ANY)
    grid_spec = pltpu.PrefetchScalarGridSpec(
        num_scalar_prefetch=2,
        grid=(nb,),
        in_specs=[pl.BlockSpec((bm, D_MODEL), lambda b, *_: (b, 0)), any_space, any_space, any_space],
        out_specs=pl.BlockSpec((bm, D_MODEL), lambda b, *_: (b, 0)),
        scratch_shapes=[pltpu.VMEM((2, D_MODEL, D_EXPERT), F32), pltpu.VMEM((2, D_MODEL, D_EXPERT), F32),
                        pltpu.VMEM((2, D_EXPERT, D_MODEL), F32),
                        pltpu.VMEM((D_MODEL, D_EXPERT), BF16), pltpu.VMEM((D_MODEL, D_EXPERT), BF16),
                        pltpu.VMEM((D_EXPERT, D_MODEL), BF16),
                        pltpu.SMEM((1,), I32),
                        pltpu.SemaphoreType.DMA((2, 3))])
    return pl.pallas_call(
        _expert_kernel,
        grid_spec=grid_spec,
        out_shape=jax.ShapeDtypeStruct((rows, D_MODEL), F32),
        compiler_params=_params(("arbitrary",)),
        name="experts",
    )(block_e, n_used, xs, wg, wu, wd)


def _combine_kernel(dest_ref, gates_ref, x_ref, ys_ref, o_ref, buf, sem, *, tc):
    def issue(t, carry):
        for s in range(2):
            _row_copy(ys_ref, buf.at[s], dest_ref[0, 0, 2 * t + s], t, sem).start()
        return carry

    lax.fori_loop(0, tc, issue, 0, unroll=8)
    for s in range(2):
        pltpu.make_async_copy(ys_ref.at[pl.ds(0, tc)], buf.at[s], sem).wait()
    gates = gates_ref[...]
    o_ref[...] = x_ref[...] + gates[:, 0:1] * buf[0] + gates[:, 1:2] * buf[1]


def _combine(dest, gates, x1, ys, tc):
    n = x1.shape[0]
    return pl.pallas_call(
        functools.partial(_combine_kernel, tc=tc),
        grid=(n // tc,),
        in_specs=[pl.BlockSpec((1, 1, 2 * tc), lambda i: (i, 0, 0), memory_space=pltpu.SMEM),
                  pl.BlockSpec((tc, LANES), lambda i: (i, 0)),
                  pl.BlockSpec((tc, D_MODEL), lambda i: (i, 0)),
                  pl.BlockSpec(memory_space=pl.ANY)],
        out_specs=pl.BlockSpec((tc, D_MODEL), lambda i: (i, 0)),
        out_shape=jax.ShapeDtypeStruct((n, D_MODEL), F32),
        scratch_shapes=[pltpu.VMEM((2, tc, D_MODEL), F32), pltpu.SemaphoreType.DMA(())],
        compiler_params=_params(("arbitrary",), disable_bounds_checks=True),
        name="combine",
    )(dest.reshape(n // tc, 1, 2 * tc), gates, x1, ys)


def _moe(x1s, p, tms, bm):
    routed = [_router(x1, p['norm_ffn'], p['wr_hi'], p['wr_lo'], p['b_router'], tm) for x1, tm in zip(x1s, tms)]
    flat_e = jnp.concatenate([ids[:, :2].reshape(-1) for _, ids, _ in routed])
    m = flat_e.shape[0]
    onehot = (flat_e[:, None] == jnp.arange(N_EXPERTS, dtype=I32)[None, :]).astype(I32)
    csum = jnp.cumsum(onehot, axis=0)
    counts = csum[-1]
    padded = (counts + bm - 1) // bm * bm
    pad_end = jnp.cumsum(padded)
    pad_start = pad_end - padded
    dest = jnp.sum(onehot * (csum - 1 + pad_start[None, :]), axis=1).astype(I32)
    nb = -(-(m + N_EXPERTS * (bm - 1)) // bm)
    block_start = jnp.arange(nb, dtype=I32) * bm
    block_e = jnp.minimum(jnp.sum((pad_end[None, :] <= block_start[:, None]).astype(I32), axis=1),
                          N_EXPERTS - 1).astype(I32)
    n_used = (pad_end[-1] // bm).astype(I32).reshape(1)
    xs = jnp.zeros((nb * bm, D_MODEL), F32)
    dests = []
    first = 0
    for hn, _, _ in routed:
        dests.append(dest[first:first + 2 * hn.shape[0]])
        first += 2 * hn.shape[0]
        xs = _dispatch(dests[-1], hn, xs, min(2 * hn.shape[0], 2048))
    ys = _experts(block_e, n_used, xs, p['w_gate'], p['w_up'], p['w_down'], bm)
    return [_combine(d, gates, x1, ys, min(x1.shape[0], 1024)) for d, (_, _, gates), x1 in zip(dests, routed, x1s)]


def _prep(norm_mix, w_in, s5_a_re, s5_a_im, s5_log_dt, s5_b_re, s5_b_im, s5_c_re, s5_c_im, s5_d,
          w_glu, q_norm, k_norm, idx_k_norm, w_branch_a, w_branch_b, w_out, norm_ffn,
          w_router_group, b_router_group, w_router_expert, b_router_expert, w_gate, w_up, w_down):
    pts = []
    acc = 0
    for s in IN_SIZES[:-1]:
        acc += s
        pts.append(acc)
    w_u, w_q, w_k, w_v, w_qi, w_ki, w_wi, w_ga, w_gb = jnp.split(w_in, pts, axis=1)
    pad = PROJ_COLS - sum(IN_SIZES)
    w_proj = jnp.concatenate([w_u, w_q, w_ga, w_gb, w_qi, w_k, w_v, w_ki, w_wi,
                              jnp.zeros((D_MODEL, pad), w_in.dtype)], axis=1).astype(BF16)
    one = lambda k: jnp.ones((k,), F32)
    gain = jnp.concatenate([one(S5_WIDTH), jnp.tile(q_norm.astype(F32), N_HEADS), one(2 * D_MODEL),
                            one(N_IDX_HEADS * IDX_DIM), jnp.tile(k_norm.astype(F32), N_KV_HEADS), one(KV_WIDTH),
                            idx_k_norm.astype(F32), one(N_IDX_HEADS + pad)]).reshape(1, PROJ_COLS)
    half = 256
    wv, wg = w_glu[:, :S5_WIDTH], w_glu[:, S5_WIDTH:]
    w_glu_p = jnp.concatenate(
        [jnp.concatenate([wv[:, c * half:(c + 1) * half], wg[:, c * half:(c + 1) * half]], axis=1)
         for c in range(S5_WIDTH // half)], axis=1).astype(BF16)
    w_r = jnp.concatenate([w_router_expert.astype(F32), w_router_group.astype(F32),
                           jnp.zeros((D_MODEL, LANES - N_EXPERTS - N_GROUPS), F32)], axis=1)
    wr_hi = w_r.astype(BF16)
    wr_lo = (w_r - wr_hi.astype(F32)).astype(BF16)
    b_r = jnp.concatenate([b_router_expert.astype(F32), b_router_group.astype(F32),
                           jnp.zeros((LANES - N_EXPERTS - N_GROUPS,), F32)]).reshape(1, LANES)
    return dict(norm_mix=norm_mix.astype(F32).reshape(1, D_MODEL), w_proj=w_proj, gain=gain,
                s5=_s5_tables(s5_a_re, s5_a_im, s5_log_dt, s5_b_re, s5_b_im, s5_c_re, s5_c_im, S5_MAX_CHUNK),
                s5_d=s5_d.astype(F32).reshape(1, S5_WIDTH), w_glu=w_glu_p,
                w_a=w_branch_a.astype(BF16), w_b=w_branch_b.astype(BF16), w_out=w_out.astype(BF16),
                norm_ffn=norm_ffn.astype(F32).reshape(1, D_MODEL), wr_hi=wr_hi, wr_lo=wr_lo, b_router=b_r,
                w_gate=w_gate, w_up=w_up, w_down=w_down)


MOE_BLOCK_ROWS = 256


def _tile_plan(bsz, t, n_keys):
    n = bsz * t
    long_seq = t % 128 == 0
    kb = 512 if long_seq else 384
    assert n % 256 == 0 and (long_seq or t % 16 == 0) and n_keys > 0
    return dict(tm=1024 if n % 1024 == 0 else 256,
                lc=S5_MAX_CHUNK if t % S5_MAX_CHUNK == 0 else t,
                tt=4096 if t % 4096 == 0 else t,
                tq=128 if long_seq else t,
                kb=kb)


def _mixer(x, h0_re, h0_im, k_past, v_past, ki_past, p, *, tm, lc, tt, tq, kb):
    bsz, t, _ = x.shape
    n = bsz * t
    x2 = x.reshape(n, D_MODEL)
    proj = _inproj(x2, p['norm_mix'], p['w_proj'], p['gain'], tm)
    proj3 = proj.reshape(bsz, t, PROJ_COLS)
    k = proj3[:, :, COL_K:COL_K + KV_WIDTH]
    v = proj3[:, :, COL_V:COL_V + KV_WIDTH]
    ki = proj3[:, :, COL_KI:COL_KI + IDX_DIM]
    n_state = S5_GROUPS * S5_STATE
    tables = dict(p['s5'])
    for name in ('air', 'aii', 'apr', 'api'):
        tables[name] = tables[name][:lc]
    tables['tri'] = jnp.tile(tables['tri'][:lc, :lc], (1, 2))
    yg, s_re, s_im = _s5(proj3, h0_re.reshape(bsz, 1, n_state), h0_im.reshape(bsz, 1, n_state),
                         tables, p['s5_d'], lc, tt)
    tm_wide = 2 * tm if n % (2 * tm) == 0 else tm
    y_a = _glu(yg.reshape(n, S5_WIDTH), p['w_glu'], tm_wide)
    if k_past is None:
        pos0 = 0
        k_all, v_all, ki_all = k, v, ki
    else:
        pos0 = k_past.shape[1]
        k_all = jnp.concatenate([k_past.reshape(bsz, pos0, KV_WIDTH), k], axis=1)
        v_all = jnp.concatenate([v_past.reshape(bsz, pos0, KV_WIDTH), v], axis=1)
        ki_all = jnp.concatenate([ki_past, ki], axis=1)
    n_keys = k_all.shape[1]
    n_top = min(TOP_K_MAX, n_keys // 4)
    nkp = -(-n_keys // kb) * kb
    padk = lambda a: jnp.pad(a.astype(BF16), ((0, 0), (0, nkp - n_keys), (0, 0)))
    kit = jnp.swapaxes(padk(ki_all), 1, 2)
    kt = jnp.swapaxes(padk(k_all), 1, 2)
    y_b = _dsa(proj3, kit, kt, padk(v_all), tq=tq, kb=kb, pos0=pos0, n_keys=n_keys, n_top=n_top)
    merged = _merge(y_a, y_b.reshape(n, ATT_WIDTH), p['w_a'], p['w_b'], proj, tm_wide)
    x1 = _outproj(merged, p['w_out'], x2, tm_wide)
    return (x1, s_re.reshape(bsz, S5_GROUPS, S5_STATE),
            s_im.reshape(bsz, S5_GROUPS, S5_STATE), k.reshape(bsz, t, N_KV_HEADS, HEAD_DIM),
            v.reshape(bsz, t, N_KV_HEADS, HEAD_DIM), ki)


def kernel(x_prompt, x_sample, state_s5_re, state_s5_im, cache_k, cache_v, cache_idx_k, norm_mix, w_in, s5_a_re, s5_a_im, s5_log_dt, s5_b_re, s5_b_im, s5_c_re, s5_c_im, s5_d, w_glu, q_norm, k_norm, idx_k_norm, w_branch_a, w_branch_b, w_out, norm_ffn, w_router_group, b_router_group, w_router_expert, b_router_expert, w_gate, w_up, w_down):
    p = _prep(norm_mix, w_in, s5_a_re, s5_a_im, s5_log_dt, s5_b_re, s5_b_im, s5_c_re, s5_c_im, s5_d,
              w_glu, q_norm, k_norm, idx_k_norm, w_branch_a, w_branch_b, w_out, norm_ffn,
              w_router_group, b_router_group, w_router_expert, b_router_expert, w_gate, w_up, w_down)
    h0 = jnp.zeros((x_prompt.shape[0], S5_GROUPS, S5_STATE), F32)
    plan_p = _tile_plan(x_prompt.shape[0], x_prompt.shape[1], x_prompt.shape[1])
    plan_s = _tile_plan(x_sample.shape[0], x_sample.shape[1], cache_k.shape[1] + x_sample.shape[1])
    xp, srp, sip, kp, vp, kip = _mixer(x_prompt, h0, h0, None, None, None, p, **plan_p)
    xs, srs, sis, ks, vs, kis = _mixer(x_sample, state_s5_re, state_s5_im, cache_k, cache_v, cache_idx_k, p, **plan_s)
    yp, ys = _moe([xp, xs], p, [plan_p['tm'], plan_s['tm']], MOE_BLOCK_ROWS)
    return (yp.reshape(x_prompt.shape), ys.reshape(x_sample.shape), srp, sip, kp, vp, kip, srs, sis, ks, vs, kis)
```
